```python
import math
import jax
import jax.numpy as jnp
from jax import lax
import numpy as np

D_MODEL = 1024
BATCH = 2
SEQ = 8192
DEPTH = 2
DEC_BATCH = 32
DEC_SEQ = 1
PAST_LEN = 16384
PAGE_SIZE = 128

N_ATTN_LAYERS = (DEPTH + 1) // 2
N_SSM_LAYERS = DEPTH // 2

N_HEADS = 16
HEAD_DIM = 64
N_KV_HEADS = 4
GROUP = N_HEADS // N_KV_HEADS
CMP_BLOCK = 32
CMP_STRIDE = 16
CMP_RATIO = CMP_BLOCK // CMP_STRIDE
CMP_HIDDEN = 2 * HEAD_DIM
SEL_BLOCK = 64
SEL_RATIO = SEL_BLOCK // CMP_STRIDE
N_SELECT = 16
WINDOW = 512
FORCE_BONUS = 1e4
Q_BLOCK = 128
N_KV_SLOTS = 4
N_WIN_SLOTS = 2
ATTN_IN = N_HEADS * HEAD_DIM + (N_KV_SLOTS + N_WIN_SLOTS) * N_KV_HEADS * HEAD_DIM + 3 * N_HEADS

N_BUCKETS = 32
MAX_DISTANCE = 128

SSM_D_INNER = 2 * D_MODEL
SSM_HEAD_DIM = 64
SSM_HEADS = SSM_D_INNER // SSM_HEAD_DIM
SSM_GROUPS = 8
SSM_STATE = 128
SSM_CONV = 4
SSM_CONV_DIM = SSM_D_INNER + 2 * SSM_GROUPS * SSM_STATE
SSM_CHUNK = 128
SSM_IN = SSM_D_INNER + SSM_CONV_DIM + SSM_HEADS

D_FF = 4 * D_MODEL
N_MOD = 6
EPS = 1e-6

kernel_name = 'hybrid_nsa_mamba2_adaln_decode_step'


def rmsnorm(x, g):
    x32 = x.astype(jnp.float32)
    y = x32 * lax.rsqrt(jnp.mean(x32 * x32, axis=-1, keepdims=True) + EPS)
    return (y * g.astype(jnp.float32)).astype(x.dtype)


def modulation(c, w, b):
    return (jax.nn.silu(c) @ w + b).reshape(c.shape[0], N_MOD, 1, D_MODEL)


def modulate(h, shift, scale):
    return h * (1.0 + scale) + shift


def masked_softmax(s, mask):
    s = jnp.where(mask, s.astype(jnp.float32), -jnp.inf)
    m = jnp.max(s, axis=-1, keepdims=True)
    m = jnp.where(jnp.isfinite(m), m, 0.0)
    e = jnp.exp(s - m)
    return e / jnp.maximum(jnp.sum(e, axis=-1, keepdims=True), 1e-30)


def rel_bucket(dist):
    n = jnp.maximum(dist, 0)
    exact = N_BUCKETS // 2
    nf = jnp.maximum(n, 1).astype(jnp.float32)
    large = exact + (jnp.log(nf / exact) / math.log(MAX_DISTANCE / exact) * (N_BUCKETS - exact)).astype(jnp.int32)
    return jnp.where(n < exact, n, jnp.minimum(large, N_BUCKETS - 1))


def compress(rows, pos_emb, w1, w2):
    b, l = rows.shape[0], rows.shape[1]
    n_half = l // CMP_STRIDE
    nc = n_half - CMP_RATIO + 1
    halves = rows[:, :n_half * CMP_STRIDE].reshape(b, n_half, CMP_STRIDE, N_KV_HEADS, HEAD_DIM)
    w1r = w1.reshape(CMP_RATIO, CMP_STRIDE, HEAD_DIM, CMP_HIDDEN)
    part = jnp.einsum('bnskd,rsdh->bnkrh', halves, w1r)
    pre = pos_emb.reshape(-1) @ w1
    for r in range(CMP_RATIO):
        pre = pre + part[:, r:r + nc, :, r]
    return jax.nn.silu(pre) @ w2


def nsa_project(h, w_in):
    b, l, _ = h.shape
    proj = h @ w_in
    n_q = N_HEADS * HEAD_DIM
    n_kv = (N_KV_SLOTS + N_WIN_SLOTS) * N_KV_HEADS * HEAD_DIM
    q = proj[..., :n_q].reshape(b, l, N_KV_HEADS, GROUP, HEAD_DIM) * HEAD_DIM ** -0.5
    kv = proj[..., n_q:n_q + n_kv].reshape(b, l, N_KV_SLOTS + N_WIN_SLOTS, N_KV_HEADS, HEAD_DIM)
    gates = proj[..., n_q + n_kv:].reshape(b, l, N_KV_HEADS, GROUP, 3)
    return q, kv, gates


def nsa_core(q, q_pos, cmp_k, cmp_v, sel_k, sel_v, win_k, win_v, win_pos, gates, rel_bias):
    b, nq = q.shape[0], q.shape[1]
    table = rel_bias.astype(jnp.float32).reshape(N_BUCKETS, N_KV_HEADS, GROUP)

    def shared_bias(k_pos):
        return jnp.transpose(table[rel_bucket(q_pos[:, None] - k_pos[None, :])], (2, 3, 0, 1))

    nc = cmp_k.shape[1]
    cmp_end = jnp.arange(nc) * CMP_STRIDE + CMP_BLOCK - 1
    s = jnp.einsum('bqkgd,bnkd->bkgqn', q, cmp_k).astype(jnp.float32) + shared_bias(cmp_end)
    p_cmp = masked_softmax(s, cmp_end[None, :] <= q_pos[:, None])
    o_cmp = jnp.einsum('bkgqn,bnkd->bqkgd', p_cmp.astype(cmp_v.dtype), cmp_v)

    ns = sel_k.shape[1] // SEL_BLOCK
    imp = jnp.sum(p_cmp, axis=2)
    imp = jnp.pad(imp, ((0, 0), (0, 0), (0, 0), (0, ns * SEL_RATIO - nc)))
    imp = imp.reshape(b, N_KV_HEADS, nq, ns, SEL_RATIO)
    prev = jnp.sum(imp[:, :, :, :-1, SEL_RATIO - (CMP_RATIO - 1):], axis=-1)
    blk_score = jnp.sum(imp, axis=-1) + jnp.pad(prev, ((0, 0), (0, 0), (0, 0), (1, 0)))
    blk = jnp.arange(ns)
    q_blk = (q_pos // SEL_BLOCK)[:, None]
    causal = blk[None, :] * SEL_BLOCK <= q_pos[:, None]
    forced = (blk[None, :] == 0) | (blk[None, :] == q_blk) | (blk[None, :] == q_blk - 1)
    score = jnp.where(causal, blk_score + jnp.where(forced, FORCE_BONUS, 0.0), -jnp.inf)
    n_sel = min(N_SELECT, ns)
    _, top_idx = lax.top_k(score, n_sel)

    b_ix = jnp.arange(b)[:, None, None]
    k_ix = jnp.arange(N_KV_HEADS)[None, :, None]
    flat_idx = top_idx.reshape(b, N_KV_HEADS, nq * n_sel)

    def gather_blocks(rows):
        blocks = rows.reshape(b, ns, SEL_BLOCK, N_KV_HEADS, HEAD_DIM)
        return blocks[b_ix, flat_idx, :, k_ix].reshape(b, N_KV_HEADS, nq, n_sel * SEL_BLOCK, HEAD_DIM)

    ks = gather_blocks(sel_k)
    vs = gather_blocks(sel_v)
    key_pos = (top_idx[..., None] * SEL_BLOCK + jnp.arange(SEL_BLOCK)).reshape(b, N_KV_HEADS, nq, n_sel * SEL_BLOCK)
    sel_bias = jnp.transpose(table, (1, 0, 2))[k_ix[..., None], rel_bucket(q_pos[:, None] - key_pos)]
    s = jnp.einsum('bqkgd,bkqjd->bkgqj', q, ks).astype(jnp.float32) + jnp.moveaxis(sel_bias, -1, 2)
    p_sel = masked_softmax(s, (key_pos <= q_pos[:, None])[:, :, None])
    o_sel = jnp.einsum('bkgqj,bkqjd->bqkgd', p_sel.astype(vs.dtype), vs)

    dist = q_pos[:, None] - win_pos[None, :]
    s = jnp.einsum('bqkgd,bwkd->bkgqw', q, win_k).astype(jnp.float32) + shared_bias(win_pos)
    p_win = masked_softmax(s, (dist >= 0) & (dist < WINDOW) & (win_pos[None, :] >= 0))
    o_win = jnp.einsum('bkgqw,bwkd->bqkgd', p_win.astype(win_v.dtype), win_v)

    g = jax.nn.sigmoid(gates.astype(jnp.float32))
    o = g[..., 0:1] * o_cmp + g[..., 1:2] * o_sel + g[..., 2:3] * o_win
    return o.astype(q.dtype)


def nsa_prompt(h, w_in, w_out, cmp_pos, cmp_w1, cmp_w2, rel_bias):
    b, t, _ = h.shape
    q, kv, gates = nsa_project(h, w_in)
    cmp_k = compress(kv[:, :, 0], cmp_pos[0], cmp_w1[0], cmp_w2[0])
    cmp_v = compress(kv[:, :, 1], cmp_pos[1], cmp_w1[1], cmp_w2[1])
    sel_pad = ((0, 0), (0, (-t) % SEL_BLOCK), (0, 0), (0, 0))
    sel_k = jnp.pad(kv[:, :, 2], sel_pad)
    sel_v = jnp.pad(kv[:, :, 3], sel_pad)
    win = jnp.pad(kv[:, :, N_KV_SLOTS:], ((0, 0), (WINDOW, 0), (0, 0), (0, 0), (0, 0)))

    def query_block(i):
        start = i * Q_BLOCK
        win_blk = lax.dynamic_slice_in_dim(win, start, WINDOW + Q_BLOCK, axis=1)
        return nsa_core(lax.dynamic_slice_in_dim(q, start, Q_BLOCK, axis=1), start + jnp.arange(Q_BLOCK),
                        cmp_k, cmp_v, sel_k, sel_v, win_blk[:, :, 0], win_blk[:, :, 1],
                        start - WINDOW + jnp.arange(WINDOW + Q_BLOCK),
                        lax.dynamic_slice_in_dim(gates, start, Q_BLOCK, axis=1), rel_bias)

    o = lax.map(query_block, jnp.arange(t // Q_BLOCK))
    o = jnp.moveaxis(o, 0, 1).reshape(b, t, N_HEADS * HEAD_DIM)
    return o @ w_out, kv[:, :, :N_KV_SLOTS], kv[:, -min(WINDOW, t):, N_KV_SLOTS:]


def nsa_sample(h, cache_kv, layer, win_buf, page_table, w_in, w_out, cmp_pos, cmp_w1, cmp_w2, rel_bias):
    b, s, _ = h.shape
    q, kv, gates = nsa_project(h, w_in)
    p_len = page_table.shape[1] * cache_kv.shape[2]
    l = p_len + s

    def logical_rows(slot):
        past = cache_kv[layer, page_table, :, slot].reshape(b, p_len, N_KV_HEADS, HEAD_DIM)
        return jnp.concatenate([past, kv[:, :, slot].astype(past.dtype)], axis=1)

    cmp_k = compress(logical_rows(0), cmp_pos[0], cmp_w1[0], cmp_w2[0])
    cmp_v = compress(logical_rows(1), cmp_pos[1], cmp_w1[1], cmp_w2[1])
    sel_pad = ((0, 0), (0, (-l) % SEL_BLOCK), (0, 0), (0, 0))
    sel_k = jnp.pad(logical_rows(2), sel_pad)
    sel_v = jnp.pad(logical_rows(3), sel_pad)
    n_buf = win_buf.shape[1]
    win = jnp.concatenate([win_buf, kv[:, :, N_KV_SLOTS:].astype(win_buf.dtype)], axis=1)
    o = nsa_core(q, p_len + jnp.arange(s), cmp_k, cmp_v, sel_k, sel_v, win[:, :, 0], win[:, :, 1],
                 p_len - n_buf + jnp.arange(n_buf + s), gates, rel_bias)
    return o.reshape(b, s, N_HEADS * HEAD_DIM) @ w_out, kv[:, :, :N_KV_SLOTS], win[:, -n_buf:]


def ssd_scan(x, dt, a, b_in, c_in, h0):
    bsz, l = x.shape[0], x.shape[1]
    q = math.gcd(l, SSM_CHUNK)
    nc = l // q
    r = SSM_HEADS // SSM_GROUPS
    xr = x.reshape(bsz, nc, q, SSM_GROUPS, r, SSM_HEAD_DIM)
    dtr = dt.reshape(bsz, nc, q, SSM_GROUPS, r)
    br = b_in.astype(jnp.float32).reshape(bsz, nc, q, SSM_GROUPS, SSM_STATE)
    cr = c_in.astype(jnp.float32).reshape(bsz, nc, q, SSM_GROUPS, SSM_STATE)
    acum = jnp.cumsum(dtr * a.reshape(SSM_GROUPS, r), axis=2)
    causal = jnp.tril(jnp.ones((q, q), bool))[:, :, None, None]
    seg = acum[:, :, :, None] - acum[:, :, None, :]
    decay = jnp.exp(jnp.where(causal, seg, -jnp.inf))
    cb = jnp.einsum('bcign,bcjgn->bcijg', cr, br)
    w = cb[..., None] * decay * dtr[:, :, None]
    y_diag = jnp.einsum('bcijgr,bcjgrp->bcigrp', w, xr)
    to_end = jnp.exp(acum[:, :, -1:] - acum) * dtr
    states = jnp.einsum('bcjgn,bcjgr,bcjgrp->bcgrpn', br, to_end, xr)
    chunk_decay = jnp.exp(acum[:, :, -1])

    def step(h, inp):
        st, dec = inp
        return h * dec[..., None, None] + st, h

    h_fin, h_prev = lax.scan(step, h0.reshape(bsz, SSM_GROUPS, r, SSM_HEAD_DIM, SSM_STATE),
                             (jnp.moveaxis(states, 1, 0), jnp.moveaxis(chunk_decay, 1, 0)))
    h_prev = jnp.moveaxis(h_prev, 0, 1)
    y_off = jnp.einsum('bcign,bcgrpn,bcigr->bcigrp', cr, h_prev, jnp.exp(acum))
    y = (y_diag + y_off).reshape(bsz, l, SSM_HEADS, SSM_HEAD_DIM)
    return y, h_fin.reshape(bsz, SSM_HEADS, SSM_HEAD_DIM, SSM_STATE)


def mamba_mixer(h, conv_buf, ssm_state, w_in, conv_w, conv_b, dt_bias, a_log, d_skip, norm_g, w_out):
    b, l, _ = h.shape
    proj = h @ w_in
    z = proj[..., :SSM_D_INNER]
    xbc = proj[..., SSM_D_INNER:SSM_D_INNER + SSM_CONV_DIM]
    dt_raw = proj[..., SSM_D_INNER + SSM_CONV_DIM:]
    xpad = jnp.concatenate([conv_buf.astype(xbc.dtype), xbc], axis=1)
    conv = xpad[:, 0:l] * conv_w[0]
    for k in range(1, SSM_CONV):
        conv = conv + xpad[:, k:k + l] * conv_w[k]
    xbc = jax.nn.silu(conv + conv_b)
    n_bc = SSM_GROUPS * SSM_STATE
    xc = xbc[..., :SSM_D_INNER].reshape(b, l, SSM_HEADS, SSM_HEAD_DIM).astype(jnp.float32)
    bc = xbc[..., SSM_D_INNER:SSM_D_INNER + n_bc].reshape(b, l, SSM_GROUPS, SSM_STATE)
    cc = xbc[..., SSM_D_INNER + n_bc:].reshape(b, l, SSM_GROUPS, SSM_STATE)
    dt = jax.nn.softplus(dt_raw.astype(jnp.float32) + dt_bias.astype(jnp.float32))
    a = -jnp.exp(a_log.astype(jnp.float32))
    y, h_fin = ssd_scan(xc, dt, a, bc, cc, ssm_state.astype(jnp.float32))
    y = y + d_skip.astype(jnp.float32)[:, None] * xc
    y = y.reshape(b, l, SSM_D_INNER) * jax.nn.silu(z.astype(jnp.float32))
    yg = y.reshape(b, l, SSM_GROUPS, SSM_D_INNER // SSM_GROUPS)
    yg = yg * lax.rsqrt(jnp.mean(yg * yg, axis=-1, keepdims=True) + EPS)
    y = (yg.reshape(b, l, SSM_D_INNER) * norm_g.astype(jnp.float32)).astype(h.dtype)
    return y @ w_out, h_fin.astype(ssm_state.dtype), xpad[:, -(SSM_CONV - 1):]


def sq_relu_mlp(h, w1, w2):
    return jnp.square(jax.nn.relu(h @ w1)) @ w2


def setup_inputs(seed: int = 0) -> dict:
    key = jax.random.key(seed)
    k = list(jax.random.split(key, 32))

    def nrm(i, shape, scale):
        return jax.random.normal(k[i], shape, jnp.float32) * scale

    n_pages = PAST_LEN // PAGE_SIZE
    n_used = DEC_BATCH * n_pages
    n_pool = n_used + max(1, n_used // 4)
    n_buf = min(WINDOW, PAST_LEN)
    page_table = jax.random.permutation(k[0], n_pool)[:n_used].reshape(DEC_BATCH, n_pages).astype(jnp.int32)
    dt = jnp.exp(jax.random.uniform(k[1], (N_SSM_LAYERS, SSM_HEADS), jnp.float32, math.log(1e-3), math.log(1e-1)))
    return {
        'x_prompt': nrm(2, (BATCH, SEQ, D_MODEL), 1.0),
        'x_sample': nrm(3, (DEC_BATCH, DEC_SEQ, D_MODEL), 1.0),
        'cache_kv': nrm(4, (N_ATTN_LAYERS, n_pool, PAGE_SIZE, N_KV_SLOTS, N_KV_HEADS, HEAD_DIM), 1.0),
        'cache_win': nrm(5, (N_ATTN_LAYERS, DEC_BATCH, n_buf, N_WIN_SLOTS, N_KV_HEADS, HEAD_DIM), 1.0),
        'state_ssm': nrm(6, (N_SSM_LAYERS, DEC_BATCH, SSM_HEADS, SSM_HEAD_DIM, SSM_STATE), 0.5),
        'state_conv': nrm(7, (N_SSM_LAYERS, DEC_BATCH, SSM_CONV - 1, SSM_CONV_DIM), 1.0),
        'page_table': page_table,
        'c_prompt': nrm(8, (BATCH, D_MODEL), 1.0),
        'c_sample': nrm(9, (DEC_BATCH, D_MODEL), 1.0),
        'rel_bias': nrm(10, (N_BUCKETS, N_HEADS), 0.5),
        'ada_w': nrm(11, (DEPTH, D_MODEL, N_MOD * D_MODEL), 0.5 * D_MODEL ** -0.5),
        'ada_b': nrm(12, (DEPTH, N_MOD * D_MODEL), 0.1),
        'norm_g': 1.0 + nrm(13, (DEPTH, 4, D_MODEL), 0.1),
        'mlp_w1': nrm(14, (DEPTH, D_MODEL, D_FF), D_MODEL ** -0.5),
        'mlp_w2': nrm(15, (DEPTH, D_FF, D_MODEL), D_FF ** -0.5),
        'attn_w_in': nrm(16, (N_ATTN_LAYERS, D_MODEL, ATTN_IN), D_MODEL ** -0.5),
        'attn_w_out': nrm(17, (N_ATTN_LAYERS, N_HEADS * HEAD_DIM, D_MODEL), (N_HEADS * HEAD_DIM) ** -0.5),
        'cmp_pos': nrm(18, (N_ATTN_LAYERS, 2, CMP_BLOCK, HEAD_DIM), 0.1),
        'cmp_w1': nrm(19, (N_ATTN_LAYERS, 2, CMP_BLOCK * HEAD_DIM, CMP_HIDDEN), (CMP_BLOCK * HEAD_DIM) ** -0.5),
        'cmp_w2': nrm(20, (N_ATTN_LAYERS, 2, CMP_HIDDEN, HEAD_DIM), CMP_HIDDEN ** -0.5),
        'ssm_w_in': nrm(21, (N_SSM_LAYERS, D_MODEL, SSM_IN), D_MODEL ** -0.5),
        'ssm_conv_w': nrm(22, (N_SSM_LAYERS, SSM_CONV, SSM_CONV_DIM), SSM_CONV ** -0.5),
        'ssm_conv_b': nrm(23, (N_SSM_LAYERS, SSM_CONV_DIM), 0.1),
        'ssm_dt_bias': dt + jnp.log(-jnp.expm1(-dt)),
        'ssm_a_log': jnp.log(jax.random.uniform(k[24], (N_SSM_LAYERS, SSM_HEADS), jnp.float32, 1.0, 16.0)),
        'ssm_d': 1.0 + nrm(25, (N_SSM_LAYERS, SSM_HEADS), 0.1),
        'ssm_norm_g': 1.0 + nrm(26, (N_SSM_LAYERS, SSM_D_INNER), 0.1),
        'ssm_w_out': nrm(27, (N_SSM_LAYERS, SSM_D_INNER, D_MODEL), SSM_D_INNER ** -0.5),
    }


def reference(x_prompt, x_sample, cache_kv, cache_win, state_ssm, state_conv, page_table,
              c_prompt, c_sample, rel_bias, ada_w, ada_b, norm_g, mlp_w1, mlp_w2,
              attn_w_in, attn_w_out, cmp_pos, cmp_w1, cmp_w2,
              ssm_w_in, ssm_conv_w, ssm_conv_b, ssm_dt_bias, ssm_a_log, ssm_d, ssm_norm_g, ssm_w_out):
    xp, xs = x_prompt, x_sample
    kv_p, win_p, ssm_p, conv_p = [], [], [], []
    kv_s, win_s, ssm_s, conv_s = [], [], [], []
    for i in range(DEPTH):
        mp = modulation(c_prompt, ada_w[i], ada_b[i])
        ms = modulation(c_sample, ada_w[i], ada_b[i])
        hp = modulate(rmsnorm(xp, norm_g[i, 0]), mp[:, 0], mp[:, 1])
        hs = modulate(rmsnorm(xs, norm_g[i, 0]), ms[:, 0], ms[:, 1])
        if i % 2 == 0:
            a = i // 2
            yp, new_kv, new_win = nsa_prompt(hp, attn_w_in[a], attn_w_out[a], cmp_pos[a], cmp_w1[a], cmp_w2[a], rel_bias)
            kv_p.append(new_kv)
            win_p.append(new_win)
            ys, new_kv, new_win = nsa_sample(hs, cache_kv, a, cache_win[a], page_table, attn_w_in[a], attn_w_out[a],
                                             cmp_pos[a], cmp_w1[a], cmp_w2[a], rel_bias)
            kv_s.append(new_kv)
            win_s.append(new_win)
        else:
            m = i // 2
            ssm_args = (ssm_w_in[m], ssm_conv_w[m], ssm_conv_b[m], ssm_dt_bias[m], ssm_a_log[m],
                        ssm_d[m], ssm_norm_g[m], ssm_w_out[m])
            bp = xp.shape[0]
            yp, new_ssm, new_conv = mamba_mixer(hp, jnp.zeros((bp, SSM_CONV - 1, SSM_CONV_DIM), hp.dtype),
                                                jnp.zeros((bp, SSM_HEADS, SSM_HEAD_DIM, SSM_STATE), state_ssm.dtype),
                                                *ssm_args)
            ssm_p.append(new_ssm)
            conv_p.append(new_conv)
            ys, new_ssm, new_conv = mamba_mixer(hs, state_conv[m], state_ssm[m], *ssm_args)
            ssm_s.append(new_ssm)
            conv_s.append(new_conv)
        xp = xp + mp[:, 2] * rmsnorm(yp, norm_g[i, 1])
        xs = xs + ms[:, 2] * rmsnorm(ys, norm_g[i, 1])
        hp = modulate(rmsnorm(xp, norm_g[i, 2]), mp[:, 3], mp[:, 4])
        hs = modulate(rmsnorm(xs, norm_g[i, 2]), ms[:, 3], ms[:, 4])
        xp = xp + mp[:, 5] * rmsnorm(sq_relu_mlp(hp, mlp_w1[i], mlp_w2[i]), norm_g[i, 3])
        xs = xs + ms[:, 5] * rmsnorm(sq_relu_mlp(hs, mlp_w1[i], mlp_w2[i]), norm_g[i, 3])
    return (xp, xs, jnp.stack(kv_p), jnp.stack(win_p), jnp.stack(ssm_p), jnp.stack(conv_p),
            jnp.stack(kv_s), jnp.stack(win_s), jnp.stack(ssm_s), jnp.stack(conv_s))
```

```python
import functools
import math

import numpy as np
import jax
import jax.numpy as jnp
from jax import lax
from jax.experimental import pallas as pl
from jax.experimental.pallas import tpu as pltpu

F32 = jnp.float32
BF16 = jnp.bfloat16

D_MODEL = 1024
N_HEADS = 16
HEAD_DIM = 64
N_KV_HEADS = 4
GROUP = N_HEADS // N_KV_HEADS
CMP_BLOCK = 32
CMP_STRIDE = 16
CMP_RATIO = CMP_BLOCK // CMP_STRIDE
CMP_HIDDEN = 2 * HEAD_DIM
SEL_BLOCK = 64
SEL_RATIO = SEL_BLOCK // CMP_STRIDE
N_SELECT = 16
WINDOW = 512
FORCE_BONUS = 1e4
Q_BLOCK = 128
N_KV_SLOTS = 4
N_WIN_SLOTS = 2
N_Q_COLS = N_HEADS * HEAD_DIM
N_KV_COLS = (N_KV_SLOTS + N_WIN_SLOTS) * N_KV_HEADS * HEAD_DIM
N_GATE_COLS = 3 * N_HEADS
KV_ROW = N_KV_HEADS * HEAD_DIM
N_BUCKETS = 32
MAX_DISTANCE = 128
SSM_D_INNER = 2 * D_MODEL
SSM_HEAD_DIM = 64
SSM_HEADS = SSM_D_INNER // SSM_HEAD_DIM
SSM_GROUPS = 8
SSM_STATE = 128
SSM_CONV = 4
SSM_CONV_DIM = SSM_D_INNER + 2 * SSM_GROUPS * SSM_STATE
SSM_CHUNK = 128
D_FF = 4 * D_MODEL
N_MOD = 6
EPS = 1e-6

LANES = 128
NEG = -1e30
VMEM_LIMIT = 56 * 1024 * 1024
NEAR = 2 * LANES

_NT = (((1,), (1,)), ((), ()))


def _cparams(sem):
    return pltpu.CompilerParams(dimension_semantics=sem, vmem_limit_bytes=VMEM_LIMIT)


def _rms(x, g):
    return x * lax.rsqrt(jnp.mean(x * x, axis=-1, keepdims=True) + EPS) * g


def _silu(x):
    return x * jax.nn.sigmoid(x)


def _split_bf16(x, n):
    parts = []
    for _ in range(n - 1):
        p = x.astype(BF16)
        parts.append(p)
        x = x - p.astype(F32)
    parts.append(x.astype(BF16))
    return parts


def _ada_kernel(c_ref, w_ref, b_ref, o_ref):
    s = _silu(c_ref[...]).astype(BF16)
    o_ref[...] = jnp.dot(s, w_ref[...].astype(BF16), preferred_element_type=F32) + b_ref[...]


def _ada(c, w, b, tn=1024):
    m, d = c.shape
    n = w.shape[1]
    return pl.pallas_call(
        _ada_kernel,
        grid=(n // tn,),
        in_specs=[pl.BlockSpec((m, d), lambda j: (0, 0)),
                  pl.BlockSpec((d, tn), lambda j: (0, j)),
                  pl.BlockSpec((1, tn), lambda j: (0, j))],
        out_specs=pl.BlockSpec((m, tn), lambda j: (0, j)),
        out_shape=jax.ShapeDtypeStruct((m, n), F32),
        compiler_params=_cparams(("arbitrary",)),
    )(c, w, b.reshape(1, n))


def _nm_matmul_kernel(x_ref, g_ref, sh_ref, sc_ref, w_ref, o_ref, h_ref):
    @pl.when(pl.program_id(1) == 0)
    def _():
        h = _rms(x_ref[...], g_ref[...]) * (1.0 + sc_ref[0]) + sh_ref[0]
        h_ref[...] = h.astype(BF16)

    o_ref[...] = jnp.dot(h_ref[...], w_ref[...], preferred_element_type=F32)


def _nm_matmul(x, g, shift, scale, w, tm, tn):
    m, d = x.shape
    n = w.shape[1]
    nb = shift.shape[0]
    tpb = m // nb // tm
    mod_spec = pl.BlockSpec((1,) + shift.shape[1:], lambda i, j: (i // tpb, 0, 0))
    return pl.pallas_call(
        _nm_matmul_kernel,
        grid=(m // tm, n // tn),
        in_specs=[pl.BlockSpec((tm, d), lambda i, j: (i, 0)),
                  pl.BlockSpec((1, d), lambda i, j: (0, 0)),
                  mod_spec, mod_spec,
                  pl.BlockSpec((d, tn), lambda i, j: (0, j))],
        out_specs=pl.BlockSpec((tm, tn), lambda i, j: (i, j)),
        out_shape=jax.ShapeDtypeStruct((m, n), F32),
        scratch_shapes=[pltpu.VMEM((tm, d), BF16)],
        compiler_params=_cparams(("arbitrary", "arbitrary")),
    )(x, g.reshape(1, d), shift, scale, w)


def _mm_norm_res_kernel(a_ref, w_ref, g_ref, gate_ref, x_ref, o_ref):
    y = jnp.dot(a_ref[...].astype(BF16), w_ref[...], preferred_element_type=F32)
    o_ref[...] = x_ref[...] + gate_ref[0] * _rms(y, g_ref[...])


def _mm_norm_res(a, w, g, gate, x, tm):
    m, k = a.shape
    d = w.shape[1]
    nb = gate.shape[0]
    tpb = m // nb // tm
    return pl.pallas_call(
        _mm_norm_res_kernel,
        grid=(m // tm,),
        in_specs=[pl.BlockSpec((tm, k), lambda i: (i, 0)),
                  pl.BlockSpec((k, d), lambda i: (0, 0)),
                  pl.BlockSpec((1, d), lambda i: (0, 0)),
                  pl.BlockSpec((1,) + gate.shape[1:], lambda i: (i // tpb, 0, 0)),
                  pl.BlockSpec((tm, d), lambda i: (i, 0))],
        out_specs=pl.BlockSpec((tm, d), lambda i: (i, 0)),
        out_shape=jax.ShapeDtypeStruct((m, d), F32),
        compiler_params=_cparams(("arbitrary",)),
    )(a, w, g.reshape(1, d), gate, x)


def _mlp_kernel(x_ref, g1_ref, sh_ref, sc_ref, w1_ref, w2_ref, g2_ref, gate_ref, o_ref, h_ref, acc_ref):
    c = pl.program_id(1)

    @pl.when(c == 0)
    def _():
        h = _rms(x_ref[...], g1_ref[...]) * (1.0 + sc_ref[0]) + sh_ref[0]
        h_ref[...] = h.astype(BF16)
        acc_ref[...] = jnp.zeros_like(acc_ref)

    a = jnp.dot(h_ref[...], w1_ref[...], preferred_element_type=F32)
    a = jnp.square(jnp.maximum(a, 0.0)).astype(BF16)
    acc_ref[...] += jnp.dot(a, w2_ref[...], preferred_element_type=F32)

    @pl.when(c == pl.num_programs(1) - 1)
    def _():
        o_ref[...] = x_ref[...] + gate_ref[0] * _rms(acc_ref[...], g2_ref[...])


def _mlp(x, g1, shift, scale, w1, w2, g2, gate, tm, tf):
    m, d = x.shape
    f = w1.shape[1]
    nb = shift.shape[0]
    tpb = m // nb // tm
    mod_spec = pl.BlockSpec((1,) + shift.shape[1:], lambda i, c: (i // tpb, 0, 0))
    vec_spec = pl.BlockSpec((1, d), lambda i, c: (0, 0))
    return pl.pallas_call(
        _mlp_kernel,
        grid=(m // tm, f // tf),
        in_specs=[pl.BlockSpec((tm, d), lambda i, c: (i, 0)), vec_spec, mod_spec, mod_spec,
                  pl.BlockSpec((d, tf), lambda i, c: (0, c)),
                  pl.BlockSpec((tf, d), lambda i, c: (c, 0)),
                  vec_spec, mod_spec],
        out_specs=pl.BlockSpec((tm, d), lambda i, c: (i, 0)),
        out_shape=jax.ShapeDtypeStruct((m, d), F32),
        scratch_shapes=[pltpu.VMEM((tm, d), BF16), pltpu.VMEM((tm, d), F32)],
        compiler_params=_cparams(("arbitrary", "arbitrary")),
    )(x, g1.reshape(1, d), shift, scale, w1, w2, g2.reshape(1, d), gate)


def _bias_kernel(oh_ref, t_ref, o_ref):
    t = t_ref[...]
    t = t - t[N_BUCKETS - 1:N_BUCKETS, :]
    oh = oh_ref[...]
    o_ref[...] = sum(jnp.dot(oh, p, preferred_element_type=F32) for p in _split_bf16(t, 3))


def _rel_bucket_np(dist):
    n = np.maximum(dist, 0)
    exact = N_BUCKETS // 2
    nf = np.maximum(n, 1).astype(np.float32)
    large = exact + (np.log(nf / exact) / math.log(MAX_DISTANCE / exact) * (N_BUCKETS - exact)).astype(np.int32)
    return np.where(n < exact, n, np.minimum(large, N_BUCKETS - 1))


def _bias_vector(rel_bias):
    assert _rel_bucket_np(np.array([NEAR - 1]))[0] == N_BUCKETS - 1 == _rel_bucket_np(np.array([MAX_DISTANCE]))[0]
    oh = np.zeros((NEAR + 8, LANES), np.float32)
    oh[np.arange(NEAR), _rel_bucket_np(np.arange(NEAR))] = 1.0
    oh[NEAR:, N_BUCKETS - 1] = 1.0
    table = jnp.zeros((LANES, LANES), F32).at[:N_BUCKETS, :N_HEADS].set(rel_bias.astype(F32))
    bv = pl.pallas_call(
        _bias_kernel,
        out_shape=jax.ShapeDtypeStruct((NEAR + 8, LANES), F32),
    )(jnp.asarray(oh, BF16), table)
    return bv[:NEAR + 1, :N_HEADS]


def _toeplitz(bvz, idx):
    r, c = idx.shape
    t = jnp.take(bvz, jnp.asarray(idx.reshape(-1), jnp.int32), axis=0).reshape(r, c, N_KV_HEADS, GROUP)
    return jnp.transpose(t, (2, 3, 0, 1)).reshape(N_KV_HEADS, GROUP * r, c)


SUB = 8
HEADS_PER_CG = LANES // HEAD_DIM
N_CG = 2 * KV_ROW // LANES
CG_PER_SLOT = KV_ROW // LANES
CG_HIDDEN = HEADS_PER_CG * CMP_HIDDEN


def _compress_parts(rows_s, wbd_ref, nh):
    part0 = jnp.zeros((nh, CG_HIDDEN), F32)
    part1 = jnp.zeros((nh, CG_HIDDEN), F32)
    for s in range(CMP_STRIDE):
        xs = rows_s(s).astype(BF16)
        part0 = part0 + jnp.dot(xs, wbd_ref[0, 0, s], preferred_element_type=F32)
        part1 = part1 + jnp.dot(xs, wbd_ref[0, 1, s], preferred_element_type=F32)
    return part0, part1


def _compress_finish(part0, part1, pos_ref, w1_ref, w2bd_ref, nh):
    pre0 = jnp.dot(pos_ref[0].astype(BF16), w1_ref[0].astype(BF16), preferred_element_type=F32)[0:1]
    pre = jnp.concatenate([pre0] * HEADS_PER_CG, axis=1) + part0 + pltpu.roll(part1, nh - 1, axis=0)
    return jnp.dot(_silu(pre).astype(BF16), w2bd_ref[0], preferred_element_type=F32)


def _compress_kernel(x_ref, pos_ref, w1_ref, wbd_ref, w2bd_ref, o_ref, *, nh):
    part0, part1 = _compress_parts(lambda s: x_ref[pl.ds(s, nh, stride=CMP_STRIDE), :], wbd_ref, nh)
    o_ref[0, 0] = _compress_finish(part0, part1, pos_ref, w1_ref, w2bd_ref, nh)


def _compress_weights(cmp_pos, cmp_w1, cmp_w2):
    eye = jnp.eye(HEADS_PER_CG, dtype=F32)
    w1r = cmp_w1.reshape(2, CMP_RATIO, CMP_STRIDE, HEAD_DIM, CMP_HIDDEN)
    wbd = jnp.einsum('kl,artdh->artkdlh', eye, w1r).reshape(2, CMP_RATIO, CMP_STRIDE, LANES, CG_HIDDEN).astype(BF16)
    w2bd = jnp.einsum('kl,ahd->akhld', eye, cmp_w2).reshape(2, CG_HIDDEN, LANES).astype(BF16)
    pos = jnp.zeros((2, SUB, CMP_BLOCK * HEAD_DIM), F32).at[:, 0].set(cmp_pos.reshape(2, -1))
    return pos, cmp_w1, wbd, w2bd


def _compress_specs(cw, imap):
    def spec(a):
        return pl.BlockSpec((1,) + a.shape[1:], lambda *g: (imap(*g) // CG_PER_SLOT,) + (0,) * (a.ndim - 1))
    return [spec(a) for a in cw]


def _pack_cmp(cmp):
    nb, _, n, _ = cmp.shape
    c = cmp.reshape(nb, 2, CG_PER_SLOT, n, HEADS_PER_CG, HEAD_DIM)
    return jnp.transpose(c, (0, 2, 4, 3, 1, 5)).reshape(nb, N_KV_HEADS, n, 2 * HEAD_DIM).astype(BF16)


def _compress_prompt(proj, cw, nb, t):
    nh = t // CMP_STRIDE
    col0 = N_Q_COLS // LANES
    return pl.pallas_call(
        functools.partial(_compress_kernel, nh=nh),
        grid=(nb, N_CG),
        in_specs=[pl.BlockSpec((t, LANES), lambda b, c: (b, col0 + c))] + _compress_specs(cw, lambda b, c: c),
        out_specs=pl.BlockSpec((1, 1, nh, LANES), lambda b, c: (b, c, 0, 0)),
        out_shape=jax.ShapeDtypeStruct((nb, N_CG, nh, LANES), F32),
        compiler_params=_cparams(("arbitrary", "arbitrary")),
    )(proj, *cw)


def _online_softmax_step(s, mask, kv, m, l, acc):
    sm = jnp.where(mask, s, NEG)
    m_new = jnp.maximum(m, jnp.max(sm, axis=-1, keepdims=True))
    alpha = jnp.exp(m - m_new)
    e = jnp.where(mask, jnp.exp(s - m_new), 0.0)
    l = alpha * l + jnp.sum(e, axis=-1, keepdims=True)
    acc = alpha * acc + jnp.dot(e.astype(BF16), kv, preferred_element_type=F32)
    return m_new, l, acc


def _nsa_consts(t):
    ncp = t // CMP_STRIDE
    nblk = t // SEL_BLOCK
    nbp = -(-nblk // LANES) * LANES
    sq = np.zeros((GROUP, GROUP * HEAD_DIM, LANES), np.float32)
    ro = np.zeros((GROUP, LANES, GROUP * HEAD_DIM), np.float32)
    for g in range(GROUP):
        sq[g, g * HEAD_DIM + np.arange(HEAD_DIM), np.arange(HEAD_DIM)] = 1.0
        ro[g, HEAD_DIM + np.arange(HEAD_DIM), g * HEAD_DIM + np.arange(HEAD_DIM)] = 1.0
    at = np.zeros((nbp, ncp), np.float32)
    for j in range(nblk):
        for n in range(SEL_RATIO * j - (CMP_RATIO - 1), SEL_RATIO * (j + 1)):
            if 0 <= n < ncp - CMP_RATIO + 1:
                at[j, n] = 1.0
    key_blk = np.arange(t) // SEL_BLOCK
    e = (np.arange(nbp)[:, None] == key_blk[None, :]).astype(np.float32)
    e1 = e.reshape(nbp, t // LANES, LANES).transpose(1, 0, 2)
    big = (np.arange(2 * ncp + 16)[:, None] == np.arange(LANES)[None, :] + ncp).astype(np.float32)
    ge = np.zeros((N_KV_HEADS, 3, LANES, GROUP * HEAD_DIM), np.float32)
    for k in range(N_KV_HEADS):
        for j in range(3):
            for g in range(GROUP):
                ge[k, j, (k * GROUP + g) * 3 + j, g * HEAD_DIM:(g + 1) * HEAD_DIM] = 1.0
    r = np.arange(Q_BLOCK)[:, None]
    c = np.arange(LANES)[None, :]
    d0 = r - c
    idx0 = np.where(d0 >= 0, d0, NEAR)
    idx1 = Q_BLOCK + r - c
    dc = r - CMP_STRIDE * c + (2 * Q_BLOCK - CMP_BLOCK + 1)
    idxc = np.where((dc >= 0) & (dc < NEAR), dc, NEAR)
    bf = lambda a: jnp.asarray(a, BF16)
    return dict(sq=bf(sq), ro=bf(ro), at=bf(at), e1=bf(e1), big=jnp.asarray(big), ge=bf(ge),
                idx0=idx0, idx1=idx1, idxc=idxc, ncp=ncp, nbp=nbp, nblk=nblk)


def _nsa_kernel(q_ref, gl_ref, kvc_ref, kvs_ref, kvw_ref, sq_ref, at_ref, e1_ref, tb_ref, cb_ref, big_ref,
                ge_ref, ro_ref, o_ref, sc_ref, *, ncp, nbp, n_sel):
    i = pl.program_id(2)
    qb = (q_ref[...] * (HEAD_DIM ** -0.5)).astype(BF16)
    qs = jnp.concatenate([jnp.dot(qb, sq_ref[g], preferred_element_type=F32) for g in range(GROUP)],
                         axis=0).astype(BF16)
    rows = GROUP * Q_BLOCK
    r_col = lax.broadcasted_iota(jnp.int32, (rows, 1), 0) % Q_BLOCK
    q_pos = i * Q_BLOCK + r_col
    lane = lax.broadcasted_iota(jnp.int32, (1, LANES), 1)

    kc = kvc_ref[0, 0]
    s = lax.dot_general(qs, kc, _NT, preferred_element_type=F32)
    sh = big_ref[pl.ds(pl.multiple_of(ncp + 2 * CMP_STRIDE // CMP_RATIO - 8 * i, 8), ncp), :].astype(BF16)
    for p in range(3):
        s = s + lax.dot_general(cb_ref[0, p], sh, _NT, preferred_element_type=F32)
    cmp_end = lax.broadcasted_iota(jnp.int32, (1, ncp), 1) * CMP_STRIDE + (CMP_BLOCK - 1)
    mask = cmp_end <= q_pos
    m = jnp.max(jnp.where(mask, s, NEG), axis=-1, keepdims=True)
    e = jnp.where(mask, jnp.exp(s - m), 0.0)
    p_cmp = e / jnp.maximum(jnp.sum(e, axis=-1, keepdims=True), 1e-30)
    o_cmp = jnp.dot(p_cmp.astype(BF16), kc, preferred_element_type=F32)
    imp = p_cmp[0:Q_BLOCK]
    for g in range(1, GROUP):
        imp = imp + p_cmp[g * Q_BLOCK:(g + 1) * Q_BLOCK]
    bst = sum(lax.dot_general(at_ref[...], part, _NT, preferred_element_type=F32)
              for part in _split_bf16(imp, 3))

    jb = lax.broadcasted_iota(jnp.int32, (nbp, Q_BLOCK), 0)
    q_blk = (i * Q_BLOCK + lax.broadcasted_iota(jnp.int32, (nbp, Q_BLOCK), 1)) // SEL_BLOCK
    causal = jb <= q_blk
    forced = (jb == 0) | (jb == q_blk) | (jb == q_blk - 1)
    score = jnp.where(causal, bst + jnp.where(forced, FORCE_BONUS, 0.0), -jnp.inf)
    sc_ref[...] = score

    def rank_body(jp, cnt):
        row = sc_ref[pl.ds(jp, 1), :]
        ahead = (row > score) | ((row == score) & (jp < jb))
        return cnt + jnp.where(ahead, 1.0, 0.0)

    n_causal = (i * Q_BLOCK + Q_BLOCK - 1) // SEL_BLOCK + 1
    cnt = lax.fori_loop(0, n_causal, rank_body, jnp.zeros((nbp, Q_BLOCK), F32))
    sel_t = jnp.where((cnt < n_sel) & causal, 1.0, 0.0)
    sel = sel_t.T.astype(BF16)

    init = (jnp.full((rows, 1), NEG, F32), jnp.zeros((rows, 1), F32), jnp.zeros((rows, LANES), F32))

    carry = init
    n_wt = WINDOW // Q_BLOCK
    for u in range(n_wt + 1):
        tt = i - n_wt + u
        kv = kvw_ref[0, 0, pl.ds(pl.multiple_of(jnp.maximum(tt, 0) * Q_BLOCK, Q_BLOCK), Q_BLOCK), :]
        s = lax.dot_general(qs, kv, _NT, preferred_element_type=F32)
        mask = (lane * 0 + tt) >= 0
        if u == 0:
            mask = mask & (lane > r_col)
        if u == n_wt:
            mask = mask & (lane <= r_col)
            s = s + tb_ref[0, 0]
        if u == n_wt - 1:
            s = s + tb_ref[0, 1]
        carry = _online_softmax_step(s, mask, kv, *carry)
    o_win = carry[2] / jnp.maximum(carry[1], 1e-30)

    def sel_body(tt, carry):
        kv = kvs_ref[0, 0, pl.ds(pl.multiple_of(tt * Q_BLOCK, Q_BLOCK), Q_BLOCK), :]
        s = lax.dot_general(qs, kv, _NT, preferred_element_type=F32)
        s = s + tb_ref[0, jnp.clip(i - tt, 0, 2)]
        mk = jnp.dot(sel, e1_ref[tt], preferred_element_type=F32)
        mk = jnp.concatenate([mk] * GROUP, axis=0)
        mask = (mk > 0.5) & (tt * Q_BLOCK + lane <= q_pos)
        return _online_softmax_step(s, mask, kv, *carry)

    carry = lax.fori_loop(0, i + 1, sel_body, init)
    o_sel = carry[2] / jnp.maximum(carry[1], 1e-30)

    sg = _split_bf16(jax.nn.sigmoid(gl_ref[...]), 2)
    out = jnp.zeros((Q_BLOCK, GROUP * HEAD_DIM), F32)
    for j, ob in enumerate((o_cmp, o_sel, o_win)):
        gate = sum(jnp.dot(part, ge_ref[0, j], preferred_element_type=F32) for part in sg)
        placed = jnp.zeros((Q_BLOCK, GROUP * HEAD_DIM), F32)
        for g in range(GROUP):
            for part in _split_bf16(ob[g * Q_BLOCK:(g + 1) * Q_BLOCK], 2):
                placed = placed + jnp.dot(part, ro_ref[g], preferred_element_type=F32)
        out = out + gate * placed
    o_ref[...] = out


def _nsa_prompt(proj, kvc, kvs, kvw, tb, cb, consts, nb, t):
    ni = t // Q_BLOCK
    ncp, nbp = consts['ncp'], consts['nbp']
    gcol = (N_Q_COLS + N_KV_COLS) // LANES
    qw = GROUP * HEAD_DIM
    full = lambda a: pl.BlockSpec(a.shape, lambda b, k, i: (0,) * a.ndim)
    per_kvh = lambda a: pl.BlockSpec((1,) + a.shape[1:], lambda b, k, i: (k,) + (0,) * (a.ndim - 1))
    per_bk = lambda a: pl.BlockSpec((1, 1) + a.shape[2:], lambda b, k, i: (b, k, 0, 0))
    c = consts
    return pl.pallas_call(
        functools.partial(_nsa_kernel, ncp=ncp, nbp=nbp, n_sel=min(N_SELECT, c['nblk'])),
        grid=(nb, N_KV_HEADS, ni),
        in_specs=[pl.BlockSpec((Q_BLOCK, qw), lambda b, k, i: (b * ni + i, k)),
                  pl.BlockSpec((Q_BLOCK, LANES), lambda b, k, i: (b * ni + i, gcol)),
                  per_bk(kvc), per_bk(kvs), per_bk(kvw),
                  full(c['sq']), full(c['at']), full(c['e1']), per_kvh(tb), per_kvh(cb), full(c['big']),
                  per_kvh(c['ge']), full(c['ro'])],
        out_specs=pl.BlockSpec((Q_BLOCK, qw), lambda b, k, i: (b * ni + i, k)),
        out_shape=jax.ShapeDtypeStruct((nb * t, N_Q_COLS), F32),
        scratch_shapes=[pltpu.VMEM((nbp, Q_BLOCK), F32)],
        compiler_params=_cparams(("arbitrary", "arbitrary", "arbitrary")),
    )(proj, proj, kvc, kvs, kvw, c['sq'], c['at'], c['e1'], tb, cb, c['big'], c['ge'], c['ro'])


NP_ATTN = -(-(N_Q_COLS + N_KV_COLS + N_GATE_COLS) // LANES) * LANES


def _row_tile(m, cap):
    return m if m <= cap else cap


def _pack_kv(kv6, s0):
    pair = jnp.transpose(kv6[:, :, s0:s0 + 2], (0, 3, 1, 2, 4))
    return pair.reshape(pair.shape[0], N_KV_HEADS, pair.shape[2], 2 * HEAD_DIM).astype(BF16)


def _nsa_prompt_mixer(xp, g, shift, scale, w_in, cw, bvz, nb, t):
    proj = _nm_matmul(xp, g, shift, scale, w_in, _row_tile(t, 512), NP_ATTN)
    kv6 = proj[:, N_Q_COLS:N_Q_COLS + N_KV_COLS].reshape(nb, t, N_KV_SLOTS + N_WIN_SLOTS, N_KV_HEADS, HEAD_DIM)
    kvc = _pack_cmp(_compress_prompt(proj, cw, nb, t))
    kvs = _pack_kv(kv6, 2)
    kvw = _pack_kv(kv6, N_KV_SLOTS)
    c = _nsa_consts(t)
    zeros = jnp.zeros((N_KV_HEADS, GROUP * Q_BLOCK, LANES), F32)
    tb = jnp.stack([_toeplitz(bvz, c['idx0']), _toeplitz(bvz, c['idx1']), zeros], axis=1)
    cbf = _toeplitz(bvz, c['idxc'])
    cb = jnp.stack(_split_bf16(cbf, 3), axis=1)
    o = _nsa_prompt(proj, kvc, kvs, kvw, tb, cb, c, nb, t)
    return o, kv6


SC_PAGES = 32


def _compress_sample_kernel(pt_ref, cache_ref, pos_ref, w1_ref, wbd_ref, w2bd_ref, o_ref, buf_ref, sem_ref,
                            p0_ref, p1_ref, *, n_pages, page, row_base):
    step = pl.program_id(0) * N_CG + pl.program_id(1)
    nsteps = pl.num_programs(0) * N_CG
    chunk_pages = min(SC_PAGES, n_pages // 2)
    nch = n_pages // chunk_pages
    hpc = chunk_pages * page // CMP_STRIDE
    nh = nch * hpc

    def copies(step_, ch, slot):
        b_, c_ = step_ // N_CG, step_ % N_CG
        out = []
        for p in range(chunk_pages):
            pg = pt_ref[b_, ch * chunk_pages + p]
            out.append(pltpu.make_async_copy(
                cache_ref.at[pl.ds((row_base + pg) * page, page), pl.ds(c_ * LANES, LANES)],
                buf_ref.at[slot, pl.ds(p * page, page), :], sem_ref.at[slot]))
        return out

    @pl.when(step == 0)
    def _():
        for cp in copies(step, 0, 0):
            cp.start()

    for ch in range(nch):
        slot = ch % 2
        if ch + 1 < nch:
            for cp in copies(step, ch + 1, 1 - slot):
                cp.start()
        else:
            @pl.when(step + 1 < nsteps)
            def _():
                for cp in copies(step + 1, 0, 1 - slot):
                    cp.start()
        for cp in copies(step, ch, slot):
            cp.wait()
        sbuf = buf_ref.at[slot]
        p0, p1 = _compress_parts(lambda s: sbuf[pl.ds(s, hpc, stride=CMP_STRIDE), :], wbd_ref, hpc)
        p0_ref[ch * hpc:(ch + 1) * hpc] = p0
        p1_ref[ch * hpc:(ch + 1) * hpc] = p1

    o_ref[0, 0] = _compress_finish(p0_ref[...], p1_ref[...], pos_ref, w1_ref, w2bd_ref, nh)


def _compress_sample(page_table, cache2d, cw, page, row_base):
    nb, n_pages = page_table.shape
    chunk_pages = min(SC_PAGES, n_pages // 2)
    assert n_pages % (2 * chunk_pages) == 0
    nh = n_pages * page // CMP_STRIDE
    grid_spec = pltpu.PrefetchScalarGridSpec(
        num_scalar_prefetch=1,
        grid=(nb, N_CG),
        in_specs=[pl.BlockSpec(memory_space=pl.ANY)] + _compress_specs(cw, lambda b, c, pt: c),
        out_specs=pl.BlockSpec((1, 1, nh, LANES), lambda b, c, pt: (b, c, 0, 0)),
        scratch_shapes=[pltpu.VMEM((2, chunk_pages * page, LANES), F32),
                        pltpu.SemaphoreType.DMA((2,)),
                        pltpu.VMEM((nh, CG_HIDDEN), F32),
                        pltpu.VMEM((nh, CG_HIDDEN), F32)])
    return pl.pallas_call(
        functools.partial(_compress_sample_kernel, n_pages=n_pages, page=page, row_base=row_base),
        grid_spec=grid_spec,
        out_shape=jax.ShapeDtypeStruct((nb, N_CG, nh, LANES), F32),
        compiler_params=_cparams(("arbitrary", "arbitrary")),
    )(page_table, cache2d, *cw)


def _softmax_with_new_key(s, ok, s_new):
    m = jnp.maximum(jnp.max(jnp.where(ok, s, NEG), axis=-1, keepdims=True), s_new)
    e = jnp.where(ok, jnp.exp(s - m), 0.0)
    e_new = jnp.exp(s_new - m)
    inv = 1.0 / jnp.maximum(jnp.sum(e, axis=-1, keepdims=True) + e_new, 1e-30)
    return e * inv, e_new * inv


def _nsa_sample_kernel(pt_ref, qs_ref, qw_ref, gl_ref, nsel_ref, nwin_ref, kvc_ref, cache_ref, win_ref, cbs_ref,
                       sbz_ref, wb_ref, b0_ref, as_ref, ro_ref, o_ref, wout_ref, gbuf_ref, sem_ref, idx_ref,
                       *, past, page, row_base, n_pick):
    b = pl.program_id(0)
    blk_per_page = page // SEL_BLOCK
    nbs = past // SEL_BLOCK
    ncs = kvc_ref.shape[2]
    width = 2 * KV_ROW

    @pl.when(b == 0)
    def _():
        gbuf_ref[...] = jnp.zeros_like(gbuf_ref)

    row = lax.broadcasted_iota(jnp.int32, (SUB, 1), 0)

    cmp_ok = lax.broadcasted_iota(jnp.int32, (SUB, ncs), 1) * CMP_STRIDE + (CMP_BLOCK - 1) <= past
    imp = jnp.zeros((SUB, ncs), F32)
    o_cmp = []
    for k in range(N_KV_HEADS):
        kc = kvc_ref[0, k]
        s = lax.dot_general(qs_ref[0, k], kc, _NT, preferred_element_type=F32) + cbs_ref[k]
        m = jnp.max(jnp.where(cmp_ok, s, NEG), axis=-1, keepdims=True)
        e = jnp.where(cmp_ok, jnp.exp(s - m), 0.0)
        p = e / jnp.maximum(jnp.sum(e, axis=-1, keepdims=True), 1e-30)
        o_cmp.append(jnp.dot(p.astype(BF16), kc, preferred_element_type=F32))
        imp_k = p[0:1]
        for g in range(1, GROUP):
            imp_k = imp_k + p[g:g + 1]
        imp = imp + jnp.where(row == k, imp_k, 0.0)
    bs = sum(jnp.dot(part, as_ref[...], preferred_element_type=F32) for part in _split_bf16(imp, 3))

    lane_b = lax.broadcasted_iota(jnp.int32, (SUB, nbs), 1)
    lane_f = lane_b.astype(F32)
    score = bs + jnp.where((lane_b == 0) | (lane_b == nbs - 1), FORCE_BONUS, 0.0)

    def gather(k, r):
        j = idx_ref[k * n_pick + r]
        pg = pt_ref[b, j // blk_per_page]
        row0 = (row_base + pg) * page + (j % blk_per_page) * SEL_BLOCK
        return pltpu.make_async_copy(cache_ref.at[pl.ds(row0, SEL_BLOCK), pl.ds(width, width)],
                                     gbuf_ref.at[k * n_pick + r, pl.ds(0, SEL_BLOCK), :], sem_ref.at[0])

    for r in range(n_pick):
        m = jnp.max(score, axis=-1, keepdims=True)
        pick = jnp.min(jnp.where(score == m, lane_f, 1e9), axis=-1, keepdims=True)
        score = jnp.where(lane_f == pick, -jnp.inf, score)
        for k in range(N_KV_HEADS):
            idx_ref[k * n_pick + r] = jnp.sum(jnp.where(row == k, pick, 0.0)).astype(jnp.int32)
            gather(k, r).start()

    wout_ref[0, pl.ds(0, WINDOW - 1), :] = win_ref[0, pl.ds(1, WINDOW - 1), :]
    wout_ref[0, pl.ds(WINDOW - 1, 1), :] = nwin_ref[0]
    wkb = win_ref[0].astype(BF16)
    nw = nwin_ref[0].astype(BF16).astype(F32)
    ns = nsel_ref[0].astype(BF16).astype(F32)
    win_ok = lax.broadcasted_iota(jnp.int32, (SUB, WINDOW), 1) >= 1
    o_win = []
    for k in range(N_KV_HEADS):
        qw = qw_ref[0, k]
        s = lax.dot_general(qw, wkb, _NT, preferred_element_type=F32) + wb_ref[k]
        s_new = jnp.sum(qw.astype(F32) * nw, axis=-1, keepdims=True) + b0_ref[k][:, 0:1]
        p, p_new = _softmax_with_new_key(s, win_ok, s_new)
        o_win.append(jnp.dot(p.astype(BF16), wkb, preferred_element_type=F32) + p_new * nw)

    for k in range(N_KV_HEADS):
        for r in range(n_pick):
            gather(k, r).wait()

    sel_ok = lax.broadcasted_iota(jnp.int32, (SUB, n_pick * LANES), 1) % LANES < SEL_BLOCK
    for k in range(N_KV_HEADS):
        qw = qw_ref[0, k]
        tiles = []
        for r in range(n_pick):
            near = idx_ref[k * n_pick + r] - (nbs - NEAR // SEL_BLOCK)
            tiles.append(lax.dot_general(qw, gbuf_ref[k * n_pick + r].astype(BF16), _NT, preferred_element_type=F32)
                         + sbz_ref[k, jnp.where(near >= 0, near, NEAR // SEL_BLOCK)])
        s = jnp.concatenate(tiles, axis=1)
        s_new = jnp.sum(qw.astype(F32) * ns, axis=-1, keepdims=True) + b0_ref[k][:, 0:1]
        p, p_new = _softmax_with_new_key(s, sel_ok, s_new)
        o_sel = p_new * ns
        for r in range(n_pick):
            o_sel = o_sel + jnp.dot(p[:, r * LANES:(r + 1) * LANES].astype(BF16),
                                    gbuf_ref[k * n_pick + r].astype(BF16), preferred_element_type=F32)
        g_cmp, g_sel, g_win = (jax.nn.sigmoid(gl_ref[0, k, j]) for j in range(3))
        out = jnp.concatenate([g_sel] * (width // LANES), axis=1) * o_sel
        out = out + jnp.concatenate([g_win] * (width // LANES), axis=1) * o_win[k]
        for part in _split_bf16(g_cmp * o_cmp[k], 2):
            out = out + jnp.dot(part, ro_ref[k], preferred_element_type=F32)
        o_ref[0, k] = out


def _bias_rows(bvz, idx):
    t = jnp.take(bvz, jnp.asarray(idx, jnp.int32), axis=0).reshape(len(idx), N_KV_HEADS, GROUP)
    t = jnp.transpose(t, (1, 2, 0))
    return jnp.pad(t, ((0, 0), (0, SUB - GROUP), (0, 0)))


def _nsa_sample(proj_s, kvc, page_table, cache2d, win, bvz, past, page, row_base):
    nb = proj_s.shape[0]
    n_pick = N_SELECT - 1
    nbs = past // SEL_BLOCK
    ncs = past // CMP_STRIDE
    width = 2 * KV_ROW
    assert past % page == 0 and nbs > n_pick + NEAR // SEL_BLOCK and win.shape[1] == WINDOW < past
    q = (proj_s[:, :N_Q_COLS] * HEAD_DIM ** -0.5).astype(BF16).reshape(nb, N_KV_HEADS, GROUP, HEAD_DIM)
    qs = jnp.zeros((nb, N_KV_HEADS, SUB, LANES), BF16).at[:, :, :GROUP, :HEAD_DIM].set(q)
    qw = jnp.zeros((nb, N_KV_HEADS, SUB, width), BF16)
    for k in range(N_KV_HEADS):
        qw = qw.at[:, k, :GROUP, k * HEAD_DIM:(k + 1) * HEAD_DIM].set(q[:, k])
    gl = proj_s[:, N_Q_COLS + N_KV_COLS:N_Q_COLS + N_KV_COLS + N_GATE_COLS].reshape(nb, N_KV_HEADS, GROUP, 3)
    gl = jnp.pad(jnp.transpose(gl, (0, 1, 3, 2)), ((0, 0), (0, 0), (0, 0), (0, SUB - GROUP)))
    gl = jnp.broadcast_to(gl[..., None], gl.shape + (LANES,))
    nsel = proj_s[:, N_Q_COLS + width:N_Q_COLS + 2 * width].reshape(nb, 1, width)
    nwin = proj_s[:, N_Q_COLS + 2 * width:N_Q_COLS + 3 * width].reshape(nb, 1, width)

    dc = past - (np.arange(ncs) * CMP_STRIDE + CMP_BLOCK - 1)
    cbs = _bias_rows(bvz, np.where((dc >= 0) & (dc < NEAR), dc, NEAR))
    n_near = NEAR // SEL_BLOCK
    c = np.arange(LANES)
    sb = []
    for jj in range(n_near + 1):
        d = NEAR - SEL_BLOCK * jj - c
        sb.append(_bias_rows(bvz, np.where((c < SEL_BLOCK) & (jj < n_near) & (d < NEAR), d, NEAR)))
    sbz = jnp.stack(sb, axis=1)
    dw = WINDOW - np.arange(WINDOW)
    wb = _bias_rows(bvz, np.where(dw < NEAR, dw, NEAR))
    b0 = _bias_rows(bvz, np.zeros(LANES, np.int64))
    a_s = np.zeros((ncs, nbs), np.float32)
    for j in range(nbs):
        for n in range(SEL_RATIO * j - (CMP_RATIO - 1), SEL_RATIO * (j + 1)):
            if 0 <= n < ncs - CMP_RATIO + 1:
                a_s[n, j] = 1.0
    ro = np.zeros((N_KV_HEADS, LANES, width), np.float32)
    for k in range(N_KV_HEADS):
        ro[k, HEAD_DIM + np.arange(HEAD_DIM), KV_ROW + k * HEAD_DIM + np.arange(HEAD_DIM)] = 1.0
    a_s, ro = jnp.asarray(a_s, BF16), jnp.asarray(ro, BF16)

    per_b = lambda a: pl.BlockSpec((1,) + a.shape[1:], lambda b, pt: (b,) + (0,) * (a.ndim - 1))
    full = lambda a: pl.BlockSpec(a.shape, lambda b, pt: (0,) * a.ndim)
    grid_spec = pltpu.PrefetchScalarGridSpec(
        num_scalar_prefetch=1,
        grid=(nb,),
        in_specs=[per_b(qs), per_b(qw), per_b(gl), per_b(nsel), per_b(nwin), per_b(kvc),
                  pl.BlockSpec(memory_space=pl.ANY), per_b(win),
                  full(cbs), full(sbz), full(wb), full(b0), full(a_s), full(ro)],
        out_specs=[pl.BlockSpec((1, N_KV_HEADS, SUB, width), lambda b, pt: (b, 0, 0, 0)), per_b(win)],
        scratch_shapes=[pltpu.VMEM((N_KV_HEADS * n_pick, LANES, width), F32),
                        pltpu.SemaphoreType.DMA((1,)),
                        pltpu.SMEM((N_KV_HEADS * n_pick,), jnp.int32)])
    o, win_out = pl.pallas_call(
        functools.partial(_nsa_sample_kernel, past=past, page=page, row_base=row_base, n_pick=n_pick),
        grid_spec=grid_spec,
        out_shape=[jax.ShapeDtypeStruct((nb, N_KV_HEADS, SUB, width), F32),
                   jax.ShapeDtypeStruct(win.shape, F32)],
        compiler_params=_cparams(("arbitrary",)),
    )(page_table, qs, qw, gl, nsel, nwin, kvc, cache2d, win, cbs, sbz, wb, b0, a_s, ro)
    o = jnp.stack([o[:, k, :GROUP, KV_ROW + k * HEAD_DIM:KV_ROW + (k + 1) * HEAD_DIM] for k in range(N_KV_HEADS)],
                  axis=1)
    return o.reshape(nb, N_Q_COLS), win_out


NP_SSM = -(-(SSM_CONV_DIM + SSM_D_INNER + SSM_HEADS) // LANES) * LANES
SSM_COL_TILE = NP_SSM // 7
assert SSM_COL_TILE % LANES == 0 and SSM_COL_TILE * 7 == NP_SSM
N_BC = SSM_GROUPS * SSM_STATE
HEADS_PER_GROUP = SSM_HEADS // SSM_GROUPS
PAIR = LANES // SSM_HEAD_DIM
N_PAIRS = SSM_HEADS // PAIR
CONV_PAD = 8


def _softplus(x):
    return jnp.maximum(x, 0.0) + jnp.log1p(jnp.exp(-jnp.abs(x)))


def _cumsum_rows(x):
    n = x.shape[0]
    row = lax.broadcasted_iota(jnp.int32, x.shape, 0)
    s = 1
    while s < n:
        x = x + jnp.where(row >= s, pltpu.roll(x, s, axis=0), 0.0)
        s *= 2
    return x


def _ssd_kernel(xbc_ref, z_ref, dt_ref, h0_ref, cinit_ref, cw_ref, cb_ref, dtb_ref, alog_ref, dsk_ref, ng_ref,
                y_ref, hout_ref, xs_ref, act_ref, ybuf_ref, h_ref, *, nc):
    c = pl.program_id(1)
    q = SSM_CHUNK

    @pl.when(c == 0)
    def _():
        xs_ref[0:CONV_PAD] = cinit_ref[0]
        h_ref[...] = h0_ref[0]

    xs_ref[CONV_PAD:CONV_PAD + q] = xbc_ref[...]
    conv = cb_ref[...] + cw_ref[0:1] * xs_ref[CONV_PAD - 3:CONV_PAD - 3 + q]
    for k in range(1, SSM_CONV):
        conv = conv + cw_ref[k:k + 1] * xs_ref[CONV_PAD - 3 + k:CONV_PAD - 3 + k + q]
    xs_ref[0:CONV_PAD] = xs_ref[q:q + CONV_PAD]
    act_ref[...] = _silu(conv)

    dt = _softplus(dt_ref[...] + dtb_ref[...])
    acum = _cumsum_rows(dt * (-jnp.exp(alog_ref[...])))
    acum_t = acum.T
    dt_t = dt.T
    last = acum[q - 1:q, :]
    ea = jnp.exp(acum)
    te = jnp.exp(last - acum) * dt
    cd = jnp.exp(last)
    ii = lax.broadcasted_iota(jnp.int32, (q, q), 0)
    jj = lax.broadcasted_iota(jnp.int32, (q, q), 1)
    tri = ii >= jj
    low = jj < SSM_HEAD_DIM
    low_rows = ii < SSM_HEAD_DIM

    def col(a, h):
        return a[:, h:h + 1]

    for g in range(SSM_GROUPS):
        bg = act_ref[:, SSM_D_INNER + g * SSM_STATE:SSM_D_INNER + (g + 1) * SSM_STATE].astype(BF16)
        cg = act_ref[:, SSM_D_INNER + N_BC + g * SSM_STATE:SSM_D_INNER + N_BC + (g + 1) * SSM_STATE].astype(BF16)
        cbg = lax.dot_general(cg, bg, _NT, preferred_element_type=F32)
        for pr in range(HEADS_PER_GROUP // PAIR):
            k = g * (HEADS_PER_GROUP // PAIR) + pr
            ha, hb = PAIR * k, PAIR * k + 1
            xp = act_ref[:, k * LANES:(k + 1) * LANES]
            xpb = xp.astype(BF16)
            ys = []
            for h in (ha, hb):
                decay = jnp.exp(jnp.where(tri, col(acum, h) - acum_t[h:h + 1, :], NEG))
                w = cbg * decay * dt_t[h:h + 1, :]
                ys.append(jnp.dot(w.astype(BF16), xpb, preferred_element_type=F32))
            y = jnp.where(low, ys[0], ys[1])
            xs_pair = xp * jnp.where(low, col(te, ha), col(te, hb))
            st = jnp.dot(xs_pair.T.astype(BF16), bg, preferred_element_type=F32)
            hprev = h_ref[k]
            yoff = lax.dot_general(cg, hprev.astype(BF16), _NT, preferred_element_type=F32)
            y = y + yoff * jnp.where(low, col(ea, ha), col(ea, hb)) + dsk_ref[:, k * LANES:(k + 1) * LANES] * xp
            h_ref[k] = hprev * jnp.where(low_rows, cd[:, ha:ha + 1], cd[:, hb:hb + 1]) + st
            ybuf_ref[:, k * LANES:(k + 1) * LANES] = y

    yz = ybuf_ref[...] * _silu(z_ref[...])
    gw = SSM_D_INNER // SSM_GROUPS
    outs = []
    for g in range(SSM_GROUPS):
        seg = yz[:, g * gw:(g + 1) * gw]
        outs.append(seg * lax.rsqrt(jnp.mean(seg * seg, axis=-1, keepdims=True) + EPS))
    y_ref[...] = (jnp.concatenate(outs, axis=1) * ng_ref[...]).astype(y_ref.dtype)

    @pl.when(c == nc - 1)
    def _():
        hout_ref[0] = h_ref[...]


def _ssd(proj, h0, cinit, sw, nb, t):
    q = SSM_CHUNK
    nc = t // q
    vec = lambda a: pl.BlockSpec(a.shape, lambda b, c: (0, 0))
    return pl.pallas_call(
        functools.partial(_ssd_kernel, nc=nc),
        grid=(nb, nc),
        in_specs=[pl.BlockSpec((q, SSM_CONV_DIM), lambda b, c: (b * nc + c, 0)),
                  pl.BlockSpec((q, SSM_D_INNER), lambda b, c: (b * nc + c, SSM_CONV_DIM // SSM_D_INNER)),
                  pl.BlockSpec((q, LANES), lambda b, c: (b * nc + c, (SSM_CONV_DIM + SSM_D_INNER) // LANES)),
                  pl.BlockSpec((1,) + h0.shape[1:], lambda b, c: (b, 0, 0, 0)),
                  pl.BlockSpec((1,) + cinit.shape[1:], lambda b, c: (b, 0, 0)),
                  vec(sw['conv_w']), vec(sw['conv_b']), vec(sw['dt_bias']), vec(sw['a_log']), vec(sw['d_lane']),
                  vec(sw['norm_g'])],
        out_specs=[pl.BlockSpec((q, SSM_D_INNER), lambda b, c: (b * nc + c, 0)),
                   pl.BlockSpec((1,) + h0.shape[1:], lambda b, c: (b, 0, 0, 0))],
        out_shape=[jax.ShapeDtypeStruct((nb * t, SSM_D_INNER), BF16),
                   jax.ShapeDtypeStruct(h0.shape, F32)],
        scratch_shapes=[pltpu.VMEM((CONV_PAD + q, SSM_CONV_DIM), F32),
                        pltpu.VMEM((q, SSM_CONV_DIM), F32),
                        pltpu.VMEM((q, SSM_D_INNER), F32),
                        pltpu.VMEM(h0.shape[1:], F32)],
        compiler_params=_cparams(("arbitrary", "arbitrary")),
    )(proj, proj, proj, h0, cinit, sw['conv_w'], sw['conv_b'], sw['dt_bias'], sw['a_log'], sw['d_lane'],
      sw['norm_g'])


def _ssm_sample_step(proj_s, state_ssm, state_conv, sw):
    nb = proj_s.shape[0]
    q = SSM_CHUNK
    xbc_new = proj_s[:, :SSM_CONV_DIM]
    rows = jnp.zeros((nb, q, NP_SSM), F32)
    rows = rows.at[:, :, SSM_CONV_DIM + SSM_D_INNER:].set(NEG)
    rows = rows.at[:, q - SSM_CONV:q - 1, :SSM_CONV_DIM].set(state_conv.astype(F32))
    rows = rows.at[:, q - 1].set(proj_s)
    h0 = state_ssm.astype(F32).reshape(nb, N_PAIRS, LANES, SSM_STATE)
    cinit = jnp.zeros((nb, CONV_PAD, SSM_CONV_DIM), F32)
    yn, hfin = _ssd(rows.reshape(nb * q, NP_SSM), h0, cinit, sw, nb, q)
    conv_new = jnp.concatenate([state_conv[:, 1:].astype(F32), xbc_new[:, None]], axis=1)
    return yn.reshape(nb, q, SSM_D_INNER)[:, q - 1], hfin.reshape(state_ssm.shape), conv_new


def _ssm_weights(w_in, conv_w, conv_b, dt_bias, a_log, d_skip, norm_g):
    z_w = w_in[:, :SSM_D_INNER]
    xbc_w = w_in[:, SSM_D_INNER:SSM_D_INNER + SSM_CONV_DIM]
    dt_w = w_in[:, SSM_D_INNER + SSM_CONV_DIM:]
    pad = NP_SSM - w_in.shape[1]
    w = jnp.concatenate([xbc_w, z_w, dt_w, jnp.zeros((w_in.shape[0], pad), w_in.dtype)], axis=1).astype(BF16)
    lane_pad = lambda v: jnp.zeros((1, LANES), F32).at[0, :SSM_HEADS].set(v.astype(F32))
    return dict(w_in=w, conv_w=conv_w.astype(F32), conv_b=conv_b.astype(F32).reshape(1, -1),
                dt_bias=lane_pad(dt_bias), a_log=lane_pad(a_log),
                d_lane=jnp.repeat(d_skip.astype(F32), SSM_HEAD_DIM).reshape(1, -1),
                norm_g=norm_g.astype(F32).reshape(1, -1))


ROW_TILE = 512
FF_TILE = 1024


def kernel(x_prompt, x_sample, cache_kv, cache_win, state_ssm, state_conv, page_table, c_prompt, c_sample, rel_bias,
           ada_w, ada_b, norm_g, mlp_w1, mlp_w2, attn_w_in, attn_w_out, cmp_pos, cmp_w1, cmp_w2, ssm_w_in,
           ssm_conv_w, ssm_conv_b, ssm_dt_bias, ssm_a_log, ssm_d, ssm_norm_g, ssm_w_out):
    nb, t, d = x_prompt.shape
    db = x_sample.shape[0]
    assert x_sample.shape[1] == 1 and t % SSM_CHUNK == 0 and t % Q_BLOCK == 0
    n_pool, page = cache_kv.shape[1], cache_kv.shape[2]
    past = page_table.shape[1] * page
    depth = ada_w.shape[0]
    tm = _row_tile(t, ROW_TILE)

    xp = x_prompt.reshape(nb * t, d).astype(F32)
    xs = x_sample.reshape(db, d).astype(F32)
    c_all = jnp.concatenate([c_prompt, c_sample], axis=0).astype(F32)
    c_all = jnp.pad(c_all, ((0, (-c_all.shape[0]) % SUB), (0, 0)))
    bvz = _bias_vector(rel_bias)
    cache2d = cache_kv.reshape(-1, N_KV_SLOTS * KV_ROW)
    win2d = cache_win.reshape(cache_win.shape[0], db, cache_win.shape[2], N_WIN_SLOTS * KV_ROW)

    kv_p, win_p, ssm_p, conv_p, kv_s, win_s, ssm_s, conv_s = ([] for _ in range(8))
    for i in range(depth):
        mod = _ada(c_all, ada_w[i], ada_b[i])
        mp = [mod[:nb, j * d:(j + 1) * d].reshape(nb, 1, d) for j in range(N_MOD)]
        ms = [mod[nb:nb + db, j * d:(j + 1) * d].reshape(1, db, d) for j in range(N_MOD)]
        g = norm_g[i].astype(F32)
        if i % 2 == 0:
            a = i // 2
            w_in = jnp.pad(attn_w_in[a], ((0, 0), (0, NP_ATTN - attn_w_in.shape[2]))).astype(BF16)
            w_out = attn_w_out[a].astype(BF16)
            cw = _compress_weights(cmp_pos[a], cmp_w1[a], cmp_w2[a])
            o_p, kv6 = _nsa_prompt_mixer(xp, g[0], mp[0], mp[1], w_in, cw, bvz, nb, t)
            kv_p.append(kv6[:, :, :N_KV_SLOTS])
            win_p.append(kv6[:, -min(WINDOW, t):, N_KV_SLOTS:])
            xp = _mm_norm_res(o_p, w_out, g[1], mp[2], xp, tm)

            proj_s = _nm_matmul(xs, g[0], ms[0], ms[1], w_in, db, NP_ATTN)
            kvc_s = _pack_cmp(_compress_sample(page_table, cache2d, cw, page, a * n_pool))
            o_s, win_new = _nsa_sample(proj_s, kvc_s, page_table, cache2d, win2d[a], bvz, past, page, a * n_pool)
            kv_s.append(proj_s[:, N_Q_COLS:N_Q_COLS + N_KV_SLOTS * KV_ROW].reshape(
                db, 1, N_KV_SLOTS, N_KV_HEADS, HEAD_DIM))
            win_s.append(win_new.reshape(db, -1, N_WIN_SLOTS, N_KV_HEADS, HEAD_DIM))
            xs = _mm_norm_res(o_s, w_out, g[1], ms[2], xs, db)
        else:
            m = i // 2
            sw = _ssm_weights(ssm_w_in[m], ssm_conv_w[m], ssm_conv_b[m], ssm_dt_bias[m], ssm_a_log[m], ssm_d[m],
                              ssm_norm_g[m])
            w_out = ssm_w_out[m].astype(BF16)
            proj = _nm_matmul(xp, g[0], mp[0], mp[1], sw['w_in'], tm, SSM_COL_TILE)
            h0 = jnp.zeros((nb, N_PAIRS, LANES, SSM_STATE), F32)
            cinit = jnp.zeros((nb, CONV_PAD, SSM_CONV_DIM), F32)
            yn, hfin = _ssd(proj, h0, cinit, sw, nb, t)
            ssm_p.append(hfin.reshape(nb, SSM_HEADS, SSM_HEAD_DIM, SSM_STATE).astype(state_ssm.dtype))
            conv_p.append(proj.reshape(nb, t, NP_SSM)[:, t - (SSM_CONV - 1):, :SSM_CONV_DIM])
            xp = _mm_norm_res(yn, w_out, g[1], mp[2], xp, tm)

            proj_s = _nm_matmul(xs, g[0], ms[0], ms[1], sw['w_in'], db, SSM_COL_TILE)
            yn_s, h_s, conv_new = _ssm_sample_step(proj_s, state_ssm[m], state_conv[m], sw)
            ssm_s.append(h_s.astype(state_ssm.dtype))
            conv_s.append(conv_new)
            xs = _mm_norm_res(yn_s, w_out, g[1], ms[2], xs, db)
        w1 = mlp_w1[i].astype(BF16)
        w2 = mlp_w2[i].astype(BF16)
        xp = _mlp(xp, g[2], mp[3], mp[4], w1, w2, g[3], mp[5], tm, FF_TILE)
        xs = _mlp(xs, g[2], ms[3], ms[4], w1, w2, g[3], ms[5], db, FF_TILE)
    return (xp.reshape(nb, t, d), xs.reshape(db, 1, d), jnp.stack(kv_p), jnp.stack(win_p), jnp.stack(ssm_p),
            jnp.stack(conv_p), jnp.stack(kv_s), jnp.stack(win_s), jnp.stack(ssm_s), jnp.stack(conv_s))
```

```python
import functools
import math

import numpy as np
import jax
import jax.numpy as jnp
from jax import lax
from jax.experimental import pallas as pl
from jax.experimental.pallas import tpu as pltpu

F32 = jnp.float32
BF16 = jnp.bfloat16

D_MODEL = 1024
N_HEADS = 16
HEAD_DIM = 64
N_KV_HEADS = 4
GROUP = N_HEADS // N_KV_HEADS
CMP_BLOCK = 32
CMP_STRIDE = 16
CMP_RATIO = CMP_BLOCK // CMP_STRIDE
CMP_HIDDEN = 2 * HEAD_DIM
SEL_BLOCK = 64
SEL_RATIO = SEL_BLOCK // CMP_STRIDE
N_SELECT = 16
WINDOW = 512
FORCE_BONUS = 1e4
Q_BLOCK = 128
N_KV_SLOTS = 4
N_WIN_SLOTS = 2
N_Q_COLS = N_HEADS * HEAD_DIM
N_KV_COLS = (N_KV_SLOTS + N_WIN_SLOTS) * N_KV_HEADS * HEAD_DIM
N_GATE_COLS = 3 * N_HEADS
KV_ROW = N_KV_HEADS * HEAD_DIM
N_BUCKETS = 32
MAX_DISTANCE = 128
SSM_D_INNER = 2 * D_MODEL
SSM_HEAD_DIM = 64
SSM_HEADS = SSM_D_INNER // SSM_HEAD_DIM
SSM_GROUPS = 8
SSM_STATE = 128
SSM_CONV = 4
SSM_CONV_DIM = SSM_D_INNER + 2 * SSM_GROUPS * SSM_STATE
SSM_CHUNK = 128
D_FF = 4 * D_MODEL
N_MOD = 6
EPS = 1e-6

LANES = 128
NEG = -1e30
VMEM_LIMIT = 56 * 1024 * 1024
NEAR = 2 * LANES

_NT = (((1,), (1,)), ((), ()))


def _cparams(sem):
    return pltpu.CompilerParams(dimension_semantics=sem, vmem_limit_bytes=VMEM_LIMIT)


def _rms(x, g):
    return x * lax.rsqrt(jnp.mean(x * x, axis=-1, keepdims=True) + EPS) * g


def _silu(x):
    return x * jax.nn.sigmoid(x)


def _split_bf16(x, n):
    parts = []
    for _ in range(n - 1):
        p = x.astype(BF16)
        parts.append(p)
        x = x - p.astype(F32)
    parts.append(x.astype(BF16))
    return parts


def _ada_kernel(c_ref, w_ref, b_ref, o_ref):
    s = _silu(c_ref[...]).astype(BF16)
    o_ref[...] = jnp.dot(s, w_ref[...].astype(BF16), preferred_element_type=F32) + b_ref[...]


def _ada(c, w, b, tn=1024):
    m, d = c.shape
    n = w.shape[1]
    return pl.pallas_call(
        _ada_kernel,
        grid=(n // tn,),
        in_specs=[pl.BlockSpec((m, d), lambda j: (0, 0)),
                  pl.BlockSpec((d, tn), lambda j: (0, j)),
                  pl.BlockSpec((1, tn), lambda j: (0, j))],
        out_specs=pl.BlockSpec((m, tn), lambda j: (0, j)),
        out_shape=jax.ShapeDtypeStruct((m, n), F32),
        compiler_params=_cparams(("arbitrary",)),
    )(c, w, b.reshape(1, n))


def _nm_matmul_kernel(x_ref, g_ref, sh_ref, sc_ref, w_ref, o_ref, h_ref):
    @pl.when(pl.program_id(1) == 0)
    def _():
        h = _rms(x_ref[...], g_ref[...]) * (1.0 + sc_ref[0]) + sh_ref[0]
        h_ref[...] = h.astype(BF16)

    o_ref[...] = jnp.dot(h_ref[...], w_ref[...], preferred_element_type=F32)


def _nm_matmul(x, g, shift, scale, w, tm, tn):
    m, d = x.shape
    n = w.shape[1]
    nb = shift.shape[0]
    tpb = m // nb // tm
    mod_spec = pl.BlockSpec((1,) + shift.shape[1:], lambda i, j: (i // tpb, 0, 0))
    return pl.pallas_call(
        _nm_matmul_kernel,
        grid=(m // tm, n // tn),
        in_specs=[pl.BlockSpec((tm, d), lambda i, j: (i, 0)),
                  pl.BlockSpec((1, d), lambda i, j: (0, 0)),
                  mod_spec, mod_spec,
                  pl.BlockSpec((d, tn), lambda i, j: (0, j))],
        out_specs=pl.BlockSpec((tm, tn), lambda i, j: (i, j)),
        out_shape=jax.ShapeDtypeStruct((m, n), F32),
        scratch_shapes=[pltpu.VMEM((tm, d), BF16)],
        compiler_params=_cparams(("arbitrary", "arbitrary")),
    )(x, g.reshape(1, d), shift, scale, w)


def _mm_norm_res_kernel(a_ref, w_ref, g_ref, gate_ref, x_ref, o_ref):
    y = jnp.dot(a_ref[...].astype(BF16), w_ref[...], preferred_element_type=F32)
    o_ref[...] = x_ref[...] + gate_ref[0] * _rms(y, g_ref[...])


def _mm_norm_res(a, w, g, gate, x, tm):
    m, k = a.shape
    d = w.shape[1]
    nb = gate.shape[0]
    tpb = m // nb // tm
    return pl.pallas_call(
        _mm_norm_res_kernel,
        grid=(m // tm,),
        in_specs=[pl.BlockSpec((tm, k), lambda i: (i, 0)),
                  pl.BlockSpec((k, d), lambda i: (0, 0)),
                  pl.BlockSpec((1, d), lambda i: (0, 0)),
                  pl.BlockSpec((1,) + gate.shape[1:], lambda i: (i // tpb, 0, 0)),
                  pl.BlockSpec((tm, d), lambda i: (i, 0))],
        out_specs=pl.BlockSpec((tm, d), lambda i: (i, 0)),
        out_shape=jax.ShapeDtypeStruct((m, d), F32),
        compiler_params=_cparams(("arbitrary",)),
    )(a, w, g.reshape(1, d), gate, x)


def _mlp_kernel(x_ref, g1_ref, sh_ref, sc_ref, w1_ref, w2_ref, g2_ref, gate_ref, o_ref, h_ref, acc_ref):
    c = pl.program_id(1)

    @pl.when(c == 0)
    def _():
        h = _rms(x_ref[...], g1_ref[...]) * (1.0 + sc_ref[0]) + sh_ref[0]
        h_ref[...] = h.astype(BF16)
        acc_ref[...] = jnp.zeros_like(acc_ref)

    a = jnp.dot(h_ref[...], w1_ref[...], preferred_element_type=F32)
    a = jnp.square(jnp.maximum(a, 0.0)).astype(BF16)
    acc_ref[...] += jnp.dot(a, w2_ref[...], preferred_element_type=F32)

    @pl.when(c == pl.num_programs(1) - 1)
    def _():
        o_ref[...] = x_ref[...] + gate_ref[0] * _rms(acc_ref[...], g2_ref[...])


def _mlp(x, g1, shift, scale, w1, w2, g2, gate, tm, tf):
    m, d = x.shape
    f = w1.shape[1]
    nb = shift.shape[0]
    tpb = m // nb // tm
    mod_spec = pl.BlockSpec((1,) + shift.shape[1:], lambda i, c: (i // tpb, 0, 0))
    vec_spec = pl.BlockSpec((1, d), lambda i, c: (0, 0))
    return pl.pallas_call(
        _mlp_kernel,
        grid=(m // tm, f // tf),
        in_specs=[pl.BlockSpec((tm, d), lambda i, c: (i, 0)), vec_spec, mod_spec, mod_spec,
                  pl.BlockSpec((d, tf), lambda i, c: (0, c)),
                  pl.BlockSpec((tf, d), lambda i, c: (c, 0)),
                  vec_spec, mod_spec],
        out_specs=pl.BlockSpec((tm, d), lambda i, c: (i, 0)),
        out_shape=jax.ShapeDtypeStruct((m, d), F32),
        scratch_shapes=[pltpu.VMEM((tm, d), BF16), pltpu.VMEM((tm, d), F32)],
        compiler_params=_cparams(("arbitrary", "arbitrary")),
    )(x, g1.reshape(1, d), shift, scale, w1, w2, g2.reshape(1, d), gate)


def _bias_kernel(oh_ref, t_ref, o_ref):
    t = t_ref[...]
    t = t - t[N_BUCKETS - 1:N_BUCKETS, :]
    oh = oh_ref[...]
    o_ref[...] = sum(jnp.dot(oh, p, preferred_element_type=F32) for p in _split_bf16(t, 3))


def _rel_bucket_np(dist):
    n = np.maximum(dist, 0)
    exact = N_BUCKETS // 2
    nf = np.maximum(n, 1).astype(np.float32)
    large = exact + (np.log(nf / exact) / math.log(MAX_DISTANCE / exact) * (N_BUCKETS - exact)).astype(np.int32)
    return np.where(n < exact, n, np.minimum(large, N_BUCKETS - 1))


def _bias_vector(rel_bias):
    assert _rel_bucket_np(np.array([NEAR - 1]))[0] == N_BUCKETS - 1 == _rel_bucket_np(np.array([MAX_DISTANCE]))[0]
    oh = np.zeros((NEAR + 8, LANES), np.float32)
    oh[np.arange(NEAR), _rel_bucket_np(np.arange(NEAR))] = 1.0
    oh[NEAR:, N_BUCKETS - 1] = 1.0
    table = jnp.zeros((LANES, LANES), F32).at[:N_BUCKETS, :N_HEADS].set(rel_bias.astype(F32))
    bv = pl.pallas_call(
        _bias_kernel,
        out_shape=jax.ShapeDtypeStruct((NEAR + 8, LANES), F32),
    )(jnp.asarray(oh, BF16), table)
    return bv[:NEAR + 1, :N_HEADS]


def _toeplitz(bvz, idx):
    r, c = idx.shape
    t = jnp.take(bvz, jnp.asarray(idx.reshape(-1), jnp.int32), axis=0).reshape(r, c, N_KV_HEADS, GROUP)
    return jnp.transpose(t, (2, 3, 0, 1)).reshape(N_KV_HEADS, GROUP * r, c)


SUB = 8
HEADS_PER_CG = LANES // HEAD_DIM
N_CG = 2 * KV_ROW // LANES
CG_PER_SLOT = KV_ROW // LANES
CG_HIDDEN = HEADS_PER_CG * CMP_HIDDEN


def _compress_parts(rows_s, wbd_ref, nh):
    part0 = jnp.zeros((nh, CG_HIDDEN), F32)
    part1 = jnp.zeros((nh, CG_HIDDEN), F32)
    for s in range(CMP_STRIDE):
        xs = rows_s(s).astype(BF16)
        part0 = part0 + jnp.dot(xs, wbd_ref[0, 0, s], preferred_element_type=F32)
        part1 = part1 + jnp.dot(xs, wbd_ref[0, 1, s], preferred_element_type=F32)
    return part0, part1


def _compress_finish(part0, part1, pos_ref, w1_ref, w2bd_ref, nh):
    pre0 = jnp.dot(pos_ref[0].astype(BF16), w1_ref[0].astype(BF16), preferred_element_type=F32)[0:1]
    pre = jnp.concatenate([pre0] * HEADS_PER_CG, axis=1) + part0 + pltpu.roll(part1, nh - 1, axis=0)
    return jnp.dot(_silu(pre).astype(BF16), w2bd_ref[0], preferred_element_type=F32)


def _compress_kernel(x_ref, pos_ref, w1_ref, wbd_ref, w2bd_ref, o_ref, *, nh):
    part0, part1 = _compress_parts(lambda s: x_ref[pl.ds(s, nh, stride=CMP_STRIDE), :], wbd_ref, nh)
    o_ref[0, 0] = _compress_finish(part0, part1, pos_ref, w1_ref, w2bd_ref, nh)


def _compress_weights(cmp_pos, cmp_w1, cmp_w2):
    eye = jnp.eye(HEADS_PER_CG, dtype=F32)
    w1r = cmp_w1.reshape(2, CMP_RATIO, CMP_STRIDE, HEAD_DIM, CMP_HIDDEN)
    wbd = jnp.einsum('kl,artdh->artkdlh', eye, w1r).reshape(2, CMP_RATIO, CMP_STRIDE, LANES, CG_HIDDEN).astype(BF16)
    w2bd = jnp.einsum('kl,ahd->akhld', eye, cmp_w2).reshape(2, CG_HIDDEN, LANES).astype(BF16)
    pos = jnp.zeros((2, SUB, CMP_BLOCK * HEAD_DIM), F32).at[:, 0].set(cmp_pos.reshape(2, -1))
    return pos, cmp_w1, wbd, w2bd


def _compress_specs(cw, imap):
    def spec(a):
        return pl.BlockSpec((1,) + a.shape[1:], lambda *g: (imap(*g) // CG_PER_SLOT,) + (0,) * (a.ndim - 1))
    return [spec(a) for a in cw]


def _pack_cmp(cmp, dtype=None):
    nb, _, n, _ = cmp.shape
    c = cmp.reshape(nb, 2, CG_PER_SLOT, n, HEADS_PER_CG, HEAD_DIM)
    c = jnp.transpose(c, (0, 2, 4, 3, 1, 5)).reshape(nb, N_KV_HEADS, n, 2 * HEAD_DIM)
    return c.astype(BF16 if dtype is None else dtype)


def _compress_prompt(proj, cw, nb, t):
    nh = t // CMP_STRIDE
    col0 = N_Q_COLS // LANES
    return pl.pallas_call(
        functools.partial(_compress_kernel, nh=nh),
        grid=(nb, N_CG),
        in_specs=[pl.BlockSpec((t, LANES), lambda b, c: (b, col0 + c))] + _compress_specs(cw, lambda b, c: c),
        out_specs=pl.BlockSpec((1, 1, nh, LANES), lambda b, c: (b, c, 0, 0)),
        out_shape=jax.ShapeDtypeStruct((nb, N_CG, nh, LANES), F32),
        compiler_params=_cparams(("arbitrary", "arbitrary")),
    )(proj, *cw)


KEY_TILE = 4 * Q_BLOCK
FRONT_PAD = KEY_TILE
SUBTILES = KEY_TILE // Q_BLOCK
BAND_OFF = 2 * Q_BLOCK // CMP_STRIDE
BAND_W = 32
Q_PER_CMP = Q_BLOCK // CMP_STRIDE


def _nsa_consts(t):
    ncp = t // CMP_STRIDE
    nblk = t // SEL_BLOCK
    nbp = -(-nblk // LANES) * LANES
    at = np.zeros((nbp, ncp), np.float32)
    for j in range(nblk):
        for n in range(SEL_RATIO * j - (CMP_RATIO - 1), SEL_RATIO * (j + 1)):
            if 0 <= n < ncp - CMP_RATIO + 1:
                at[j, n] = 1.0
    key_blk = np.arange(t) // SEL_BLOCK
    e = (np.arange(nbp)[:, None] == key_blk[None, :]).astype(np.float32)
    e1 = e.reshape(nbp, t // LANES, LANES).transpose(1, 0, 2)
    e1 = np.concatenate([np.zeros((SUBTILES,) + e1.shape[1:], np.float32), e1], axis=0)
    band = np.zeros((2 * ncp + 2 * Q_PER_CMP, LANES), np.float32)
    u = np.arange(BAND_W)
    band[u + ncp, HEAD_DIM + u] = 1.0
    band[u + ncp, HEAD_DIM + BAND_W + u] = 1.0
    r = np.arange(Q_BLOCK)[:, None]
    c = np.arange(LANES)[None, :]
    d0 = r - c
    idx0 = np.where(d0 >= 0, d0, NEAR)
    idx1 = Q_BLOCK + r - c
    dc = r - CMP_STRIDE * c + (CMP_STRIDE * BAND_OFF - CMP_BLOCK + 1)
    idxc = np.where((dc >= 0) & (dc < NEAR), dc, NEAR)
    assert np.all(idxc[:, BAND_W:] == NEAR)
    bf = lambda a: jnp.asarray(a, BF16)
    return dict(at=bf(at), e1=bf(e1), band=jnp.asarray(band),
                idx0=idx0, idx1=idx1, idxc=idxc, ncp=ncp, nbp=nbp, nblk=nblk)


def _nsa_kernel(q_ref, gl_ref, kck_ref, kvc_ref, kvs_ref, kvw_ref, at_ref, e1_ref, tb_ref, cbq_ref, band_ref,
                o_ref, sc_ref, *, ncp, nbp, n_sel):
    i = pl.program_id(2)
    rows = GROUP * Q_BLOCK
    lane = lax.broadcasted_iota(jnp.int32, (1, LANES), 1)
    low = lane < HEAD_DIM
    r_col = lax.broadcasted_iota(jnp.int32, (rows, 1), 0) % Q_BLOCK
    q_pos = i * Q_BLOCK + r_col

    q = q_ref[...] * (HEAD_DIM ** -0.5)
    parts = []
    for h in range(GROUP * HEAD_DIM // LANES):
        qh = q[:, h * LANES:(h + 1) * LANES]
        parts.append(jnp.where(low, qh, 0.0))
        parts.append(jnp.where(low, pltpu.roll(qh, HEAD_DIM, axis=1), 0.0))
    qs = jnp.concatenate(parts, axis=0).astype(BF16)

    qc = jnp.where(low, qs, cbq_ref[0])
    start = pl.multiple_of(ncp + BAND_OFF - Q_PER_CMP * i, 8)
    kq = (kck_ref[0, 0] + band_ref[pl.ds(start, ncp), :]).astype(BF16)
    s = lax.dot_general(qc, kq, _NT, preferred_element_type=F32)
    cmp_end = lax.broadcasted_iota(jnp.int32, (1, ncp), 1) * CMP_STRIDE + (CMP_BLOCK - 1)
    mask = cmp_end <= q_pos
    m = jnp.max(jnp.where(mask, s, NEG), axis=-1, keepdims=True)
    e = jnp.where(mask, jnp.exp(s - m), 0.0)
    p_cmp = e / jnp.maximum(jnp.sum(e, axis=-1, keepdims=True), 1e-30)
    o_cmp = jnp.dot(p_cmp.astype(BF16), kvc_ref[0, 0], preferred_element_type=F32)
    imp = p_cmp[0:Q_BLOCK]
    for g in range(1, GROUP):
        imp = imp + p_cmp[g * Q_BLOCK:(g + 1) * Q_BLOCK]
    bst = sum(lax.dot_general(at_ref[...], part, _NT, preferred_element_type=F32)
              for part in _split_bf16(imp, 2))

    jb = lax.broadcasted_iota(jnp.int32, (nbp, Q_BLOCK), 0)
    q_blk = (i * Q_BLOCK + lax.broadcasted_iota(jnp.int32, (nbp, Q_BLOCK), 1)) // SEL_BLOCK
    causal = jb <= q_blk
    forced = (jb == 0) | (jb == q_blk) | (jb == q_blk - 1)
    score = jnp.where(causal, bst + jnp.where(forced, FORCE_BONUS, 0.0), -jnp.inf)
    sc_ref[...] = score

    def ahead_of(jp):
        row = sc_ref[pl.ds(jp, 1), :]
        tie = jnp.where(jb > jp, 1.0, 0.0)
        return jnp.where(row > score, 1.0, jnp.where(row == score, tie, 0.0))

    def rank_body(jh, cnt):
        return cnt + ahead_of(2 * jh) + ahead_of(2 * jh + 1)

    n_causal_pairs = Q_BLOCK // SEL_BLOCK * (i + 1) // 2
    cnt = lax.fori_loop(0, n_causal_pairs, rank_body, jnp.zeros((nbp, Q_BLOCK), F32))
    sel_t = jnp.where(causal, jnp.where(cnt < n_sel, 1.0, 0.0), 0.0)
    sel = sel_t.T.astype(BF16)

    n_wt = WINDOW // Q_BLOCK
    kw = kvw_ref[0, 0, pl.ds(pl.multiple_of((i + SUBTILES - n_wt) * Q_BLOCK, Q_BLOCK), WINDOW + Q_BLOCK), :]
    s = lax.dot_general(qs, kw, _NT, preferred_element_type=F32)
    pieces = []
    for u in range(n_wt + 1):
        su = s[:, u * Q_BLOCK:(u + 1) * Q_BLOCK]
        if u == n_wt:
            su = jnp.where(lane <= r_col, su + tb_ref[0, 0], NEG)
        else:
            if u == n_wt - 1:
                su = su + tb_ref[0, 1]
            su = su + jnp.where(i - n_wt + u >= 0, 0.0, NEG)
            if u == 0:
                su = jnp.where(lane > r_col, su, NEG)
        pieces.append(su)
    sm = jnp.concatenate(pieces, axis=1)
    e = jnp.exp(sm - jnp.max(sm, axis=-1, keepdims=True))
    o_win = jnp.dot(e.astype(BF16), kw, preferred_element_type=F32) / jnp.maximum(
        jnp.sum(e, axis=-1, keepdims=True), 1e-30)

    def tile(t0p):
        kv = kvs_ref[0, 0, pl.ds(pl.multiple_of(t0p * Q_BLOCK, Q_BLOCK), KEY_TILE), :]
        mk = jnp.concatenate([jnp.dot(sel, e1_ref[t0p + u], preferred_element_type=F32) for u in range(SUBTILES)],
                             axis=1)
        return kv, mk > 0.5

    def scores(g, kv):
        return lax.dot_general(qs[g * Q_BLOCK:(g + 1) * Q_BLOCK], kv, _NT, preferred_element_type=F32)

    kv, mk = tile(i + 1)
    r2 = lax.broadcasted_iota(jnp.int32, (Q_BLOCK, LANES), 0)
    c2 = lax.broadcasted_iota(jnp.int32, (Q_BLOCK, LANES), 1)
    carry = []
    for g in range(GROUP):
        s = scores(g, kv)
        tb0 = tb_ref[0, 0, g * Q_BLOCK:(g + 1) * Q_BLOCK]
        tb1 = tb_ref[0, 1, g * Q_BLOCK:(g + 1) * Q_BLOCK]
        s = jnp.concatenate([s[:, :KEY_TILE - 2 * Q_BLOCK],
                             s[:, KEY_TILE - 2 * Q_BLOCK:KEY_TILE - Q_BLOCK] + tb1,
                             jnp.where(c2 <= r2, s[:, KEY_TILE - Q_BLOCK:] + tb0, NEG)], axis=1)
        sm = jnp.where(mk, s, NEG)
        m = jnp.max(sm, axis=-1, keepdims=True)
        e = jnp.exp(sm - m)
        carry += [m, jnp.sum(e, axis=-1, keepdims=True),
                  jnp.dot(e.astype(BF16), kv, preferred_element_type=F32)]

    def far_body(mt, carry):
        kv, mk = tile(i + 1 - SUBTILES * mt)
        out = []
        for g in range(GROUP):
            m, l, acc = carry[3 * g:3 * g + 3]
            sm = jnp.where(mk, scores(g, kv), NEG)
            m_new = jnp.maximum(m, jnp.max(sm, axis=-1, keepdims=True))
            alpha = jnp.exp(m - m_new)
            e = jnp.exp(sm - m_new)
            out += [m_new, alpha * l + jnp.sum(e, axis=-1, keepdims=True),
                    alpha * acc + jnp.dot(e.astype(BF16), kv, preferred_element_type=F32)]
        return tuple(out)

    carry = lax.fori_loop(1, (i + SUBTILES) // SUBTILES, far_body, tuple(carry))
    o_sel = jnp.concatenate([carry[3 * g + 2] / jnp.maximum(carry[3 * g + 1], 1e-30) for g in range(GROUP)], axis=0)

    sg = jax.nn.sigmoid(gl_ref[...])
    for g in range(GROUP):
        out = jnp.zeros((Q_BLOCK, LANES), F32)
        for j, ob in enumerate((o_cmp, o_sel, o_win)):
            out = out + sg[:, 3 * g + j:3 * g + j + 1] * ob[g * Q_BLOCK:(g + 1) * Q_BLOCK]
        o_ref[0, 0, g] = out.astype(o_ref.dtype)


GATE_COL0 = N_Q_COLS + N_KV_COLS
NP_ATTN = GATE_COL0 + N_KV_HEADS * LANES


def _nsa_prompt(proj, kck, kvc, kvs, kvw, tb, cbq, consts, nb, t):
    ni = t // Q_BLOCK
    ncp, nbp = consts['ncp'], consts['nbp']
    gcol = GATE_COL0 // LANES
    qw = GROUP * HEAD_DIM
    full = lambda a: pl.BlockSpec(a.shape, lambda b, k, i: (0,) * a.ndim)
    per_kvh = lambda a: pl.BlockSpec((1,) + a.shape[1:], lambda b, k, i: (k,) + (0,) * (a.ndim - 1))
    per_bk = lambda a: pl.BlockSpec((1, 1) + a.shape[2:], lambda b, k, i: (b, k, 0, 0))
    c = consts
    n_sel = min(N_SELECT, c['nblk'])
    assert n_sel >= 3
    return pl.pallas_call(
        functools.partial(_nsa_kernel, ncp=ncp, nbp=nbp, n_sel=n_sel),
        grid=(nb, N_KV_HEADS, ni),
        in_specs=[pl.BlockSpec((Q_BLOCK, qw), lambda b, k, i: (b * ni + i, k)),
                  pl.BlockSpec((Q_BLOCK, LANES), lambda b, k, i: (b * ni + i, gcol + k)),
                  per_bk(kck), per_bk(kvc), per_bk(kvs), per_bk(kvw),
                  full(c['at']), full(c['e1']), per_kvh(tb), per_kvh(cbq), full(c['band'])],
        out_specs=pl.BlockSpec((1, 1, GROUP, Q_BLOCK, LANES), lambda b, k, i: (b, k, 0, i, 0)),
        out_shape=jax.ShapeDtypeStruct((nb, N_KV_HEADS, GROUP, t, LANES), BF16),
        scratch_shapes=[pltpu.VMEM((nbp, Q_BLOCK), F32)],
        compiler_params=_cparams(("arbitrary", "arbitrary", "arbitrary")),
    )(proj, proj, kck, kvc, kvs, kvw, c['at'], c['e1'], tb, cbq, c['band'])


def _heads_out_kernel(a_ref, w_ref, g_ref, gate_ref, x_ref, o_ref):
    y = jnp.zeros(o_ref.shape, F32)
    for k in range(N_KV_HEADS):
        for g in range(0, GROUP, 2):
            a = jnp.concatenate([a_ref[0, k, g], a_ref[0, k, g + 1]], axis=1)
            h = k * GROUP + g
            w = w_ref[h * LANES:(h + 2) * LANES, :]
            y = y + jnp.dot(a, w, preferred_element_type=F32)
    o_ref[...] = x_ref[...] + gate_ref[0] * _rms(y, g_ref[...])


def _heads_out(a, w_pad, g, gate, x, tm):
    nb, _, _, t, _ = a.shape
    d = w_pad.shape[1]
    tpb = t // tm
    return pl.pallas_call(
        _heads_out_kernel,
        grid=(nb * tpb,),
        in_specs=[pl.BlockSpec((1, N_KV_HEADS, GROUP, tm, LANES), lambda i: (i // tpb, 0, 0, i % tpb, 0)),
                  pl.BlockSpec(w_pad.shape, lambda i: (0, 0)),
                  pl.BlockSpec((1, d), lambda i: (0, 0)),
                  pl.BlockSpec((1,) + gate.shape[1:], lambda i: (i // tpb, 0, 0)),
                  pl.BlockSpec((tm, d), lambda i: (i, 0))],
        out_specs=pl.BlockSpec((tm, d), lambda i: (i, 0)),
        out_shape=jax.ShapeDtypeStruct((nb * t, d), F32),
        compiler_params=_cparams(("arbitrary",)),
    )(a, w_pad, g.reshape(1, d), gate, x)


def _row_tile(m, cap):
    return m if m <= cap else cap


def _pack_kv(kv6, s0):
    pair = jnp.transpose(kv6[:, :, s0:s0 + 2], (0, 3, 1, 2, 4))
    pair = pair.reshape(pair.shape[0], N_KV_HEADS, pair.shape[2], 2 * HEAD_DIM).astype(BF16)
    return jnp.pad(pair, ((0, 0), (0, 0), (FRONT_PAD, 0), (0, 0)))


def _attn_in_weights(w_in):
    gates = w_in[:, GATE_COL0:].reshape(w_in.shape[0], N_KV_HEADS, GROUP * 3)
    gates = jnp.pad(gates, ((0, 0), (0, 0), (0, LANES - GROUP * 3))).reshape(w_in.shape[0], -1)
    return jnp.concatenate([w_in[:, :GATE_COL0], gates], axis=1).astype(BF16)


def _attn_out_weights(w_out):
    w = w_out.reshape(N_HEADS, HEAD_DIM, w_out.shape[1])
    return jnp.pad(w, ((0, 0), (LANES - HEAD_DIM, 0), (0, 0))).reshape(N_HEADS * LANES, -1).astype(BF16)


def _nsa_prompt_mixer(xp, g, shift, scale, w_in, cw, bvz, nb, t):
    proj = _nm_matmul(xp, g, shift, scale, w_in, _row_tile(t, 512), NP_ATTN)
    kv6 = proj[:, N_Q_COLS:GATE_COL0].reshape(nb, t, N_KV_SLOTS + N_WIN_SLOTS, N_KV_HEADS, HEAD_DIM)
    cmp = _compress_prompt(proj, cw, nb, t)
    kvc = _pack_cmp(cmp)
    kck = jnp.pad(_pack_cmp(cmp, F32)[..., :HEAD_DIM], ((0, 0), (0, 0), (0, 0), (0, LANES - HEAD_DIM)))
    kvs = _pack_kv(kv6, 2)
    kvw = _pack_kv(kv6, N_KV_SLOTS)
    c = _nsa_consts(t)
    tb = jnp.stack([_toeplitz(bvz, c['idx0']), _toeplitz(bvz, c['idx1'])], axis=1)
    hi, lo = _split_bf16(_toeplitz(bvz, c['idxc']), 2)
    cbq = jnp.concatenate([jnp.zeros(hi.shape[:2] + (HEAD_DIM,), BF16), hi[..., :BAND_W], lo[..., :BAND_W]], axis=-1)
    o = _nsa_prompt(proj, kck, kvc, kvs, kvw, tb, cbq, c, nb, t)
    return o, kv6


SC_PAGES = 32
CG_PER_PAGE = N_KV_SLOTS * KV_ROW // LANES


def _cache_tiles(cache_kv):
    page = cache_kv.shape[2]
    assert page == LANES
    return jnp.transpose(cache_kv, (0, 1, 3, 4, 5, 2)).reshape(-1, page)


def _compress_sample_kernel(pt_ref, cache_ref, pos_ref, w1_ref, wbd_ref, w2bd_ref, o_ref, tbuf_ref, buf_ref,
                            sem_ref, p0_ref, p1_ref, *, n_pages, page, row_base):
    step = pl.program_id(0) * N_CG + pl.program_id(1)
    nsteps = pl.num_programs(0) * N_CG
    chunk_pages = min(SC_PAGES, n_pages // 2)
    nch = n_pages // chunk_pages
    hpc = chunk_pages * page // CMP_STRIDE
    nh = nch * hpc

    def copies(step_, ch, slot):
        b_, c_ = step_ // N_CG, step_ % N_CG
        out = []
        for p in range(chunk_pages):
            pg = pt_ref[b_, ch * chunk_pages + p]
            out.append(pltpu.make_async_copy(
                cache_ref.at[pl.ds(((row_base + pg) * CG_PER_PAGE + c_) * LANES, LANES), :],
                tbuf_ref.at[slot, p], sem_ref.at[slot]))
        return out

    @pl.when(step == 0)
    def _():
        for cp in copies(step, 0, 0):
            cp.start()

    for ch in range(nch):
        slot = ch % 2
        if ch + 1 < nch:
            for cp in copies(step, ch + 1, 1 - slot):
                cp.start()
        else:
            @pl.when(step + 1 < nsteps)
            def _():
                for cp in copies(step + 1, 0, 1 - slot):
                    cp.start()
        for cp in copies(step, ch, slot):
            cp.wait()

        def to_rows(p, _):
            buf_ref[pl.ds(pl.multiple_of(p * page, page), page), :] = tbuf_ref[slot, p].T
            return 0

        lax.fori_loop(0, chunk_pages, to_rows, 0)
        p0, p1 = _compress_parts(lambda s: buf_ref[pl.ds(s, hpc, stride=CMP_STRIDE), :], wbd_ref, hpc)
        p0_ref[ch * hpc:(ch + 1) * hpc] = p0
        p1_ref[ch * hpc:(ch + 1) * hpc] = p1

    o_ref[0, 0] = _compress_finish(p0_ref[...], p1_ref[...], pos_ref, w1_ref, w2bd_ref, nh)


def _compress_sample(page_table, cache_t, cw, page, row_base):
    nb, n_pages = page_table.shape
    chunk_pages = min(SC_PAGES, n_pages // 2)
    assert n_pages % (2 * chunk_pages) == 0
    nh = n_pages * page // CMP_STRIDE
    grid_spec = pltpu.PrefetchScalarGridSpec(
        num_scalar_prefetch=1,
        grid=(nb, N_CG),
        in_specs=[pl.BlockSpec(memory_space=pl.ANY)] + _compress_specs(cw, lambda b, c, pt: c),
        out_specs=pl.BlockSpec((1, 1, nh, LANES), lambda b, c, pt: (b, c, 0, 0)),
        scratch_shapes=[pltpu.VMEM((2, chunk_pages, LANES, page), F32),
                        pltpu.VMEM((chunk_pages * page, LANES), F32),
                        pltpu.SemaphoreType.DMA((2,)),
                        pltpu.VMEM((nh, CG_HIDDEN), F32),
                        pltpu.VMEM((nh, CG_HIDDEN), F32)])
    return pl.pallas_call(
        functools.partial(_compress_sample_kernel, n_pages=n_pages, page=page, row_base=row_base),
        grid_spec=grid_spec,
        out_shape=jax.ShapeDtypeStruct((nb, N_CG, nh, LANES), F32),
        compiler_params=_cparams(("arbitrary", "arbitrary")),
    )(page_table, cache_t, *cw)


def _softmax_with_new_key(s, s_new):
    m = jnp.maximum(jnp.max(s, axis=-1, keepdims=True), s_new)
    e = jnp.exp(s - m)
    e_new = jnp.exp(s_new - m)
    inv = 1.0 / jnp.maximum(jnp.sum(e, axis=-1, keepdims=True) + e_new, 1e-30)
    return e * inv, e_new * inv


def _nsa_sample_kernel(pt_ref, qs_ref, q_ref, gl_ref, new_ref, ncol_ref, kvc_ref, cache_ref, win_ref, cbs_ref,
                       sbz_ref, wb_ref, b0_ref, as_ref, o_ref, wout_ref, gk_ref, gv_ref, sem_ref, idx_ref,
                       *, past, page, row_base, n_pick):
    b = pl.program_id(0)
    blk_per_page = page // SEL_BLOCK
    nbs = past // SEL_BLOCK
    ncs = kvc_ref.shape[2]
    row = lax.broadcasted_iota(jnp.int32, (SUB, 1), 0)

    cmp_ok = lax.broadcasted_iota(jnp.int32, (SUB, ncs), 1) * CMP_STRIDE + (CMP_BLOCK - 1) <= past
    imp = jnp.zeros((SUB, ncs), F32)
    o_cmp = []
    for k in range(N_KV_HEADS):
        kc = kvc_ref[0, k]
        s = lax.dot_general(qs_ref[0, k], kc, _NT, preferred_element_type=F32) + cbs_ref[k]
        m = jnp.max(jnp.where(cmp_ok, s, NEG), axis=-1, keepdims=True)
        e = jnp.where(cmp_ok, jnp.exp(s - m), 0.0)
        p = e / jnp.maximum(jnp.sum(e, axis=-1, keepdims=True), 1e-30)
        o_cmp.append(jnp.dot(p.astype(BF16), kc, preferred_element_type=F32))
        imp_k = p[0:1]
        for g in range(1, GROUP):
            imp_k = imp_k + p[g:g + 1]
        imp = imp + jnp.where(row == k, imp_k, 0.0)
    bs = sum(jnp.dot(part, as_ref[...], preferred_element_type=F32) for part in _split_bf16(imp, 3))

    lane_b = lax.broadcasted_iota(jnp.int32, (SUB, nbs), 1)
    lane_f = lane_b.astype(F32)
    score = bs + jnp.where((lane_b == 0) | (lane_b == nbs - 1), FORCE_BONUS, 0.0)

    def gathers(k, r):
        j = idx_ref[k * n_pick + r]
        pg = pt_ref[b, j // blk_per_page]
        tile0 = (row_base + pg) * N_KV_SLOTS
        return [pltpu.make_async_copy(
            cache_ref.at[pl.ds(((tile0 + slot) * N_KV_HEADS + k) * HEAD_DIM, HEAD_DIM), :],
            buf.at[k * n_pick + r], sem_ref.at[0]) for slot, buf in ((2, gk_ref), (3, gv_ref))]

    for r in range(n_pick):
        m = jnp.max(score, axis=-1, keepdims=True)
        pick = jnp.min(jnp.where(score == m, lane_f, 1e9), axis=-1, keepdims=True)
        score = jnp.where(lane_f == pick, -jnp.inf, score)
        for k in range(N_KV_HEADS):
            idx_ref[k * n_pick + r] = jnp.sum(jnp.where(row == k, pick, 0.0)).astype(jnp.int32)
            for cp in gathers(k, r):
                cp.start()

    def new_row(k, j):
        return new_ref[0, k][j:j + 1].astype(BF16).astype(F32)

    tok = lax.broadcasted_iota(jnp.int32, (1, WINDOW), 1)
    o_win = []
    for k in range(N_KV_HEADS):
        for slot in range(N_WIN_SLOTS):
            shifted = pltpu.roll(win_ref[0, slot, k], WINDOW - 1, axis=1)
            wout_ref[0, slot, k] = jnp.where(tok == WINDOW - 1, ncol_ref[0, slot, k], shifted)
        q = q_ref[0, k]
        s = jnp.dot(q, win_ref[0, 0, k].astype(BF16), preferred_element_type=F32) + wb_ref[k]
        s = jnp.where(tok >= 1, s, NEG)
        s_new = jnp.sum(q.astype(F32) * new_row(k, 2), axis=-1, keepdims=True) + b0_ref[k][:, 0:1]
        p, p_new = _softmax_with_new_key(s, s_new)
        o_win.append(lax.dot_general(p.astype(BF16), win_ref[0, 1, k].astype(BF16), _NT,
                                     preferred_element_type=F32) + p_new * new_row(k, 3))

    for k in range(N_KV_HEADS):
        for r in range(n_pick):
            for cp in gathers(k, r):
                cp.wait()

    upper = lax.broadcasted_iota(jnp.int32, (1, LANES), 1) // SEL_BLOCK
    for k in range(N_KV_HEADS):
        q = q_ref[0, k]
        tiles = []
        for r in range(n_pick):
            j = idx_ref[k * n_pick + r]
            near = j - (nbs - NEAR // SEL_BLOCK)
            s = jnp.dot(q, gk_ref[k * n_pick + r].astype(BF16), preferred_element_type=F32)
            s = s + sbz_ref[k, jnp.where(near >= 0, near, NEAR // SEL_BLOCK)]
            tiles.append(jnp.where(upper == j % blk_per_page, s, NEG))
        s_new = jnp.sum(q.astype(F32) * new_row(k, 0), axis=-1, keepdims=True) + b0_ref[k][:, 0:1]
        p, p_new = _softmax_with_new_key(jnp.concatenate(tiles, axis=1), s_new)
        o_sel = p_new * new_row(k, 1)
        for r in range(n_pick):
            o_sel = o_sel + lax.dot_general(p[:, r * LANES:(r + 1) * LANES].astype(BF16),
                                            gv_ref[k * n_pick + r].astype(BF16), _NT, preferred_element_type=F32)
        g_cmp, g_sel, g_win = (jax.nn.sigmoid(gl_ref[0, k, j])[:, :HEAD_DIM] for j in range(3))
        o_ref[0, k] = g_cmp * o_cmp[k][:, HEAD_DIM:] + g_sel * o_sel + g_win * o_win[k]


def _bias_rows(bvz, idx):
    t = jnp.take(bvz, jnp.asarray(idx, jnp.int32), axis=0).reshape(len(idx), N_KV_HEADS, GROUP)
    t = jnp.transpose(t, (1, 2, 0))
    return jnp.pad(t, ((0, 0), (0, SUB - GROUP), (0, 0)))


def _nsa_sample(proj_s, kvc, page_table, cache_t, win_t, bvz, past, page, row_base):
    nb = proj_s.shape[0]
    n_pick = N_SELECT - 1
    nbs = past // SEL_BLOCK
    ncs = past // CMP_STRIDE
    n_near = NEAR // SEL_BLOCK
    blk_per_page = page // SEL_BLOCK
    assert past % page == 0 and page == LANES and nbs > n_pick + n_near and (nbs - n_near) % blk_per_page == 0
    assert win_t.shape[-1] == WINDOW < past
    q = (proj_s[:, :N_Q_COLS] * HEAD_DIM ** -0.5).astype(BF16).reshape(nb, N_KV_HEADS, GROUP, HEAD_DIM)
    q = jnp.pad(q, ((0, 0), (0, 0), (0, SUB - GROUP), (0, 0)))
    qs = jnp.pad(q, ((0, 0), (0, 0), (0, 0), (0, LANES - HEAD_DIM)))
    gl = proj_s[:, GATE_COL0:].reshape(nb, N_KV_HEADS, LANES)[:, :, :GROUP * 3].reshape(nb, N_KV_HEADS, GROUP, 3)
    gl = jnp.pad(jnp.transpose(gl, (0, 1, 3, 2)), ((0, 0), (0, 0), (0, 0), (0, SUB - GROUP)))
    gl = jnp.broadcast_to(gl[..., None], gl.shape + (LANES,))
    new = proj_s[:, N_Q_COLS + 2 * KV_ROW:GATE_COL0].reshape(nb, 4, N_KV_HEADS, HEAD_DIM)
    new_rows = jnp.transpose(new, (0, 2, 1, 3))
    new_cols = new[:, 2:].reshape(nb, N_WIN_SLOTS, N_KV_HEADS, HEAD_DIM, 1)

    dc = past - (np.arange(ncs) * CMP_STRIDE + CMP_BLOCK - 1)
    cbs = _bias_rows(bvz, np.where((dc >= 0) & (dc < NEAR), dc, NEAR))
    lane = np.arange(LANES)
    sb = []
    for jj in range(n_near + 1):
        d = NEAR - SEL_BLOCK * jj - lane % SEL_BLOCK
        ok = (lane // SEL_BLOCK == jj % blk_per_page) & (jj < n_near) & (d < NEAR)
        sb.append(_bias_rows(bvz, np.where(ok, d, NEAR)))
    sbz = jnp.stack(sb, axis=1)
    dw = WINDOW - np.arange(WINDOW)
    wb = _bias_rows(bvz, np.where(dw < NEAR, dw, NEAR))
    b0 = _bias_rows(bvz, np.zeros(LANES, np.int64))
    a_s = np.zeros((ncs, nbs), np.float32)
    for j in range(nbs):
        for n in range(SEL_RATIO * j - (CMP_RATIO - 1), SEL_RATIO * (j + 1)):
            if 0 <= n < ncs - CMP_RATIO + 1:
                a_s[n, j] = 1.0
    a_s = jnp.asarray(a_s, BF16)

    per_b = lambda a: pl.BlockSpec((1,) + a.shape[1:], lambda b, pt: (b,) + (0,) * (a.ndim - 1))
    full = lambda a: pl.BlockSpec(a.shape, lambda b, pt: (0,) * a.ndim)
    grid_spec = pltpu.PrefetchScalarGridSpec(
        num_scalar_prefetch=1,
        grid=(nb,),
        in_specs=[per_b(qs), per_b(q), per_b(gl), per_b(new_rows), per_b(new_cols), per_b(kvc),
                  pl.BlockSpec(memory_space=pl.ANY), per_b(win_t),
                  full(cbs), full(sbz), full(wb), full(b0), full(a_s)],
        out_specs=[pl.BlockSpec((1, N_KV_HEADS, SUB, HEAD_DIM), lambda b, pt: (b, 0, 0, 0)), per_b(win_t)],
        scratch_shapes=[pltpu.VMEM((N_KV_HEADS * n_pick, HEAD_DIM, page), F32),
                        pltpu.VMEM((N_KV_HEADS * n_pick, HEAD_DIM, page), F32),
                        pltpu.SemaphoreType.DMA((1,)),
                        pltpu.SMEM((N_KV_HEADS * n_pick,), jnp.int32)])
    o, win_out = pl.pallas_call(
        functools.partial(_nsa_sample_kernel, past=past, page=page, row_base=row_base, n_pick=n_pick),
        grid_spec=grid_spec,
        out_shape=[jax.ShapeDtypeStruct((nb, N_KV_HEADS, SUB, HEAD_DIM), F32),
                   jax.ShapeDtypeStruct(win_t.shape, F32)],
        compiler_params=_cparams(("arbitrary",)),
    )(page_table, qs, q, gl, new_rows, new_cols, kvc, cache_t, win_t, cbs, sbz, wb, b0, a_s)
    return o[:, :, :GROUP].reshape(nb, N_Q_COLS), win_out


NP_SSM = -(-(SSM_CONV_DIM + SSM_D_INNER + SSM_HEADS) // LANES) * LANES
SSM_COL_TILE = NP_SSM // 7
assert SSM_COL_TILE % LANES == 0 and SSM_COL_TILE * 7 == NP_SSM
N_BC = SSM_GROUPS * SSM_STATE
HEADS_PER_GROUP = SSM_HEADS // SSM_GROUPS
PAIR = LANES // SSM_HEAD_DIM
N_PAIRS = SSM_HEADS // PAIR
CONV_PAD = 8


def _softplus(x):
    return jnp.maximum(x, 0.0) + jnp.log1p(jnp.exp(-jnp.abs(x)))


def _cumsum_rows(x):
    n = x.shape[0]
    row = lax.broadcasted_iota(jnp.int32, x.shape, 0)
    s = 1
    while s < n:
        x = x + jnp.where(row >= s, pltpu.roll(x, s, axis=0), 0.0)
        s *= 2
    return x


def _ssd_kernel(xbc_ref, z_ref, dt_ref, h0_ref, cinit_ref, cw_ref, cb_ref, dtb_ref, alog_ref, dsk_ref, ng_ref,
                y_ref, hout_ref, xs_ref, act_ref, ybuf_ref, h_ref, *, nc):
    c = pl.program_id(1)
    q = SSM_CHUNK

    @pl.when(c == 0)
    def _():
        xs_ref[0:CONV_PAD] = cinit_ref[0]
        h_ref[...] = h0_ref[0]

    xs_ref[CONV_PAD:CONV_PAD + q] = xbc_ref[...]
    conv = cb_ref[...] + cw_ref[0:1] * xs_ref[CONV_PAD - 3:CONV_PAD - 3 + q]
    for k in range(1, SSM_CONV):
        conv = conv + cw_ref[k:k + 1] * xs_ref[CONV_PAD - 3 + k:CONV_PAD - 3 + k + q]
    xs_ref[0:CONV_PAD] = xs_ref[q:q + CONV_PAD]
    act_ref[...] = _silu(conv)

    dt = _softplus(dt_ref[...] + dtb_ref[...])
    acum = _cumsum_rows(dt * (-jnp.exp(alog_ref[...])))
    acum_t = acum.T
    dt_t = dt.T
    last = acum[q - 1:q, :]
    ea = jnp.exp(acum)
    te = jnp.exp(last - acum) * dt
    cd = jnp.exp(last)
    ii = lax.broadcasted_iota(jnp.int32, (q, q), 0)
    jj = lax.broadcasted_iota(jnp.int32, (q, q), 1)
    tri = ii >= jj
    low = jj < SSM_HEAD_DIM
    low_rows = ii < SSM_HEAD_DIM

    def col(a, h):
        return a[:, h:h + 1]

    for g in range(SSM_GROUPS):
        bg = act_ref[:, SSM_D_INNER + g * SSM_STATE:SSM_D_INNER + (g + 1) * SSM_STATE].astype(BF16)
        cg = act_ref[:, SSM_D_INNER + N_BC + g * SSM_STATE:SSM_D_INNER + N_BC + (g + 1) * SSM_STATE].astype(BF16)
        cbg = lax.dot_general(cg, bg, _NT, preferred_element_type=F32)
        for pr in range(HEADS_PER_GROUP // PAIR):
            k = g * (HEADS_PER_GROUP // PAIR) + pr
            ha, hb = PAIR * k, PAIR * k + 1
            xp = act_ref[:, k * LANES:(k + 1) * LANES]
            xpb = xp.astype(BF16)
            ys = []
            for h in (ha, hb):
                decay = jnp.exp(jnp.where(tri, col(acum, h) - acum_t[h:h + 1, :], NEG))
                w = cbg * decay * dt_t[h:h + 1, :]
                ys.append(jnp.dot(w.astype(BF16), xpb, preferred_element_type=F32))
            y = jnp.where(low, ys[0], ys[1])
            xs_pair = xp * jnp.where(low, col(te, ha), col(te, hb))
            st = jnp.dot(xs_pair.T.astype(BF16), bg, preferred_element_type=F32)
            hprev = h_ref[k]
            yoff = lax.dot_general(cg, hprev.astype(BF16), _NT, preferred_element_type=F32)
            y = y + yoff * jnp.where(low, col(ea, ha), col(ea, hb)) + dsk_ref[:, k * LANES:(k + 1) * LANES] * xp
            h_ref[k] = hprev * jnp.where(low_rows, cd[:, ha:ha + 1], cd[:, hb:hb + 1]) + st
            ybuf_ref[:, k * LANES:(k + 1) * LANES] = y

    yz = ybuf_ref[...] * _silu(z_ref[...])
    gw = SSM_D_INNER // SSM_GROUPS
    outs = []
    for g in range(SSM_GROUPS):
        seg = yz[:, g * gw:(g + 1) * gw]
        outs.append(seg * lax.rsqrt(jnp.mean(seg * seg, axis=-1, keepdims=True) + EPS))
    y_ref[...] = (jnp.concatenate(outs, axis=1) * ng_ref[...]).astype(y_ref.dtype)

    @pl.when(c == nc - 1)
    def _():
        hout_ref[0] = h_ref[...]


def _ssd(proj, h0, cinit, sw, nb, t):
    q = SSM_CHUNK
    nc = t // q
    vec = lambda a: pl.BlockSpec(a.shape, lambda b, c: (0, 0))
    return pl.pallas_call(
        functools.partial(_ssd_kernel, nc=nc),
        grid=(nb, nc),
        in_specs=[pl.BlockSpec((q, SSM_CONV_DIM), lambda b, c: (b * nc + c, 0)),
                  pl.BlockSpec((q, SSM_D_INNER), lambda b, c: (b * nc + c, SSM_CONV_DIM // SSM_D_INNER)),
                  pl.BlockSpec((q, LANES), lambda b, c: (b * nc + c, (SSM_CONV_DIM + SSM_D_INNER) // LANES)),
                  pl.BlockSpec((1,) + h0.shape[1:], lambda b, c: (b, 0, 0, 0)),
                  pl.BlockSpec((1,) + cinit.shape[1:], lambda b, c: (b, 0, 0)),
                  vec(sw['conv_w']), vec(sw['conv_b']), vec(sw['dt_bias']), vec(sw['a_log']), vec(sw['d_lane']),
                  vec(sw['norm_g'])],
        out_specs=[pl.BlockSpec((q, SSM_D_INNER), lambda b, c: (b * nc + c, 0)),
                   pl.BlockSpec((1,) + h0.shape[1:], lambda b, c: (b, 0, 0, 0))],
        out_shape=[jax.ShapeDtypeStruct((nb * t, SSM_D_INNER), BF16),
                   jax.ShapeDtypeStruct(h0.shape, F32)],
        scratch_shapes=[pltpu.VMEM((CONV_PAD + q, SSM_CONV_DIM), F32),
                        pltpu.VMEM((q, SSM_CONV_DIM), F32),
                        pltpu.VMEM((q, SSM_D_INNER), F32),
                        pltpu.VMEM(h0.shape[1:], F32)],
        compiler_params=_cparams(("arbitrary", "arbitrary")),
    )(proj, proj, proj, h0, cinit, sw['conv_w'], sw['conv_b'], sw['dt_bias'], sw['a_log'], sw['d_lane'],
      sw['norm_g'])


def _ssm_sample_step(proj_s, state_ssm, state_conv, sw):
    nb = proj_s.shape[0]
    q = SSM_CHUNK
    xbc_new = proj_s[:, :SSM_CONV_DIM]
    rows = jnp.zeros((nb, q, NP_SSM), F32)
    rows = rows.at[:, :, SSM_CONV_DIM + SSM_D_INNER:].set(NEG)
    rows = rows.at[:, q - SSM_CONV:q - 1, :SSM_CONV_DIM].set(state_conv.astype(F32))
    rows = rows.at[:, q - 1].set(proj_s)
    h0 = state_ssm.astype(F32).reshape(nb, N_PAIRS, LANES, SSM_STATE)
    cinit = jnp.zeros((nb, CONV_PAD, SSM_CONV_DIM), F32)
    yn, hfin = _ssd(rows.reshape(nb * q, NP_SSM), h0, cinit, sw, nb, q)
    conv_new = jnp.concatenate([state_conv[:, 1:].astype(F32), xbc_new[:, None]], axis=1)
    return yn.reshape(nb, q, SSM_D_INNER)[:, q - 1], hfin.reshape(state_ssm.shape), conv_new


def _ssm_weights(w_in, conv_w, conv_b, dt_bias, a_log, d_skip, norm_g):
    z_w = w_in[:, :SSM_D_INNER]
    xbc_w = w_in[:, SSM_D_INNER:SSM_D_INNER + SSM_CONV_DIM]
    dt_w = w_in[:, SSM_D_INNER + SSM_CONV_DIM:]
    pad = NP_SSM - w_in.shape[1]
    w = jnp.concatenate([xbc_w, z_w, dt_w, jnp.zeros((w_in.shape[0], pad), w_in.dtype)], axis=1).astype(BF16)
    lane_pad = lambda v: jnp.zeros((1, LANES), F32).at[0, :SSM_HEADS].set(v.astype(F32))
    return dict(w_in=w, conv_w=conv_w.astype(F32), conv_b=conv_b.astype(F32).reshape(1, -1),
                dt_bias=lane_pad(dt_bias), a_log=lane_pad(a_log),
                d_lane=jnp.repeat(d_skip.astype(F32), SSM_HEAD_DIM).reshape(1, -1),
                norm_g=norm_g.astype(F32).reshape(1, -1))


ROW_TILE = 512
FF_TILE = 1024


def kernel(x_prompt, x_sample, cache_kv, cache_win, state_ssm, state_conv, page_table, c_prompt, c_sample, rel_bias,
           ada_w, ada_b, norm_g, mlp_w1, mlp_w2, attn_w_in, attn_w_out, cmp_pos, cmp_w1, cmp_w2, ssm_w_in,
           ssm_conv_w, ssm_conv_b, ssm_dt_bias, ssm_a_log, ssm_d, ssm_norm_g, ssm_w_out):
    nb, t, d = x_prompt.shape
    db = x_sample.shape[0]
    assert x_sample.shape[1] == 1 and t % SSM_CHUNK == 0 and t % Q_BLOCK == 0
    n_pool, page = cache_kv.shape[1], cache_kv.shape[2]
    past = page_table.shape[1] * page
    depth = ada_w.shape[0]
    tm = _row_tile(t, ROW_TILE)

    xp = x_prompt.reshape(nb * t, d).astype(F32)
    xs = x_sample.reshape(db, d).astype(F32)
    c_all = jnp.concatenate([c_prompt, c_sample], axis=0).astype(F32)
    c_all = jnp.pad(c_all, ((0, (-c_all.shape[0]) % SUB), (0, 0)))
    bvz = _bias_vector(rel_bias)
    cache_t = _cache_tiles(cache_kv)
    win_t = jnp.transpose(cache_win, (0, 1, 3, 4, 5, 2))

    kv_p, win_p, ssm_p, conv_p, kv_s, win_s, ssm_s, conv_s = ([] for _ in range(8))
    for i in range(depth):
        mod = _ada(c_all, ada_w[i], ada_b[i])
        mp = [mod[:nb, j * d:(j + 1) * d].reshape(nb, 1, d) for j in range(N_MOD)]
        ms = [mod[nb:nb + db, j * d:(j + 1) * d].reshape(1, db, d) for j in range(N_MOD)]
        g = norm_g[i].astype(F32)
        if i % 2 == 0:
            a = i // 2
            w_in = _attn_in_weights(attn_w_in[a])
            w_out = attn_w_out[a].astype(BF16)
            cw = _compress_weights(cmp_pos[a], cmp_w1[a], cmp_w2[a])
            o_p, kv6 = _nsa_prompt_mixer(xp, g[0], mp[0], mp[1], w_in, cw, bvz, nb, t)
            kv_p.append(kv6[:, :, :N_KV_SLOTS])
            win_p.append(kv6[:, -min(WINDOW, t):, N_KV_SLOTS:])
            xp = _heads_out(o_p, _attn_out_weights(attn_w_out[a]), g[1], mp[2], xp, tm)

            proj_s = _nm_matmul(xs, g[0], ms[0], ms[1], w_in, db, NP_ATTN)
            kvc_s = _pack_cmp(_compress_sample(page_table, cache_t, cw, page, a * n_pool))
            o_s, win_new = _nsa_sample(proj_s, kvc_s, page_table, cache_t, win_t[a], bvz, past, page, a * n_pool)
            kv_s.append(proj_s[:, N_Q_COLS:N_Q_COLS + N_KV_SLOTS * KV_ROW].reshape(
                db, 1, N_KV_SLOTS, N_KV_HEADS, HEAD_DIM))
            win_s.append(jnp.transpose(win_new, (0, 4, 1, 2, 3)))
            xs = _mm_norm_res(o_s, w_out, g[1], ms[2], xs, db)
        else:
            m = i // 2
            sw = _ssm_weights(ssm_w_in[m], ssm_conv_w[m], ssm_conv_b[m], ssm_dt_bias[m], ssm_a_log[m], ssm_d[m],
                              ssm_norm_g[m])
            w_out = ssm_w_out[m].astype(BF16)
            proj = _nm_matmul(xp, g[0], mp[0], mp[1], sw['w_in'], tm, SSM_COL_TILE)
            h0 = jnp.zeros((nb, N_PAIRS, LANES, SSM_STATE), F32)
            cinit = jnp.zeros((nb, CONV_PAD, SSM_CONV_DIM), F32)
            yn, hfin = _ssd(proj, h0, cinit, sw, nb, t)
            ssm_p.append(hfin.reshape(nb, SSM_HEADS, SSM_HEAD_DIM, SSM_STATE).astype(state_ssm.dtype))
            conv_p.append(proj.reshape(nb, t, NP_SSM)[:, t - (SSM_CONV - 1):, :SSM_CONV_DIM])
            xp = _mm_norm_res(yn, w_out, g[1], mp[2], xp, tm)

            proj_s = _nm_matmul(xs, g[0], ms[0], ms[1], sw['w_in'], db, SSM_COL_TILE)
            yn_s, h_s, conv_new = _ssm_sample_step(proj_s, state_ssm[m], state_conv[m], sw)
            ssm_s.append(h_s.astype(state_ssm.dtype))
            conv_s.append(conv_new)
            xs = _mm_norm_res(yn_s, w_out, g[1], ms[2], xs, db)
        w1 = mlp_w1[i].astype(BF16)
        w2 = mlp_w2[i].astype(BF16)
        xp = _mlp(xp, g[2], mp[3], mp[4], w1, w2, g[3], mp[5], tm, FF_TILE)
        xs = _mlp(xs, g[2], ms[3], ms[4], w1, w2, g[3], ms[5], db, FF_TILE)
    return (xp.reshape(nb, t, d), xs.reshape(db, 1, d), jnp.stack(kv_p), jnp.stack(win_p), jnp.stack(ssm_p),
            jnp.stack(conv_p), jnp.stack(kv_s), jnp.stack(win_s), jnp.stack(ssm_s), jnp.stack(conv_s))
```

```python
import functools
import math

import numpy as np
import jax
import jax.numpy as jnp
from jax import lax
from jax.experimental import pallas as pl
from jax.experimental.pallas import tpu as pltpu

F32 = jnp.float32
BF16 = jnp.bfloat16

D_MODEL = 1024
N_HEADS = 16
HEAD_DIM = 64
N_KV_HEADS = 4
GROUP = N_HEADS // N_KV_HEADS
CMP_BLOCK = 32
CMP_STRIDE = 16
CMP_RATIO = CMP_BLOCK // CMP_STRIDE
CMP_HIDDEN = 2 * HEAD_DIM
SEL_BLOCK = 64
SEL_RATIO = SEL_BLOCK // CMP_STRIDE
N_SELECT = 16
WINDOW = 512
FORCE_BONUS = 1e4
Q_BLOCK = 128
N_KV_SLOTS = 4
N_WIN_SLOTS = 2
N_Q_COLS = N_HEADS * HEAD_DIM
N_KV_COLS = (N_KV_SLOTS + N_WIN_SLOTS) * N_KV_HEADS * HEAD_DIM
N_GATE_COLS = 3 * N_HEADS
KV_ROW = N_KV_HEADS * HEAD_DIM
N_BUCKETS = 32
MAX_DISTANCE = 128
SSM_D_INNER = 2 * D_MODEL
SSM_HEAD_DIM = 64
SSM_HEADS = SSM_D_INNER // SSM_HEAD_DIM
SSM_GROUPS = 8
SSM_STATE = 128
SSM_CONV = 4
SSM_CONV_DIM = SSM_D_INNER + 2 * SSM_GROUPS * SSM_STATE
SSM_CHUNK = 128
D_FF = 4 * D_MODEL
N_MOD = 6
EPS = 1e-6

LANES = 128
NEG = -1e30
LOG2E = math.log2(math.e)
VMEM_LIMIT = 56 * 1024 * 1024
NEAR = 2 * LANES

_NT = (((1,), (1,)), ((), ()))


def _cparams(sem):
    return pltpu.CompilerParams(dimension_semantics=sem, vmem_limit_bytes=VMEM_LIMIT)


def _rms(x, g):
    return x * lax.rsqrt(jnp.mean(x * x, axis=-1, keepdims=True) + EPS) * g


def _silu(x):
    return x * jax.nn.sigmoid(x)


def _split_bf16(x, n):
    parts = []
    for _ in range(n - 1):
        p = x.astype(BF16)
        parts.append(p)
        x = x - p.astype(F32)
    parts.append(x.astype(BF16))
    return parts


def _ada_kernel(c_ref, w_ref, b_ref, o_ref):
    s = _silu(c_ref[...]).astype(BF16)
    o_ref[...] = jnp.dot(s, w_ref[...].astype(BF16), preferred_element_type=F32) + b_ref[...]


def _ada(c, w, b, tn=1024):
    m, d = c.shape
    n = w.shape[1]
    return pl.pallas_call(
        _ada_kernel,
        grid=(n // tn,),
        in_specs=[pl.BlockSpec((m, d), lambda j: (0, 0)),
                  pl.BlockSpec((d, tn), lambda j: (0, j)),
                  pl.BlockSpec((1, tn), lambda j: (0, j))],
        out_specs=pl.BlockSpec((m, tn), lambda j: (0, j)),
        out_shape=jax.ShapeDtypeStruct((m, n), F32),
        compiler_params=_cparams(("arbitrary",)),
    )(c, w, b.reshape(1, n))


def _nm_matmul_kernel(x_ref, g_ref, sh_ref, sc_ref, w_ref, o_ref, h_ref):
    @pl.when(pl.program_id(1) == 0)
    def _():
        h = _rms(x_ref[...], g_ref[...]) * (1.0 + sc_ref[0]) + sh_ref[0]
        h_ref[...] = h.astype(BF16)

    o_ref[...] = jnp.dot(h_ref[...], w_ref[...], preferred_element_type=F32)


def _nm_matmul(x, g, shift, scale, w, tm, tn):
    m, d = x.shape
    n = w.shape[1]
    nb = shift.shape[0]
    tpb = m // nb // tm
    mod_spec = pl.BlockSpec((1,) + shift.shape[1:], lambda i, j: (i // tpb, 0, 0))
    return pl.pallas_call(
        _nm_matmul_kernel,
        grid=(m // tm, n // tn),
        in_specs=[pl.BlockSpec((tm, d), lambda i, j: (i, 0)),
                  pl.BlockSpec((1, d), lambda i, j: (0, 0)),
                  mod_spec, mod_spec,
                  pl.BlockSpec((d, tn), lambda i, j: (0, j))],
        out_specs=pl.BlockSpec((tm, tn), lambda i, j: (i, j)),
        out_shape=jax.ShapeDtypeStruct((m, n), F32),
        scratch_shapes=[pltpu.VMEM((tm, d), BF16)],
        compiler_params=_cparams(("arbitrary", "arbitrary")),
    )(x, g.reshape(1, d), shift, scale, w)


def _mm_norm_res_kernel(a_ref, w_ref, g_ref, gate_ref, x_ref, o_ref):
    y = jnp.dot(a_ref[...].astype(BF16), w_ref[...], preferred_element_type=F32)
    o_ref[...] = x_ref[...] + gate_ref[0] * _rms(y, g_ref[...])


def _mm_norm_res(a, w, g, gate, x, tm):
    m, k = a.shape
    d = w.shape[1]
    nb = gate.shape[0]
    tpb = m // nb // tm
    return pl.pallas_call(
        _mm_norm_res_kernel,
        grid=(m // tm,),
        in_specs=[pl.BlockSpec((tm, k), lambda i: (i, 0)),
                  pl.BlockSpec((k, d), lambda i: (0, 0)),
                  pl.BlockSpec((1, d), lambda i: (0, 0)),
                  pl.BlockSpec((1,) + gate.shape[1:], lambda i: (i // tpb, 0, 0)),
                  pl.BlockSpec((tm, d), lambda i: (i, 0))],
        out_specs=pl.BlockSpec((tm, d), lambda i: (i, 0)),
        out_shape=jax.ShapeDtypeStruct((m, d), F32),
        compiler_params=_cparams(("arbitrary",)),
    )(a, w, g.reshape(1, d), gate, x)


def _mlp_kernel(x_ref, g1_ref, sh_ref, sc_ref, w1_ref, w2_ref, g2_ref, gate_ref, o_ref, h_ref, acc_ref):
    c = pl.program_id(1)

    @pl.when(c == 0)
    def _():
        h = _rms(x_ref[...], g1_ref[...]) * (1.0 + sc_ref[0]) + sh_ref[0]
        h_ref[...] = h.astype(BF16)
        acc_ref[...] = jnp.zeros_like(acc_ref)

    a = jnp.dot(h_ref[...], w1_ref[...], preferred_element_type=F32)
    a = jnp.square(jnp.maximum(a, 0.0)).astype(BF16)
    acc_ref[...] += jnp.dot(a, w2_ref[...], preferred_element_type=F32)

    @pl.when(c == pl.num_programs(1) - 1)
    def _():
        o_ref[...] = x_ref[...] + gate_ref[0] * _rms(acc_ref[...], g2_ref[...])


def _mlp(x, g1, shift, scale, w1, w2, g2, gate, tm, tf):
    m, d = x.shape
    f = w1.shape[1]
    nb = shift.shape[0]
    tpb = m // nb // tm
    mod_spec = pl.BlockSpec((1,) + shift.shape[1:], lambda i, c: (i // tpb, 0, 0))
    vec_spec = pl.BlockSpec((1, d), lambda i, c: (0, 0))
    return pl.pallas_call(
        _mlp_kernel,
        grid=(m // tm, f // tf),
        in_specs=[pl.BlockSpec((tm, d), lambda i, c: (i, 0)), vec_spec, mod_spec, mod_spec,
                  pl.BlockSpec((d, tf), lambda i, c: (0, c)),
                  pl.BlockSpec((tf, d), lambda i, c: (c, 0)),
                  vec_spec, mod_spec],
        out_specs=pl.BlockSpec((tm, d), lambda i, c: (i, 0)),
        out_shape=jax.ShapeDtypeStruct((m, d), F32),
        scratch_shapes=[pltpu.VMEM((tm, d), BF16), pltpu.VMEM((tm, d), F32)],
        compiler_params=_cparams(("arbitrary", "arbitrary")),
    )(x, g1.reshape(1, d), shift, scale, w1, w2, g2.reshape(1, d), gate)


def _bias_kernel(oh_ref, t_ref, o_ref):
    t = t_ref[...]
    t = t - t[N_BUCKETS - 1:N_BUCKETS, :]
    oh = oh_ref[...]
    o_ref[...] = sum(jnp.dot(oh, p, preferred_element_type=F32) for p in _split_bf16(t, 3))


def _rel_bucket_np(dist):
    n = np.maximum(dist, 0)
    exact = N_BUCKETS // 2
    nf = np.maximum(n, 1).astype(np.float32)
    large = exact + (np.log(nf / exact) / math.log(MAX_DISTANCE / exact) * (N_BUCKETS - exact)).astype(np.int32)
    return np.where(n < exact, n, np.minimum(large, N_BUCKETS - 1))


def _bias_vector(rel_bias):
    assert _rel_bucket_np(np.array([NEAR - 1]))[0] == N_BUCKETS - 1 == _rel_bucket_np(np.array([MAX_DISTANCE]))[0]
    oh = np.zeros((NEAR + 8, LANES), np.float32)
    oh[np.arange(NEAR), _rel_bucket_np(np.arange(NEAR))] = 1.0
    oh[NEAR:, N_BUCKETS - 1] = 1.0
    table = jnp.zeros((LANES, LANES), F32).at[:N_BUCKETS, :N_HEADS].set(rel_bias.astype(F32))
    bv = pl.pallas_call(
        _bias_kernel,
        out_shape=jax.ShapeDtypeStruct((NEAR + 8, LANES), F32),
    )(jnp.asarray(oh, BF16), table)
    return bv[:NEAR + 1, :N_HEADS]


def _toeplitz(bvz, idx):
    r, c = idx.shape
    t = jnp.take(bvz, jnp.asarray(idx.reshape(-1), jnp.int32), axis=0).reshape(r, c, N_KV_HEADS, GROUP)
    return jnp.transpose(t, (2, 3, 0, 1)).reshape(N_KV_HEADS, GROUP * r, c)


SUB = 8
HEADS_PER_CG = LANES // HEAD_DIM
N_CG = 2 * KV_ROW // LANES
CG_PER_SLOT = KV_ROW // LANES
CG_HIDDEN = HEADS_PER_CG * CMP_HIDDEN


def _compress_parts(rows_s, wbd_ref, nh):
    part0 = jnp.zeros((nh, CG_HIDDEN), F32)
    part1 = jnp.zeros((nh, CG_HIDDEN), F32)
    for s in range(CMP_STRIDE):
        xs = rows_s(s).astype(BF16)
        part0 = part0 + jnp.dot(xs, wbd_ref[0, 0, s], preferred_element_type=F32)
        part1 = part1 + jnp.dot(xs, wbd_ref[0, 1, s], preferred_element_type=F32)
    return part0, part1


def _compress_finish(part0, part1, pos_ref, w1_ref, w2bd_ref, nh):
    pre0 = jnp.dot(pos_ref[0].astype(BF16), w1_ref[0].astype(BF16), preferred_element_type=F32)[0:1]
    pre = jnp.concatenate([pre0] * HEADS_PER_CG, axis=1) + part0 + pltpu.roll(part1, nh - 1, axis=0)
    return jnp.dot(_silu(pre).astype(BF16), w2bd_ref[0], preferred_element_type=F32)


def _compress_kernel(x_ref, pos_ref, w1_ref, wbd_ref, w2bd_ref, o_ref, *, nh):
    part0, part1 = _compress_parts(lambda s: x_ref[pl.ds(s, nh, stride=CMP_STRIDE), :], wbd_ref, nh)
    o_ref[0, 0] = _compress_finish(part0, part1, pos_ref, w1_ref, w2bd_ref, nh)


def _compress_weights(cmp_pos, cmp_w1, cmp_w2):
    eye = jnp.eye(HEADS_PER_CG, dtype=F32)
    w1r = cmp_w1.reshape(2, CMP_RATIO, CMP_STRIDE, HEAD_DIM, CMP_HIDDEN)
    wbd = jnp.einsum('kl,artdh->artkdlh', eye, w1r).reshape(2, CMP_RATIO, CMP_STRIDE, LANES, CG_HIDDEN).astype(BF16)
    w2bd = jnp.einsum('kl,ahd->akhld', eye, cmp_w2).reshape(2, CG_HIDDEN, LANES).astype(BF16)
    pos = jnp.zeros((2, SUB, CMP_BLOCK * HEAD_DIM), F32).at[:, 0].set(cmp_pos.reshape(2, -1))
    return pos, cmp_w1, wbd, w2bd


def _compress_specs(cw, imap):
    def spec(a):
        return pl.BlockSpec((1,) + a.shape[1:], lambda *g: (imap(*g) // CG_PER_SLOT,) + (0,) * (a.ndim - 1))
    return [spec(a) for a in cw]


def _pack_cmp(cmp, dtype=None):
    nb, _, n, _ = cmp.shape
    c = cmp.reshape(nb, 2, CG_PER_SLOT, n, HEADS_PER_CG, HEAD_DIM)
    c = jnp.transpose(c, (0, 2, 4, 3, 1, 5)).reshape(nb, N_KV_HEADS, n, 2 * HEAD_DIM)
    return c.astype(BF16 if dtype is None else dtype)


def _compress_prompt(proj, cw, nb, t):
    nh = t // CMP_STRIDE
    col0 = N_Q_COLS // LANES
    return pl.pallas_call(
        functools.partial(_compress_kernel, nh=nh),
        grid=(nb, N_CG),
        in_specs=[pl.BlockSpec((t, LANES), lambda b, c: (b, col0 + c))] + _compress_specs(cw, lambda b, c: c),
        out_specs=pl.BlockSpec((1, 1, nh, LANES), lambda b, c: (b, c, 0, 0)),
        out_shape=jax.ShapeDtypeStruct((nb, N_CG, nh, LANES), F32),
        compiler_params=_cparams(("arbitrary", "arbitrary")),
    )(proj, *cw)


KEY_TILE = 4 * Q_BLOCK
FRONT_PAD = KEY_TILE
SUBTILES = KEY_TILE // Q_BLOCK
BAND_OFF = 2 * Q_BLOCK // CMP_STRIDE
BAND_W = 32
Q_PER_CMP = Q_BLOCK // CMP_STRIDE


def _nsa_consts(t):
    ncp = t // CMP_STRIDE
    nblk = t // SEL_BLOCK
    nbp = -(-nblk // LANES) * LANES
    at = np.zeros((nbp, ncp), np.float32)
    for j in range(nblk):
        for n in range(SEL_RATIO * j - (CMP_RATIO - 1), SEL_RATIO * (j + 1)):
            if 0 <= n < ncp - CMP_RATIO + 1:
                at[j, n] = 1.0
    key_blk = np.arange(t) // SEL_BLOCK
    e = (np.arange(nbp)[:, None] == key_blk[None, :]).astype(np.float32)
    e1 = e.reshape(nbp, t // LANES, LANES).transpose(1, 0, 2)
    e1 = np.concatenate([np.zeros((SUBTILES,) + e1.shape[1:], np.float32), e1], axis=0)
    band = np.zeros((2 * ncp + 2 * Q_PER_CMP, LANES), np.float32)
    u = np.arange(BAND_W)
    band[u + ncp, HEAD_DIM + u] = 1.0
    band[u + ncp, HEAD_DIM + BAND_W + u] = 1.0
    r = np.arange(Q_BLOCK)[:, None]
    c = np.arange(LANES)[None, :]
    d0 = r - c
    idx0 = np.where(d0 >= 0, d0, NEAR)
    idx1 = Q_BLOCK + r - c
    dc = r - CMP_STRIDE * c + (CMP_STRIDE * BAND_OFF - CMP_BLOCK + 1)
    idxc = np.where((dc >= 0) & (dc < NEAR), dc, NEAR)
    assert np.all(idxc[:, BAND_W:] == NEAR)
    bf = lambda a: jnp.asarray(a, BF16)
    return dict(at=bf(at), e1=bf(e1), band=jnp.asarray(band),
                idx0=idx0, idx1=idx1, idxc=idxc, ncp=ncp, nbp=nbp, nblk=nblk)


def _nsa_kernel(q_ref, gl_ref, kck_ref, kvc_ref, kvs_ref, kvw_ref, at_ref, e1_ref, tb_ref, cbq_ref, band_ref,
                o_ref, sc_ref, *, ncp, nbp, n_sel):
    i = pl.program_id(2)
    rows = GROUP * Q_BLOCK
    lane = lax.broadcasted_iota(jnp.int32, (1, LANES), 1)
    low = lane < HEAD_DIM
    r_col = lax.broadcasted_iota(jnp.int32, (rows, 1), 0) % Q_BLOCK
    q_pos = i * Q_BLOCK + r_col

    q = q_ref[...] * (HEAD_DIM ** -0.5 * LOG2E)
    parts = []
    for h in range(GROUP * HEAD_DIM // LANES):
        qh = q[:, h * LANES:(h + 1) * LANES]
        parts.append(jnp.where(low, qh, 0.0))
        parts.append(jnp.where(low, pltpu.roll(qh, HEAD_DIM, axis=1), 0.0))
    qs = jnp.concatenate(parts, axis=0).astype(BF16)

    qc = jnp.where(low, qs, cbq_ref[0])
    start = pl.multiple_of(ncp + BAND_OFF - Q_PER_CMP * i, 8)
    kq = (kck_ref[0, 0] + band_ref[pl.ds(start, ncp), :]).astype(BF16)
    s = lax.dot_general(qc, kq, _NT, preferred_element_type=F32)
    cmp_end = lax.broadcasted_iota(jnp.int32, (1, ncp), 1) * CMP_STRIDE + (CMP_BLOCK - 1)
    mask = cmp_end <= q_pos
    m = jnp.max(jnp.where(mask, s, NEG), axis=-1, keepdims=True)
    e = jnp.where(mask, jnp.exp2(s - m), 0.0)
    p_cmp = e / jnp.maximum(jnp.sum(e, axis=-1, keepdims=True), 1e-30)
    o_cmp = jnp.dot(p_cmp.astype(BF16), kvc_ref[0, 0], preferred_element_type=F32)
    imp = p_cmp[0:Q_BLOCK]
    for g in range(1, GROUP):
        imp = imp + p_cmp[g * Q_BLOCK:(g + 1) * Q_BLOCK]
    bst = sum(lax.dot_general(at_ref[...], part, _NT, preferred_element_type=F32)
              for part in _split_bf16(imp, 2))

    jb = lax.broadcasted_iota(jnp.int32, (nbp, Q_BLOCK), 0)
    q_blk = (i * Q_BLOCK + lax.broadcasted_iota(jnp.int32, (nbp, Q_BLOCK), 1)) // SEL_BLOCK
    causal = jb <= q_blk
    forced = (jb == 0) | (jb == q_blk) | (jb == q_blk - 1)
    score = jnp.where(causal, bst + jnp.where(forced, FORCE_BONUS, 0.0), -jnp.inf)
    sc_ref[...] = score

    def ahead_of(jp):
        row = sc_ref[pl.ds(jp, 1), :]
        tie = jnp.where(jb > jp, 1.0, 0.0)
        return jnp.where(row > score, 1.0, jnp.where(row == score, tie, 0.0))

    def rank_body(jh, cnt):
        return cnt + ahead_of(2 * jh) + ahead_of(2 * jh + 1)

    n_causal_pairs = Q_BLOCK // SEL_BLOCK * (i + 1) // 2
    cnt = lax.fori_loop(0, n_causal_pairs, rank_body, jnp.zeros((nbp, Q_BLOCK), F32))
    sel_t = jnp.where(causal, jnp.where(cnt < n_sel, 1.0, 0.0), 0.0)
    sel = sel_t.T.astype(BF16)

    def ones_and_values(kv):
        return jnp.where(low, jnp.ones_like(kv), kv)

    def normalise(acc):
        return jnp.where(low, 0.0, acc / jnp.maximum(pltpu.roll(acc, HEAD_DIM, axis=1), 1e-30))

    n_wt = WINDOW // Q_BLOCK
    kw = kvw_ref[0, 0, pl.ds(pl.multiple_of((i + SUBTILES - n_wt) * Q_BLOCK, Q_BLOCK), WINDOW + Q_BLOCK), :]
    s = lax.dot_general(qs, kw, _NT, preferred_element_type=F32)
    pieces = []
    for u in range(n_wt + 1):
        su = s[:, u * Q_BLOCK:(u + 1) * Q_BLOCK]
        if u == n_wt:
            su = jnp.where(lane <= r_col, su + tb_ref[0, 0], NEG)
        else:
            if u == n_wt - 1:
                su = su + tb_ref[0, 1]
            su = su + jnp.where(i - n_wt + u >= 0, 0.0, NEG)
            if u == 0:
                su = jnp.where(lane > r_col, su, NEG)
        pieces.append(su)
    sm = jnp.concatenate(pieces, axis=1)
    e = jnp.exp2((sm - jnp.max(sm, axis=-1, keepdims=True)).astype(BF16))
    o_win = normalise(jnp.dot(e, ones_and_values(kw), preferred_element_type=F32))

    n_tiles = (i + SUBTILES) // SUBTILES

    def keys(u):
        return kvs_ref[0, 0, pl.ds(pl.multiple_of((i + 1 - SUBTILES * u) * Q_BLOCK, Q_BLOCK), KEY_TILE), :]

    def masked_scores(u, diagonal=False):
        t0p = i + 1 - SUBTILES * u
        mk = jnp.concatenate([jnp.dot(sel, e1_ref[t0p + v], preferred_element_type=F32) for v in range(SUBTILES)],
                             axis=1) > 0.5
        s3 = lax.dot_general(qs, keys(u), _NT, preferred_element_type=F32).reshape(GROUP, Q_BLOCK, KEY_TILE)
        if diagonal:
            r3 = lax.broadcasted_iota(jnp.int32, (1, Q_BLOCK, LANES), 1)
            c3 = lax.broadcasted_iota(jnp.int32, (1, Q_BLOCK, LANES), 2)
            tb0 = tb_ref[0, 0].reshape(GROUP, Q_BLOCK, LANES)
            tb1 = tb_ref[0, 1].reshape(GROUP, Q_BLOCK, LANES)
            s3 = jnp.concatenate([s3[:, :, :KEY_TILE - 2 * Q_BLOCK],
                                  s3[:, :, KEY_TILE - 2 * Q_BLOCK:KEY_TILE - Q_BLOCK] + tb1,
                                  jnp.where(c3 <= r3, s3[:, :, KEY_TILE - Q_BLOCK:] + tb0, NEG)], axis=2)
        return jnp.where(mk[None], s3, NEG).reshape(rows, KEY_TILE)

    def sel_body(u, carry):
        m, acc, sm, e_prev = carry
        pv = jnp.dot(e_prev, ones_and_values(keys(jnp.maximum(u - 1, 0))), preferred_element_type=F32)
        sm_next = masked_scores(jnp.minimum(u + 1, n_tiles - 1))
        m_new = jnp.maximum(m, jnp.max(sm, axis=-1, keepdims=True))
        alpha = jnp.exp2(m - m_new)
        e = jnp.exp2((sm - m_new).astype(BF16))
        return m_new, alpha * (acc + pv), sm_next, e

    init = (jnp.full((rows, 1), NEG, F32), jnp.zeros((rows, LANES), F32),
            masked_scores(0, diagonal=True), jnp.zeros((rows, KEY_TILE), BF16))
    m, acc, _, e_last = lax.fori_loop(0, n_tiles, sel_body, init)
    o_sel = normalise(acc + jnp.dot(e_last, ones_and_values(keys(n_tiles - 1)), preferred_element_type=F32))

    sg = jax.nn.sigmoid(gl_ref[...])
    for g in range(GROUP):
        out = jnp.zeros((Q_BLOCK, LANES), F32)
        for j, ob in enumerate((o_cmp, o_sel, o_win)):
            out = out + sg[:, 3 * g + j:3 * g + j + 1] * ob[g * Q_BLOCK:(g + 1) * Q_BLOCK]
        o_ref[0, 0, g] = out.astype(o_ref.dtype)


GATE_COL0 = N_Q_COLS + N_KV_COLS
NP_ATTN = GATE_COL0 + N_KV_HEADS * LANES


def _nsa_prompt(proj, kck, kvc, kvs, kvw, tb, cbq, consts, nb, t):
    ni = t // Q_BLOCK
    ncp, nbp = consts['ncp'], consts['nbp']
    gcol = GATE_COL0 // LANES
    qw = GROUP * HEAD_DIM
    full = lambda a: pl.BlockSpec(a.shape, lambda b, k, i: (0,) * a.ndim)
    per_kvh = lambda a: pl.BlockSpec((1,) + a.shape[1:], lambda b, k, i: (k,) + (0,) * (a.ndim - 1))
    per_bk = lambda a: pl.BlockSpec((1, 1) + a.shape[2:], lambda b, k, i: (b, k, 0, 0))
    c = consts
    n_sel = min(N_SELECT, c['nblk'])
    assert n_sel >= 3
    return pl.pallas_call(
        functools.partial(_nsa_kernel, ncp=ncp, nbp=nbp, n_sel=n_sel),
        grid=(nb, N_KV_HEADS, ni),
        in_specs=[pl.BlockSpec((Q_BLOCK, qw), lambda b, k, i: (b * ni + i, k)),
                  pl.BlockSpec((Q_BLOCK, LANES), lambda b, k, i: (b * ni + i, gcol + k)),
                  per_bk(kck), per_bk(kvc), per_bk(kvs), per_bk(kvw),
                  full(c['at']), full(c['e1']), per_kvh(tb), per_kvh(cbq), full(c['band'])],
        out_specs=pl.BlockSpec((1, 1, GROUP, Q_BLOCK, LANES), lambda b, k, i: (b, k, 0, i, 0)),
        out_shape=jax.ShapeDtypeStruct((nb, N_KV_HEADS, GROUP, t, LANES), BF16),
        scratch_shapes=[pltpu.VMEM((nbp, Q_BLOCK), F32)],
        compiler_params=_cparams(("arbitrary", "arbitrary", "arbitrary")),
    )(proj, proj, kck, kvc, kvs, kvw, c['at'], c['e1'], tb, cbq, c['band'])


def _heads_out_kernel(a_ref, w_ref, g_ref, gate_ref, x_ref, o_ref):
    y = jnp.zeros(o_ref.shape, F32)
    for k in range(N_KV_HEADS):
        for g in range(0, GROUP, 2):
            a = jnp.concatenate([a_ref[0, k, g], a_ref[0, k, g + 1]], axis=1)
            h = k * GROUP + g
            w = w_ref[h * LANES:(h + 2) * LANES, :]
            y = y + jnp.dot(a, w, preferred_element_type=F32)
    o_ref[...] = x_ref[...] + gate_ref[0] * _rms(y, g_ref[...])


def _heads_out(a, w_pad, g, gate, x, tm):
    nb, _, _, t, _ = a.shape
    d = w_pad.shape[1]
    tpb = t // tm
    return pl.pallas_call(
        _heads_out_kernel,
        grid=(nb * tpb,),
        in_specs=[pl.BlockSpec((1, N_KV_HEADS, GROUP, tm, LANES), lambda i: (i // tpb, 0, 0, i % tpb, 0)),
                  pl.BlockSpec(w_pad.shape, lambda i: (0, 0)),
                  pl.BlockSpec((1, d), lambda i: (0, 0)),
                  pl.BlockSpec((1,) + gate.shape[1:], lambda i: (i // tpb, 0, 0)),
                  pl.BlockSpec((tm, d), lambda i: (i, 0))],
        out_specs=pl.BlockSpec((tm, d), lambda i: (i, 0)),
        out_shape=jax.ShapeDtypeStruct((nb * t, d), F32),
        compiler_params=_cparams(("arbitrary",)),
    )(a, w_pad, g.reshape(1, d), gate, x)


def _row_tile(m, cap):
    return m if m <= cap else cap


def _pack_kv(kv6, s0):
    pair = jnp.transpose(kv6[:, :, s0:s0 + 2], (0, 3, 1, 2, 4))
    pair = pair.reshape(pair.shape[0], N_KV_HEADS, pair.shape[2], 2 * HEAD_DIM).astype(BF16)
    return jnp.pad(pair, ((0, 0), (0, 0), (FRONT_PAD, 0), (0, 0)))


def _attn_in_weights(w_in):
    gates = w_in[:, GATE_COL0:].reshape(w_in.shape[0], N_KV_HEADS, GROUP * 3)
    gates = jnp.pad(gates, ((0, 0), (0, 0), (0, LANES - GROUP * 3))).reshape(w_in.shape[0], -1)
    return jnp.concatenate([w_in[:, :GATE_COL0], gates], axis=1).astype(BF16)


def _attn_out_weights(w_out):
    w = w_out.reshape(N_HEADS, HEAD_DIM, w_out.shape[1])
    return jnp.pad(w, ((0, 0), (LANES - HEAD_DIM, 0), (0, 0))).reshape(N_HEADS * LANES, -1).astype(BF16)


def _nsa_prompt_mixer(xp, g, shift, scale, w_in, cw, bvz, nb, t):
    proj = _nm_matmul(xp, g, shift, scale, w_in, _row_tile(t, 512), NP_ATTN)
    kv6 = proj[:, N_Q_COLS:GATE_COL0].reshape(nb, t, N_KV_SLOTS + N_WIN_SLOTS, N_KV_HEADS, HEAD_DIM)
    cmp = _compress_prompt(proj, cw, nb, t)
    kvc = _pack_cmp(cmp)
    kck = jnp.pad(_pack_cmp(cmp, F32)[..., :HEAD_DIM], ((0, 0), (0, 0), (0, 0), (0, LANES - HEAD_DIM)))
    kvs = _pack_kv(kv6, 2)
    kvw = _pack_kv(kv6, N_KV_SLOTS)
    c = _nsa_consts(t)
    bv2 = bvz * LOG2E
    tb = jnp.stack([_toeplitz(bv2, c['idx0']), _toeplitz(bv2, c['idx1'])], axis=1)
    hi, lo = _split_bf16(_toeplitz(bv2, c['idxc']), 2)
    cbq = jnp.concatenate([jnp.zeros(hi.shape[:2] + (HEAD_DIM,), BF16), hi[..., :BAND_W], lo[..., :BAND_W]], axis=-1)
    o = _nsa_prompt(proj, kck, kvc, kvs, kvw, tb, cbq, c, nb, t)
    return o, kv6


SC_PAGES = 32
CG_PER_PAGE = N_KV_SLOTS * KV_ROW // LANES


def _cache_tiles(cache_kv):
    page = cache_kv.shape[2]
    assert page == LANES
    return jnp.transpose(cache_kv, (0, 1, 3, 4, 5, 2)).reshape(-1, page)


def _compress_sample_kernel(pt_ref, cache_ref, pos_ref, w1_ref, wbd_ref, w2bd_ref, o_ref, tbuf_ref, buf_ref,
                            sem_ref, p0_ref, p1_ref, *, n_pages, page, row_base):
    step = pl.program_id(0) * N_CG + pl.program_id(1)
    nsteps = pl.num_programs(0) * N_CG
    chunk_pages = min(SC_PAGES, n_pages // 2)
    nch = n_pages // chunk_pages
    hpc = chunk_pages * page // CMP_STRIDE
    nh = nch * hpc

    def copies(step_, ch, slot):
        b_, c_ = step_ // N_CG, step_ % N_CG
        out = []
        for p in range(chunk_pages):
            pg = pt_ref[b_, ch * chunk_pages + p]
            out.append(pltpu.make_async_copy(
                cache_ref.at[pl.ds(((row_base + pg) * CG_PER_PAGE + c_) * LANES, LANES), :],
                tbuf_ref.at[slot, p], sem_ref.at[slot]))
        return out

    @pl.when(step == 0)
    def _():
        for cp in copies(step, 0, 0):
            cp.start()

    eye = jnp.where(lax.broadcasted_iota(jnp.int32, (page, page), 0) == lax.broadcasted_iota(jnp.int32, (page, page), 1),
                    1.0, 0.0).astype(BF16)
    for ch in range(nch):
        slot = ch % 2
        if ch + 1 < nch:
            for cp in copies(step, ch + 1, 1 - slot):
                cp.start()
        else:
            @pl.when(step + 1 < nsteps)
            def _():
                for cp in copies(step + 1, 0, 1 - slot):
                    cp.start()
        for cp in copies(step, ch, slot):
            cp.wait()

        def to_rows(p, _):
            buf_ref[pl.ds(pl.multiple_of(p * page, page), page), :] = lax.dot_general(
                eye, tbuf_ref[slot, p].astype(BF16), _NT, preferred_element_type=F32)
            return 0

        lax.fori_loop(0, chunk_pages, to_rows, 0)
        p0, p1 = _compress_parts(lambda s: buf_ref[pl.ds(s, hpc, stride=CMP_STRIDE), :], wbd_ref, hpc)
        p0_ref[ch * hpc:(ch + 1) * hpc] = p0
        p1_ref[ch * hpc:(ch + 1) * hpc] = p1

    o_ref[0, 0] = _compress_finish(p0_ref[...], p1_ref[...], pos_ref, w1_ref, w2bd_ref, nh)


def _compress_sample(page_table, cache_t, cw, page, row_base):
    nb, n_pages = page_table.shape
    chunk_pages = min(SC_PAGES, n_pages // 2)
    assert n_pages % (2 * chunk_pages) == 0
    nh = n_pages * page // CMP_STRIDE
    grid_spec = pltpu.PrefetchScalarGridSpec(
        num_scalar_prefetch=1,
        grid=(nb, N_CG),
        in_specs=[pl.BlockSpec(memory_space=pl.ANY)] + _compress_specs(cw, lambda b, c, pt: c),
        out_specs=pl.BlockSpec((1, 1, nh, LANES), lambda b, c, pt: (b, c, 0, 0)),
        scratch_shapes=[pltpu.VMEM((2, chunk_pages, LANES, page), F32),
                        pltpu.VMEM((chunk_pages * page, LANES), F32),
                        pltpu.SemaphoreType.DMA((2,)),
                        pltpu.VMEM((nh, CG_HIDDEN), F32),
                        pltpu.VMEM((nh, CG_HIDDEN), F32)])
    return pl.pallas_call(
        functools.partial(_compress_sample_kernel, n_pages=n_pages, page=page, row_base=row_base),
        grid_spec=grid_spec,
        out_shape=jax.ShapeDtypeStruct((nb, N_CG, nh, LANES), F32),
        compiler_params=_cparams(("arbitrary", "arbitrary")),
    )(page_table, cache_t, *cw)


def _softmax_with_new_key(s, s_new):
    m = jnp.maximum(jnp.max(s, axis=-1, keepdims=True), s_new)
    e = jnp.exp(s - m)
    e_new = jnp.exp(s_new - m)
    inv = 1.0 / jnp.maximum(jnp.sum(e, axis=-1, keepdims=True) + e_new, 1e-30)
    return e * inv, e_new * inv


def _nsa_sample_kernel(pt_ref, qs_ref, q_ref, gl_ref, new_ref, ncol_ref, kvc_ref, cache_ref, win_ref, cbs_ref,
                       sbz_ref, wb_ref, b0_ref, as_ref, o_ref, wout_ref, gk_ref, gv_ref, sem_ref, idx_ref,
                       *, past, page, row_base, n_pick):
    b = pl.program_id(0)
    blk_per_page = page // SEL_BLOCK
    nbs = past // SEL_BLOCK
    ncs = kvc_ref.shape[2]
    row = lax.broadcasted_iota(jnp.int32, (SUB, 1), 0)

    cmp_ok = lax.broadcasted_iota(jnp.int32, (SUB, ncs), 1) * CMP_STRIDE + (CMP_BLOCK - 1) <= past
    imp = jnp.zeros((SUB, ncs), F32)
    o_cmp = []
    for k in range(N_KV_HEADS):
        kc = kvc_ref[0, k]
        s = lax.dot_general(qs_ref[0, k], kc, _NT, preferred_element_type=F32) + cbs_ref[k]
        m = jnp.max(jnp.where(cmp_ok, s, NEG), axis=-1, keepdims=True)
        e = jnp.where(cmp_ok, jnp.exp(s - m), 0.0)
        p = e / jnp.maximum(jnp.sum(e, axis=-1, keepdims=True), 1e-30)
        o_cmp.append(jnp.dot(p.astype(BF16), kc, preferred_element_type=F32))
        imp_k = p[0:1]
        for g in range(1, GROUP):
            imp_k = imp_k + p[g:g + 1]
        imp = imp + jnp.where(row == k, imp_k, 0.0)
    bs = sum(jnp.dot(part, as_ref[...], preferred_element_type=F32) for part in _split_bf16(imp, 3))

    lane_b = lax.broadcasted_iota(jnp.int32, (SUB, nbs), 1)
    lane_f = lane_b.astype(F32)
    score = bs + jnp.where((lane_b == 0) | (lane_b == nbs - 1), FORCE_BONUS, 0.0)

    def gathers(k, r):
        j = idx_ref[k * n_pick + r]
        pg = pt_ref[b, j // blk_per_page]
        tile0 = (row_base + pg) * N_KV_SLOTS
        return [pltpu.make_async_copy(
            cache_ref.at[pl.ds(((tile0 + slot) * N_KV_HEADS + k) * HEAD_DIM, HEAD_DIM), :],
            buf.at[k * n_pick + r], sem_ref.at[0]) for slot, buf in ((2, gk_ref), (3, gv_ref))]

    for r in range(n_pick):
        m = jnp.max(score, axis=-1, keepdims=True)
        pick = jnp.min(jnp.where(score == m, lane_f, 1e9), axis=-1, keepdims=True)
        score = jnp.where(lane_f == pick, -jnp.inf, score)
        for k in range(N_KV_HEADS):
            idx_ref[k * n_pick + r] = jnp.sum(jnp.where(row == k, pick, 0.0)).astype(jnp.int32)
            for cp in gathers(k, r):
                cp.start()

    def new_row(k, j):
        return new_ref[0, k][j:j + 1].astype(BF16).astype(F32)

    tok = lax.broadcasted_iota(jnp.int32, (1, WINDOW), 1)
    o_win = []
    for k in range(N_KV_HEADS):
        for slot in range(N_WIN_SLOTS):
            shifted = pltpu.roll(win_ref[0, slot, k], WINDOW - 1, axis=1)
            wout_ref[0, slot, k] = jnp.where(tok == WINDOW - 1, ncol_ref[0, slot, k], shifted)
        q = q_ref[0, k]
        s = jnp.dot(q, win_ref[0, 0, k].astype(BF16), preferred_element_type=F32) + wb_ref[k]
        s = jnp.where(tok >= 1, s, NEG)
        s_new = jnp.sum(q.astype(F32) * new_row(k, 2), axis=-1, keepdims=True) + b0_ref[k][:, 0:1]
        p, p_new = _softmax_with_new_key(s, s_new)
        o_win.append(lax.dot_general(p.astype(BF16), win_ref[0, 1, k].astype(BF16), _NT,
                                     preferred_element_type=F32) + p_new * new_row(k, 3))

    for k in range(N_KV_HEADS):
        for r in range(n_pick):
            for cp in gathers(k, r):
                cp.wait()

    upper = lax.broadcasted_iota(jnp.int32, (1, LANES), 1) // SEL_BLOCK
    for k in range(N_KV_HEADS):
        q = q_ref[0, k]
        tiles = []
        for r in range(n_pick):
            j = idx_ref[k * n_pick + r]
            near = j - (nbs - NEAR // SEL_BLOCK)
            s = jnp.dot(q, gk_ref[k * n_pick + r].astype(BF16), preferred_element_type=F32)
            s = s + sbz_ref[k, jnp.where(near >= 0, near, NEAR // SEL_BLOCK)]
            tiles.append(jnp.where(upper == j % blk_per_page, s, NEG))
        s_new = jnp.sum(q.astype(F32) * new_row(k, 0), axis=-1, keepdims=True) + b0_ref[k][:, 0:1]
        p, p_new = _softmax_with_new_key(jnp.concatenate(tiles, axis=1), s_new)
        o_sel = p_new * new_row(k, 1)
        for r in range(n_pick):
            o_sel = o_sel + lax.dot_general(p[:, r * LANES:(r + 1) * LANES].astype(BF16),
                                            gv_ref[k * n_pick + r].astype(BF16), _NT, preferred_element_type=F32)
        g_cmp, g_sel, g_win = (jax.nn.sigmoid(gl_ref[0, k, j])[:, :HEAD_DIM] for j in range(3))
        o_ref[0, k] = g_cmp * o_cmp[k][:, HEAD_DIM:] + g_sel * o_sel + g_win * o_win[k]


def _bias_rows(bvz, idx):
    t = jnp.take(bvz, jnp.asarray(idx, jnp.int32), axis=0).reshape(len(idx), N_KV_HEADS, GROUP)
    t = jnp.transpose(t, (1, 2, 0))
    return jnp.pad(t, ((0, 0), (0, SUB - GROUP), (0, 0)))


def _nsa_sample(proj_s, kvc, page_table, cache_t, win_t, bvz, past, page, row_base):
    nb = proj_s.shape[0]
    n_pick = N_SELECT - 1
    nbs = past // SEL_BLOCK
    ncs = past // CMP_STRIDE
    n_near = NEAR // SEL_BLOCK
    blk_per_page = page // SEL_BLOCK
    assert past % page == 0 and page == LANES and nbs > n_pick + n_near and (nbs - n_near) % blk_per_page == 0
    assert win_t.shape[-1] == WINDOW < past
    q = (proj_s[:, :N_Q_COLS] * HEAD_DIM ** -0.5).astype(BF16).reshape(nb, N_KV_HEADS, GROUP, HEAD_DIM)
    q = jnp.pad(q, ((0, 0), (0, 0), (0, SUB - GROUP), (0, 0)))
    qs = jnp.pad(q, ((0, 0), (0, 0), (0, 0), (0, LANES - HEAD_DIM)))
    gl = proj_s[:, GATE_COL0:].reshape(nb, N_KV_HEADS, LANES)[:, :, :GROUP * 3].reshape(nb, N_KV_HEADS, GROUP, 3)
    gl = jnp.pad(jnp.transpose(gl, (0, 1, 3, 2)), ((0, 0), (0, 0), (0, 0), (0, SUB - GROUP)))
    gl = jnp.broadcast_to(gl[..., None], gl.shape + (LANES,))
    new = proj_s[:, N_Q_COLS + 2 * KV_ROW:GATE_COL0].reshape(nb, 4, N_KV_HEADS, HEAD_DIM)
    new_rows = jnp.transpose(new, (0, 2, 1, 3))
    new_cols = new[:, 2:].reshape(nb, N_WIN_SLOTS, N_KV_HEADS, HEAD_DIM, 1)

    dc = past - (np.arange(ncs) * CMP_STRIDE + CMP_BLOCK - 1)
    cbs = _bias_rows(bvz, np.where((dc >= 0) & (dc < NEAR), dc, NEAR))
    lane = np.arange(LANES)
    sb = []
    for jj in range(n_near + 1):
        d = NEAR - SEL_BLOCK * jj - lane % SEL_BLOCK
        ok = (lane // SEL_BLOCK == jj % blk_per_page) & (jj < n_near) & (d < NEAR)
        sb.append(_bias_rows(bvz, np.where(ok, d, NEAR)))
    sbz = jnp.stack(sb, axis=1)
    dw = WINDOW - np.arange(WINDOW)
    wb = _bias_rows(bvz, np.where(dw < NEAR, dw, NEAR))
    b0 = _bias_rows(bvz, np.zeros(LANES, np.int64))
    a_s = np.zeros((ncs, nbs), np.float32)
    for j in range(nbs):
        for n in range(SEL_RATIO * j - (CMP_RATIO - 1), SEL_RATIO * (j + 1)):
            if 0 <= n < ncs - CMP_RATIO + 1:
                a_s[n, j] = 1.0
    a_s = jnp.asarray(a_s, BF16)

    per_b = lambda a: pl.BlockSpec((1,) + a.shape[1:], lambda b, pt: (b,) + (0,) * (a.ndim - 1))
    full = lambda a: pl.BlockSpec(a.shape, lambda b, pt: (0,) * a.ndim)
    grid_spec = pltpu.PrefetchScalarGridSpec(
        num_scalar_prefetch=1,
        grid=(nb,),
        in_specs=[per_b(qs), per_b(q), per_b(gl), per_b(new_rows), per_b(new_cols), per_b(kvc),
                  pl.BlockSpec(memory_space=pl.ANY), per_b(win_t),
                  full(cbs), full(sbz), full(wb), full(b0), full(a_s)],
        out_specs=[pl.BlockSpec((1, N_KV_HEADS, SUB, HEAD_DIM), lambda b, pt: (b, 0, 0, 0)), per_b(win_t)],
        scratch_shapes=[pltpu.VMEM((N_KV_HEADS * n_pick, HEAD_DIM, page), F32),
                        pltpu.VMEM((N_KV_HEADS * n_pick, HEAD_DIM, page), F32),
                        pltpu.SemaphoreType.DMA((1,)),
                        pltpu.SMEM((N_KV_HEADS * n_pick,), jnp.int32)])
    o, win_out = pl.pallas_call(
        functools.partial(_nsa_sample_kernel, past=past, page=page, row_base=row_base, n_pick=n_pick),
        grid_spec=grid_spec,
        out_shape=[jax.ShapeDtypeStruct((nb, N_KV_HEADS, SUB, HEAD_DIM), F32),
                   jax.ShapeDtypeStruct(win_t.shape, F32)],
        compiler_params=_cparams(("arbitrary",)),
    )(page_table, qs, q, gl, new_rows, new_cols, kvc, cache_t, win_t, cbs, sbz, wb, b0, a_s)
    return o[:, :, :GROUP].reshape(nb, N_Q_COLS), win_out


NP_SSM = -(-(SSM_CONV_DIM + SSM_D_INNER + SSM_HEADS) // LANES) * LANES
SSM_COL_TILE = NP_SSM // 7
assert SSM_COL_TILE % LANES == 0 and SSM_COL_TILE * 7 == NP_SSM
N_BC = SSM_GROUPS * SSM_STATE
HEADS_PER_GROUP = SSM_HEADS // SSM_GROUPS
PAIR = LANES // SSM_HEAD_DIM
N_PAIRS = SSM_HEADS // PAIR
CONV_PAD = 8


def _softplus(x):
    return jnp.maximum(x, 0.0) + jnp.log1p(jnp.exp(-jnp.abs(x)))


def _cumsum_rows(x):
    n = x.shape[0]
    row = lax.broadcasted_iota(jnp.int32, x.shape, 0)
    s = 1
    while s < n:
        x = x + jnp.where(row >= s, pltpu.roll(x, s, axis=0), 0.0)
        s *= 2
    return x


def _ssd_kernel(xbc_ref, z_ref, dt_ref, h0_ref, cinit_ref, cw_ref, cb_ref, dtb_ref, alog_ref, dsk_ref, ng_ref,
                y_ref, hout_ref, xs_ref, act_ref, ybuf_ref, h_ref, *, nc):
    c = pl.program_id(1)
    q = SSM_CHUNK

    @pl.when(c == 0)
    def _():
        xs_ref[0:CONV_PAD] = cinit_ref[0]
        h_ref[...] = h0_ref[0]

    xs_ref[CONV_PAD:CONV_PAD + q] = xbc_ref[...]
    conv = cb_ref[...] + cw_ref[0:1] * xs_ref[CONV_PAD - 3:CONV_PAD - 3 + q]
    for k in range(1, SSM_CONV):
        conv = conv + cw_ref[k:k + 1] * xs_ref[CONV_PAD - 3 + k:CONV_PAD - 3 + k + q]
    xs_ref[0:CONV_PAD] = xs_ref[q:q + CONV_PAD]
    act_ref[...] = _silu(conv)

    dt = _softplus(dt_ref[...] + dtb_ref[...])
    acum = _cumsum_rows(dt * (-jnp.exp(alog_ref[...])))
    acum_t = acum.T
    dt_t = dt.T
    last = acum[q - 1:q, :]
    ea = jnp.exp(acum)
    te = jnp.exp(last - acum) * dt
    cd = jnp.exp(last)
    ii = lax.broadcasted_iota(jnp.int32, (q, q), 0)
    jj = lax.broadcasted_iota(jnp.int32, (q, q), 1)
    tri = ii >= jj
    low = jj < SSM_HEAD_DIM
    low_rows = ii < SSM_HEAD_DIM

    def col(a, h):
        return a[:, h:h + 1]

    for g in range(SSM_GROUPS):
        bg = act_ref[:, SSM_D_INNER + g * SSM_STATE:SSM_D_INNER + (g + 1) * SSM_STATE].astype(BF16)
        cg = act_ref[:, SSM_D_INNER + N_BC + g * SSM_STATE:SSM_D_INNER + N_BC + (g + 1) * SSM_STATE].astype(BF16)
        cbg = lax.dot_general(cg, bg, _NT, preferred_element_type=F32)
        for pr in range(HEADS_PER_GROUP // PAIR):
            k = g * (HEADS_PER_GROUP // PAIR) + pr
            ha, hb = PAIR * k, PAIR * k + 1
            xp = act_ref[:, k * LANES:(k + 1) * LANES]
            xpb = xp.astype(BF16)
            ys = []
            for h in (ha, hb):
                decay = jnp.exp(jnp.where(tri, col(acum, h) - acum_t[h:h + 1, :], NEG))
                w = cbg * decay * dt_t[h:h + 1, :]
                ys.append(jnp.dot(w.astype(BF16), xpb, preferred_element_type=F32))
            y = jnp.where(low, ys[0], ys[1])
            xs_pair = xp * jnp.where(low, col(te, ha), col(te, hb))
            st = jnp.dot(xs_pair.T.astype(BF16), bg, preferred_element_type=F32)
            hprev = h_ref[k]
            yoff = lax.dot_general(cg, hprev.astype(BF16), _NT, preferred_element_type=F32)
            y = y + yoff * jnp.where(low, col(ea, ha), col(ea, hb)) + dsk_ref[:, k * LANES:(k + 1) * LANES] * xp
            h_ref[k] = hprev * jnp.where(low_rows, cd[:, ha:ha + 1], cd[:, hb:hb + 1]) + st
            ybuf_ref[:, k * LANES:(k + 1) * LANES] = y

    yz = ybuf_ref[...] * _silu(z_ref[...])
    gw = SSM_D_INNER // SSM_GROUPS
    outs = []
    for g in range(SSM_GROUPS):
        seg = yz[:, g * gw:(g + 1) * gw]
        outs.append(seg * lax.rsqrt(jnp.mean(seg * seg, axis=-1, keepdims=True) + EPS))
    y_ref[...] = (jnp.concatenate(outs, axis=1) * ng_ref[...]).astype(y_ref.dtype)

    @pl.when(c == nc - 1)
    def _():
        hout_ref[0] = h_ref[...]


def _ssd(proj, h0, cinit, sw, nb, t):
    q = SSM_CHUNK
    nc = t // q
    vec = lambda a: pl.BlockSpec(a.shape, lambda b, c: (0, 0))
    return pl.pallas_call(
        functools.partial(_ssd_kernel, nc=nc),
        grid=(nb, nc),
        in_specs=[pl.BlockSpec((q, SSM_CONV_DIM), lambda b, c: (b * nc + c, 0)),
                  pl.BlockSpec((q, SSM_D_INNER), lambda b, c: (b * nc + c, SSM_CONV_DIM // SSM_D_INNER)),
                  pl.BlockSpec((q, LANES), lambda b, c: (b * nc + c, (SSM_CONV_DIM + SSM_D_INNER) // LANES)),
                  pl.BlockSpec((1,) + h0.shape[1:], lambda b, c: (b, 0, 0, 0)),
                  pl.BlockSpec((1,) + cinit.shape[1:], lambda b, c: (b, 0, 0)),
                  vec(sw['conv_w']), vec(sw['conv_b']), vec(sw['dt_bias']), vec(sw['a_log']), vec(sw['d_lane']),
                  vec(sw['norm_g'])],
        out_specs=[pl.BlockSpec((q, SSM_D_INNER), lambda b, c: (b * nc + c, 0)),
                   pl.BlockSpec((1,) + h0.shape[1:], lambda b, c: (b, 0, 0, 0))],
        out_shape=[jax.ShapeDtypeStruct((nb * t, SSM_D_INNER), BF16),
                   jax.ShapeDtypeStruct(h0.shape, F32)],
        scratch_shapes=[pltpu.VMEM((CONV_PAD + q, SSM_CONV_DIM), F32),
                        pltpu.VMEM((q, SSM_CONV_DIM), F32),
                        pltpu.VMEM((q, SSM_D_INNER), F32),
                        pltpu.VMEM(h0.shape[1:], F32)],
        compiler_params=_cparams(("arbitrary", "arbitrary")),
    )(proj, proj, proj, h0, cinit, sw['conv_w'], sw['conv_b'], sw['dt_bias'], sw['a_log'], sw['d_lane'],
      sw['norm_g'])


def _ssm_sample_step(proj_s, state_ssm, state_conv, sw):
    nb = proj_s.shape[0]
    q = SSM_CHUNK
    xbc_new = proj_s[:, :SSM_CONV_DIM]
    rows = jnp.zeros((nb, q, NP_SSM), F32)
    rows = rows.at[:, :, SSM_CONV_DIM + SSM_D_INNER:].set(NEG)
    rows = rows.at[:, q - SSM_CONV:q - 1, :SSM_CONV_DIM].set(state_conv.astype(F32))
    rows = rows.at[:, q - 1].set(proj_s)
    h0 = state_ssm.astype(F32).reshape(nb, N_PAIRS, LANES, SSM_STATE)
    cinit = jnp.zeros((nb, CONV_PAD, SSM_CONV_DIM), F32)
    yn, hfin = _ssd(rows.reshape(nb * q, NP_SSM), h0, cinit, sw, nb, q)
    conv_new = jnp.concatenate([state_conv[:, 1:].astype(F32), xbc_new[:, None]], axis=1)
    return yn.reshape(nb, q, SSM_D_INNER)[:, q - 1], hfin.reshape(state_ssm.shape), conv_new


def _ssm_weights(w_in, conv_w, conv_b, dt_bias, a_log, d_skip, norm_g):
    z_w = w_in[:, :SSM_D_INNER]
    xbc_w = w_in[:, SSM_D_INNER:SSM_D_INNER + SSM_CONV_DIM]
    dt_w = w_in[:, SSM_D_INNER + SSM_CONV_DIM:]
    pad = NP_SSM - w_in.shape[1]
    w = jnp.concatenate([xbc_w, z_w, dt_w, jnp.zeros((w_in.shape[0], pad), w_in.dtype)], axis=1).astype(BF16)
    lane_pad = lambda v: jnp.zeros((1, LANES), F32).at[0, :SSM_HEADS].set(v.astype(F32))
    return dict(w_in=w, conv_w=conv_w.astype(F32), conv_b=conv_b.astype(F32).reshape(1, -1),
                dt_bias=lane_pad(dt_bias), a_log=lane_pad(a_log),
                d_lane=jnp.repeat(d_skip.astype(F32), SSM_HEAD_DIM).reshape(1, -1),
                norm_g=norm_g.astype(F32).reshape(1, -1))


ROW_TILE = 512
FF_TILE = 1024


def kernel(x_prompt, x_sample, cache_kv, cache_win, state_ssm, state_conv, page_table, c_prompt, c_sample, rel_bias,
           ada_w, ada_b, norm_g, mlp_w1, mlp_w2, attn_w_in, attn_w_out, cmp_pos, cmp_w1, cmp_w2, ssm_w_in,
           ssm_conv_w, ssm_conv_b, ssm_dt_bias, ssm_a_log, ssm_d, ssm_norm_g, ssm_w_out):
    nb, t, d = x_prompt.shape
    db = x_sample.shape[0]
    assert x_sample.shape[1] == 1 and t % SSM_CHUNK == 0 and t % Q_BLOCK == 0
    n_pool, page = cache_kv.shape[1], cache_kv.shape[2]
    past = page_table.shape[1] * page
    depth = ada_w.shape[0]
    tm = _row_tile(t, ROW_TILE)

    xp = x_prompt.reshape(nb * t, d).astype(F32)
    xs = x_sample.reshape(db, d).astype(F32)
    c_all = jnp.concatenate([c_prompt, c_sample], axis=0).astype(F32)
    c_all = jnp.pad(c_all, ((0, (-c_all.shape[0]) % SUB), (0, 0)))
    bvz = _bias_vector(rel_bias)
    cache_t = _cache_tiles(cache_kv)
    win_t = jnp.transpose(cache_win, (0, 1, 3, 4, 5, 2))

    kv_p, win_p, ssm_p, conv_p, kv_s, win_s, ssm_s, conv_s = ([] for _ in range(8))
    for i in range(depth):
        mod = _ada(c_all, ada_w[i], ada_b[i])
        mp = [mod[:nb, j * d:(j + 1) * d].reshape(nb, 1, d) for j in range(N_MOD)]
        ms = [mod[nb:nb + db, j * d:(j + 1) * d].reshape(1, db, d) for j in range(N_MOD)]
        g = norm_g[i].astype(F32)
        if i % 2 == 0:
            a = i // 2
            w_in = _attn_in_weights(attn_w_in[a])
            w_out = attn_w_out[a].astype(BF16)
            cw = _compress_weights(cmp_pos[a], cmp_w1[a], cmp_w2[a])
            o_p, kv6 = _nsa_prompt_mixer(xp, g[0], mp[0], mp[1], w_in, cw, bvz, nb, t)
            kv_p.append(kv6[:, :, :N_KV_SLOTS])
            win_p.append(kv6[:, -min(WINDOW, t):, N_KV_SLOTS:])
            xp = _heads_out(o_p, _attn_out_weights(attn_w_out[a]), g[1], mp[2], xp, tm)

            proj_s = _nm_matmul(xs, g[0], ms[0], ms[1], w_in, db, NP_ATTN)
            kvc_s = _pack_cmp(_compress_sample(page_table, cache_t, cw, page, a * n_pool))
            o_s, win_new = _nsa_sample(proj_s, kvc_s, page_table, cache_t, win_t[a], bvz, past, page, a * n_pool)
            kv_s.append(proj_s[:, N_Q_COLS:N_Q_COLS + N_KV_SLOTS * KV_ROW].reshape(
                db, 1, N_KV_SLOTS, N_KV_HEADS, HEAD_DIM))
            win_s.append(jnp.transpose(win_new, (0, 4, 1, 2, 3)))
            xs = _mm_norm_res(o_s, w_out, g[1], ms[2], xs, db)
        else:
            m = i // 2
            sw = _ssm_weights(ssm_w_in[m], ssm_conv_w[m], ssm_conv_b[m], ssm_dt_bias[m], ssm_a_log[m], ssm_d[m],
                              ssm_norm_g[m])
            w_out = ssm_w_out[m].astype(BF16)
            proj = _nm_matmul(xp, g[0], mp[0], mp[1], sw['w_in'], tm, SSM_COL_TILE)
            h0 = jnp.zeros((nb, N_PAIRS, LANES, SSM_STATE), F32)
            cinit = jnp.zeros((nb, CONV_PAD, SSM_CONV_DIM), F32)
            yn, hfin = _ssd(proj, h0, cinit, sw, nb, t)
            ssm_p.append(hfin.reshape(nb, SSM_HEADS, SSM_HEAD_DIM, SSM_STATE).astype(state_ssm.dtype))
            conv_p.append(proj.reshape(nb, t, NP_SSM)[:, t - (SSM_CONV - 1):, :SSM_CONV_DIM])
            xp = _mm_norm_res(yn, w_out, g[1], mp[2], xp, tm)

            proj_s = _nm_matmul(xs, g[0], ms[0], ms[1], sw['w_in'], db, SSM_COL_TILE)
            yn_s, h_s, conv_new = _ssm_sample_step(proj_s, state_ssm[m], state_conv[m], sw)
            ssm_s.append(h_s.astype(state_ssm.dtype))
            conv_s.append(conv_new)
            xs = _mm_norm_res(yn_s, w_out, g[1], ms[2], xs, db)
        w1 = mlp_w1[i].astype(BF16)
        w2 = mlp_w2[i].astype(BF16)
        xp = _mlp(xp, g[2], mp[3], mp[4], w1, w2, g[3], mp[5], tm, FF_TILE)
        xs = _mlp(xs, g[2], ms[3], ms[4], w1, w2, g[3], ms[5], db, FF_TILE)
    return (xp.reshape(nb, t, d), xs.reshape(db, 1, d), jnp.stack(kv_p), jnp.stack(win_p), jnp.stack(ssm_p),
            jnp.stack(conv_p), jnp.stack(kv_s), jnp.stack(win_s), jnp.stack(ssm_s), jnp.stack(conv_s))
```

```python
import functools
import math

import numpy as np
import jax
import jax.numpy as jnp
from jax import lax
from jax.experimental import pallas as pl
from jax.experimental.pallas import tpu as pltpu

F32 = jnp.float32
BF16 = jnp.bfloat16

D_MODEL = 1024
N_HEADS = 16
HEAD_DIM = 64
N_KV_HEADS = 4
GROUP = N_HEADS // N_KV_HEADS
CMP_BLOCK = 32
CMP_STRIDE = 16
CMP_RATIO = CMP_BLOCK // CMP_STRIDE
CMP_HIDDEN = 2 * HEAD_DIM
SEL_BLOCK = 64
SEL_RATIO = SEL_BLOCK // CMP_STRIDE
N_SELECT = 16
WINDOW = 512
FORCE_BONUS = 1e4
Q_BLOCK = 128
N_KV_SLOTS = 4
N_WIN_SLOTS = 2
N_Q_COLS = N_HEADS * HEAD_DIM
N_KV_COLS = (N_KV_SLOTS + N_WIN_SLOTS) * N_KV_HEADS * HEAD_DIM
N_GATE_COLS = 3 * N_HEADS
KV_ROW = N_KV_HEADS * HEAD_DIM
N_BUCKETS = 32
MAX_DISTANCE = 128
SSM_D_INNER = 2 * D_MODEL
SSM_HEAD_DIM = 64
SSM_HEADS = SSM_D_INNER // SSM_HEAD_DIM
SSM_GROUPS = 8
SSM_STATE = 128
SSM_CONV = 4
SSM_CONV_DIM = SSM_D_INNER + 2 * SSM_GROUPS * SSM_STATE
SSM_CHUNK = 128
D_FF = 4 * D_MODEL
N_MOD = 6
EPS = 1e-6

LANES = 128
NEG = -1e30
LOG2E = math.log2(math.e)
VMEM_LIMIT = 56 * 1024 * 1024
NEAR = 2 * LANES

_NT = (((1,), (1,)), ((), ()))


def _cparams(sem):
    return pltpu.CompilerParams(dimension_semantics=sem, vmem_limit_bytes=VMEM_LIMIT)


def _rms(x, g):
    return x * lax.rsqrt(jnp.mean(x * x, axis=-1, keepdims=True) + EPS) * g


def _silu(x):
    return x * jax.nn.sigmoid(x)


def _split_bf16(x, n):
    parts = []
    for _ in range(n - 1):
        p = x.astype(BF16)
        parts.append(p)
        x = x - p.astype(F32)
    parts.append(x.astype(BF16))
    return parts


def _ada_kernel(c_ref, w_ref, b_ref, o_ref):
    s = _silu(c_ref[...]).astype(BF16)
    o_ref[...] = jnp.dot(s, w_ref[...].astype(BF16), preferred_element_type=F32) + b_ref[...]


def _ada(c, w, b, tn=1024):
    m, d = c.shape
    n = w.shape[1]
    return pl.pallas_call(
        _ada_kernel,
        grid=(n // tn,),
        in_specs=[pl.BlockSpec((m, d), lambda j: (0, 0)),
                  pl.BlockSpec((d, tn), lambda j: (0, j)),
                  pl.BlockSpec((1, tn), lambda j: (0, j))],
        out_specs=pl.BlockSpec((m, tn), lambda j: (0, j)),
        out_shape=jax.ShapeDtypeStruct((m, n), F32),
        compiler_params=_cparams(("arbitrary",)),
    )(c, w, b.reshape(1, n))


def _nm_matmul_kernel(x_ref, g_ref, sh_ref, sc_ref, w_ref, o_ref, h_ref):
    @pl.when(pl.program_id(1) == 0)
    def _():
        h = _rms(x_ref[...], g_ref[...]) * (1.0 + sc_ref[0]) + sh_ref[0]
        h_ref[...] = h.astype(BF16)

    o_ref[...] = jnp.dot(h_ref[...], w_ref[...], preferred_element_type=F32)


def _nm_matmul(x, g, shift, scale, w, tm, tn):
    m, d = x.shape
    n = w.shape[1]
    nb = shift.shape[0]
    tpb = m // nb // tm
    mod_spec = pl.BlockSpec((1,) + shift.shape[1:], lambda i, j: (i // tpb, 0, 0))
    return pl.pallas_call(
        _nm_matmul_kernel,
        grid=(m // tm, n // tn),
        in_specs=[pl.BlockSpec((tm, d), lambda i, j: (i, 0)),
                  pl.BlockSpec((1, d), lambda i, j: (0, 0)),
                  mod_spec, mod_spec,
                  pl.BlockSpec((d, tn), lambda i, j: (0, j))],
        out_specs=pl.BlockSpec((tm, tn), lambda i, j: (i, j)),
        out_shape=jax.ShapeDtypeStruct((m, n), F32),
        scratch_shapes=[pltpu.VMEM((tm, d), BF16)],
        compiler_params=_cparams(("arbitrary", "arbitrary")),
    )(x, g.reshape(1, d), shift, scale, w)


def _mm_norm_res_kernel(a_ref, w_ref, g_ref, gate_ref, x_ref, o_ref):
    y = jnp.dot(a_ref[...].astype(BF16), w_ref[...], preferred_element_type=F32)
    o_ref[...] = x_ref[...] + gate_ref[0] * _rms(y, g_ref[...])


def _mm_norm_res(a, w, g, gate, x, tm):
    m, k = a.shape
    d = w.shape[1]
    nb = gate.shape[0]
    tpb = m // nb // tm
    return pl.pallas_call(
        _mm_norm_res_kernel,
        grid=(m // tm,),
        in_specs=[pl.BlockSpec((tm, k), lambda i: (i, 0)),
                  pl.BlockSpec((k, d), lambda i: (0, 0)),
                  pl.BlockSpec((1, d), lambda i: (0, 0)),
                  pl.BlockSpec((1,) + gate.shape[1:], lambda i: (i // tpb, 0, 0)),
                  pl.BlockSpec((tm, d), lambda i: (i, 0))],
        out_specs=pl.BlockSpec((tm, d), lambda i: (i, 0)),
        out_shape=jax.ShapeDtypeStruct((m, d), F32),
        compiler_params=_cparams(("arbitrary",)),
    )(a, w, g.reshape(1, d), gate, x)


def _mlp_kernel(x_ref, g1_ref, sh_ref, sc_ref, w1_ref, w2_ref, g2_ref, gate_ref, o_ref, h_ref, acc_ref):
    c = pl.program_id(1)

    @pl.when(c == 0)
    def _():
        h = _rms(x_ref[...], g1_ref[...]) * (1.0 + sc_ref[0]) + sh_ref[0]
        h_ref[...] = h.astype(BF16)
        acc_ref[...] = jnp.zeros_like(acc_ref)

    a = jnp.dot(h_ref[...], w1_ref[...], preferred_element_type=F32)
    a = jnp.square(jnp.maximum(a, 0.0)).astype(BF16)
    acc_ref[...] += jnp.dot(a, w2_ref[...], preferred_element_type=F32)

    @pl.when(c == pl.num_programs(1) - 1)
    def _():
        o_ref[...] = x_ref[...] + gate_ref[0] * _rms(acc_ref[...], g2_ref[...])


def _mlp(x, g1, shift, scale, w1, w2, g2, gate, tm, tf):
    m, d = x.shape
    f = w1.shape[1]
    nb = shift.shape[0]
    tpb = m // nb // tm
    mod_spec = pl.BlockSpec((1,) + shift.shape[1:], lambda i, c: (i // tpb, 0, 0))
    vec_spec = pl.BlockSpec((1, d), lambda i, c: (0, 0))
    return pl.pallas_call(
        _mlp_kernel,
        grid=(m // tm, f // tf),
        in_specs=[pl.BlockSpec((tm, d), lambda i, c: (i, 0)), vec_spec, mod_spec, mod_spec,
                  pl.BlockSpec((d, tf), lambda i, c: (0, c)),
                  pl.BlockSpec((tf, d), lambda i, c: (c, 0)),
                  vec_spec, mod_spec],
        out_specs=pl.BlockSpec((tm, d), lambda i, c: (i, 0)),
        out_shape=jax.ShapeDtypeStruct((m, d), F32),
        scratch_shapes=[pltpu.VMEM((tm, d), BF16), pltpu.VMEM((tm, d), F32)],
        compiler_params=_cparams(("arbitrary", "arbitrary")),
    )(x, g1.reshape(1, d), shift, scale, w1, w2, g2.reshape(1, d), gate)


def _bias_kernel(oh_ref, t_ref, o_ref):
    t = t_ref[...]
    t = t - t[N_BUCKETS - 1:N_BUCKETS, :]
    oh = oh_ref[...]
    o_ref[...] = sum(jnp.dot(oh, p, preferred_element_type=F32) for p in _split_bf16(t, 3))


def _rel_bucket_np(dist):
    n = np.maximum(dist, 0)
    exact = N_BUCKETS // 2
    nf = np.maximum(n, 1).astype(np.float32)
    large = exact + (np.log(nf / exact) / math.log(MAX_DISTANCE / exact) * (N_BUCKETS - exact)).astype(np.int32)
    return np.where(n < exact, n, np.minimum(large, N_BUCKETS - 1))


def _bias_vector(rel_bias):
    assert _rel_bucket_np(np.array([NEAR - 1]))[0] == N_BUCKETS - 1 == _rel_bucket_np(np.array([MAX_DISTANCE]))[0]
    oh = np.zeros((NEAR + 8, LANES), np.float32)
    oh[np.arange(NEAR), _rel_bucket_np(np.arange(NEAR))] = 1.0
    oh[NEAR:, N_BUCKETS - 1] = 1.0
    table = jnp.zeros((LANES, LANES), F32).at[:N_BUCKETS, :N_HEADS].set(rel_bias.astype(F32))
    bv = pl.pallas_call(
        _bias_kernel,
        out_shape=jax.ShapeDtypeStruct((NEAR + 8, LANES), F32),
    )(jnp.asarray(oh, BF16), table)
    return bv[:NEAR + 1, :N_HEADS]


def _toeplitz(bvz, idx):
    r, c = idx.shape
    t = jnp.take(bvz, jnp.asarray(idx.reshape(-1), jnp.int32), axis=0).reshape(r, c, N_KV_HEADS, GROUP)
    return jnp.transpose(t, (2, 3, 0, 1)).reshape(N_KV_HEADS, GROUP * r, c)


SUB = 8
HEADS_PER_CG = LANES // HEAD_DIM
N_CG = 2 * KV_ROW // LANES
CG_PER_SLOT = KV_ROW // LANES
CG_HIDDEN = HEADS_PER_CG * CMP_HIDDEN


def _compress_parts(rows_s, wbd_ref, nh):
    part0 = jnp.zeros((nh, CG_HIDDEN), F32)
    part1 = jnp.zeros((nh, CG_HIDDEN), F32)
    for s in range(0, CMP_STRIDE, 2):
        xs = jnp.concatenate([rows_s(s), rows_s(s + 1)], axis=1).astype(BF16)
        part0 = part0 + jnp.dot(xs, wbd_ref[0, 0, s // 2], preferred_element_type=F32)
        part1 = part1 + jnp.dot(xs, wbd_ref[0, 1, s // 2], preferred_element_type=F32)
    return part0, part1


def _compress_finish(part0, part1, pos_ref, w1_ref, w2bd_ref, nh):
    pre0 = jnp.dot(pos_ref[0].astype(BF16), w1_ref[0].astype(BF16), preferred_element_type=F32)[0:1]
    pre = jnp.concatenate([pre0] * HEADS_PER_CG, axis=1) + part0 + pltpu.roll(part1, nh - 1, axis=0)
    return jnp.dot(_silu(pre).astype(BF16), w2bd_ref[0], preferred_element_type=F32)


def _compress_kernel(x_ref, pos_ref, w1_ref, wbd_ref, w2bd_ref, o_ref, *, nh):
    part0, part1 = _compress_parts(lambda s: x_ref[pl.ds(s, nh, stride=CMP_STRIDE), :], wbd_ref, nh)
    o_ref[0, 0] = _compress_finish(part0, part1, pos_ref, w1_ref, w2bd_ref, nh)


def _compress_weights(cmp_pos, cmp_w1, cmp_w2):
    eye = jnp.eye(HEADS_PER_CG, dtype=F32)
    w1r = cmp_w1.reshape(2, CMP_RATIO, CMP_STRIDE, HEAD_DIM, CMP_HIDDEN)
    wbd = jnp.einsum('kl,artdh->artkdlh', eye, w1r).reshape(
        2, CMP_RATIO, CMP_STRIDE // 2, 2 * LANES, CG_HIDDEN).astype(BF16)
    w2bd = jnp.einsum('kl,ahd->akhld', eye, cmp_w2).reshape(2, CG_HIDDEN, LANES).astype(BF16)
    pos = jnp.zeros((2, SUB, CMP_BLOCK * HEAD_DIM), F32).at[:, 0].set(cmp_pos.reshape(2, -1))
    return pos, cmp_w1, wbd, w2bd


def _compress_specs(cw, imap):
    def spec(a):
        return pl.BlockSpec((1,) + a.shape[1:], lambda *g: (imap(*g) // CG_PER_SLOT,) + (0,) * (a.ndim - 1))
    return [spec(a) for a in cw]


def _pack_cmp(cmp, dtype=None):
    nb, _, n, _ = cmp.shape
    c = cmp.reshape(nb, 2, CG_PER_SLOT, n, HEADS_PER_CG, HEAD_DIM)
    c = jnp.transpose(c, (0, 2, 4, 3, 1, 5)).reshape(nb, N_KV_HEADS, n, 2 * HEAD_DIM)
    return c.astype(BF16 if dtype is None else dtype)


def _compress_prompt(proj, cw, nb, t):
    nh = t // CMP_STRIDE
    col0 = N_Q_COLS // LANES
    return pl.pallas_call(
        functools.partial(_compress_kernel, nh=nh),
        grid=(nb, N_CG),
        in_specs=[pl.BlockSpec((t, LANES), lambda b, c: (b, col0 + c))] + _compress_specs(cw, lambda b, c: c),
        out_specs=pl.BlockSpec((1, 1, nh, LANES), lambda b, c: (b, c, 0, 0)),
        out_shape=jax.ShapeDtypeStruct((nb, N_CG, nh, LANES), F32),
        compiler_params=_cparams(("arbitrary", "arbitrary")),
    )(proj, *cw)


KEY_TILE = 4 * Q_BLOCK
FRONT_PAD = KEY_TILE
SUBTILES = KEY_TILE // Q_BLOCK
BAND_OFF = 2 * Q_BLOCK // CMP_STRIDE
BAND_W = 32
Q_PER_CMP = Q_BLOCK // CMP_STRIDE


def _nsa_consts(t):
    ncp = t // CMP_STRIDE
    nblk = t // SEL_BLOCK
    nbp = -(-nblk // LANES) * LANES
    at = np.zeros((nbp, ncp), np.float32)
    for j in range(nblk):
        for n in range(SEL_RATIO * j - (CMP_RATIO - 1), SEL_RATIO * (j + 1)):
            if 0 <= n < ncp - CMP_RATIO + 1:
                at[j, n] = 1.0
    key_blk = np.arange(t) // SEL_BLOCK
    e = (np.arange(nbp)[:, None] == key_blk[None, :]).astype(np.float32)
    e1 = e.reshape(nbp, t // LANES, LANES).transpose(1, 0, 2)
    e1 = np.concatenate([np.zeros((SUBTILES,) + e1.shape[1:], np.float32), e1], axis=0)
    band = np.zeros((2 * ncp + 2 * Q_PER_CMP, LANES), np.float32)
    u = np.arange(BAND_W)
    band[u + ncp, HEAD_DIM + u] = 1.0
    band[u + ncp, HEAD_DIM + BAND_W + u] = 1.0
    r = np.arange(Q_BLOCK)[:, None]
    c = np.arange(LANES)[None, :]
    d0 = r - c
    idx0 = np.where(d0 >= 0, d0, NEAR)
    idx1 = Q_BLOCK + r - c
    dc = r - CMP_STRIDE * c + (CMP_STRIDE * BAND_OFF - CMP_BLOCK + 1)
    idxc = np.where((dc >= 0) & (dc < NEAR), dc, NEAR)
    assert np.all(idxc[:, BAND_W:] == NEAR)
    bf = lambda a: jnp.asarray(a, BF16)
    return dict(at=bf(at), e1=bf(e1), band=jnp.asarray(band),
                idx0=idx0, idx1=idx1, idxc=idxc, ncp=ncp, nbp=nbp, nblk=nblk)


def _nsa_kernel(q_ref, gl_ref, kck_ref, kvc_ref, kvs_ref, kvw_ref, at_ref, e1_ref, tb_ref, cbq_ref, band_ref,
                o_ref, *, ncp, nbp, n_sel):
    i = pl.program_id(2)
    rows = GROUP * Q_BLOCK
    lane = lax.broadcasted_iota(jnp.int32, (1, LANES), 1)
    low = lane < HEAD_DIM
    r_col = lax.broadcasted_iota(jnp.int32, (rows, 1), 0) % Q_BLOCK
    q_pos = i * Q_BLOCK + r_col

    q = q_ref[...] * (HEAD_DIM ** -0.5 * LOG2E)
    parts = []
    for h in range(GROUP * HEAD_DIM // LANES):
        qh = q[:, h * LANES:(h + 1) * LANES]
        parts.append(jnp.where(low, qh, 0.0))
        parts.append(jnp.where(low, pltpu.roll(qh, HEAD_DIM, axis=1), 0.0))
    qs = jnp.concatenate(parts, axis=0).astype(BF16)

    qc = jnp.where(low, qs, cbq_ref[0])
    start = pl.multiple_of(ncp + BAND_OFF - Q_PER_CMP * i, 8)
    kq = (kck_ref[0, 0] + band_ref[pl.ds(start, ncp), :]).astype(BF16)
    s = lax.dot_general(qc, kq, _NT, preferred_element_type=F32)
    cmp_end = lax.broadcasted_iota(jnp.int32, (1, ncp), 1) * CMP_STRIDE + (CMP_BLOCK - 1)
    mask = cmp_end <= q_pos
    m = jnp.max(jnp.where(mask, s, NEG), axis=-1, keepdims=True)
    e = jnp.where(mask, jnp.exp2(s - m), 0.0)
    p_cmp = e / jnp.maximum(jnp.sum(e, axis=-1, keepdims=True), 1e-30)
    o_cmp = jnp.dot(p_cmp.astype(BF16), kvc_ref[0, 0], preferred_element_type=F32)
    imp = p_cmp[0:Q_BLOCK]
    for g in range(1, GROUP):
        imp = imp + p_cmp[g * Q_BLOCK:(g + 1) * Q_BLOCK]
    bst = sum(lax.dot_general(at_ref[...], part, _NT, preferred_element_type=F32)
              for part in _split_bf16(imp, 2))

    jb = lax.broadcasted_iota(jnp.int32, (nbp, Q_BLOCK), 0)
    q_blk = (i * Q_BLOCK + lax.broadcasted_iota(jnp.int32, (nbp, Q_BLOCK), 1)) // SEL_BLOCK
    causal = jb <= q_blk
    forced = (jb == 0) | (jb == q_blk) | (jb == q_blk - 1)
    work = jnp.where(causal, bst + jnp.where(forced, FORCE_BONUS, 0.0), -jnp.inf)
    jbf = jb.astype(F32)
    sel_t = jnp.zeros((nbp, Q_BLOCK), F32)
    for _ in range(n_sel):
        top = jnp.max(work, axis=0, keepdims=True)
        first = jnp.min(jnp.where(work == top, jbf, float(nbp)), axis=0, keepdims=True)
        hit = jbf == first
        sel_t = jnp.where(hit, 1.0, sel_t)
        work = jnp.where(hit, -jnp.inf, work)
    sel = jnp.where(causal, sel_t, 0.0).T.astype(BF16)

    def ones_and_values(kv):
        return jnp.where(low, jnp.ones_like(kv), kv)

    def normalise(acc):
        return jnp.where(low, 0.0, acc / jnp.maximum(pltpu.roll(acc, HEAD_DIM, axis=1), 1e-30))

    n_wt = WINDOW // Q_BLOCK
    kw = kvw_ref[0, 0, pl.ds(pl.multiple_of((i + SUBTILES - n_wt) * Q_BLOCK, Q_BLOCK), WINDOW + Q_BLOCK), :]
    s = lax.dot_general(qs, kw, _NT, preferred_element_type=F32)
    pieces = []
    for u in range(n_wt + 1):
        su = s[:, u * Q_BLOCK:(u + 1) * Q_BLOCK]
        if u == n_wt:
            su = jnp.where(lane <= r_col, su + tb_ref[0, 0], NEG)
        else:
            if u == n_wt - 1:
                su = su + tb_ref[0, 1]
            su = su + jnp.where(i - n_wt + u >= 0, 0.0, NEG)
            if u == 0:
                su = jnp.where(lane > r_col, su, NEG)
        pieces.append(su)
    sm = jnp.concatenate(pieces, axis=1)
    e = jnp.exp2((sm - jnp.max(sm, axis=-1, keepdims=True)).astype(BF16))
    o_win = normalise(jnp.dot(e, ones_and_values(kw), preferred_element_type=F32))

    n_tiles = (i + SUBTILES) // SUBTILES

    def keys(u):
        return kvs_ref[0, 0, pl.ds(pl.multiple_of((i + 1 - SUBTILES * u) * Q_BLOCK, Q_BLOCK), KEY_TILE), :]

    def masked_scores(u, diagonal=False):
        t0p = i + 1 - SUBTILES * u
        mk = jnp.concatenate([jnp.dot(sel, e1_ref[t0p + v], preferred_element_type=F32) for v in range(SUBTILES)],
                             axis=1).astype(BF16)
        if diagonal:
            s3 = lax.dot_general(qs, keys(u), _NT, preferred_element_type=F32).reshape(GROUP, Q_BLOCK, KEY_TILE)
            r3 = lax.broadcasted_iota(jnp.int32, (1, Q_BLOCK, LANES), 1)
            c3 = lax.broadcasted_iota(jnp.int32, (1, Q_BLOCK, LANES), 2)
            tb0 = tb_ref[0, 0].reshape(GROUP, Q_BLOCK, LANES)
            tb1 = tb_ref[0, 1].reshape(GROUP, Q_BLOCK, LANES)
            s3 = jnp.concatenate([s3[:, :, :KEY_TILE - 2 * Q_BLOCK],
                                  s3[:, :, KEY_TILE - 2 * Q_BLOCK:KEY_TILE - Q_BLOCK] + tb1,
                                  jnp.where(c3 <= r3, s3[:, :, KEY_TILE - Q_BLOCK:] + tb0, NEG)], axis=2).astype(BF16)
        else:
            s3 = lax.dot_general(qs, keys(u), _NT, preferred_element_type=F32).astype(BF16).reshape(
                GROUP, Q_BLOCK, KEY_TILE)
        return jnp.where((mk > 0.5)[None], s3, NEG).reshape(rows, KEY_TILE)

    def sel_body(u, carry):
        m, acc, sm, e_prev = carry
        pv = jnp.dot(e_prev, ones_and_values(keys(jnp.maximum(u - 1, 0))), preferred_element_type=F32)
        sm_next = masked_scores(jnp.minimum(u + 1, n_tiles - 1))
        m_new = jnp.maximum(m, jnp.max(sm, axis=-1, keepdims=True).astype(F32))
        alpha = jnp.exp2(m - m_new)
        e = jnp.exp2(sm - m_new.astype(BF16))
        return m_new, alpha * (acc + pv), sm_next, e

    init = (jnp.full((rows, 1), NEG, F32), jnp.zeros((rows, LANES), F32),
            masked_scores(0, diagonal=True), jnp.zeros((rows, KEY_TILE), BF16))
    m, acc, _, e_last = lax.fori_loop(0, n_tiles, sel_body, init)
    o_sel = normalise(acc + jnp.dot(e_last, ones_and_values(keys(n_tiles - 1)), preferred_element_type=F32))

    sg = jax.nn.sigmoid(gl_ref[...])
    for g in range(GROUP):
        out = jnp.zeros((Q_BLOCK, LANES), F32)
        for j, ob in enumerate((o_cmp, o_sel, o_win)):
            out = out + sg[:, 3 * g + j:3 * g + j + 1] * ob[g * Q_BLOCK:(g + 1) * Q_BLOCK]
        o_ref[0, 0, g] = out.astype(o_ref.dtype)


GATE_COL0 = N_Q_COLS + N_KV_COLS
NP_ATTN = GATE_COL0 + N_KV_HEADS * LANES


def _nsa_prompt(proj, kck, kvc, kvs, kvw, tb, cbq, consts, nb, t):
    ni = t // Q_BLOCK
    ncp, nbp = consts['ncp'], consts['nbp']
    gcol = GATE_COL0 // LANES
    qw = GROUP * HEAD_DIM
    full = lambda a: pl.BlockSpec(a.shape, lambda b, k, i: (0,) * a.ndim)
    per_kvh = lambda a: pl.BlockSpec((1,) + a.shape[1:], lambda b, k, i: (k,) + (0,) * (a.ndim - 1))
    per_bk = lambda a: pl.BlockSpec((1, 1) + a.shape[2:], lambda b, k, i: (b, k, 0, 0))
    c = consts
    n_sel = min(N_SELECT, c['nblk'])
    assert n_sel >= 3
    return pl.pallas_call(
        functools.partial(_nsa_kernel, ncp=ncp, nbp=nbp, n_sel=n_sel),
        grid=(nb, N_KV_HEADS, ni),
        in_specs=[pl.BlockSpec((Q_BLOCK, qw), lambda b, k, i: (b * ni + i, k)),
                  pl.BlockSpec((Q_BLOCK, LANES), lambda b, k, i: (b * ni + i, gcol + k)),
                  per_bk(kck), per_bk(kvc), per_bk(kvs), per_bk(kvw),
                  full(c['at']), full(c['e1']), per_kvh(tb), per_kvh(cbq), full(c['band'])],
        out_specs=pl.BlockSpec((1, 1, GROUP, Q_BLOCK, LANES), lambda b, k, i: (b, k, 0, i, 0)),
        out_shape=jax.ShapeDtypeStruct((nb, N_KV_HEADS, GROUP, t, LANES), BF16),
        compiler_params=_cparams(("arbitrary", "arbitrary", "arbitrary")),
    )(proj, proj, kck, kvc, kvs, kvw, c['at'], c['e1'], tb, cbq, c['band'])


def _heads_out_kernel(a_ref, w_ref, g_ref, gate_ref, x_ref, o_ref):
    y = jnp.zeros(o_ref.shape, F32)
    for k in range(N_KV_HEADS):
        for g in range(0, GROUP, 2):
            a = jnp.concatenate([a_ref[0, k, g], a_ref[0, k, g + 1]], axis=1)
            h = k * GROUP + g
            w = w_ref[h * LANES:(h + 2) * LANES, :]
            y = y + jnp.dot(a, w, preferred_element_type=F32)
    o_ref[...] = x_ref[...] + gate_ref[0] * _rms(y, g_ref[...])


def _heads_out(a, w_pad, g, gate, x, tm):
    nb, _, _, t, _ = a.shape
    d = w_pad.shape[1]
    tpb = t // tm
    return pl.pallas_call(
        _heads_out_kernel,
        grid=(nb * tpb,),
        in_specs=[pl.BlockSpec((1, N_KV_HEADS, GROUP, tm, LANES), lambda i: (i // tpb, 0, 0, i % tpb, 0)),
                  pl.BlockSpec(w_pad.shape, lambda i: (0, 0)),
                  pl.BlockSpec((1, d), lambda i: (0, 0)),
                  pl.BlockSpec((1,) + gate.shape[1:], lambda i: (i // tpb, 0, 0)),
                  pl.BlockSpec((tm, d), lambda i: (i, 0))],
        out_specs=pl.BlockSpec((tm, d), lambda i: (i, 0)),
        out_shape=jax.ShapeDtypeStruct((nb * t, d), F32),
        compiler_params=_cparams(("arbitrary",)),
    )(a, w_pad, g.reshape(1, d), gate, x)


def _row_tile(m, cap):
    return m if m <= cap else cap


def _pack_kv(kv6, s0):
    pair = jnp.transpose(kv6[:, :, s0:s0 + 2], (0, 3, 1, 2, 4))
    pair = pair.reshape(pair.shape[0], N_KV_HEADS, pair.shape[2], 2 * HEAD_DIM).astype(BF16)
    return jnp.pad(pair, ((0, 0), (0, 0), (FRONT_PAD, 0), (0, 0)))


def _attn_in_weights(w_in):
    gates = w_in[:, GATE_COL0:].reshape(w_in.shape[0], N_KV_HEADS, GROUP * 3)
    gates = jnp.pad(gates, ((0, 0), (0, 0), (0, LANES - GROUP * 3))).reshape(w_in.shape[0], -1)
    return jnp.concatenate([w_in[:, :GATE_COL0], gates], axis=1).astype(BF16)


def _attn_out_weights(w_out):
    w = w_out.reshape(N_HEADS, HEAD_DIM, w_out.shape[1])
    return jnp.pad(w, ((0, 0), (LANES - HEAD_DIM, 0), (0, 0))).reshape(N_HEADS * LANES, -1).astype(BF16)


def _nsa_prompt_mixer(xp, g, shift, scale, w_in, cw, bvz, nb, t):
    proj = _nm_matmul(xp, g, shift, scale, w_in, _row_tile(t, 512), NP_ATTN)
    kv6 = proj[:, N_Q_COLS:GATE_COL0].reshape(nb, t, N_KV_SLOTS + N_WIN_SLOTS, N_KV_HEADS, HEAD_DIM)
    cmp = _compress_prompt(proj, cw, nb, t)
    kvc = _pack_cmp(cmp)
    kck = jnp.pad(_pack_cmp(cmp, F32)[..., :HEAD_DIM], ((0, 0), (0, 0), (0, 0), (0, LANES - HEAD_DIM)))
    kvs = _pack_kv(kv6, 2)
    kvw = _pack_kv(kv6, N_KV_SLOTS)
    c = _nsa_consts(t)
    bv2 = bvz * LOG2E
    tb = jnp.stack([_toeplitz(bv2, c['idx0']), _toeplitz(bv2, c['idx1'])], axis=1)
    hi, lo = _split_bf16(_toeplitz(bv2, c['idxc']), 2)
    cbq = jnp.concatenate([jnp.zeros(hi.shape[:2] + (HEAD_DIM,), BF16), hi[..., :BAND_W], lo[..., :BAND_W]], axis=-1)
    o = _nsa_prompt(proj, kck, kvc, kvs, kvw, tb, cbq, c, nb, t)
    return o, kv6


SC_PAGES = 32
CG_PER_PAGE = N_KV_SLOTS * KV_ROW // LANES


def _cache_tiles(cache_kv):
    page = cache_kv.shape[2]
    assert page == LANES
    return jnp.transpose(cache_kv, (0, 1, 3, 4, 5, 2)).reshape(-1, page)


def _compress_sample_kernel(pt_ref, cache_ref, pos_ref, w1_ref, wbd_ref, w2bd_ref, o_ref, tbuf_ref, buf_ref,
                            sem_ref, p0_ref, p1_ref, *, n_pages, page, row_base):
    step = pl.program_id(0) * N_CG + pl.program_id(1)
    nsteps = pl.num_programs(0) * N_CG
    chunk_pages = min(SC_PAGES, n_pages // 2)
    nch = n_pages // chunk_pages
    hpc = chunk_pages * page // CMP_STRIDE
    nh = nch * hpc

    def copies(step_, ch, slot):
        b_, c_ = step_ // N_CG, step_ % N_CG
        out = []
        for p in range(chunk_pages):
            pg = pt_ref[b_, ch * chunk_pages + p]
            out.append(pltpu.make_async_copy(
                cache_ref.at[pl.ds(((row_base + pg) * CG_PER_PAGE + c_) * LANES, LANES), :],
                tbuf_ref.at[slot, p], sem_ref.at[slot]))
        return out

    @pl.when(step == 0)
    def _():
        for cp in copies(step, 0, 0):
            cp.start()

    for ch in range(nch):
        slot = ch % 2
        if ch + 1 < nch:
            for cp in copies(step, ch + 1, 1 - slot):
                cp.start()
        else:
            @pl.when(step + 1 < nsteps)
            def _():
                for cp in copies(step + 1, 0, 1 - slot):
                    cp.start()
        for cp in copies(step, ch, slot):
            cp.wait()

        for p in range(chunk_pages):
            buf_ref[p * page:(p + 1) * page, :] = tbuf_ref[slot, p].T
        p0, p1 = _compress_parts(lambda s: buf_ref[pl.ds(s, hpc, stride=CMP_STRIDE), :], wbd_ref, hpc)
        p0_ref[ch * hpc:(ch + 1) * hpc] = p0
        p1_ref[ch * hpc:(ch + 1) * hpc] = p1

    o_ref[0, 0] = _compress_finish(p0_ref[...], p1_ref[...], pos_ref, w1_ref, w2bd_ref, nh)


def _compress_sample(page_table, cache_t, cw, page, row_base):
    nb, n_pages = page_table.shape
    chunk_pages = min(SC_PAGES, n_pages // 2)
    assert n_pages % (2 * chunk_pages) == 0
    nh = n_pages * page // CMP_STRIDE
    grid_spec = pltpu.PrefetchScalarGridSpec(
        num_scalar_prefetch=1,
        grid=(nb, N_CG),
        in_specs=[pl.BlockSpec(memory_space=pl.ANY)] + _compress_specs(cw, lambda b, c, pt: c),
        out_specs=pl.BlockSpec((1, 1, nh, LANES), lambda b, c, pt: (b, c, 0, 0)),
        scratch_shapes=[pltpu.VMEM((2, chunk_pages, LANES, page), F32),
                        pltpu.VMEM((chunk_pages * page, LANES), F32),
                        pltpu.SemaphoreType.DMA((2,)),
                        pltpu.VMEM((nh, CG_HIDDEN), F32),
                        pltpu.VMEM((nh, CG_HIDDEN), F32)])
    return pl.pallas_call(
        functools.partial(_compress_sample_kernel, n_pages=n_pages, page=page, row_base=row_base),
        grid_spec=grid_spec,
        out_shape=jax.ShapeDtypeStruct((nb, N_CG, nh, LANES), F32),
        compiler_params=_cparams(("arbitrary", "arbitrary")),
    )(page_table, cache_t, *cw)


def _softmax_with_new_key(s, s_new):
    m = jnp.maximum(jnp.max(s, axis=-1, keepdims=True), s_new)
    e = jnp.exp(s - m)
    e_new = jnp.exp(s_new - m)
    inv = 1.0 / jnp.maximum(jnp.sum(e, axis=-1, keepdims=True) + e_new, 1e-30)
    return e * inv, e_new * inv


def _nsa_sample_kernel(pt_ref, qs_ref, q_ref, gl_ref, new_ref, ncol_ref, kvc_ref, cache_ref, win_ref, cbs_ref,
                       sbz_ref, wb_ref, b0_ref, as_ref, o_ref, wout_ref, gk_ref, gv_ref, sem_ref, idx_ref,
                       *, past, page, row_base, n_pick):
    b = pl.program_id(0)
    blk_per_page = page // SEL_BLOCK
    nbs = past // SEL_BLOCK
    ncs = kvc_ref.shape[2]
    row = lax.broadcasted_iota(jnp.int32, (SUB, 1), 0)

    cmp_ok = lax.broadcasted_iota(jnp.int32, (SUB, ncs), 1) * CMP_STRIDE + (CMP_BLOCK - 1) <= past
    imp = jnp.zeros((SUB, ncs), F32)
    o_cmp = []
    for k in range(N_KV_HEADS):
        kc = kvc_ref[0, k]
        s = lax.dot_general(qs_ref[0, k], kc, _NT, preferred_element_type=F32) + cbs_ref[k]
        m = jnp.max(jnp.where(cmp_ok, s, NEG), axis=-1, keepdims=True)
        e = jnp.where(cmp_ok, jnp.exp(s - m), 0.0)
        p = e / jnp.maximum(jnp.sum(e, axis=-1, keepdims=True), 1e-30)
        o_cmp.append(jnp.dot(p.astype(BF16), kc, preferred_element_type=F32))
        imp_k = p[0:1]
        for g in range(1, GROUP):
            imp_k = imp_k + p[g:g + 1]
        imp = imp + jnp.where(row == k, imp_k, 0.0)
    bs = sum(jnp.dot(part, as_ref[...], preferred_element_type=F32) for part in _split_bf16(imp, 3))

    lane_b = lax.broadcasted_iota(jnp.int32, (SUB, nbs), 1)
    lane_f = lane_b.astype(F32)
    score = bs + jnp.where((lane_b == 0) | (lane_b == nbs - 1), FORCE_BONUS, 0.0)

    def gathers(k, r):
        j = idx_ref[k * n_pick + r]
        pg = pt_ref[b, j // blk_per_page]
        tile0 = (row_base + pg) * N_KV_SLOTS
        return [pltpu.make_async_copy(
            cache_ref.at[pl.ds(((tile0 + slot) * N_KV_HEADS + k) * HEAD_DIM, HEAD_DIM), :],
            buf.at[k * n_pick + r], sem_ref.at[0]) for slot, buf in ((2, gk_ref), (3, gv_ref))]

    for r in range(n_pick):
        m = jnp.max(score, axis=-1, keepdims=True)
        pick = jnp.min(jnp.where(score == m, lane_f, 1e9), axis=-1, keepdims=True)
        score = jnp.where(lane_f == pick, -jnp.inf, score)
        for k in range(N_KV_HEADS):
            idx_ref[k * n_pick + r] = jnp.sum(jnp.where(row == k, pick, 0.0)).astype(jnp.int32)
            for cp in gathers(k, r):
                cp.start()

    def new_row(k, j):
        return new_ref[0, k][j:j + 1].astype(BF16).astype(F32)

    tok = lax.broadcasted_iota(jnp.int32, (1, WINDOW), 1)
    o_win = []
    for k in range(N_KV_HEADS):
        for slot in range(N_WIN_SLOTS):
            shifted = pltpu.roll(win_ref[0, slot, k], WINDOW - 1, axis=1)
            wout_ref[0, slot, k] = jnp.where(tok == WINDOW - 1, ncol_ref[0, slot, k], shifted)
        q = q_ref[0, k]
        s = jnp.dot(q, win_ref[0, 0, k].astype(BF16), preferred_element_type=F32) + wb_ref[k]
        s = jnp.where(tok >= 1, s, NEG)
        s_new = jnp.sum(q.astype(F32) * new_row(k, 2), axis=-1, keepdims=True) + b0_ref[k][:, 0:1]
        p, p_new = _softmax_with_new_key(s, s_new)
        o_win.append(lax.dot_general(p.astype(BF16), win_ref[0, 1, k].astype(BF16), _NT,
                                     preferred_element_type=F32) + p_new * new_row(k, 3))

    for k in range(N_KV_HEADS):
        for r in range(n_pick):
            for cp in gathers(k, r):
                cp.wait()

    upper = lax.broadcasted_iota(jnp.int32, (1, LANES), 1) // SEL_BLOCK
    for k in range(N_KV_HEADS):
        q = q_ref[0, k]
        tiles = []
        for r in range(n_pick):
            j = idx_ref[k * n_pick + r]
            near = j - (nbs - NEAR // SEL_BLOCK)
            s = jnp.dot(q, gk_ref[k * n_pick + r].astype(BF16), preferred_element_type=F32)
            s = s + sbz_ref[k, jnp.where(near >= 0, near, NEAR // SEL_BLOCK)]
            tiles.append(jnp.where(upper == j % blk_per_page, s, NEG))
        s_new = jnp.sum(q.astype(F32) * new_row(k, 0), axis=-1, keepdims=True) + b0_ref[k][:, 0:1]
        p, p_new = _softmax_with_new_key(jnp.concatenate(tiles, axis=1), s_new)
        o_sel = p_new * new_row(k, 1)
        for r in range(n_pick):
            o_sel = o_sel + lax.dot_general(p[:, r * LANES:(r + 1) * LANES].astype(BF16),
                                            gv_ref[k * n_pick + r].astype(BF16), _NT, preferred_element_type=F32)
        g_cmp, g_sel, g_win = (jax.nn.sigmoid(gl_ref[0, k, j])[:, :HEAD_DIM] for j in range(3))
        o_ref[0, k] = g_cmp * o_cmp[k][:, HEAD_DIM:] + g_sel * o_sel + g_win * o_win[k]


def _bias_rows(bvz, idx):
    t = jnp.take(bvz, jnp.asarray(idx, jnp.int32), axis=0).reshape(len(idx), N_KV_HEADS, GROUP)
    t = jnp.transpose(t, (1, 2, 0))
    return jnp.pad(t, ((0, 0), (0, SUB - GROUP), (0, 0)))


def _nsa_sample(proj_s, kvc, page_table, cache_t, win_t, bvz, past, page, row_base):
    nb = proj_s.shape[0]
    n_pick = N_SELECT - 1
    nbs = past // SEL_BLOCK
    ncs = past // CMP_STRIDE
    n_near = NEAR // SEL_BLOCK
    blk_per_page = page // SEL_BLOCK
    assert past % page == 0 and page == LANES and nbs > n_pick + n_near and (nbs - n_near) % blk_per_page == 0
    assert win_t.shape[-1] == WINDOW < past
    q = (proj_s[:, :N_Q_COLS] * HEAD_DIM ** -0.5).astype(BF16).reshape(nb, N_KV_HEADS, GROUP, HEAD_DIM)
    q = jnp.pad(q, ((0, 0), (0, 0), (0, SUB - GROUP), (0, 0)))
    qs = jnp.pad(q, ((0, 0), (0, 0), (0, 0), (0, LANES - HEAD_DIM)))
    gl = proj_s[:, GATE_COL0:].reshape(nb, N_KV_HEADS, LANES)[:, :, :GROUP * 3].reshape(nb, N_KV_HEADS, GROUP, 3)
    gl = jnp.pad(jnp.transpose(gl, (0, 1, 3, 2)), ((0, 0), (0, 0), (0, 0), (0, SUB - GROUP)))
    gl = jnp.broadcast_to(gl[..., None], gl.shape + (LANES,))
    new = proj_s[:, N_Q_COLS + 2 * KV_ROW:GATE_COL0].reshape(nb, 4, N_KV_HEADS, HEAD_DIM)
    new_rows = jnp.transpose(new, (0, 2, 1, 3))
    new_cols = new[:, 2:].reshape(nb, N_WIN_SLOTS, N_KV_HEADS, HEAD_DIM, 1)

    dc = past - (np.arange(ncs) * CMP_STRIDE + CMP_BLOCK - 1)
    cbs = _bias_rows(bvz, np.where((dc >= 0) & (dc < NEAR), dc, NEAR))
    lane = np.arange(LANES)
    sb = []
    for jj in range(n_near + 1):
        d = NEAR - SEL_BLOCK * jj - lane % SEL_BLOCK
        ok = (lane // SEL_BLOCK == jj % blk_per_page) & (jj < n_near) & (d < NEAR)
        sb.append(_bias_rows(bvz, np.where(ok, d, NEAR)))
    sbz = jnp.stack(sb, axis=1)
    dw = WINDOW - np.arange(WINDOW)
    wb = _bias_rows(bvz, np.where(dw < NEAR, dw, NEAR))
    b0 = _bias_rows(bvz, np.zeros(LANES, np.int64))
    a_s = np.zeros((ncs, nbs), np.float32)
    for j in range(nbs):
        for n in range(SEL_RATIO * j - (CMP_RATIO - 1), SEL_RATIO * (j + 1)):
            if 0 <= n < ncs - CMP_RATIO + 1:
                a_s[n, j] = 1.0
    a_s = jnp.asarray(a_s, BF16)

    per_b = lambda a: pl.BlockSpec((1,) + a.shape[1:], lambda b, pt: (b,) + (0,) * (a.ndim - 1))
    full = lambda a: pl.BlockSpec(a.shape, lambda b, pt: (0,) * a.ndim)
    grid_spec = pltpu.PrefetchScalarGridSpec(
        num_scalar_prefetch=1,
        grid=(nb,),
        in_specs=[per_b(qs), per_b(q), per_b(gl), per_b(new_rows), per_b(new_cols), per_b(kvc),
                  pl.BlockSpec(memory_space=pl.ANY), per_b(win_t),
                  full(cbs), full(sbz), full(wb), full(b0), full(a_s)],
        out_specs=[pl.BlockSpec((1, N_KV_HEADS, SUB, HEAD_DIM), lambda b, pt: (b, 0, 0, 0)), per_b(win_t)],
        scratch_shapes=[pltpu.VMEM((N_KV_HEADS * n_pick, HEAD_DIM, page), F32),
                        pltpu.VMEM((N_KV_HEADS * n_pick, HEAD_DIM, page), F32),
                        pltpu.SemaphoreType.DMA((1,)),
                        pltpu.SMEM((N_KV_HEADS * n_pick,), jnp.int32)])
    o, win_out = pl.pallas_call(
        functools.partial(_nsa_sample_kernel, past=past, page=page, row_base=row_base, n_pick=n_pick),
        grid_spec=grid_spec,
        out_shape=[jax.ShapeDtypeStruct((nb, N_KV_HEADS, SUB, HEAD_DIM), F32),
                   jax.ShapeDtypeStruct(win_t.shape, F32)],
        compiler_params=_cparams(("arbitrary",)),
    )(page_table, qs, q, gl, new_rows, new_cols, kvc, cache_t, win_t, cbs, sbz, wb, b0, a_s)
    return o[:, :, :GROUP].reshape(nb, N_Q_COLS), win_out


NP_SSM = -(-(SSM_CONV_DIM + SSM_D_INNER + SSM_HEADS) // LANES) * LANES
SSM_COL_TILE = NP_SSM // 7
assert SSM_COL_TILE % LANES == 0 and SSM_COL_TILE * 7 == NP_SSM
N_BC = SSM_GROUPS * SSM_STATE
HEADS_PER_GROUP = SSM_HEADS // SSM_GROUPS
PAIR = LANES // SSM_HEAD_DIM
N_PAIRS = SSM_HEADS // PAIR
CONV_PAD = 8


def _softplus(x):
    return jnp.maximum(x, 0.0) + jnp.log1p(jnp.exp(-jnp.abs(x)))


def _cumsum_rows(x):
    n = x.shape[0]
    row = lax.broadcasted_iota(jnp.int32, x.shape, 0)
    s = 1
    while s < n:
        x = x + jnp.where(row >= s, pltpu.roll(x, s, axis=0), 0.0)
        s *= 2
    return x


def _ssd_kernel(xbc_ref, z_ref, dt_ref, h0_ref, cinit_ref, cw_ref, cb_ref, dtb_ref, alog_ref, dsk_ref, ng_ref,
                y_ref, hout_ref, xs_ref, act_ref, ybuf_ref, h_ref, *, nc):
    c = pl.program_id(1)
    q = SSM_CHUNK

    @pl.when(c == 0)
    def _():
        xs_ref[0:CONV_PAD] = cinit_ref[0]
        h_ref[...] = h0_ref[0]

    xs_ref[CONV_PAD:CONV_PAD + q] = xbc_ref[...]
    conv = cb_ref[...] + cw_ref[0:1] * xs_ref[CONV_PAD - 3:CONV_PAD - 3 + q]
    for k in range(1, SSM_CONV):
        conv = conv + cw_ref[k:k + 1] * xs_ref[CONV_PAD - 3 + k:CONV_PAD - 3 + k + q]
    xs_ref[0:CONV_PAD] = xs_ref[q:q + CONV_PAD]
    act_ref[...] = _silu(conv)

    dt = _softplus(dt_ref[...] + dtb_ref[...])
    acum = _cumsum_rows(dt * (-jnp.exp(alog_ref[...])))
    acum_t = acum.T
    dt_t = dt.T
    last = acum[q - 1:q, :]
    ea = jnp.exp(acum)
    te = jnp.exp(last - acum) * dt
    cd = jnp.exp(last)
    ii = lax.broadcasted_iota(jnp.int32, (q, q), 0)
    jj = lax.broadcasted_iota(jnp.int32, (q, q), 1)
    tri = ii >= jj
    low = jj < SSM_HEAD_DIM
    low_rows = ii < SSM_HEAD_DIM

    def col(a, h):
        return a[:, h:h + 1]

    for g in range(SSM_GROUPS):
        bg = act_ref[:, SSM_D_INNER + g * SSM_STATE:SSM_D_INNER + (g + 1) * SSM_STATE].astype(BF16)
        cg = act_ref[:, SSM_D_INNER + N_BC + g * SSM_STATE:SSM_D_INNER + N_BC + (g + 1) * SSM_STATE].astype(BF16)
        cbg = lax.dot_general(cg, bg, _NT, preferred_element_type=F32)
        for pr in range(HEADS_PER_GROUP // PAIR):
            k = g * (HEADS_PER_GROUP // PAIR) + pr
            ha, hb = PAIR * k, PAIR * k + 1
            xp = act_ref[:, k * LANES:(k + 1) * LANES]
            xpb = xp.astype(BF16)
            ys = []
            for h in (ha, hb):
                decay = jnp.exp(jnp.where(tri, col(acum, h) - acum_t[h:h + 1, :], NEG))
                w = cbg * decay * dt_t[h:h + 1, :]
                ys.append(jnp.dot(w.astype(BF16), xpb, preferred_element_type=F32))
            y = jnp.where(low, ys[0], ys[1])
            xs_pair = xp * jnp.where(low, col(te, ha), col(te, hb))
            st = jnp.dot(xs_pair.T.astype(BF16), bg, preferred_element_type=F32)
            hprev = h_ref[k]
            yoff = lax.dot_general(cg, hprev.astype(BF16), _NT, preferred_element_type=F32)
            y = y + yoff * jnp.where(low, col(ea, ha), col(ea, hb)) + dsk_ref[:, k * LANES:(k + 1) * LANES] * xp
            h_ref[k] = hprev * jnp.where(low_rows, cd[:, ha:ha + 1], cd[:, hb:hb + 1]) + st
            ybuf_ref[:, k * LANES:(k + 1) * LANES] = y

    yz = ybuf_ref[...] * _silu(z_ref[...])
    gw = SSM_D_INNER // SSM_GROUPS
    outs = []
    for g in range(SSM_GROUPS):
        seg = yz[:, g * gw:(g + 1) * gw]
        outs.append(seg * lax.rsqrt(jnp.mean(seg * seg, axis=-1, keepdims=True) + EPS))
    y_ref[...] = (jnp.concatenate(outs, axis=1) * ng_ref[...]).astype(y_ref.dtype)

    @pl.when(c == nc - 1)
    def _():
        hout_ref[0] = h_ref[...]


def _ssd(proj, h0, cinit, sw, nb, t):
    q = SSM_CHUNK
    nc = t // q
    vec = lambda a: pl.BlockSpec(a.shape, lambda b, c: (0, 0))
    return pl.pallas_call(
        functools.partial(_ssd_kernel, nc=nc),
        grid=(nb, nc),
        in_specs=[pl.BlockSpec((q, SSM_CONV_DIM), lambda b, c: (b * nc + c, 0)),
                  pl.BlockSpec((q, SSM_D_INNER), lambda b, c: (b * nc + c, SSM_CONV_DIM // SSM_D_INNER)),
                  pl.BlockSpec((q, LANES), lambda b, c: (b * nc + c, (SSM_CONV_DIM + SSM_D_INNER) // LANES)),
                  pl.BlockSpec((1,) + h0.shape[1:], lambda b, c: (b, 0, 0, 0)),
                  pl.BlockSpec((1,) + cinit.shape[1:], lambda b, c: (b, 0, 0)),
                  vec(sw['conv_w']), vec(sw['conv_b']), vec(sw['dt_bias']), vec(sw['a_log']), vec(sw['d_lane']),
                  vec(sw['norm_g'])],
        out_specs=[pl.BlockSpec((q, SSM_D_INNER), lambda b, c: (b * nc + c, 0)),
                   pl.BlockSpec((1,) + h0.shape[1:], lambda b, c: (b, 0, 0, 0))],
        out_shape=[jax.ShapeDtypeStruct((nb * t, SSM_D_INNER), BF16),
                   jax.ShapeDtypeStruct(h0.shape, F32)],
        scratch_shapes=[pltpu.VMEM((CONV_PAD + q, SSM_CONV_DIM), F32),
                        pltpu.VMEM((q, SSM_CONV_DIM), F32),
                        pltpu.VMEM((q, SSM_D_INNER), F32),
                        pltpu.VMEM(h0.shape[1:], F32)],
        compiler_params=_cparams(("arbitrary", "arbitrary")),
    )(proj, proj, proj, h0, cinit, sw['conv_w'], sw['conv_b'], sw['dt_bias'], sw['a_log'], sw['d_lane'],
      sw['norm_g'])


def _ssm_sample_step(proj_s, state_ssm, state_conv, sw):
    nb = proj_s.shape[0]
    q = SSM_CHUNK
    xbc_new = proj_s[:, :SSM_CONV_DIM]
    rows = jnp.zeros((nb, q, NP_SSM), F32)
    rows = rows.at[:, :, SSM_CONV_DIM + SSM_D_INNER:].set(NEG)
    rows = rows.at[:, q - SSM_CONV:q - 1, :SSM_CONV_DIM].set(state_conv.astype(F32))
    rows = rows.at[:, q - 1].set(proj_s)
    h0 = state_ssm.astype(F32).reshape(nb, N_PAIRS, LANES, SSM_STATE)
    cinit = jnp.zeros((nb, CONV_PAD, SSM_CONV_DIM), F32)
    yn, hfin = _ssd(rows.reshape(nb * q, NP_SSM), h0, cinit, sw, nb, q)
    conv_new = jnp.concatenate([state_conv[:, 1:].astype(F32), xbc_new[:, None]], axis=1)
    return yn.reshape(nb, q, SSM_D_INNER)[:, q - 1], hfin.reshape(state_ssm.shape), conv_new


def _ssm_weights(w_in, conv_w, conv_b, dt_bias, a_log, d_skip, norm_g):
    z_w = w_in[:, :SSM_D_INNER]
    xbc_w = w_in[:, SSM_D_INNER:SSM_D_INNER + SSM_CONV_DIM]
    dt_w = w_in[:, SSM_D_INNER + SSM_CONV_DIM:]
    pad = NP_SSM - w_in.shape[1]
    w = jnp.concatenate([xbc_w, z_w, dt_w, jnp.zeros((w_in.shape[0], pad), w_in.dtype)], axis=1).astype(BF16)
    lane_pad = lambda v: jnp.zeros((1, LANES), F32).at[0, :SSM_HEADS].set(v.astype(F32))
    return dict(w_in=w, conv_w=conv_w.astype(F32), conv_b=conv_b.astype(F32).reshape(1, -1),
                dt_bias=lane_pad(dt_bias), a_log=lane_pad(a_log),
                d_lane=jnp.repeat(d_skip.astype(F32), SSM_HEAD_DIM).reshape(1, -1),
                norm_g=norm_g.astype(F32).reshape(1, -1))


ROW_TILE = 512
FF_TILE = 1024


def kernel(x_prompt, x_sample, cache_kv, cache_win, state_ssm, state_conv, page_table, c_prompt, c_sample, rel_bias,
           ada_w, ada_b, norm_g, mlp_w1, mlp_w2, attn_w_in, attn_w_out, cmp_pos, cmp_w1, cmp_w2, ssm_w_in,
           ssm_conv_w, ssm_conv_b, ssm_dt_bias, ssm_a_log, ssm_d, ssm_norm_g, ssm_w_out):
    nb, t, d = x_prompt.shape
    db = x_sample.shape[0]
    assert x_sample.shape[1] == 1 and t % SSM_CHUNK == 0 and t % Q_BLOCK == 0
    n_pool, page = cache_kv.shape[1], cache_kv.shape[2]
    past = page_table.shape[1] * page
    depth = ada_w.shape[0]
    tm = _row_tile(t, ROW_TILE)

    xp = x_prompt.reshape(nb * t, d).astype(F32)
    xs = x_sample.reshape(db, d).astype(F32)
    c_all = jnp.concatenate([c_prompt, c_sample], axis=0).astype(F32)
    c_all = jnp.pad(c_all, ((0, (-c_all.shape[0]) % SUB), (0, 0)))
    bvz = _bias_vector(rel_bias)
    cache_t = _cache_tiles(cache_kv)
    win_t = jnp.transpose(cache_win, (0, 1, 3, 4, 5, 2))

    kv_p, win_p, ssm_p, conv_p, kv_s, win_s, ssm_s, conv_s = ([] for _ in range(8))
    for i in range(depth):
        mod = _ada(c_all, ada_w[i], ada_b[i])
        mp = [mod[:nb, j * d:(j + 1) * d].reshape(nb, 1, d) for j in range(N_MOD)]
        ms = [mod[nb:nb + db, j * d:(j + 1) * d].reshape(1, db, d) for j in range(N_MOD)]
        g = norm_g[i].astype(F32)
        if i % 2 == 0:
            a = i // 2
            w_in = _attn_in_weights(attn_w_in[a])
            w_out = attn_w_out[a].astype(BF16)
            cw = _compress_weights(cmp_pos[a], cmp_w1[a], cmp_w2[a])
            o_p, kv6 = _nsa_prompt_mixer(xp, g[0], mp[0], mp[1], w_in, cw, bvz, nb, t)
            kv_p.append(kv6[:, :, :N_KV_SLOTS])
            win_p.append(kv6[:, -min(WINDOW, t):, N_KV_SLOTS:])
            xp = _heads_out(o_p, _attn_out_weights(attn_w_out[a]), g[1], mp[2], xp, tm)

            proj_s = _nm_matmul(xs, g[0], ms[0], ms[1], w_in, db, NP_ATTN)
            kvc_s = _pack_cmp(_compress_sample(page_table, cache_t, cw, page, a * n_pool))
            o_s, win_new = _nsa_sample(proj_s, kvc_s, page_table, cache_t, win_t[a], bvz, past, page, a * n_pool)
            kv_s.append(proj_s[:, N_Q_COLS:N_Q_COLS + N_KV_SLOTS * KV_ROW].reshape(
                db, 1, N_KV_SLOTS, N_KV_HEADS, HEAD_DIM))
            win_s.append(jnp.transpose(win_new, (0, 4, 1, 2, 3)))
            xs = _mm_norm_res(o_s, w_out, g[1], ms[2], xs, db)
        else:
            m = i // 2
            sw = _ssm_weights(ssm_w_in[m], ssm_conv_w[m], ssm_conv_b[m], ssm_dt_bias[m], ssm_a_log[m], ssm_d[m],
                              ssm_norm_g[m])
            w_out = ssm_w_out[m].astype(BF16)
            proj = _nm_matmul(xp, g[0], mp[0], mp[1], sw['w_in'], tm, SSM_COL_TILE)
            h0 = jnp.zeros((nb, N_PAIRS, LANES, SSM_STATE), F32)
            cinit = jnp.zeros((nb, CONV_PAD, SSM_CONV_DIM), F32)
            yn, hfin = _ssd(proj, h0, cinit, sw, nb, t)
            ssm_p.append(hfin.reshape(nb, SSM_HEADS, SSM_HEAD_DIM, SSM_STATE).astype(state_ssm.dtype))
            conv_p.append(proj.reshape(nb, t, NP_SSM)[:, t - (SSM_CONV - 1):, :SSM_CONV_DIM])
            xp = _mm_norm_res(yn, w_out, g[1], mp[2], xp, tm)

            proj_s = _nm_matmul(xs, g[0], ms[0], ms[1], sw['w_in'], db, SSM_COL_TILE)
            yn_s, h_s, conv_new = _ssm_sample_step(proj_s, state_ssm[m], state_conv[m], sw)
            ssm_s.append(h_s.astype(state_ssm.dtype))
            conv_s.append(conv_new)
            xs = _mm_norm_res(yn_s, w_out, g[1], ms[2], xs, db)
        w1 = mlp_w1[i].astype(BF16)
        w2 = mlp_w2[i].astype(BF16)
        xp = _mlp(xp, g[2], mp[3], mp[4], w1, w2, g[3], mp[5], tm, FF_TILE)
        xs = _mlp(xs, g[2], ms[3], ms[4], w1, w2, g[3], ms[5], db, FF_TILE)
    return (xp.reshape(nb, t, d), xs.reshape(db, 1, d), jnp.stack(kv_p), jnp.stack(win_p), jnp.stack(ssm_p),
            jnp.stack(conv_p), jnp.stack(kv_s), jnp.stack(win_s), jnp.stack(ssm_s), jnp.stack(conv_s))
```

```python
import functools
import math

import numpy as np
import jax
import jax.numpy as jnp
from jax import lax
from jax.experimental import pallas as pl
from jax.experimental.pallas import tpu as pltpu

F32 = jnp.float32
BF16 = jnp.bfloat16

D_MODEL = 1024
N_HEADS = 16
HEAD_DIM = 64
N_KV_HEADS = 4
GROUP = N_HEADS // N_KV_HEADS
CMP_BLOCK = 32
CMP_STRIDE = 16
CMP_RATIO = CMP_BLOCK // CMP_STRIDE
CMP_HIDDEN = 2 * HEAD_DIM
SEL_BLOCK = 64
SEL_RATIO = SEL_BLOCK // CMP_STRIDE
N_SELECT = 16
WINDOW = 512
FORCE_BONUS = 1e4
Q_BLOCK = 128
N_KV_SLOTS = 4
N_WIN_SLOTS = 2
N_Q_COLS = N_HEADS * HEAD_DIM
N_KV_COLS = (N_KV_SLOTS + N_WIN_SLOTS) * N_KV_HEADS * HEAD_DIM
N_GATE_COLS = 3 * N_HEADS
KV_ROW = N_KV_HEADS * HEAD_DIM
KV_HEAD_COLS = (N_KV_SLOTS + N_WIN_SLOTS) * HEAD_DIM
N_BUCKETS = 32
MAX_DISTANCE = 128
SSM_D_INNER = 2 * D_MODEL
SSM_HEAD_DIM = 64
SSM_HEADS = SSM_D_INNER // SSM_HEAD_DIM
SSM_GROUPS = 8
SSM_STATE = 128
SSM_CONV = 4
SSM_CONV_DIM = SSM_D_INNER + 2 * SSM_GROUPS * SSM_STATE
SSM_CHUNK = 128
D_FF = 4 * D_MODEL
N_MOD = 6
EPS = 1e-6

LANES = 128
NEG = -1e30
LOG2E = math.log2(math.e)
VMEM_LIMIT = 56 * 1024 * 1024
NEAR = 2 * LANES

_NT = (((1,), (1,)), ((), ()))


def _cparams(sem):
    return pltpu.CompilerParams(dimension_semantics=sem, vmem_limit_bytes=VMEM_LIMIT)


def _rms(x, g):
    return x * lax.rsqrt(jnp.mean(x * x, axis=-1, keepdims=True) + EPS) * g


def _silu(x):
    return x * jax.nn.sigmoid(x)


def _split_bf16(x, n):
    parts = []
    for _ in range(n - 1):
        p = x.astype(BF16)
        parts.append(p)
        x = x - p.astype(F32)
    parts.append(x.astype(BF16))
    return parts


def _ada_kernel(c_ref, w_ref, b_ref, o_ref):
    s = _silu(c_ref[...]).astype(BF16)
    o_ref[...] = jnp.dot(s, w_ref[...].astype(BF16), preferred_element_type=F32) + b_ref[...]


def _ada(c, w, b, tn=1024):
    m, d = c.shape
    n = w.shape[1]
    return pl.pallas_call(
        _ada_kernel,
        grid=(n // tn,),
        in_specs=[pl.BlockSpec((m, d), lambda j: (0, 0)),
                  pl.BlockSpec((d, tn), lambda j: (0, j)),
                  pl.BlockSpec((1, tn), lambda j: (0, j))],
        out_specs=pl.BlockSpec((m, tn), lambda j: (0, j)),
        out_shape=jax.ShapeDtypeStruct((m, n), F32),
        compiler_params=_cparams(("arbitrary",)),
    )(c, w, b.reshape(1, n))


def _nm_matmul_kernel(x_ref, g_ref, sh_ref, sc_ref, w_ref, o_ref, h_ref):
    @pl.when(pl.program_id(1) == 0)
    def _():
        h = _rms(x_ref[...], g_ref[...]) * (1.0 + sc_ref[0]) + sh_ref[0]
        h_ref[...] = h.astype(BF16)

    o_ref[...] = jnp.dot(h_ref[...], w_ref[...], preferred_element_type=F32)


def _nm_matmul(x, g, shift, scale, w, tm, tn):
    m, d = x.shape
    n = w.shape[1]
    nb = shift.shape[0]
    tpb = m // nb // tm
    mod_spec = pl.BlockSpec((1,) + shift.shape[1:], lambda i, j: (i // tpb, 0, 0))
    return pl.pallas_call(
        _nm_matmul_kernel,
        grid=(m // tm, n // tn),
        in_specs=[pl.BlockSpec((tm, d), lambda i, j: (i, 0)),
                  pl.BlockSpec((1, d), lambda i, j: (0, 0)),
                  mod_spec, mod_spec,
                  pl.BlockSpec((d, tn), lambda i, j: (0, j))],
        out_specs=pl.BlockSpec((tm, tn), lambda i, j: (i, j)),
        out_shape=jax.ShapeDtypeStruct((m, n), F32),
        scratch_shapes=[pltpu.VMEM((tm, d), BF16)],
        compiler_params=_cparams(("arbitrary", "arbitrary")),
    )(x, g.reshape(1, d), shift, scale, w)


def _mm_norm_res_kernel(a_ref, w_ref, g_ref, gate_ref, x_ref, o_ref):
    y = jnp.dot(a_ref[...].astype(BF16), w_ref[...], preferred_element_type=F32)
    o_ref[...] = x_ref[...] + gate_ref[0] * _rms(y, g_ref[...])


def _mm_norm_res(a, w, g, gate, x, tm):
    m, k = a.shape
    d = w.shape[1]
    nb = gate.shape[0]
    tpb = m // nb // tm
    return pl.pallas_call(
        _mm_norm_res_kernel,
        grid=(m // tm,),
        in_specs=[pl.BlockSpec((tm, k), lambda i: (i, 0)),
                  pl.BlockSpec((k, d), lambda i: (0, 0)),
                  pl.BlockSpec((1, d), lambda i: (0, 0)),
                  pl.BlockSpec((1,) + gate.shape[1:], lambda i: (i // tpb, 0, 0)),
                  pl.BlockSpec((tm, d), lambda i: (i, 0))],
        out_specs=pl.BlockSpec((tm, d), lambda i: (i, 0)),
        out_shape=jax.ShapeDtypeStruct((m, d), F32),
        compiler_params=_cparams(("arbitrary",)),
    )(a, w, g.reshape(1, d), gate, x)


def _mlp_kernel(x_ref, g1_ref, sh_ref, sc_ref, w1_ref, w2_ref, g2_ref, gate_ref, o_ref, h_ref, acc_ref):
    c = pl.program_id(1)

    @pl.when(c == 0)
    def _():
        h = _rms(x_ref[...], g1_ref[...]) * (1.0 + sc_ref[0]) + sh_ref[0]
        h_ref[...] = h.astype(BF16)
        acc_ref[...] = jnp.zeros_like(acc_ref)

    a = jnp.dot(h_ref[...], w1_ref[...], preferred_element_type=F32)
    a = jnp.square(jnp.maximum(a, 0.0)).astype(BF16)
    acc_ref[...] += jnp.dot(a, w2_ref[...], preferred_element_type=F32)

    @pl.when(c == pl.num_programs(1) - 1)
    def _():
        o_ref[...] = x_ref[...] + gate_ref[0] * _rms(acc_ref[...], g2_ref[...])


def _mlp(x, g1, shift, scale, w1, w2, g2, gate, tm, tf):
    m, d = x.shape
    f = w1.shape[1]
    nb = shift.shape[0]
    tpb = m // nb // tm
    mod_spec = pl.BlockSpec((1,) + shift.shape[1:], lambda i, c: (i // tpb, 0, 0))
    vec_spec = pl.BlockSpec((1, d), lambda i, c: (0, 0))
    return pl.pallas_call(
        _mlp_kernel,
        grid=(m // tm, f // tf),
        in_specs=[pl.BlockSpec((tm, d), lambda i, c: (i, 0)), vec_spec, mod_spec, mod_spec,
                  pl.BlockSpec((d, tf), lambda i, c: (0, c)),
                  pl.BlockSpec((tf, d), lambda i, c: (c, 0)),
                  vec_spec, mod_spec],
        out_specs=pl.BlockSpec((tm, d), lambda i, c: (i, 0)),
        out_shape=jax.ShapeDtypeStruct((m, d), F32),
        scratch_shapes=[pltpu.VMEM((tm, d), BF16), pltpu.VMEM((tm, d), F32)],
        compiler_params=_cparams(("arbitrary", "arbitrary")),
    )(x, g1.reshape(1, d), shift, scale, w1, w2, g2.reshape(1, d), gate)


def _bias_kernel(oh_ref, t_ref, o_ref):
    t = t_ref[...]
    t = t - t[N_BUCKETS - 1:N_BUCKETS, :]
    oh = oh_ref[...]
    o_ref[...] = sum(jnp.dot(oh, p, preferred_element_type=F32) for p in _split_bf16(t, 3))


def _rel_bucket_np(dist):
    n = np.maximum(dist, 0)
    exact = N_BUCKETS // 2
    nf = np.maximum(n, 1).astype(np.float32)
    large = exact + (np.log(nf / exact) / math.log(MAX_DISTANCE / exact) * (N_BUCKETS - exact)).astype(np.int32)
    return np.where(n < exact, n, np.minimum(large, N_BUCKETS - 1))


def _bias_vector(rel_bias):
    assert _rel_bucket_np(np.array([NEAR - 1]))[0] == N_BUCKETS - 1 == _rel_bucket_np(np.array([MAX_DISTANCE]))[0]
    oh = np.zeros((NEAR + 8, LANES), np.float32)
    oh[np.arange(NEAR), _rel_bucket_np(np.arange(NEAR))] = 1.0
    oh[NEAR:, N_BUCKETS - 1] = 1.0
    table = jnp.zeros((LANES, LANES), F32).at[:N_BUCKETS, :N_HEADS].set(rel_bias.astype(F32))
    bv = pl.pallas_call(
        _bias_kernel,
        out_shape=jax.ShapeDtypeStruct((NEAR + 8, LANES), F32),
    )(jnp.asarray(oh, BF16), table)
    return bv[:NEAR + 1, :N_HEADS]


SUB = 8
HEADS_PER_CG = LANES // HEAD_DIM
N_CG = 2 * KV_ROW // LANES
CG_PER_SLOT = KV_ROW // LANES
CG_HIDDEN = HEADS_PER_CG * CMP_HIDDEN


def _compress_parts(rows_s, wbd_ref, nh):
    part0 = jnp.zeros((nh, CG_HIDDEN), F32)
    part1 = jnp.zeros((nh, CG_HIDDEN), F32)
    for s in range(0, CMP_STRIDE, 2):
        xs = jnp.concatenate([rows_s(s), rows_s(s + 1)], axis=1).astype(BF16)
        part0 = part0 + jnp.dot(xs, wbd_ref[0, 0, s // 2], preferred_element_type=F32)
        part1 = part1 + jnp.dot(xs, wbd_ref[0, 1, s // 2], preferred_element_type=F32)
    return part0, part1


def _compress_finish(part0, part1, pos_ref, w1_ref, w2bd_ref, nh):
    pre0 = jnp.dot(pos_ref[0], w1_ref[0], preferred_element_type=F32)[0:1]
    pre = pre0 + part0 + pltpu.roll(part1, nh - 1, axis=0)
    return jnp.dot(_silu(pre).astype(BF16), w2bd_ref[0], preferred_element_type=F32)


def _compress_kernel(x_ref, pos_ref, w1_ref, wbd_ref, w2bd_ref, o_ref, *, nh):
    part0, part1 = _compress_parts(lambda s: x_ref[pl.ds(s, nh, stride=CMP_STRIDE), :], wbd_ref, nh)
    o_ref[0, 0] = _compress_finish(part0, part1, pos_ref, w1_ref, w2bd_ref, nh)


def _block_diag(a, b):
    za = jnp.zeros(a.shape[:-1] + (b.shape[-1],), a.dtype)
    zb = jnp.zeros(b.shape[:-1] + (a.shape[-1],), b.dtype)
    return jnp.concatenate([jnp.concatenate([a, za], axis=-1), jnp.concatenate([zb, b], axis=-1)], axis=-2)


def _compress_weight_set(pos, w1, w2, a, b):
    w1r = w1.reshape(2, CMP_RATIO, CMP_STRIDE, HEAD_DIM, CMP_HIDDEN)
    wbd = _block_diag(w1r[a], w1r[b]).reshape(CMP_RATIO, CMP_STRIDE // 2, 2 * LANES, CG_HIDDEN)
    posx = jnp.zeros((SUB, 2 * CMP_BLOCK * HEAD_DIM), F32).at[0].set(
        jnp.concatenate([pos[a].reshape(-1), pos[b].reshape(-1)]))
    return [x.astype(BF16) for x in (posx, _block_diag(w1[a], w1[b]), wbd, _block_diag(w2[a], w2[b]))]


def _compress_weights(cmp_pos, cmp_w1, cmp_w2, pairs):
    sets = [_compress_weight_set(cmp_pos, cmp_w1, cmp_w2, a, b) for a, b in pairs]
    return [jnp.stack(x) for x in zip(*sets)]


def _compress_specs(cw, imap):
    def spec(a):
        return pl.BlockSpec((1,) + a.shape[1:], lambda *g: (imap(*g),) + (0,) * (a.ndim - 1))
    return [spec(a) for a in cw]


def _pack_cmp(cmp, dtype=None):
    nb, _, n, _ = cmp.shape
    c = cmp.reshape(nb, 2, CG_PER_SLOT, n, HEADS_PER_CG, HEAD_DIM)
    c = jnp.transpose(c, (0, 2, 4, 3, 1, 5)).reshape(nb, N_KV_HEADS, n, 2 * HEAD_DIM)
    return c.astype(BF16 if dtype is None else dtype)


def _compress_prompt(proj, cw, nb, t):
    nh = t // CMP_STRIDE
    return pl.pallas_call(
        functools.partial(_compress_kernel, nh=nh),
        grid=(nb, N_KV_HEADS),
        in_specs=[pl.BlockSpec((t, LANES), lambda b, k: (b, (N_Q_COLS + k * KV_HEAD_COLS) // LANES))]
        + _compress_specs(cw, lambda b, k: 0),
        out_specs=pl.BlockSpec((1, 1, nh, LANES), lambda b, k: (b, k, 0, 0)),
        out_shape=jax.ShapeDtypeStruct((nb, N_KV_HEADS, nh, LANES), F32),
        compiler_params=_cparams(("arbitrary", "arbitrary")),
    )(proj, *cw)


KEY_TILE = 4 * Q_BLOCK
FRONT_PAD = KEY_TILE
SUBTILES = KEY_TILE // Q_BLOCK
BAND_OFF = 2 * Q_BLOCK // CMP_STRIDE
BAND_W = 32
Q_PER_CMP = Q_BLOCK // CMP_STRIDE


def _nsa_consts(t):
    ncp = t // CMP_STRIDE
    nblk = t // SEL_BLOCK
    nbp = -(-nblk // LANES) * LANES
    at = np.zeros((nbp, ncp), np.float32)
    for j in range(nblk):
        for n in range(SEL_RATIO * j - (CMP_RATIO - 1), SEL_RATIO * (j + 1)):
            if 0 <= n < ncp - CMP_RATIO + 1:
                at[j, n] = 1.0
    key_blk = np.arange(t) // SEL_BLOCK
    e = (np.arange(nbp)[:, None] == key_blk[None, :]).astype(np.float32)
    e1 = e.reshape(nbp, t // LANES, LANES).transpose(1, 0, 2)
    e1 = np.concatenate([np.zeros((SUBTILES,) + e1.shape[1:], np.float32), e1], axis=0)
    band = np.zeros((2 * ncp + 2 * Q_PER_CMP, LANES), np.float32)
    u = np.arange(BAND_W)
    band[u + ncp, HEAD_DIM + u] = 1.0
    band[u + ncp, HEAD_DIM + BAND_W + u] = 1.0
    r = np.arange(Q_BLOCK)[:, None]
    c = np.arange(LANES)[None, :]
    d0 = r - c
    idx0 = np.where(d0 >= 0, d0, NEAR)
    idx1 = Q_BLOCK + r - c
    dc = r - CMP_STRIDE * c + (CMP_STRIDE * BAND_OFF - CMP_BLOCK + 1)
    idxc = np.where((dc >= 0) & (dc < NEAR), dc, NEAR)
    assert np.all(idxc[:, BAND_W:] == NEAR)
    bf = lambda a: jnp.asarray(a, BF16)
    return dict(at=bf(at), e1=bf(e1), band=jnp.asarray(band),
                idx0=idx0, idx1=idx1, idxc=idxc, ncp=ncp, nbp=nbp, nblk=nblk)


def _nsa_kernel(q_ref, gl_ref, kck_ref, kvc_ref, kvs_ref, kvw_ref, at_ref, e1_ref, tb_ref, cbq_ref, band_ref,
                o_ref, *, ncp, nbp, n_sel):
    i = pl.program_id(2)
    rows = GROUP * Q_BLOCK
    lane = lax.broadcasted_iota(jnp.int32, (1, LANES), 1)
    low = lane < HEAD_DIM
    r_col = lax.broadcasted_iota(jnp.int32, (rows, 1), 0) % Q_BLOCK
    q_pos = i * Q_BLOCK + r_col

    q = q_ref[...] * (HEAD_DIM ** -0.5 * LOG2E)
    parts = []
    for h in range(GROUP * HEAD_DIM // LANES):
        qh = q[:, h * LANES:(h + 1) * LANES]
        parts.append(jnp.where(low, qh, 0.0))
        parts.append(jnp.where(low, pltpu.roll(qh, HEAD_DIM, axis=1), 0.0))
    qs = jnp.concatenate(parts, axis=0).astype(BF16)

    qc = jnp.where(low, qs, cbq_ref[0])
    start = pl.multiple_of(ncp + BAND_OFF - Q_PER_CMP * i, 8)
    kq = (kck_ref[0, 0] + band_ref[pl.ds(start, ncp), :]).astype(BF16)
    s = lax.dot_general(qc, kq, _NT, preferred_element_type=F32)
    cmp_end = lax.broadcasted_iota(jnp.int32, (1, ncp), 1) * CMP_STRIDE + (CMP_BLOCK - 1)
    mask = cmp_end <= q_pos
    m = jnp.max(jnp.where(mask, s, NEG), axis=-1, keepdims=True)
    e = jnp.where(mask, jnp.exp2(s - m), 0.0)
    p_cmp = e / jnp.maximum(jnp.sum(e, axis=-1, keepdims=True), 1e-30)
    o_cmp = jnp.dot(p_cmp.astype(BF16), kvc_ref[0, 0], preferred_element_type=F32)
    imp = p_cmp[0:Q_BLOCK]
    for g in range(1, GROUP):
        imp = imp + p_cmp[g * Q_BLOCK:(g + 1) * Q_BLOCK]
    bst = sum(lax.dot_general(at_ref[...], part, _NT, preferred_element_type=F32)
              for part in _split_bf16(imp, 2))

    jb = lax.broadcasted_iota(jnp.int32, (nbp, Q_BLOCK), 0)
    q_blk = (i * Q_BLOCK + lax.broadcasted_iota(jnp.int32, (nbp, Q_BLOCK), 1)) // SEL_BLOCK
    causal = jb <= q_blk
    forced = (jb == 0) | (jb == q_blk) | (jb == q_blk - 1)
    work = jnp.where(causal, bst + jnp.where(forced, FORCE_BONUS, 0.0), -jnp.inf)
    jbf = jb.astype(F32)
    sel_t = jnp.zeros((nbp, Q_BLOCK), F32)
    for _ in range(n_sel):
        top = jnp.max(work, axis=0, keepdims=True)
        first = jnp.min(jnp.where(work == top, jbf, float(nbp)), axis=0, keepdims=True)
        hit = jbf == first
        sel_t = jnp.where(hit, 1.0, sel_t)
        work = jnp.where(hit, -jnp.inf, work)
    sel = jnp.where(causal, sel_t, 0.0).T.astype(BF16)

    def ones_and_values(kv):
        return jnp.where(low, jnp.ones_like(kv), kv)

    def normalise(acc):
        return jnp.where(low, 0.0, acc / jnp.maximum(pltpu.roll(acc, HEAD_DIM, axis=1), 1e-30))

    n_wt = WINDOW // Q_BLOCK
    kw = kvw_ref[0, 0, pl.ds(pl.multiple_of((i + SUBTILES - n_wt) * Q_BLOCK, Q_BLOCK), WINDOW + Q_BLOCK), :]
    s = lax.dot_general(qs, kw, _NT, preferred_element_type=F32)
    pieces = []
    for u in range(n_wt + 1):
        su = s[:, u * Q_BLOCK:(u + 1) * Q_BLOCK]
        if u == n_wt:
            su = jnp.where(lane <= r_col, su + tb_ref[0, 0], NEG)
        else:
            if u == n_wt - 1:
                su = su + tb_ref[0, 1]
            su = su + jnp.where(i - n_wt + u >= 0, 0.0, NEG)
            if u == 0:
                su = jnp.where(lane > r_col, su, NEG)
        pieces.append(su)
    sm = jnp.concatenate(pieces, axis=1)
    e = jnp.exp2((sm - jnp.max(sm, axis=-1, keepdims=True)).astype(BF16))
    o_win = normalise(jnp.dot(e, ones_and_values(kw), preferred_element_type=F32))

    n_tiles = (i + SUBTILES) // SUBTILES

    def keys(u):
        return kvs_ref[0, 0, pl.ds(pl.multiple_of((i + 1 - SUBTILES * u) * Q_BLOCK, Q_BLOCK), KEY_TILE), :]

    def masked_scores(u, diagonal=False):
        t0p = i + 1 - SUBTILES * u
        mk = jnp.concatenate([jnp.dot(sel, e1_ref[t0p + v], preferred_element_type=F32) for v in range(SUBTILES)],
                             axis=1)
        s3 = lax.dot_general(qs, keys(u), _NT, preferred_element_type=F32).reshape(GROUP, Q_BLOCK, KEY_TILE)
        if diagonal:
            r3 = lax.broadcasted_iota(jnp.int32, (1, Q_BLOCK, LANES), 1)
            c3 = lax.broadcasted_iota(jnp.int32, (1, Q_BLOCK, LANES), 2)
            tb0 = tb_ref[0, 0].reshape(GROUP, Q_BLOCK, LANES)
            tb1 = tb_ref[0, 1].reshape(GROUP, Q_BLOCK, LANES)
            s3 = jnp.concatenate([s3[:, :, :KEY_TILE - 2 * Q_BLOCK],
                                  s3[:, :, KEY_TILE - 2 * Q_BLOCK:KEY_TILE - Q_BLOCK] + tb1,
                                  jnp.where(c3 <= r3, s3[:, :, KEY_TILE - Q_BLOCK:] + tb0, NEG)], axis=2)
        return jnp.where((mk > 0.5)[None], s3, NEG).reshape(rows, KEY_TILE)

    def sel_body(u, carry):
        m, acc, sm, e_prev = carry
        pv = jnp.dot(e_prev, ones_and_values(keys(jnp.maximum(u - 1, 0))), preferred_element_type=F32)
        sm_next = masked_scores(jnp.minimum(u + 1, n_tiles - 1))
        m_new = jnp.maximum(m, jnp.max(sm, axis=-1, keepdims=True))
        alpha = jnp.exp2(m - m_new)
        e = jnp.exp2((sm - m_new).astype(BF16))
        return m_new, alpha * (acc + pv), sm_next, e

    init = (jnp.full((rows, 1), NEG, F32), jnp.zeros((rows, LANES), F32),
            masked_scores(0, diagonal=True), jnp.zeros((rows, KEY_TILE), BF16))
    m, acc, _, e_last = lax.fori_loop(0, n_tiles, sel_body, init)
    o_sel = normalise(acc + jnp.dot(e_last, ones_and_values(keys(n_tiles - 1)), preferred_element_type=F32))

    sg = jax.nn.sigmoid(gl_ref[...])
    for g in range(GROUP):
        out = jnp.zeros((Q_BLOCK, LANES), F32)
        for j, ob in enumerate((o_cmp, o_sel, o_win)):
            out = out + sg[:, 3 * g + j:3 * g + j + 1] * ob[g * Q_BLOCK:(g + 1) * Q_BLOCK]
        o_ref[0, 0, g] = out.astype(o_ref.dtype)


GATE_COL0 = N_Q_COLS + N_KV_COLS
NP_ATTN = GATE_COL0 + N_KV_HEADS * LANES


def _nsa_prompt(proj, kck, kvc, kvs, kvw, tb, cbq, consts, nb, t):
    ni = t // Q_BLOCK
    ncp, nbp = consts['ncp'], consts['nbp']
    gcol = GATE_COL0 // LANES
    qw = GROUP * HEAD_DIM
    full = lambda a: pl.BlockSpec(a.shape, lambda b, k, i: (0,) * a.ndim)
    per_kvh = lambda a: pl.BlockSpec((1,) + a.shape[1:], lambda b, k, i: (k,) + (0,) * (a.ndim - 1))
    per_bk = lambda a: pl.BlockSpec((1, 1) + a.shape[2:], lambda b, k, i: (b, k, 0, 0))
    c = consts
    n_sel = min(N_SELECT, c['nblk'])
    assert n_sel >= 3
    return pl.pallas_call(
        functools.partial(_nsa_kernel, ncp=ncp, nbp=nbp, n_sel=n_sel),
        grid=(nb, N_KV_HEADS, ni),
        in_specs=[pl.BlockSpec((Q_BLOCK, qw), lambda b, k, i: (b * ni + i, k)),
                  pl.BlockSpec((Q_BLOCK, LANES), lambda b, k, i: (b * ni + i, gcol + k)),
                  per_bk(kck), per_bk(kvc), per_bk(kvs), per_bk(kvw),
                  full(c['at']), full(c['e1']), per_kvh(tb), per_kvh(cbq), full(c['band'])],
        out_specs=pl.BlockSpec((1, 1, GROUP, Q_BLOCK, LANES), lambda b, k, i: (b, k, 0, i, 0)),
        out_shape=jax.ShapeDtypeStruct((nb, N_KV_HEADS, GROUP, t, LANES), BF16),
        compiler_params=_cparams(("arbitrary", "arbitrary", "arbitrary")),
    )(proj, proj, kck, kvc, kvs, kvw, c['at'], c['e1'], tb, cbq, c['band'])


def _heads_out_kernel(a_ref, w_ref, g_ref, gate_ref, x_ref, o_ref):
    y = jnp.zeros(o_ref.shape, F32)
    for k in range(N_KV_HEADS):
        for g in range(0, GROUP, 2):
            a = jnp.concatenate([a_ref[0, k, g], a_ref[0, k, g + 1]], axis=1)
            h = k * GROUP + g
            w = w_ref[h * LANES:(h + 2) * LANES, :]
            y = y + jnp.dot(a, w, preferred_element_type=F32)
    o_ref[...] = x_ref[...] + gate_ref[0] * _rms(y, g_ref[...])


def _heads_out(a, w_pad, g, gate, x, tm):
    nb, _, _, t, _ = a.shape
    d = w_pad.shape[1]
    tpb = t // tm
    return pl.pallas_call(
        _heads_out_kernel,
        grid=(nb * tpb,),
        in_specs=[pl.BlockSpec((1, N_KV_HEADS, GROUP, tm, LANES), lambda i: (i // tpb, 0, 0, i % tpb, 0)),
                  pl.BlockSpec(w_pad.shape, lambda i: (0, 0)),
                  pl.BlockSpec((1, d), lambda i: (0, 0)),
                  pl.BlockSpec((1,) + gate.shape[1:], lambda i: (i // tpb, 0, 0)),
                  pl.BlockSpec((tm, d), lambda i: (i, 0))],
        out_specs=pl.BlockSpec((tm, d), lambda i: (i, 0)),
        out_shape=jax.ShapeDtypeStruct((nb * t, d), F32),
        compiler_params=_cparams(("arbitrary",)),
    )(a, w_pad, g.reshape(1, d), gate, x)


def _row_tile(m, cap):
    return m if m <= cap else cap


def _attn_in_weights(w_in):
    d = w_in.shape[0]
    kv = w_in[:, N_Q_COLS:GATE_COL0].reshape(d, N_KV_SLOTS + N_WIN_SLOTS, N_KV_HEADS, HEAD_DIM)
    kv = jnp.swapaxes(kv, 1, 2).reshape(d, N_KV_COLS)
    gates = w_in[:, GATE_COL0:].reshape(d, N_KV_HEADS, GROUP * 3)
    gates = jnp.pad(gates, ((0, 0), (0, 0), (0, LANES - GROUP * 3))).reshape(d, -1)
    return jnp.concatenate([w_in[:, :N_Q_COLS], kv, gates], axis=1).astype(BF16)


def _attn_out_weights(w_out):
    w = w_out.reshape(N_HEADS, HEAD_DIM, w_out.shape[1])
    return jnp.pad(w, ((0, 0), (LANES - HEAD_DIM, 0), (0, 0))).reshape(N_HEADS * LANES, -1).astype(BF16)


def _attn_in_kernel(x_ref, g_ref, sh_ref, sc_ref, w_ref, zs_ref, zw_ref, o_ref, kvs_ref, kvw_ref):
    del zs_ref, zw_ref
    h = _rms(x_ref[...], g_ref[...]) * (1.0 + sc_ref[0]) + sh_ref[0]
    r = jnp.dot(h.astype(BF16), w_ref[...], preferred_element_type=F32)
    o_ref[...] = r
    for k in range(N_KV_HEADS):
        c0 = N_Q_COLS + k * KV_HEAD_COLS
        kvs_ref[0, k] = r[:, c0 + LANES:c0 + 2 * LANES].astype(BF16)
        kvw_ref[0, k] = r[:, c0 + 2 * LANES:c0 + 3 * LANES].astype(BF16)


def _attn_in_prompt(x, g, shift, scale, w, nb, t):
    m, d = x.shape
    tm = FRONT_PAD
    assert t % tm == 0
    tpb = t // tm
    mod_spec = pl.BlockSpec((1,) + shift.shape[1:], lambda i: (i // tpb, 0, 0))
    pad_shape = (nb, N_KV_HEADS, FRONT_PAD + t, LANES)
    pad_spec = pl.BlockSpec((1, N_KV_HEADS, tm, LANES), lambda i: (i // tpb, 0, 1 + i % tpb, 0))
    zeros = jnp.zeros(pad_shape, BF16)
    return pl.pallas_call(
        _attn_in_kernel,
        grid=(m // tm,),
        in_specs=[pl.BlockSpec((tm, d), lambda i: (i, 0)),
                  pl.BlockSpec((1, d), lambda i: (0, 0)),
                  mod_spec, mod_spec,
                  pl.BlockSpec(w.shape, lambda i: (0, 0)),
                  pl.BlockSpec(memory_space=pl.ANY), pl.BlockSpec(memory_space=pl.ANY)],
        out_specs=[pl.BlockSpec((tm, NP_ATTN), lambda i: (i, 0)), pad_spec, pad_spec],
        out_shape=[jax.ShapeDtypeStruct((m, NP_ATTN), F32),
                   jax.ShapeDtypeStruct(pad_shape, BF16), jax.ShapeDtypeStruct(pad_shape, BF16)],
        input_output_aliases={5: 1, 6: 2},
        compiler_params=_cparams(("arbitrary",)),
    )(x, g.reshape(1, d), shift, scale, w, zeros, zeros)


def _unpermute_kv(proj_kv, lead):
    kv = proj_kv.reshape(lead + (N_KV_HEADS, N_KV_SLOTS + N_WIN_SLOTS, HEAD_DIM))
    return jnp.swapaxes(kv, -3, -2)


def _bias_tiles_kernel(t_ref, oh_ref, o_ref):
    oh = oh_ref[...]
    o_ref[...] = sum(jnp.dot(p, oh, preferred_element_type=F32) for p in _split_bf16(t_ref[...], 3))


def _bias_tiles(rel_bias, idxs):
    n = sum(ix.size for ix in idxs)
    d = np.concatenate([ix.reshape(-1) for ix in idxs])
    oh = np.zeros((LANES, n), np.float32)
    near = d < NEAR
    oh[_rel_bucket_np(d[near]), np.nonzero(near)[0]] = 1.0
    table = (rel_bias.astype(F32) - rel_bias[N_BUCKETS - 1].astype(F32)).T * LOG2E
    table = jnp.zeros((N_HEADS, LANES), F32).at[:, :N_BUCKETS].set(table)
    tn = 8192
    assert n % tn == 0
    out = pl.pallas_call(
        _bias_tiles_kernel,
        grid=(n // tn,),
        in_specs=[pl.BlockSpec((N_HEADS, LANES), lambda j: (0, 0)), pl.BlockSpec((LANES, tn), lambda j: (0, j))],
        out_specs=pl.BlockSpec((N_HEADS, tn), lambda j: (0, j)),
        out_shape=jax.ShapeDtypeStruct((N_HEADS, n), F32),
        compiler_params=_cparams(("arbitrary",)),
    )(table, jnp.asarray(oh, BF16))
    tiles, off = [], 0
    for ix in idxs:
        r, c = ix.shape
        tiles.append(out[:, off:off + r * c].reshape(N_KV_HEADS, GROUP * r, c))
        off += r * c
    return tiles


def _nsa_prompt_mixer(xp, g, shift, scale, w_in, cw, rel_bias, nb, t):
    proj, kvs, kvw = _attn_in_prompt(xp, g, shift, scale, w_in, nb, t)
    kv6 = _unpermute_kv(proj[:, N_Q_COLS:GATE_COL0], (nb, t))
    kck = _compress_prompt(proj, cw, nb, t)
    kvc = kck.astype(BF16)
    kck = jnp.where(jnp.arange(LANES) < HEAD_DIM, kck, 0.0)
    c = _nsa_consts(t)
    tiles = _bias_tiles(rel_bias, [c['idx0'], c['idx1'], c['idxc']])
    tb = jnp.stack(tiles[:2], axis=1)
    hi, lo = _split_bf16(tiles[2], 2)
    cbq = jnp.concatenate([jnp.zeros(hi.shape[:2] + (HEAD_DIM,), BF16), hi[..., :BAND_W], lo[..., :BAND_W]], axis=-1)
    o = _nsa_prompt(proj, kck, kvc, kvs, kvw, tb, cbq, c, nb, t)
    return o, kv6


SC_PAGES = 32
CG_PER_PAGE = N_KV_SLOTS * KV_ROW // LANES


def _cache_tiles(cache_kv):
    page = cache_kv.shape[2]
    assert page == LANES
    return jnp.transpose(cache_kv, (0, 1, 3, 4, 5, 2)).reshape(-1, page)


def _compress_sample_kernel(pt_ref, cache_ref, pos_ref, w1_ref, wbd_ref, w2bd_ref, o_ref, tbuf_ref, buf_ref,
                            sem_ref, p0_ref, p1_ref, *, n_pages, page, row_base):
    step = pl.program_id(0) * N_CG + pl.program_id(1)
    nsteps = pl.num_programs(0) * N_CG
    chunk_pages = min(SC_PAGES, n_pages // 2)
    nch = n_pages // chunk_pages
    hpc = chunk_pages * page // CMP_STRIDE
    nh = nch * hpc

    def copies(step_, ch, slot):
        b_, c_ = step_ // N_CG, step_ % N_CG
        out = []
        for p in range(chunk_pages):
            pg = pt_ref[b_, ch * chunk_pages + p]
            out.append(pltpu.make_async_copy(
                cache_ref.at[pl.ds(((row_base + pg) * CG_PER_PAGE + c_) * LANES, LANES), :],
                tbuf_ref.at[slot, p], sem_ref.at[slot]))
        return out

    @pl.when(step == 0)
    def _():
        for cp in copies(step, 0, 0):
            cp.start()

    for ch in range(nch):
        slot = ch % 2
        if ch + 1 < nch:
            for cp in copies(step, ch + 1, 1 - slot):
                cp.start()
        else:
            @pl.when(step + 1 < nsteps)
            def _():
                for cp in copies(step + 1, 0, 1 - slot):
                    cp.start()
        for cp in copies(step, ch, slot):
            cp.wait()

        for p in range(chunk_pages):
            buf_ref[p * page:(p + 1) * page, :] = tbuf_ref[slot, p].T
        p0, p1 = _compress_parts(lambda s: buf_ref[pl.ds(s, hpc, stride=CMP_STRIDE), :], wbd_ref, hpc)
        p0_ref[ch * hpc:(ch + 1) * hpc] = p0
        p1_ref[ch * hpc:(ch + 1) * hpc] = p1

    o_ref[0, 0] = _compress_finish(p0_ref[...], p1_ref[...], pos_ref, w1_ref, w2bd_ref, nh)


def _compress_sample(page_table, cache_t, cw, page, row_base):
    nb, n_pages = page_table.shape
    chunk_pages = min(SC_PAGES, n_pages // 2)
    assert n_pages % (2 * chunk_pages) == 0
    nh = n_pages * page // CMP_STRIDE
    grid_spec = pltpu.PrefetchScalarGridSpec(
        num_scalar_prefetch=1,
        grid=(nb, N_CG),
        in_specs=[pl.BlockSpec(memory_space=pl.ANY)] + _compress_specs(cw, lambda b, c, pt: c // CG_PER_SLOT),
        out_specs=pl.BlockSpec((1, 1, nh, LANES), lambda b, c, pt: (b, c, 0, 0)),
        scratch_shapes=[pltpu.VMEM((2, chunk_pages, LANES, page), F32),
                        pltpu.VMEM((chunk_pages * page, LANES), F32),
                        pltpu.SemaphoreType.DMA((2,)),
                        pltpu.VMEM((nh, CG_HIDDEN), F32),
                        pltpu.VMEM((nh, CG_HIDDEN), F32)])
    return pl.pallas_call(
        functools.partial(_compress_sample_kernel, n_pages=n_pages, page=page, row_base=row_base),
        grid_spec=grid_spec,
        out_shape=jax.ShapeDtypeStruct((nb, N_CG, nh, LANES), F32),
        compiler_params=_cparams(("arbitrary", "arbitrary")),
    )(page_table, cache_t, *cw)


def _softmax_with_new_key(s, s_new):
    m = jnp.maximum(jnp.max(s, axis=-1, keepdims=True), s_new)
    e = jnp.exp(s - m)
    e_new = jnp.exp(s_new - m)
    inv = 1.0 / jnp.maximum(jnp.sum(e, axis=-1, keepdims=True) + e_new, 1e-30)
    return e * inv, e_new * inv


def _nsa_sample_kernel(pt_ref, qs_ref, q_ref, gl_ref, new_ref, ncol_ref, kvc_ref, cache_ref, win_ref, cbs_ref,
                       sbz_ref, wb_ref, b0_ref, as_ref, o_ref, wout_ref, gk_ref, gv_ref, sem_ref, idx_ref,
                       *, past, page, row_base, n_pick):
    b = pl.program_id(0)
    blk_per_page = page // SEL_BLOCK
    nbs = past // SEL_BLOCK
    ncs = kvc_ref.shape[2]
    row = lax.broadcasted_iota(jnp.int32, (SUB, 1), 0)

    cmp_ok = lax.broadcasted_iota(jnp.int32, (SUB, ncs), 1) * CMP_STRIDE + (CMP_BLOCK - 1) <= past
    imp = jnp.zeros((SUB, ncs), F32)
    o_cmp = []
    for k in range(N_KV_HEADS):
        kc = kvc_ref[0, k]
        s = lax.dot_general(qs_ref[0, k], kc, _NT, preferred_element_type=F32) + cbs_ref[k]
        m = jnp.max(jnp.where(cmp_ok, s, NEG), axis=-1, keepdims=True)
        e = jnp.where(cmp_ok, jnp.exp(s - m), 0.0)
        p = e / jnp.maximum(jnp.sum(e, axis=-1, keepdims=True), 1e-30)
        o_cmp.append(jnp.dot(p.astype(BF16), kc, preferred_element_type=F32))
        imp_k = p[0:1]
        for g in range(1, GROUP):
            imp_k = imp_k + p[g:g + 1]
        imp = imp + jnp.where(row == k, imp_k, 0.0)
    bs = sum(jnp.dot(part, as_ref[...], preferred_element_type=F32) for part in _split_bf16(imp, 3))

    lane_b = lax.broadcasted_iota(jnp.int32, (SUB, nbs), 1)
    lane_f = lane_b.astype(F32)
    score = bs + jnp.where((lane_b == 0) | (lane_b == nbs - 1), FORCE_BONUS, 0.0)

    def gathers(k, r):
        j = idx_ref[k * n_pick + r]
        pg = pt_ref[b, j // blk_per_page]
        tile0 = (row_base + pg) * N_KV_SLOTS
        return [pltpu.make_async_copy(
            cache_ref.at[pl.ds(((tile0 + slot) * N_KV_HEADS + k) * HEAD_DIM, HEAD_DIM), :],
            buf.at[k * n_pick + r], sem_ref.at[0]) for slot, buf in ((2, gk_ref), (3, gv_ref))]

    for r in range(n_pick):
        m = jnp.max(score, axis=-1, keepdims=True)
        pick = jnp.min(jnp.where(score == m, lane_f, 1e9), axis=-1, keepdims=True)
        score = jnp.where(lane_f == pick, -jnp.inf, score)
        for k in range(N_KV_HEADS):
            idx_ref[k * n_pick + r] = jnp.sum(jnp.where(row == k, pick, 0.0)).astype(jnp.int32)
            for cp in gathers(k, r):
                cp.start()

    def new_row(k, j):
        return new_ref[0, k][j:j + 1].astype(BF16).astype(F32)

    tok = lax.broadcasted_iota(jnp.int32, (1, WINDOW), 1)
    o_win = []
    for k in range(N_KV_HEADS):
        for slot in range(N_WIN_SLOTS):
            shifted = pltpu.roll(win_ref[0, slot, k], WINDOW - 1, axis=1)
            wout_ref[0, slot, k] = jnp.where(tok == WINDOW - 1, ncol_ref[0, slot, k], shifted)
        q = q_ref[0, k]
        s = jnp.dot(q, win_ref[0, 0, k].astype(BF16), preferred_element_type=F32) + wb_ref[k]
        s = jnp.where(tok >= 1, s, NEG)
        s_new = jnp.sum(q.astype(F32) * new_row(k, 2), axis=-1, keepdims=True) + b0_ref[k][:, 0:1]
        p, p_new = _softmax_with_new_key(s, s_new)
        o_win.append(lax.dot_general(p.astype(BF16), win_ref[0, 1, k].astype(BF16), _NT,
                                     preferred_element_type=F32) + p_new * new_row(k, 3))

    for k in range(N_KV_HEADS):
        for r in range(n_pick):
            for cp in gathers(k, r):
                cp.wait()

    upper = lax.broadcasted_iota(jnp.int32, (1, LANES), 1) // SEL_BLOCK
    for k in range(N_KV_HEADS):
        q = q_ref[0, k]
        tiles = []
        for r in range(n_pick):
            j = idx_ref[k * n_pick + r]
            near = j - (nbs - NEAR // SEL_BLOCK)
            s = jnp.dot(q, gk_ref[k * n_pick + r].astype(BF16), preferred_element_type=F32)
            s = s + sbz_ref[k, jnp.where(near >= 0, near, NEAR // SEL_BLOCK)]
            tiles.append(jnp.where(upper == j % blk_per_page, s, NEG))
        s_new = jnp.sum(q.astype(F32) * new_row(k, 0), axis=-1, keepdims=True) + b0_ref[k][:, 0:1]
        p, p_new = _softmax_with_new_key(jnp.concatenate(tiles, axis=1), s_new)
        o_sel = p_new * new_row(k, 1)
        for r in range(n_pick):
            o_sel = o_sel + lax.dot_general(p[:, r * LANES:(r + 1) * LANES].astype(BF16),
                                            gv_ref[k * n_pick + r].astype(BF16), _NT, preferred_element_type=F32)
        g_cmp, g_sel, g_win = (jax.nn.sigmoid(gl_ref[0, k, j])[:, :HEAD_DIM] for j in range(3))
        o_ref[0, k] = g_cmp * o_cmp[k][:, HEAD_DIM:] + g_sel * o_sel + g_win * o_win[k]


def _bias_rows(bvz, idx):
    t = jnp.take(bvz, jnp.asarray(idx, jnp.int32), axis=0).reshape(len(idx), N_KV_HEADS, GROUP)
    t = jnp.transpose(t, (1, 2, 0))
    return jnp.pad(t, ((0, 0), (0, SUB - GROUP), (0, 0)))


def _nsa_sample(proj_s, kvc, page_table, cache_t, win_t, bvz, past, page, row_base):
    nb = proj_s.shape[0]
    n_pick = N_SELECT - 1
    nbs = past // SEL_BLOCK
    ncs = past // CMP_STRIDE
    n_near = NEAR // SEL_BLOCK
    blk_per_page = page // SEL_BLOCK
    assert past % page == 0 and page == LANES and nbs > n_pick + n_near and (nbs - n_near) % blk_per_page == 0
    assert win_t.shape[-1] == WINDOW < past
    q = (proj_s[:, :N_Q_COLS] * HEAD_DIM ** -0.5).astype(BF16).reshape(nb, N_KV_HEADS, GROUP, HEAD_DIM)
    q = jnp.pad(q, ((0, 0), (0, 0), (0, SUB - GROUP), (0, 0)))
    qs = jnp.pad(q, ((0, 0), (0, 0), (0, 0), (0, LANES - HEAD_DIM)))
    gl = proj_s[:, GATE_COL0:].reshape(nb, N_KV_HEADS, LANES)[:, :, :GROUP * 3].reshape(nb, N_KV_HEADS, GROUP, 3)
    gl = jnp.pad(jnp.transpose(gl, (0, 1, 3, 2)), ((0, 0), (0, 0), (0, 0), (0, SUB - GROUP)))
    gl = jnp.broadcast_to(gl[..., None], gl.shape + (LANES,))
    new = _unpermute_kv(proj_s[:, N_Q_COLS:GATE_COL0], (nb,))[:, 2:]
    new_rows = jnp.transpose(new, (0, 2, 1, 3))
    new_cols = new[:, 2:].reshape(nb, N_WIN_SLOTS, N_KV_HEADS, HEAD_DIM, 1)

    dc = past - (np.arange(ncs) * CMP_STRIDE + CMP_BLOCK - 1)
    cbs = _bias_rows(bvz, np.where((dc >= 0) & (dc < NEAR), dc, NEAR))
    lane = np.arange(LANES)
    sb = []
    for jj in range(n_near + 1):
        d = NEAR - SEL_BLOCK * jj - lane % SEL_BLOCK
        ok = (lane // SEL_BLOCK == jj % blk_per_page) & (jj < n_near) & (d < NEAR)
        sb.append(_bias_rows(bvz, np.where(ok, d, NEAR)))
    sbz = jnp.stack(sb, axis=1)
    dw = WINDOW - np.arange(WINDOW)
    wb = _bias_rows(bvz, np.where(dw < NEAR, dw, NEAR))
    b0 = _bias_rows(bvz, np.zeros(LANES, np.int64))
    a_s = np.zeros((ncs, nbs), np.float32)
    for j in range(nbs):
        for n in range(SEL_RATIO * j - (CMP_RATIO - 1), SEL_RATIO * (j + 1)):
            if 0 <= n < ncs - CMP_RATIO + 1:
                a_s[n, j] = 1.0
    a_s = jnp.asarray(a_s, BF16)

    per_b = lambda a: pl.BlockSpec((1,) + a.shape[1:], lambda b, pt: (b,) + (0,) * (a.ndim - 1))
    full = lambda a: pl.BlockSpec(a.shape, lambda b, pt: (0,) * a.ndim)
    grid_spec = pltpu.PrefetchScalarGridSpec(
        num_scalar_prefetch=1,
        grid=(nb,),
        in_specs=[per_b(qs), per_b(q), per_b(gl), per_b(new_rows), per_b(new_cols), per_b(kvc),
                  pl.BlockSpec(memory_space=pl.ANY), per_b(win_t),
                  full(cbs), full(sbz), full(wb), full(b0), full(a_s)],
        out_specs=[pl.BlockSpec((1, N_KV_HEADS, SUB, HEAD_DIM), lambda b, pt: (b, 0, 0, 0)), per_b(win_t)],
        scratch_shapes=[pltpu.VMEM((N_KV_HEADS * n_pick, HEAD_DIM, page), F32),
                        pltpu.VMEM((N_KV_HEADS * n_pick, HEAD_DIM, page), F32),
                        pltpu.SemaphoreType.DMA((1,)),
                        pltpu.SMEM((N_KV_HEADS * n_pick,), jnp.int32)])
    o, win_out = pl.pallas_call(
        functools.partial(_nsa_sample_kernel, past=past, page=page, row_base=row_base, n_pick=n_pick),
        grid_spec=grid_spec,
        out_shape=[jax.ShapeDtypeStruct((nb, N_KV_HEADS, SUB, HEAD_DIM), F32),
                   jax.ShapeDtypeStruct(win_t.shape, F32)],
        compiler_params=_cparams(("arbitrary",)),
    )(page_table, qs, q, gl, new_rows, new_cols, kvc, cache_t, win_t, cbs, sbz, wb, b0, a_s)
    return o[:, :, :GROUP].reshape(nb, N_Q_COLS), win_out


NP_SSM = -(-(SSM_CONV_DIM + SSM_D_INNER + SSM_HEADS) // LANES) * LANES
SSM_COL_TILE = NP_SSM // 7
assert SSM_COL_TILE % LANES == 0 and SSM_COL_TILE * 7 == NP_SSM
N_BC = SSM_GROUPS * SSM_STATE
HEADS_PER_GROUP = SSM_HEADS // SSM_GROUPS
PAIR = LANES // SSM_HEAD_DIM
N_PAIRS = SSM_HEADS // PAIR
CONV_PAD = 8


def _softplus(x):
    return jnp.maximum(x, 0.0) + jnp.log1p(jnp.exp(-jnp.abs(x)))


def _cumsum_rows(x):
    n = x.shape[0]
    row = lax.broadcasted_iota(jnp.int32, x.shape, 0)
    s = 1
    while s < n:
        x = x + jnp.where(row >= s, pltpu.roll(x, s, axis=0), 0.0)
        s *= 2
    return x


def _ssd_kernel(xbc_ref, z_ref, dt_ref, h0_ref, cinit_ref, cw_ref, cb_ref, dtb_ref, alog_ref, dsk_ref, ng_ref,
                y_ref, hout_ref, xs_ref, act_ref, ybuf_ref, h_ref, *, nc):
    c = pl.program_id(1)
    q = SSM_CHUNK

    @pl.when(c == 0)
    def _():
        xs_ref[0:CONV_PAD] = cinit_ref[0]
        h_ref[...] = h0_ref[0]

    xs_ref[CONV_PAD:CONV_PAD + q] = xbc_ref[...]
    conv = cb_ref[...] + cw_ref[0:1] * xs_ref[CONV_PAD - 3:CONV_PAD - 3 + q]
    for k in range(1, SSM_CONV):
        conv = conv + cw_ref[k:k + 1] * xs_ref[CONV_PAD - 3 + k:CONV_PAD - 3 + k + q]
    xs_ref[0:CONV_PAD] = xs_ref[q:q + CONV_PAD]
    act_ref[...] = _silu(conv)

    dt = _softplus(dt_ref[...] + dtb_ref[...])
    acum = _cumsum_rows(dt * (-jnp.exp(alog_ref[...])))
    acum_t = acum.T
    dt_t = dt.T
    last = acum[q - 1:q, :]
    ea = jnp.exp(acum)
    te = jnp.exp(last - acum) * dt
    cd = jnp.exp(last)
    ii = lax.broadcasted_iota(jnp.int32, (q, q), 0)
    jj = lax.broadcasted_iota(jnp.int32, (q, q), 1)
    tri = ii >= jj
    low = jj < SSM_HEAD_DIM
    low_rows = ii < SSM_HEAD_DIM

    def col(a, h):
        return a[:, h:h + 1]

    for g in range(SSM_GROUPS):
        bg = act_ref[:, SSM_D_INNER + g * SSM_STATE:SSM_D_INNER + (g + 1) * SSM_STATE].astype(BF16)
        cg = act_ref[:, SSM_D_INNER + N_BC + g * SSM_STATE:SSM_D_INNER + N_BC + (g + 1) * SSM_STATE].astype(BF16)
        cbg = lax.dot_general(cg, bg, _NT, preferred_element_type=F32)
        for pr in range(HEADS_PER_GROUP // PAIR):
            k = g * (HEADS_PER_GROUP // PAIR) + pr
            ha, hb = PAIR * k, PAIR * k + 1
            xp = act_ref[:, k * LANES:(k + 1) * LANES]
            xpb = xp.astype(BF16)
            ys = []
            for h in (ha, hb):
                decay = jnp.exp(jnp.where(tri, col(acum, h) - acum_t[h:h + 1, :], NEG))
                w = cbg * decay * dt_t[h:h + 1, :]
                ys.append(jnp.dot(w.astype(BF16), xpb, preferred_element_type=F32))
            y = jnp.where(low, ys[0], ys[1])
            xs_pair = xp * jnp.where(low, col(te, ha), col(te, hb))
            st = jnp.dot(xs_pair.T.astype(BF16), bg, preferred_element_type=F32)
            hprev = h_ref[k]
            yoff = lax.dot_general(cg, hprev.astype(BF16), _NT, preferred_element_type=F32)
            y = y + yoff * jnp.where(low, col(ea, ha), col(ea, hb)) + dsk_ref[:, k * LANES:(k + 1) * LANES] * xp
            h_ref[k] = hprev * jnp.where(low_rows, cd[:, ha:ha + 1], cd[:, hb:hb + 1]) + st
            ybuf_ref[:, k * LANES:(k + 1) * LANES] = y

    yz = ybuf_ref[...] * _silu(z_ref[...])
    gw = SSM_D_INNER // SSM_GROUPS
    outs = []
    for g in range(SSM_GROUPS):
        seg = yz[:, g * gw:(g + 1) * gw]
        outs.append(seg * lax.rsqrt(jnp.mean(seg * seg, axis=-1, keepdims=True) + EPS))
    y_ref[...] = (jnp.concatenate(outs, axis=1) * ng_ref[...]).astype(y_ref.dtype)

    @pl.when(c == nc - 1)
    def _():
        hout_ref[0] = h_ref[...]


def _ssd(proj, h0, cinit, sw, nb, t):
    q = SSM_CHUNK
    nc = t // q
    vec = lambda a: pl.BlockSpec(a.shape, lambda b, c: (0, 0))
    return pl.pallas_call(
        functools.partial(_ssd_kernel, nc=nc),
        grid=(nb, nc),
        in_specs=[pl.BlockSpec((q, SSM_CONV_DIM), lambda b, c: (b * nc + c, 0)),
                  pl.BlockSpec((q, SSM_D_INNER), lambda b, c: (b * nc + c, SSM_CONV_DIM // SSM_D_INNER)),
                  pl.BlockSpec((q, LANES), lambda b, c: (b * nc + c, (SSM_CONV_DIM + SSM_D_INNER) // LANES)),
                  pl.BlockSpec((1,) + h0.shape[1:], lambda b, c: (b, 0, 0, 0)),
                  pl.BlockSpec((1,) + cinit.shape[1:], lambda b, c: (b, 0, 0)),
                  vec(sw['conv_w']), vec(sw['conv_b']), vec(sw['dt_bias']), vec(sw['a_log']), vec(sw['d_lane']),
                  vec(sw['norm_g'])],
        out_specs=[pl.BlockSpec((q, SSM_D_INNER), lambda b, c: (b * nc + c, 0)),
                   pl.BlockSpec((1,) + h0.shape[1:], lambda b, c: (b, 0, 0, 0))],
        out_shape=[jax.ShapeDtypeStruct((nb * t, SSM_D_INNER), BF16),
                   jax.ShapeDtypeStruct(h0.shape, F32)],
        scratch_shapes=[pltpu.VMEM((CONV_PAD + q, SSM_CONV_DIM), F32),
                        pltpu.VMEM((q, SSM_CONV_DIM), F32),
                        pltpu.VMEM((q, SSM_D_INNER), F32),
                        pltpu.VMEM(h0.shape[1:], F32)],
        compiler_params=_cparams(("arbitrary", "arbitrary")),
    )(proj, proj, proj, h0, cinit, sw['conv_w'], sw['conv_b'], sw['dt_bias'], sw['a_log'], sw['d_lane'],
      sw['norm_g'])


def _ssm_sample_step(proj_s, state_ssm, state_conv, sw):
    nb = proj_s.shape[0]
    q = SSM_CHUNK
    xbc_new = proj_s[:, :SSM_CONV_DIM]
    rows = jnp.zeros((nb, q, NP_SSM), F32)
    rows = rows.at[:, :, SSM_CONV_DIM + SSM_D_INNER:].set(NEG)
    rows = rows.at[:, q - SSM_CONV:q - 1, :SSM_CONV_DIM].set(state_conv.astype(F32))
    rows = rows.at[:, q - 1].set(proj_s)
    h0 = state_ssm.astype(F32).reshape(nb, N_PAIRS, LANES, SSM_STATE)
    cinit = jnp.zeros((nb, CONV_PAD, SSM_CONV_DIM), F32)
    yn, hfin = _ssd(rows.reshape(nb * q, NP_SSM), h0, cinit, sw, nb, q)
    conv_new = jnp.concatenate([state_conv[:, 1:].astype(F32), xbc_new[:, None]], axis=1)
    return yn.reshape(nb, q, SSM_D_INNER)[:, q - 1], hfin.reshape(state_ssm.shape), conv_new


def _ssm_weights(w_in, conv_w, conv_b, dt_bias, a_log, d_skip, norm_g):
    z_w = w_in[:, :SSM_D_INNER]
    xbc_w = w_in[:, SSM_D_INNER:SSM_D_INNER + SSM_CONV_DIM]
    dt_w = w_in[:, SSM_D_INNER + SSM_CONV_DIM:]
    pad = NP_SSM - w_in.shape[1]
    w = jnp.concatenate([xbc_w, z_w, dt_w, jnp.zeros((w_in.shape[0], pad), w_in.dtype)], axis=1).astype(BF16)
    lane_pad = lambda v: jnp.zeros((1, LANES), F32).at[0, :SSM_HEADS].set(v.astype(F32))
    return dict(w_in=w, conv_w=conv_w.astype(F32), conv_b=conv_b.astype(F32).reshape(1, -1),
                dt_bias=lane_pad(dt_bias), a_log=lane_pad(a_log),
                d_lane=jnp.repeat(d_skip.astype(F32), SSM_HEAD_DIM).reshape(1, -1),
                norm_g=norm_g.astype(F32).reshape(1, -1))


ROW_TILE = 512
MLP_ROW_TILE = 1024
FF_TILE = 1024


def kernel(x_prompt, x_sample, cache_kv, cache_win, state_ssm, state_conv, page_table, c_prompt, c_sample, rel_bias,
           ada_w, ada_b, norm_g, mlp_w1, mlp_w2, attn_w_in, attn_w_out, cmp_pos, cmp_w1, cmp_w2, ssm_w_in,
           ssm_conv_w, ssm_conv_b, ssm_dt_bias, ssm_a_log, ssm_d, ssm_norm_g, ssm_w_out):
    nb, t, d = x_prompt.shape
    db = x_sample.shape[0]
    assert x_sample.shape[1] == 1 and t % SSM_CHUNK == 0 and t % Q_BLOCK == 0
    n_pool, page = cache_kv.shape[1], cache_kv.shape[2]
    past = page_table.shape[1] * page
    depth = ada_w.shape[0]
    tm = _row_tile(t, ROW_TILE)

    xp = x_prompt.reshape(nb * t, d).astype(F32)
    xs = x_sample.reshape(db, d).astype(F32)
    c_all = jnp.concatenate([c_prompt, c_sample], axis=0).astype(F32)
    c_all = jnp.pad(c_all, ((0, (-c_all.shape[0]) % SUB), (0, 0)))
    bvz = _bias_vector(rel_bias)
    cache_t = _cache_tiles(cache_kv)
    win_t = jnp.transpose(cache_win, (0, 1, 3, 4, 5, 2))

    kv_p, win_p, ssm_p, conv_p, kv_s, win_s, ssm_s, conv_s = ([] for _ in range(8))
    for i in range(depth):
        mod = _ada(c_all, ada_w[i], ada_b[i])
        mp = [mod[:nb, j * d:(j + 1) * d].reshape(nb, 1, d) for j in range(N_MOD)]
        ms = [mod[nb:nb + db, j * d:(j + 1) * d].reshape(1, db, d) for j in range(N_MOD)]
        g = norm_g[i].astype(F32)
        if i % 2 == 0:
            a = i // 2
            w_in = _attn_in_weights(attn_w_in[a])
            w_out = attn_w_out[a].astype(BF16)
            cw = _compress_weights(cmp_pos[a], cmp_w1[a], cmp_w2[a], [(0, 1)])
            o_p, kv6 = _nsa_prompt_mixer(xp, g[0], mp[0], mp[1], w_in, cw, rel_bias, nb, t)
            kv_p.append(kv6[:, :, :N_KV_SLOTS])
            win_p.append(kv6[:, -min(WINDOW, t):, N_KV_SLOTS:])
            xp = _heads_out(o_p, _attn_out_weights(attn_w_out[a]), g[1], mp[2], xp, tm)

            proj_s = _nm_matmul(xs, g[0], ms[0], ms[1], w_in, db, NP_ATTN)
            cw_s = _compress_weights(cmp_pos[a], cmp_w1[a], cmp_w2[a], [(0, 0), (1, 1)])
            kvc_s = _pack_cmp(_compress_sample(page_table, cache_t, cw_s, page, a * n_pool))
            o_s, win_new = _nsa_sample(proj_s, kvc_s, page_table, cache_t, win_t[a], bvz, past, page, a * n_pool)
            kv_s.append(_unpermute_kv(proj_s[:, N_Q_COLS:GATE_COL0], (db, 1))[:, :, :N_KV_SLOTS])
            win_s.append(jnp.transpose(win_new, (0, 4, 1, 2, 3)))
            xs = _mm_norm_res(o_s, w_out, g[1], ms[2], xs, db)
        else:
            m = i // 2
            sw = _ssm_weights(ssm_w_in[m], ssm_conv_w[m], ssm_conv_b[m], ssm_dt_bias[m], ssm_a_log[m], ssm_d[m],
                              ssm_norm_g[m])
            w_out = ssm_w_out[m].astype(BF16)
            proj = _nm_matmul(xp, g[0], mp[0], mp[1], sw['w_in'], tm, SSM_COL_TILE)
            h0 = jnp.zeros((nb, N_PAIRS, LANES, SSM_STATE), F32)
            cinit = jnp.zeros((nb, CONV_PAD, SSM_CONV_DIM), F32)
            yn, hfin = _ssd(proj, h0, cinit, sw, nb, t)
            ssm_p.append(hfin.reshape(nb, SSM_HEADS, SSM_HEAD_DIM, SSM_STATE).astype(state_ssm.dtype))
            conv_p.append(proj.reshape(nb, t, NP_SSM)[:, t - (SSM_CONV - 1):, :SSM_CONV_DIM])
            xp = _mm_norm_res(yn, w_out, g[1], mp[2], xp, tm)

            proj_s = _nm_matmul(xs, g[0], ms[0], ms[1], sw['w_in'], db, SSM_COL_TILE)
            yn_s, h_s, conv_new = _ssm_sample_step(proj_s, state_ssm[m], state_conv[m], sw)
            ssm_s.append(h_s.astype(state_ssm.dtype))
            conv_s.append(conv_new)
            xs = _mm_norm_res(yn_s, w_out, g[1], ms[2], xs, db)
        w1 = mlp_w1[i].astype(BF16)
        w2 = mlp_w2[i].astype(BF16)
        xp = _mlp(xp, g[2], mp[3], mp[4], w1, w2, g[3], mp[5], _row_tile(t, MLP_ROW_TILE), FF_TILE)
        xs = _mlp(xs, g[2], ms[3], ms[4], w1, w2, g[3], ms[5], db, FF_TILE)
    return (xp.reshape(nb, t, d), xs.reshape(db, 1, d), jnp.stack(kv_p), jnp.stack(win_p), jnp.stack(ssm_p),
            jnp.stack(conv_p), jnp.stack(kv_s), jnp.stack(win_s), jnp.stack(ssm_s), jnp.stack(conv_s))
```

```python
import functools
import math

import numpy as np
import jax
import jax.numpy as jnp
from jax import lax
from jax.experimental import pallas as pl
from jax.experimental.pallas import tpu as pltpu

F32 = jnp.float32
BF16 = jnp.bfloat16

D_MODEL = 1024
N_HEADS = 16
HEAD_DIM = 64
N_KV_HEADS = 4
GROUP = N_HEADS // N_KV_HEADS
CMP_BLOCK = 32
CMP_STRIDE = 16
CMP_RATIO = CMP_BLOCK // CMP_STRIDE
CMP_HIDDEN = 2 * HEAD_DIM
SEL_BLOCK = 64
SEL_RATIO = SEL_BLOCK // CMP_STRIDE
N_SELECT = 16
WINDOW = 512
FORCE_BONUS = 1e4
Q_BLOCK = 128
N_KV_SLOTS = 4
N_WIN_SLOTS = 2
N_Q_COLS = N_HEADS * HEAD_DIM
N_KV_COLS = (N_KV_SLOTS + N_WIN_SLOTS) * N_KV_HEADS * HEAD_DIM
N_GATE_COLS = 3 * N_HEADS
KV_ROW = N_KV_HEADS * HEAD_DIM
KV_HEAD_COLS = (N_KV_SLOTS + N_WIN_SLOTS) * HEAD_DIM
N_BUCKETS = 32
MAX_DISTANCE = 128
SSM_D_INNER = 2 * D_MODEL
SSM_HEAD_DIM = 64
SSM_HEADS = SSM_D_INNER // SSM_HEAD_DIM
SSM_GROUPS = 8
SSM_STATE = 128
SSM_CONV = 4
SSM_CONV_DIM = SSM_D_INNER + 2 * SSM_GROUPS * SSM_STATE
SSM_CHUNK = 128
D_FF = 4 * D_MODEL
N_MOD = 6
EPS = 1e-6

LANES = 128
NEG = -1e30
LOG2E = math.log2(math.e)
VMEM_LIMIT = 56 * 1024 * 1024
NEAR = 2 * LANES

_NT = (((1,), (1,)), ((), ()))


def _cparams(sem):
    return pltpu.CompilerParams(dimension_semantics=sem, vmem_limit_bytes=VMEM_LIMIT)


def _rms(x, g):
    return x * lax.rsqrt(jnp.mean(x * x, axis=-1, keepdims=True) + EPS) * g


def _silu(x):
    return x * jax.nn.sigmoid(x)


def _split_bf16(x, n):
    parts = []
    for _ in range(n - 1):
        p = x.astype(BF16)
        parts.append(p)
        x = x - p.astype(F32)
    parts.append(x.astype(BF16))
    return parts


def _ada_kernel(c_ref, w_ref, b_ref, o_ref):
    s = _silu(c_ref[...]).astype(BF16)
    o_ref[...] = jnp.dot(s, w_ref[...].astype(BF16), preferred_element_type=F32) + b_ref[...]


def _ada(c, w, b, tn=1024):
    m, d = c.shape
    n = w.shape[1]
    return pl.pallas_call(
        _ada_kernel,
        grid=(n // tn,),
        in_specs=[pl.BlockSpec((m, d), lambda j: (0, 0)),
                  pl.BlockSpec((d, tn), lambda j: (0, j)),
                  pl.BlockSpec((1, tn), lambda j: (0, j))],
        out_specs=pl.BlockSpec((m, tn), lambda j: (0, j)),
        out_shape=jax.ShapeDtypeStruct((m, n), F32),
        compiler_params=_cparams(("arbitrary",)),
    )(c, w, b.reshape(1, n))


def _nm_matmul_kernel(x_ref, g_ref, sh_ref, sc_ref, w_ref, o_ref, h_ref):
    @pl.when(pl.program_id(1) == 0)
    def _():
        h = _rms(x_ref[...], g_ref[...]) * (1.0 + sc_ref[0]) + sh_ref[0]
        h_ref[...] = h.astype(BF16)

    o_ref[...] = jnp.dot(h_ref[...], w_ref[...], preferred_element_type=F32)


def _nm_matmul(x, g, shift, scale, w, tm, tn):
    m, d = x.shape
    n = w.shape[1]
    nb = shift.shape[0]
    tpb = m // nb // tm
    mod_spec = pl.BlockSpec((1,) + shift.shape[1:], lambda i, j: (i // tpb, 0, 0))
    return pl.pallas_call(
        _nm_matmul_kernel,
        grid=(m // tm, n // tn),
        in_specs=[pl.BlockSpec((tm, d), lambda i, j: (i, 0)),
                  pl.BlockSpec((1, d), lambda i, j: (0, 0)),
                  mod_spec, mod_spec,
                  pl.BlockSpec((d, tn), lambda i, j: (0, j))],
        out_specs=pl.BlockSpec((tm, tn), lambda i, j: (i, j)),
        out_shape=jax.ShapeDtypeStruct((m, n), F32),
        scratch_shapes=[pltpu.VMEM((tm, d), BF16)],
        compiler_params=_cparams(("arbitrary", "arbitrary")),
    )(x, g.reshape(1, d), shift, scale, w)


def _mm_norm_res_kernel(a_ref, w_ref, g_ref, gate_ref, x_ref, o_ref):
    y = jnp.dot(a_ref[...].astype(BF16), w_ref[...], preferred_element_type=F32)
    o_ref[...] = x_ref[...] + gate_ref[0] * _rms(y, g_ref[...])


def _mm_norm_res(a, w, g, gate, x, tm):
    m, k = a.shape
    d = w.shape[1]
    nb = gate.shape[0]
    tpb = m // nb // tm
    return pl.pallas_call(
        _mm_norm_res_kernel,
        grid=(m // tm,),
        in_specs=[pl.BlockSpec((tm, k), lambda i: (i, 0)),
                  pl.BlockSpec((k, d), lambda i: (0, 0)),
                  pl.BlockSpec((1, d), lambda i: (0, 0)),
                  pl.BlockSpec((1,) + gate.shape[1:], lambda i: (i // tpb, 0, 0)),
                  pl.BlockSpec((tm, d), lambda i: (i, 0))],
        out_specs=pl.BlockSpec((tm, d), lambda i: (i, 0)),
        out_shape=jax.ShapeDtypeStruct((m, d), F32),
        compiler_params=_cparams(("arbitrary",)),
    )(a, w, g.reshape(1, d), gate, x)


def _mlp_kernel(x_ref, g1_ref, sh_ref, sc_ref, w1_ref, w2_ref, g2_ref, gate_ref, o_ref, h_ref, acc_ref):
    c = pl.program_id(1)

    @pl.when(c == 0)
    def _():
        h = _rms(x_ref[...], g1_ref[...]) * (1.0 + sc_ref[0]) + sh_ref[0]
        h_ref[...] = h.astype(BF16)
        acc_ref[...] = jnp.zeros_like(acc_ref)

    a = jnp.dot(h_ref[...], w1_ref[...], preferred_element_type=F32)
    a = jnp.square(jnp.maximum(a, 0.0)).astype(BF16)
    acc_ref[...] += jnp.dot(a, w2_ref[...], preferred_element_type=F32)

    @pl.when(c == pl.num_programs(1) - 1)
    def _():
        o_ref[...] = x_ref[...] + gate_ref[0] * _rms(acc_ref[...], g2_ref[...])


def _mlp(x, g1, shift, scale, w1, w2, g2, gate, tm, tf):
    m, d = x.shape
    f = w1.shape[1]
    nb = shift.shape[0]
    tpb = m // nb // tm
    mod_spec = pl.BlockSpec((1,) + shift.shape[1:], lambda i, c: (i // tpb, 0, 0))
    vec_spec = pl.BlockSpec((1, d), lambda i, c: (0, 0))
    return pl.pallas_call(
        _mlp_kernel,
        grid=(m // tm, f // tf),
        in_specs=[pl.BlockSpec((tm, d), lambda i, c: (i, 0)), vec_spec, mod_spec, mod_spec,
                  pl.BlockSpec((d, tf), lambda i, c: (0, c)),
                  pl.BlockSpec((tf, d), lambda i, c: (c, 0)),
                  vec_spec, mod_spec],
        out_specs=pl.BlockSpec((tm, d), lambda i, c: (i, 0)),
        out_shape=jax.ShapeDtypeStruct((m, d), F32),
        scratch_shapes=[pltpu.VMEM((tm, d), BF16), pltpu.VMEM((tm, d), F32)],
        compiler_params=_cparams(("arbitrary", "arbitrary")),
    )(x, g1.reshape(1, d), shift, scale, w1, w2, g2.reshape(1, d), gate)


def _bias_kernel(oh_ref, t_ref, o_ref):
    t = t_ref[...]
    t = t - t[N_BUCKETS - 1:N_BUCKETS, :]
    oh = oh_ref[...]
    o_ref[...] = sum(jnp.dot(oh, p, preferred_element_type=F32) for p in _split_bf16(t, 3))


def _rel_bucket_np(dist):
    n = np.maximum(dist, 0)
    exact = N_BUCKETS // 2
    nf = np.maximum(n, 1).astype(np.float32)
    large = exact + (np.log(nf / exact) / math.log(MAX_DISTANCE / exact) * (N_BUCKETS - exact)).astype(np.int32)
    return np.where(n < exact, n, np.minimum(large, N_BUCKETS - 1))


def _bias_vector(rel_bias):
    assert _rel_bucket_np(np.array([NEAR - 1]))[0] == N_BUCKETS - 1 == _rel_bucket_np(np.array([MAX_DISTANCE]))[0]
    oh = np.zeros((NEAR + 8, LANES), np.float32)
    oh[np.arange(NEAR), _rel_bucket_np(np.arange(NEAR))] = 1.0
    oh[NEAR:, N_BUCKETS - 1] = 1.0
    table = jnp.zeros((LANES, LANES), F32).at[:N_BUCKETS, :N_HEADS].set(rel_bias.astype(F32))
    bv = pl.pallas_call(
        _bias_kernel,
        out_shape=jax.ShapeDtypeStruct((NEAR + 8, LANES), F32),
    )(jnp.asarray(oh, BF16), table)
    return bv[:NEAR + 1, :N_HEADS]


SUB = 8
HEADS_PER_CG = LANES // HEAD_DIM
N_CG = 2 * KV_ROW // LANES
CG_PER_SLOT = KV_ROW // LANES
CG_HIDDEN = HEADS_PER_CG * CMP_HIDDEN


def _compress_parts(rows_s, wbd_ref, nh):
    part0 = jnp.zeros((nh, CG_HIDDEN), F32)
    part1 = jnp.zeros((nh, CG_HIDDEN), F32)
    for s in range(0, CMP_STRIDE, 2):
        xs = jnp.concatenate([rows_s(s), rows_s(s + 1)], axis=1).astype(BF16)
        part0 = part0 + jnp.dot(xs, wbd_ref[0, 0, s // 2], preferred_element_type=F32)
        part1 = part1 + jnp.dot(xs, wbd_ref[0, 1, s // 2], preferred_element_type=F32)
    return part0, part1


def _compress_finish(part0, part1, pos_ref, w1_ref, w2bd_ref, nh):
    pre0 = jnp.dot(pos_ref[0], w1_ref[0], preferred_element_type=F32)[0:1]
    pre = pre0 + part0 + pltpu.roll(part1, nh - 1, axis=0)
    return jnp.dot(_silu(pre).astype(BF16), w2bd_ref[0], preferred_element_type=F32)


def _compress_kernel(x_ref, pos_ref, w1_ref, wbd_ref, w2bd_ref, o_ref, *, nh):
    part0, part1 = _compress_parts(lambda s: x_ref[pl.ds(s, nh, stride=CMP_STRIDE), :], wbd_ref, nh)
    o_ref[0, 0] = _compress_finish(part0, part1, pos_ref, w1_ref, w2bd_ref, nh)


def _block_diag(a, b):
    za = jnp.zeros(a.shape[:-1] + (b.shape[-1],), a.dtype)
    zb = jnp.zeros(b.shape[:-1] + (a.shape[-1],), b.dtype)
    return jnp.concatenate([jnp.concatenate([a, za], axis=-1), jnp.concatenate([zb, b], axis=-1)], axis=-2)


def _compress_weight_set(pos, w1, w2, a, b):
    w1r = w1.reshape(2, CMP_RATIO, CMP_STRIDE, HEAD_DIM, CMP_HIDDEN)
    wbd = _block_diag(w1r[a], w1r[b]).reshape(CMP_RATIO, CMP_STRIDE // 2, 2 * LANES, CG_HIDDEN)
    posx = jnp.zeros((SUB, 2 * CMP_BLOCK * HEAD_DIM), F32).at[0].set(
        jnp.concatenate([pos[a].reshape(-1), pos[b].reshape(-1)]))
    return [x.astype(BF16) for x in (posx, _block_diag(w1[a], w1[b]), wbd, _block_diag(w2[a], w2[b]))]


def _compress_weights(cmp_pos, cmp_w1, cmp_w2, pairs):
    sets = [_compress_weight_set(cmp_pos, cmp_w1, cmp_w2, a, b) for a, b in pairs]
    return [jnp.stack(x) for x in zip(*sets)]


def _compress_specs(cw, imap):
    def spec(a):
        return pl.BlockSpec((1,) + a.shape[1:], lambda *g: (imap(*g),) + (0,) * (a.ndim - 1))
    return [spec(a) for a in cw]


def _pack_cmp(cmp, dtype=None):
    nb, _, n, _ = cmp.shape
    c = cmp.reshape(nb, 2, CG_PER_SLOT, n, HEADS_PER_CG, HEAD_DIM)
    c = jnp.transpose(c, (0, 2, 4, 3, 1, 5)).reshape(nb, N_KV_HEADS, n, 2 * HEAD_DIM)
    return c.astype(BF16 if dtype is None else dtype)


def _compress_prompt(proj, cw, nb, t):
    nh = t // CMP_STRIDE
    return pl.pallas_call(
        functools.partial(_compress_kernel, nh=nh),
        grid=(nb, N_KV_HEADS),
        in_specs=[pl.BlockSpec((t, LANES), lambda b, k: (b, (N_Q_COLS + k * KV_HEAD_COLS) // LANES))]
        + _compress_specs(cw, lambda b, k: 0),
        out_specs=pl.BlockSpec((1, 1, nh, LANES), lambda b, k: (b, k, 0, 0)),
        out_shape=jax.ShapeDtypeStruct((nb, N_KV_HEADS, nh, LANES), F32),
        compiler_params=_cparams(("arbitrary", "arbitrary")),
    )(proj, *cw)


KEY_TILE = 4 * Q_BLOCK
FRONT_PAD = KEY_TILE
SUBTILES = KEY_TILE // Q_BLOCK
BAND_OFF = 2 * Q_BLOCK // CMP_STRIDE
BAND_W = 32
Q_PER_CMP = Q_BLOCK // CMP_STRIDE


def _nsa_consts(t):
    ncp = t // CMP_STRIDE
    nblk = t // SEL_BLOCK
    nbp = -(-nblk // LANES) * LANES
    at = np.zeros((nbp, ncp), np.float32)
    for j in range(nblk):
        for n in range(SEL_RATIO * j - (CMP_RATIO - 1), SEL_RATIO * (j + 1)):
            if 0 <= n < ncp - CMP_RATIO + 1:
                at[j, n] = 1.0
    key_blk = np.arange(t) // SEL_BLOCK
    e = (np.arange(nbp)[:, None] == key_blk[None, :]).astype(np.float32)
    e1 = e.reshape(nbp, t // LANES, LANES).transpose(1, 0, 2)
    e1 = np.concatenate([np.zeros((SUBTILES,) + e1.shape[1:], np.float32), e1], axis=0)
    band = np.zeros((2 * ncp + 2 * Q_PER_CMP, LANES), np.float32)
    u = np.arange(BAND_W)
    band[u + ncp, HEAD_DIM + u] = 1.0
    band[u + ncp, HEAD_DIM + BAND_W + u] = 1.0
    r = np.arange(Q_BLOCK)[:, None]
    c = np.arange(LANES)[None, :]
    d0 = r - c
    idx0 = np.where(d0 >= 0, d0, NEAR)
    idx1 = Q_BLOCK + r - c
    dc = r - CMP_STRIDE * c + (CMP_STRIDE * BAND_OFF - CMP_BLOCK + 1)
    idxc = np.where((dc >= 0) & (dc < NEAR), dc, NEAR)
    assert np.all(idxc[:, BAND_W:] == NEAR)
    bf = lambda a: jnp.asarray(a, BF16)
    return dict(at=bf(at), e1=bf(e1), band=jnp.asarray(band),
                idx0=idx0, idx1=idx1, idxc=idxc, ncp=ncp, nbp=nbp, nblk=nblk)


def _nsa_kernel(q_ref, gl_ref, kck_ref, kvc_ref, kvs_ref, kvw_ref, at_ref, e1_ref, tb_ref, cbq_ref, band_ref,
                o_ref, *, ncp, nbp, n_sel):
    i = pl.program_id(2)
    rows = GROUP * Q_BLOCK
    lane = lax.broadcasted_iota(jnp.int32, (1, LANES), 1)
    low = lane < HEAD_DIM
    r_col = lax.broadcasted_iota(jnp.int32, (rows, 1), 0) % Q_BLOCK
    q_pos = i * Q_BLOCK + r_col

    q = q_ref[...] * (HEAD_DIM ** -0.5 * LOG2E)
    parts = []
    for h in range(GROUP * HEAD_DIM // LANES):
        qh = q[:, h * LANES:(h + 1) * LANES]
        parts.append(jnp.where(low, qh, 0.0))
        parts.append(jnp.where(low, pltpu.roll(qh, HEAD_DIM, axis=1), 0.0))
    qs = jnp.concatenate(parts, axis=0).astype(BF16)

    qc = jnp.where(low, qs, cbq_ref[0])
    start = pl.multiple_of(ncp + BAND_OFF - Q_PER_CMP * i, 8)
    kq = (kck_ref[0, 0] + band_ref[pl.ds(start, ncp), :]).astype(BF16)
    s = lax.dot_general(qc, kq, _NT, preferred_element_type=F32)
    cmp_end = lax.broadcasted_iota(jnp.int32, (1, ncp), 1) * CMP_STRIDE + (CMP_BLOCK - 1)
    mask = cmp_end <= q_pos
    m = jnp.max(jnp.where(mask, s, NEG), axis=-1, keepdims=True)
    e = jnp.where(mask, jnp.exp2(s - m), 0.0)
    p_cmp = e * (1.0 / jnp.maximum(jnp.sum(e, axis=-1, keepdims=True), 1e-30))
    o_cmp = jnp.dot(p_cmp.astype(BF16), kvc_ref[0, 0], preferred_element_type=F32)
    imp = p_cmp[0:Q_BLOCK]
    for g in range(1, GROUP):
        imp = imp + p_cmp[g * Q_BLOCK:(g + 1) * Q_BLOCK]
    bst = sum(lax.dot_general(at_ref[...], part, _NT, preferred_element_type=F32)
              for part in _split_bf16(imp, 2))

    jb = lax.broadcasted_iota(jnp.int32, (nbp, Q_BLOCK), 0)
    q_blk = (i * Q_BLOCK + lax.broadcasted_iota(jnp.int32, (nbp, Q_BLOCK), 1)) // SEL_BLOCK
    causal = jb <= q_blk
    forced = (jb == 0) | (jb == q_blk) | (jb == q_blk - 1)
    work = jnp.where(causal, bst + jnp.where(forced, FORCE_BONUS, 0.0), -jnp.inf)
    jbf = jb.astype(F32)
    sel_t = jnp.zeros((nbp, Q_BLOCK), F32)
    for _ in range(n_sel):
        top = jnp.max(work, axis=0, keepdims=True)
        first = jnp.min(jnp.where(work == top, jbf, float(nbp)), axis=0, keepdims=True)
        hit = jbf == first
        sel_t = jnp.where(hit, 1.0, sel_t)
        work = jnp.where(hit, -jnp.inf, work)
    sel = jnp.where(causal, sel_t, 0.0).T.astype(BF16)

    def ones_and_values(kv):
        return jnp.where(low, jnp.ones_like(kv), kv)

    def normalise(acc):
        return jnp.where(low, 0.0, acc / jnp.maximum(pltpu.roll(acc, HEAD_DIM, axis=1), 1e-30))

    n_wt = WINDOW // Q_BLOCK
    kw = kvw_ref[0, 0, pl.ds(pl.multiple_of((i + SUBTILES - n_wt) * Q_BLOCK, Q_BLOCK), WINDOW + Q_BLOCK), :]
    s = lax.dot_general(qs, kw, _NT, preferred_element_type=F32)
    pieces = []
    for u in range(n_wt + 1):
        su = s[:, u * Q_BLOCK:(u + 1) * Q_BLOCK]
        if u == n_wt:
            su = jnp.where(lane <= r_col, su + tb_ref[0, 0], NEG)
        else:
            if u == n_wt - 1:
                su = su + tb_ref[0, 1]
            su = su + jnp.where(i - n_wt + u >= 0, 0.0, NEG)
            if u == 0:
                su = jnp.where(lane > r_col, su, NEG)
        pieces.append(su)
    sm = jnp.concatenate(pieces, axis=1).astype(BF16)
    e = jnp.exp2(sm - jnp.max(sm, axis=-1, keepdims=True))
    o_win = normalise(jnp.dot(e, ones_and_values(kw), preferred_element_type=F32))

    n_tiles = (i + SUBTILES) // SUBTILES

    def keys(u):
        return kvs_ref[0, 0, pl.ds(pl.multiple_of((i + 1 - SUBTILES * u) * Q_BLOCK, Q_BLOCK), KEY_TILE), :]

    def masked_scores(u, diagonal=False):
        t0p = i + 1 - SUBTILES * u
        mk = jnp.concatenate([jnp.dot(sel, e1_ref[t0p + v], preferred_element_type=F32) for v in range(SUBTILES)],
                             axis=1).astype(BF16)
        s3 = lax.dot_general(qs, keys(u), _NT, preferred_element_type=F32).reshape(GROUP, Q_BLOCK, KEY_TILE)
        if diagonal:
            r3 = lax.broadcasted_iota(jnp.int32, (1, Q_BLOCK, LANES), 1)
            c3 = lax.broadcasted_iota(jnp.int32, (1, Q_BLOCK, LANES), 2)
            tb0 = tb_ref[0, 0].reshape(GROUP, Q_BLOCK, LANES)
            tb1 = tb_ref[0, 1].reshape(GROUP, Q_BLOCK, LANES)
            s3 = jnp.concatenate([s3[:, :, :KEY_TILE - 2 * Q_BLOCK],
                                  s3[:, :, KEY_TILE - 2 * Q_BLOCK:KEY_TILE - Q_BLOCK] + tb1,
                                  jnp.where(c3 <= r3, s3[:, :, KEY_TILE - Q_BLOCK:] + tb0, NEG)], axis=2)
        return jnp.where((mk > 0.5)[None], s3.astype(BF16), NEG).reshape(rows, KEY_TILE)

    def sel_body(u, carry):
        m, acc, sm, e_prev = carry
        pv = jnp.dot(e_prev, ones_and_values(keys(jnp.maximum(u - 1, 0))), preferred_element_type=F32)
        sm_next = masked_scores(jnp.minimum(u + 1, n_tiles - 1))
        m_new = jnp.maximum(m, jnp.max(sm, axis=-1, keepdims=True).astype(F32))
        alpha = jnp.exp2(m - m_new)
        e = jnp.exp2(sm - m_new.astype(BF16))
        return m_new, alpha * (acc + pv), sm_next, e

    init = (jnp.full((rows, 1), NEG, F32), jnp.zeros((rows, LANES), F32),
            masked_scores(0, diagonal=True), jnp.zeros((rows, KEY_TILE), BF16))
    m, acc, _, e_last = lax.fori_loop(0, n_tiles, sel_body, init)
    o_sel = normalise(acc + jnp.dot(e_last, ones_and_values(keys(n_tiles - 1)), preferred_element_type=F32))

    sg = jax.nn.sigmoid(gl_ref[...])
    for g in range(GROUP):
        out = jnp.zeros((Q_BLOCK, LANES), F32)
        for j, ob in enumerate((o_cmp, o_sel, o_win)):
            out = out + sg[:, 3 * g + j:3 * g + j + 1] * ob[g * Q_BLOCK:(g + 1) * Q_BLOCK]
        o_ref[0, 0, g] = out.astype(o_ref.dtype)


GATE_COL0 = N_Q_COLS + N_KV_COLS
NP_ATTN = GATE_COL0 + N_KV_HEADS * LANES


def _nsa_prompt(proj, kck, kvc, kvs, kvw, tb, cbq, consts, nb, t):
    ni = t // Q_BLOCK
    ncp, nbp = consts['ncp'], consts['nbp']
    gcol = GATE_COL0 // LANES
    qw = GROUP * HEAD_DIM
    full = lambda a: pl.BlockSpec(a.shape, lambda b, k, i: (0,) * a.ndim)
    per_kvh = lambda a: pl.BlockSpec((1,) + a.shape[1:], lambda b, k, i: (k,) + (0,) * (a.ndim - 1))
    per_bk = lambda a: pl.BlockSpec((1, 1) + a.shape[2:], lambda b, k, i: (b, k, 0, 0))
    c = consts
    n_sel = min(N_SELECT, c['nblk'])
    assert n_sel >= 3
    return pl.pallas_call(
        functools.partial(_nsa_kernel, ncp=ncp, nbp=nbp, n_sel=n_sel),
        grid=(nb, N_KV_HEADS, ni),
        in_specs=[pl.BlockSpec((Q_BLOCK, qw), lambda b, k, i: (b * ni + i, k)),
                  pl.BlockSpec((Q_BLOCK, LANES), lambda b, k, i: (b * ni + i, gcol + k)),
                  per_bk(kck), per_bk(kvc), per_bk(kvs), per_bk(kvw),
                  full(c['at']), full(c['e1']), per_kvh(tb), per_kvh(cbq), full(c['band'])],
        out_specs=pl.BlockSpec((1, 1, GROUP, Q_BLOCK, LANES), lambda b, k, i: (b, k, 0, i, 0)),
        out_shape=jax.ShapeDtypeStruct((nb, N_KV_HEADS, GROUP, t, LANES), BF16),
        compiler_params=_cparams(("arbitrary", "arbitrary", "arbitrary")),
    )(proj, proj, kck, kvc, kvs, kvw, c['at'], c['e1'], tb, cbq, c['band'])


def _heads_out_kernel(a_ref, w_ref, g_ref, gate_ref, x_ref, o_ref):
    y = jnp.zeros(o_ref.shape, F32)
    for k in range(N_KV_HEADS):
        for g in range(0, GROUP, 2):
            a = jnp.concatenate([a_ref[0, k, g], a_ref[0, k, g + 1]], axis=1)
            h = k * GROUP + g
            w = w_ref[h * LANES:(h + 2) * LANES, :]
            y = y + jnp.dot(a, w, preferred_element_type=F32)
    o_ref[...] = x_ref[...] + gate_ref[0] * _rms(y, g_ref[...])


def _heads_out(a, w_pad, g, gate, x, tm):
    nb, _, _, t, _ = a.shape
    d = w_pad.shape[1]
    tpb = t // tm
    return pl.pallas_call(
        _heads_out_kernel,
        grid=(nb * tpb,),
        in_specs=[pl.BlockSpec((1, N_KV_HEADS, GROUP, tm, LANES), lambda i: (i // tpb, 0, 0, i % tpb, 0)),
                  pl.BlockSpec(w_pad.shape, lambda i: (0, 0)),
                  pl.BlockSpec((1, d), lambda i: (0, 0)),
                  pl.BlockSpec((1,) + gate.shape[1:], lambda i: (i // tpb, 0, 0)),
                  pl.BlockSpec((tm, d), lambda i: (i, 0))],
        out_specs=pl.BlockSpec((tm, d), lambda i: (i, 0)),
        out_shape=jax.ShapeDtypeStruct((nb * t, d), F32),
        compiler_params=_cparams(("arbitrary",)),
    )(a, w_pad, g.reshape(1, d), gate, x)


def _row_tile(m, cap):
    return m if m <= cap else cap


def _attn_in_weights(w_in):
    d = w_in.shape[0]
    kv = w_in[:, N_Q_COLS:GATE_COL0].reshape(d, N_KV_SLOTS + N_WIN_SLOTS, N_KV_HEADS, HEAD_DIM)
    kv = jnp.swapaxes(kv, 1, 2).reshape(d, N_KV_COLS)
    gates = w_in[:, GATE_COL0:].reshape(d, N_KV_HEADS, GROUP * 3)
    gates = jnp.pad(gates, ((0, 0), (0, 0), (0, LANES - GROUP * 3))).reshape(d, -1)
    return jnp.concatenate([w_in[:, :N_Q_COLS], kv, gates], axis=1).astype(BF16)


def _attn_out_weights(w_out):
    w = w_out.reshape(N_HEADS, HEAD_DIM, w_out.shape[1])
    return jnp.pad(w, ((0, 0), (LANES - HEAD_DIM, 0), (0, 0))).reshape(N_HEADS * LANES, -1).astype(BF16)


def _attn_in_kernel(x_ref, g_ref, sh_ref, sc_ref, w_ref, zs_ref, zw_ref, o_ref, kvs_ref, kvw_ref):
    del zs_ref, zw_ref
    h = _rms(x_ref[...], g_ref[...]) * (1.0 + sc_ref[0]) + sh_ref[0]
    r = jnp.dot(h.astype(BF16), w_ref[...], preferred_element_type=F32)
    o_ref[...] = r
    for k in range(N_KV_HEADS):
        c0 = N_Q_COLS + k * KV_HEAD_COLS
        kvs_ref[0, k] = r[:, c0 + LANES:c0 + 2 * LANES].astype(BF16)
        kvw_ref[0, k] = r[:, c0 + 2 * LANES:c0 + 3 * LANES].astype(BF16)


def _attn_in_prompt(x, g, shift, scale, w, nb, t):
    m, d = x.shape
    tm = FRONT_PAD
    assert t % tm == 0
    tpb = t // tm
    mod_spec = pl.BlockSpec((1,) + shift.shape[1:], lambda i: (i // tpb, 0, 0))
    pad_shape = (nb, N_KV_HEADS, FRONT_PAD + t, LANES)
    pad_spec = pl.BlockSpec((1, N_KV_HEADS, tm, LANES), lambda i: (i // tpb, 0, 1 + i % tpb, 0))
    zeros = jnp.zeros(pad_shape, BF16)
    return pl.pallas_call(
        _attn_in_kernel,
        grid=(m // tm,),
        in_specs=[pl.BlockSpec((tm, d), lambda i: (i, 0)),
                  pl.BlockSpec((1, d), lambda i: (0, 0)),
                  mod_spec, mod_spec,
                  pl.BlockSpec(w.shape, lambda i: (0, 0)),
                  pl.BlockSpec(memory_space=pl.ANY), pl.BlockSpec(memory_space=pl.ANY)],
        out_specs=[pl.BlockSpec((tm, NP_ATTN), lambda i: (i, 0)), pad_spec, pad_spec],
        out_shape=[jax.ShapeDtypeStruct((m, NP_ATTN), F32),
                   jax.ShapeDtypeStruct(pad_shape, BF16), jax.ShapeDtypeStruct(pad_shape, BF16)],
        input_output_aliases={5: 1, 6: 2},
        compiler_params=_cparams(("arbitrary",)),
    )(x, g.reshape(1, d), shift, scale, w, zeros, zeros)


def _unpermute_kv(proj_kv, lead):
    kv = proj_kv.reshape(lead + (N_KV_HEADS, N_KV_SLOTS + N_WIN_SLOTS, HEAD_DIM))
    return jnp.swapaxes(kv, -3, -2)


def _bias_tiles_kernel(t_ref, oh_ref, o_ref):
    oh = oh_ref[...]
    o_ref[...] = sum(jnp.dot(p, oh, preferred_element_type=F32) for p in _split_bf16(t_ref[...], 3))


def _bias_tiles(rel_bias, idxs):
    n = sum(ix.size for ix in idxs)
    d = np.concatenate([ix.reshape(-1) for ix in idxs])
    oh = np.zeros((LANES, n), np.float32)
    near = d < NEAR
    oh[_rel_bucket_np(d[near]), np.nonzero(near)[0]] = 1.0
    table = (rel_bias.astype(F32) - rel_bias[N_BUCKETS - 1].astype(F32)).T * LOG2E
    table = jnp.zeros((N_HEADS, LANES), F32).at[:, :N_BUCKETS].set(table)
    tn = 8192
    assert n % tn == 0
    out = pl.pallas_call(
        _bias_tiles_kernel,
        grid=(n // tn,),
        in_specs=[pl.BlockSpec((N_HEADS, LANES), lambda j: (0, 0)), pl.BlockSpec((LANES, tn), lambda j: (0, j))],
        out_specs=pl.BlockSpec((N_HEADS, tn), lambda j: (0, j)),
        out_shape=jax.ShapeDtypeStruct((N_HEADS, n), F32),
        compiler_params=_cparams(("arbitrary",)),
    )(table, jnp.asarray(oh, BF16))
    tiles, off = [], 0
    for ix in idxs:
        r, c = ix.shape
        tiles.append(out[:, off:off + r * c].reshape(N_KV_HEADS, GROUP * r, c))
        off += r * c
    return tiles


def _nsa_prompt_mixer(xp, g, shift, scale, w_in, cw, rel_bias, nb, t):
    proj, kvs, kvw = _attn_in_prompt(xp, g, shift, scale, w_in, nb, t)
    kv6 = _unpermute_kv(proj[:, N_Q_COLS:GATE_COL0], (nb, t))
    kck = _compress_prompt(proj, cw, nb, t)
    kvc = kck.astype(BF16)
    kck = jnp.where(jnp.arange(LANES) < HEAD_DIM, kck, 0.0)
    c = _nsa_consts(t)
    tiles = _bias_tiles(rel_bias, [c['idx0'], c['idx1'], c['idxc']])
    tb = jnp.stack(tiles[:2], axis=1)
    hi, lo = _split_bf16(tiles[2], 2)
    cbq = jnp.concatenate([jnp.zeros(hi.shape[:2] + (HEAD_DIM,), BF16), hi[..., :BAND_W], lo[..., :BAND_W]], axis=-1)
    o = _nsa_prompt(proj, kck, kvc, kvs, kvw, tb, cbq, c, nb, t)
    return o, kv6


SC_PAGES = 32
CG_PER_PAGE = N_KV_SLOTS * KV_ROW // LANES


def _cache_tiles(cache_kv):
    page = cache_kv.shape[2]
    assert page == LANES
    return jnp.transpose(cache_kv, (0, 1, 3, 4, 5, 2)).reshape(-1, page)


def _compress_sample_kernel(pt_ref, cache_ref, pos_ref, w1_ref, wbd_ref, w2bd_ref, o_ref, tbuf_ref, rows0_ref,
                            rows1_ref, sem_ref, p0_ref, p1_ref, *, n_pages, page, row_base):
    row_bufs = (rows0_ref, rows1_ref)
    step = pl.program_id(0) * N_CG + pl.program_id(1)
    nsteps = pl.num_programs(0) * N_CG
    chunk_pages = min(SC_PAGES, n_pages // 2)
    nch = n_pages // chunk_pages
    hpc = chunk_pages * page // CMP_STRIDE
    nh = nch * hpc

    last = nsteps * nch - 1

    def copies(chunk, slot):
        chunk = jnp.minimum(chunk, last)
        step_, ch = chunk // nch, chunk % nch
        b_, c_ = step_ // N_CG, step_ % N_CG
        out = []
        for p in range(chunk_pages):
            pg = pt_ref[b_, ch * chunk_pages + p]
            out.append(pltpu.make_async_copy(
                cache_ref.at[pl.ds(((row_base + pg) * CG_PER_PAGE + c_) * LANES, LANES), :],
                tbuf_ref.at[slot, p], sem_ref.at[slot]))
        return out

    def to_rows(slot):
        for p in range(chunk_pages):
            row_bufs[slot][p * page:(p + 1) * page, :] = tbuf_ref[slot, p].T

    @pl.when(step == 0)
    def _():
        for slot in range(2):
            for cp in copies(slot, slot):
                cp.start()
        for cp in copies(0, 0):
            cp.wait()
        to_rows(0)

    for ch in range(nch):
        slot = ch % 2
        chunk = step * nch + ch
        for cp in copies(chunk + 1, 1 - slot):
            cp.wait()
        for cp in copies(chunk + 2, slot):
            cp.start()
        to_rows(1 - slot)
        rows = row_bufs[slot]
        p0, p1 = _compress_parts(lambda s: rows[pl.ds(s, hpc, stride=CMP_STRIDE), :], wbd_ref, hpc)
        p0_ref[ch * hpc:(ch + 1) * hpc] = p0
        p1_ref[ch * hpc:(ch + 1) * hpc] = p1

    @pl.when(step == nsteps - 1)
    def _():
        for cp in copies(last, (nch - 1) % 2):
            cp.wait()

    o_ref[0, 0] = _compress_finish(p0_ref[...], p1_ref[...], pos_ref, w1_ref, w2bd_ref, nh)


def _compress_sample(page_table, cache_t, cw, page, row_base):
    nb, n_pages = page_table.shape
    chunk_pages = min(SC_PAGES, n_pages // 2)
    assert n_pages % (2 * chunk_pages) == 0
    nh = n_pages * page // CMP_STRIDE
    grid_spec = pltpu.PrefetchScalarGridSpec(
        num_scalar_prefetch=1,
        grid=(nb, N_CG),
        in_specs=[pl.BlockSpec(memory_space=pl.ANY)] + _compress_specs(cw, lambda b, c, pt: c // CG_PER_SLOT),
        out_specs=pl.BlockSpec((1, 1, nh, LANES), lambda b, c, pt: (b, c, 0, 0)),
        scratch_shapes=[pltpu.VMEM((2, chunk_pages, LANES, page), F32),
                        pltpu.VMEM((chunk_pages * page, LANES), F32),
                        pltpu.VMEM((chunk_pages * page, LANES), F32),
                        pltpu.SemaphoreType.DMA((2,)),
                        pltpu.VMEM((nh, CG_HIDDEN), F32),
                        pltpu.VMEM((nh, CG_HIDDEN), F32)])
    return pl.pallas_call(
        functools.partial(_compress_sample_kernel, n_pages=n_pages, page=page, row_base=row_base),
        grid_spec=grid_spec,
        out_shape=jax.ShapeDtypeStruct((nb, N_CG, nh, LANES), F32),
        compiler_params=_cparams(("arbitrary", "arbitrary")),
    )(page_table, cache_t, *cw)


def _softmax_with_new_key(s, s_new):
    m = jnp.maximum(jnp.max(s, axis=-1, keepdims=True), s_new)
    e = jnp.exp(s - m)
    e_new = jnp.exp(s_new - m)
    inv = 1.0 / jnp.maximum(jnp.sum(e, axis=-1, keepdims=True) + e_new, 1e-30)
    return e * inv, e_new * inv


def _nsa_sample_kernel(pt_ref, qs_ref, q_ref, gl_ref, new_ref, ncol_ref, kvc_ref, cache_ref, win_ref, cbs_ref,
                       sbz_ref, wb_ref, b0_ref, as_ref, o_ref, wout_ref, gk_ref, gv_ref, sem_ref, idx_ref,
                       *, past, page, row_base, n_pick):
    b = pl.program_id(0)
    blk_per_page = page // SEL_BLOCK
    nbs = past // SEL_BLOCK
    ncs = kvc_ref.shape[2]
    row = lax.broadcasted_iota(jnp.int32, (SUB, 1), 0)

    cmp_ok = lax.broadcasted_iota(jnp.int32, (SUB, ncs), 1) * CMP_STRIDE + (CMP_BLOCK - 1) <= past
    imp = jnp.zeros((SUB, ncs), F32)
    o_cmp = []
    for k in range(N_KV_HEADS):
        kc = kvc_ref[0, k]
        s = lax.dot_general(qs_ref[0, k], kc, _NT, preferred_element_type=F32) + cbs_ref[k]
        m = jnp.max(jnp.where(cmp_ok, s, NEG), axis=-1, keepdims=True)
        e = jnp.where(cmp_ok, jnp.exp(s - m), 0.0)
        p = e / jnp.maximum(jnp.sum(e, axis=-1, keepdims=True), 1e-30)
        o_cmp.append(jnp.dot(p.astype(BF16), kc, preferred_element_type=F32))
        imp_k = p[0:1]
        for g in range(1, GROUP):
            imp_k = imp_k + p[g:g + 1]
        imp = imp + jnp.where(row == k, imp_k, 0.0)
    bs = sum(jnp.dot(part, as_ref[...], preferred_element_type=F32) for part in _split_bf16(imp, 3))

    lane_b = lax.broadcasted_iota(jnp.int32, (SUB, nbs), 1)
    lane_f = lane_b.astype(F32)
    score = bs + jnp.where((lane_b == 0) | (lane_b == nbs - 1), FORCE_BONUS, 0.0)

    def gathers(k, r):
        j = idx_ref[k * n_pick + r]
        pg = pt_ref[b, j // blk_per_page]
        tile0 = (row_base + pg) * N_KV_SLOTS
        return [pltpu.make_async_copy(
            cache_ref.at[pl.ds(((tile0 + slot) * N_KV_HEADS + k) * HEAD_DIM, HEAD_DIM), :],
            buf.at[k * n_pick + r], sem_ref.at[0]) for slot, buf in ((2, gk_ref), (3, gv_ref))]

    for r in range(n_pick):
        m = jnp.max(score, axis=-1, keepdims=True)
        pick = jnp.min(jnp.where(score == m, lane_f, 1e9), axis=-1, keepdims=True)
        score = jnp.where(lane_f == pick, -jnp.inf, score)
        for k in range(N_KV_HEADS):
            idx_ref[k * n_pick + r] = jnp.sum(jnp.where(row == k, pick, 0.0)).astype(jnp.int32)
            for cp in gathers(k, r):
                cp.start()

    def new_row(k, j):
        return new_ref[0, k][j:j + 1].astype(BF16).astype(F32)

    tok = lax.broadcasted_iota(jnp.int32, (1, WINDOW), 1)
    o_win = []
    for k in range(N_KV_HEADS):
        for slot in range(N_WIN_SLOTS):
            shifted = pltpu.roll(win_ref[0, slot, k], WINDOW - 1, axis=1)
            wout_ref[0, slot, k] = jnp.where(tok == WINDOW - 1, ncol_ref[0, slot, k], shifted)
        q = q_ref[0, k]
        s = jnp.dot(q, win_ref[0, 0, k].astype(BF16), preferred_element_type=F32) + wb_ref[k]
        s = jnp.where(tok >= 1, s, NEG)
        s_new = jnp.sum(q.astype(F32) * new_row(k, 2), axis=-1, keepdims=True) + b0_ref[k][:, 0:1]
        p, p_new = _softmax_with_new_key(s, s_new)
        o_win.append(lax.dot_general(p.astype(BF16), win_ref[0, 1, k].astype(BF16), _NT,
                                     preferred_element_type=F32) + p_new * new_row(k, 3))

    for k in range(N_KV_HEADS):
        for r in range(n_pick):
            for cp in gathers(k, r):
                cp.wait()

    upper = lax.broadcasted_iota(jnp.int32, (1, LANES), 1) // SEL_BLOCK
    for k in range(N_KV_HEADS):
        q = q_ref[0, k]
        tiles = []
        for r in range(n_pick):
            j = idx_ref[k * n_pick + r]
            near = j - (nbs - NEAR // SEL_BLOCK)
            s = jnp.dot(q, gk_ref[k * n_pick + r].astype(BF16), preferred_element_type=F32)
            s = s + sbz_ref[k, jnp.where(near >= 0, near, NEAR // SEL_BLOCK)]
            tiles.append(jnp.where(upper == j % blk_per_page, s, NEG))
        s_new = jnp.sum(q.astype(F32) * new_row(k, 0), axis=-1, keepdims=True) + b0_ref[k][:, 0:1]
        p, p_new = _softmax_with_new_key(jnp.concatenate(tiles, axis=1), s_new)
        o_sel = p_new * new_row(k, 1)
        for r in range(n_pick):
            o_sel = o_sel + lax.dot_general(p[:, r * LANES:(r + 1) * LANES].astype(BF16),
                                            gv_ref[k * n_pick + r].astype(BF16), _NT, preferred_element_type=F32)
        g_cmp, g_sel, g_win = (jax.nn.sigmoid(gl_ref[0, k, j])[:, :HEAD_DIM] for j in range(3))
        o_ref[0, k] = g_cmp * o_cmp[k][:, HEAD_DIM:] + g_sel * o_sel + g_win * o_win[k]


def _bias_rows(bvz, idx):
    t = jnp.take(bvz, jnp.asarray(idx, jnp.int32), axis=0).reshape(len(idx), N_KV_HEADS, GROUP)
    t = jnp.transpose(t, (1, 2, 0))
    return jnp.pad(t, ((0, 0), (0, SUB - GROUP), (0, 0)))


def _nsa_sample(proj_s, kvc, page_table, cache_t, win_t, bvz, past, page, row_base):
    nb = proj_s.shape[0]
    n_pick = N_SELECT - 1
    nbs = past // SEL_BLOCK
    ncs = past // CMP_STRIDE
    n_near = NEAR // SEL_BLOCK
    blk_per_page = page // SEL_BLOCK
    assert past % page == 0 and page == LANES and nbs > n_pick + n_near and (nbs - n_near) % blk_per_page == 0
    assert win_t.shape[-1] == WINDOW < past
    q = (proj_s[:, :N_Q_COLS] * HEAD_DIM ** -0.5).astype(BF16).reshape(nb, N_KV_HEADS, GROUP, HEAD_DIM)
    q = jnp.pad(q, ((0, 0), (0, 0), (0, SUB - GROUP), (0, 0)))
    qs = jnp.pad(q, ((0, 0), (0, 0), (0, 0), (0, LANES - HEAD_DIM)))
    gl = proj_s[:, GATE_COL0:].reshape(nb, N_KV_HEADS, LANES)[:, :, :GROUP * 3].reshape(nb, N_KV_HEADS, GROUP, 3)
    gl = jnp.pad(jnp.transpose(gl, (0, 1, 3, 2)), ((0, 0), (0, 0), (0, 0), (0, SUB - GROUP)))
    gl = jnp.broadcast_to(gl[..., None], gl.shape + (LANES,))
    new = _unpermute_kv(proj_s[:, N_Q_COLS:GATE_COL0], (nb,))[:, 2:]
    new_rows = jnp.transpose(new, (0, 2, 1, 3))
    new_cols = new[:, 2:].reshape(nb, N_WIN_SLOTS, N_KV_HEADS, HEAD_DIM, 1)

    dc = past - (np.arange(ncs) * CMP_STRIDE + CMP_BLOCK - 1)
    cbs = _bias_rows(bvz, np.where((dc >= 0) & (dc < NEAR), dc, NEAR))
    lane = np.arange(LANES)
    sb = []
    for jj in range(n_near + 1):
        d = NEAR - SEL_BLOCK * jj - lane % SEL_BLOCK
        ok = (lane // SEL_BLOCK == jj % blk_per_page) & (jj < n_near) & (d < NEAR)
        sb.append(_bias_rows(bvz, np.where(ok, d, NEAR)))
    sbz = jnp.stack(sb, axis=1)
    dw = WINDOW - np.arange(WINDOW)
    wb = _bias_rows(bvz, np.where(dw < NEAR, dw, NEAR))
    b0 = _bias_rows(bvz, np.zeros(LANES, np.int64))
    a_s = np.zeros((ncs, nbs), np.float32)
    for j in range(nbs):
        for n in range(SEL_RATIO * j - (CMP_RATIO - 1), SEL_RATIO * (j + 1)):
            if 0 <= n < ncs - CMP_RATIO + 1:
                a_s[n, j] = 1.0
    a_s = jnp.asarray(a_s, BF16)

    per_b = lambda a: pl.BlockSpec((1,) + a.shape[1:], lambda b, pt: (b,) + (0,) * (a.ndim - 1))
    full = lambda a: pl.BlockSpec(a.shape, lambda b, pt: (0,) * a.ndim)
    grid_spec = pltpu.PrefetchScalarGridSpec(
        num_scalar_prefetch=1,
        grid=(nb,),
        in_specs=[per_b(qs), per_b(q), per_b(gl), per_b(new_rows), per_b(new_cols), per_b(kvc),
                  pl.BlockSpec(memory_space=pl.ANY), per_b(win_t),
                  full(cbs), full(sbz), full(wb), full(b0), full(a_s)],
        out_specs=[pl.BlockSpec((1, N_KV_HEADS, SUB, HEAD_DIM), lambda b, pt: (b, 0, 0, 0)), per_b(win_t)],
        scratch_shapes=[pltpu.VMEM((N_KV_HEADS * n_pick, HEAD_DIM, page), F32),
                        pltpu.VMEM((N_KV_HEADS * n_pick, HEAD_DIM, page), F32),
                        pltpu.SemaphoreType.DMA((1,)),
                        pltpu.SMEM((N_KV_HEADS * n_pick,), jnp.int32)])
    o, win_out = pl.pallas_call(
        functools.partial(_nsa_sample_kernel, past=past, page=page, row_base=row_base, n_pick=n_pick),
        grid_spec=grid_spec,
        out_shape=[jax.ShapeDtypeStruct((nb, N_KV_HEADS, SUB, HEAD_DIM), F32),
                   jax.ShapeDtypeStruct(win_t.shape, F32)],
        compiler_params=_cparams(("arbitrary",)),
    )(page_table, qs, q, gl, new_rows, new_cols, kvc, cache_t, win_t, cbs, sbz, wb, b0, a_s)
    return o[:, :, :GROUP].reshape(nb, N_Q_COLS), win_out


NP_SSM = -(-(SSM_CONV_DIM + SSM_D_INNER + SSM_HEADS) // LANES) * LANES
SSM_COL_TILE = NP_SSM // 7
assert SSM_COL_TILE % LANES == 0 and SSM_COL_TILE * 7 == NP_SSM
N_BC = SSM_GROUPS * SSM_STATE
HEADS_PER_GROUP = SSM_HEADS // SSM_GROUPS
PAIR = LANES // SSM_HEAD_DIM
N_PAIRS = SSM_HEADS // PAIR
CONV_PAD = 8


def _softplus(x):
    return jnp.maximum(x, 0.0) + jnp.log1p(jnp.exp(-jnp.abs(x)))


def _cumsum_rows(x):
    n = x.shape[0]
    row = lax.broadcasted_iota(jnp.int32, x.shape, 0)
    s = 1
    while s < n:
        x = x + jnp.where(row >= s, pltpu.roll(x, s, axis=0), 0.0)
        s *= 2
    return x


def _ssd_kernel(xbc_ref, z_ref, dt_ref, h0_ref, cinit_ref, cw_ref, cb_ref, dtb_ref, alog_ref, dsk_ref, ng_ref,
                y_ref, hout_ref, xs_ref, act_ref, ybuf_ref, h_ref, *, nc):
    c = pl.program_id(1)
    q = SSM_CHUNK

    @pl.when(c == 0)
    def _():
        xs_ref[0:CONV_PAD] = cinit_ref[0]
        h_ref[...] = h0_ref[0]

    xs_ref[CONV_PAD:CONV_PAD + q] = xbc_ref[...]
    conv = cb_ref[...] + cw_ref[0:1] * xs_ref[CONV_PAD - 3:CONV_PAD - 3 + q]
    for k in range(1, SSM_CONV):
        conv = conv + cw_ref[k:k + 1] * xs_ref[CONV_PAD - 3 + k:CONV_PAD - 3 + k + q]
    xs_ref[0:CONV_PAD] = xs_ref[q:q + CONV_PAD]
    act_ref[...] = _silu(conv)

    dt = _softplus(dt_ref[...] + dtb_ref[...])
    acum = _cumsum_rows(dt * (-jnp.exp(alog_ref[...])))
    acum_t = acum.T
    dt_t = dt.T
    last = acum[q - 1:q, :]
    ea = jnp.exp(acum)
    te = jnp.exp(last - acum) * dt
    cd = jnp.exp(last)
    ii = lax.broadcasted_iota(jnp.int32, (q, q), 0)
    jj = lax.broadcasted_iota(jnp.int32, (q, q), 1)
    tri = ii >= jj
    low = jj < SSM_HEAD_DIM
    low_rows = ii < SSM_HEAD_DIM

    def col(a, h):
        return a[:, h:h + 1]

    for g in range(SSM_GROUPS):
        bg = act_ref[:, SSM_D_INNER + g * SSM_STATE:SSM_D_INNER + (g + 1) * SSM_STATE].astype(BF16)
        cg = act_ref[:, SSM_D_INNER + N_BC + g * SSM_STATE:SSM_D_INNER + N_BC + (g + 1) * SSM_STATE].astype(BF16)
        cbg = lax.dot_general(cg, bg, _NT, preferred_element_type=F32)
        for pr in range(HEADS_PER_GROUP // PAIR):
            k = g * (HEADS_PER_GROUP // PAIR) + pr
            ha, hb = PAIR * k, PAIR * k + 1
            xp = act_ref[:, k * LANES:(k + 1) * LANES]
            xpb = xp.astype(BF16)
            ys = []
            for h in (ha, hb):
                decay = jnp.exp(jnp.where(tri, col(acum, h) - acum_t[h:h + 1, :], NEG))
                w = cbg * decay * dt_t[h:h + 1, :]
                ys.append(jnp.dot(w.astype(BF16), xpb, preferred_element_type=F32))
            y = jnp.where(low, ys[0], ys[1])
            xs_pair = xp * jnp.where(low, col(te, ha), col(te, hb))
            st = jnp.dot(xs_pair.T.astype(BF16), bg, preferred_element_type=F32)
            hprev = h_ref[k]
            yoff = lax.dot_general(cg, hprev.astype(BF16), _NT, preferred_element_type=F32)
            y = y + yoff * jnp.where(low, col(ea, ha), col(ea, hb)) + dsk_ref[:, k * LANES:(k + 1) * LANES] * xp
            h_ref[k] = hprev * jnp.where(low_rows, cd[:, ha:ha + 1], cd[:, hb:hb + 1]) + st
            ybuf_ref[:, k * LANES:(k + 1) * LANES] = y

    yz = ybuf_ref[...] * _silu(z_ref[...])
    gw = SSM_D_INNER // SSM_GROUPS
    outs = []
    for g in range(SSM_GROUPS):
        seg = yz[:, g * gw:(g + 1) * gw]
        outs.append(seg * lax.rsqrt(jnp.mean(seg * seg, axis=-1, keepdims=True) + EPS))
    y_ref[...] = (jnp.concatenate(outs, axis=1) * ng_ref[...]).astype(y_ref.dtype)

    @pl.when(c == nc - 1)
    def _():
        hout_ref[0] = h_ref[...]


def _ssd(proj, h0, cinit, sw, nb, t):
    q = SSM_CHUNK
    nc = t // q
    vec = lambda a: pl.BlockSpec(a.shape, lambda b, c: (0, 0))
    return pl.pallas_call(
        functools.partial(_ssd_kernel, nc=nc),
        grid=(nb, nc),
        in_specs=[pl.BlockSpec((q, SSM_CONV_DIM), lambda b, c: (b * nc + c, 0)),
                  pl.BlockSpec((q, SSM_D_INNER), lambda b, c: (b * nc + c, SSM_CONV_DIM // SSM_D_INNER)),
                  pl.BlockSpec((q, LANES), lambda b, c: (b * nc + c, (SSM_CONV_DIM + SSM_D_INNER) // LANES)),
                  pl.BlockSpec((1,) + h0.shape[1:], lambda b, c: (b, 0, 0, 0)),
                  pl.BlockSpec((1,) + cinit.shape[1:], lambda b, c: (b, 0, 0)),
                  vec(sw['conv_w']), vec(sw['conv_b']), vec(sw['dt_bias']), vec(sw['a_log']), vec(sw['d_lane']),
                  vec(sw['norm_g'])],
        out_specs=[pl.BlockSpec((q, SSM_D_INNER), lambda b, c: (b * nc + c, 0)),
                   pl.BlockSpec((1,) + h0.shape[1:], lambda b, c: (b, 0, 0, 0))],
        out_shape=[jax.ShapeDtypeStruct((nb * t, SSM_D_INNER), BF16),
                   jax.ShapeDtypeStruct(h0.shape, F32)],
        scratch_shapes=[pltpu.VMEM((CONV_PAD + q, SSM_CONV_DIM), F32),
                        pltpu.VMEM((q, SSM_CONV_DIM), F32),
                        pltpu.VMEM((q, SSM_D_INNER), F32),
                        pltpu.VMEM(h0.shape[1:], F32)],
        compiler_params=_cparams(("arbitrary", "arbitrary")),
    )(proj, proj, proj, h0, cinit, sw['conv_w'], sw['conv_b'], sw['dt_bias'], sw['a_log'], sw['d_lane'],
      sw['norm_g'])


def _ssm_sample_step(proj_s, state_ssm, state_conv, sw):
    nb = proj_s.shape[0]
    q = SSM_CHUNK
    xbc_new = proj_s[:, :SSM_CONV_DIM]
    rows = jnp.zeros((nb, q, NP_SSM), F32)
    rows = rows.at[:, :, SSM_CONV_DIM + SSM_D_INNER:].set(NEG)
    rows = rows.at[:, q - SSM_CONV:q - 1, :SSM_CONV_DIM].set(state_conv.astype(F32))
    rows = rows.at[:, q - 1].set(proj_s)
    h0 = state_ssm.astype(F32).reshape(nb, N_PAIRS, LANES, SSM_STATE)
    cinit = jnp.zeros((nb, CONV_PAD, SSM_CONV_DIM), F32)
    yn, hfin = _ssd(rows.reshape(nb * q, NP_SSM), h0, cinit, sw, nb, q)
    conv_new = jnp.concatenate([state_conv[:, 1:].astype(F32), xbc_new[:, None]], axis=1)
    return yn.reshape(nb, q, SSM_D_INNER)[:, q - 1], hfin.reshape(state_ssm.shape), conv_new


def _ssm_weights(w_in, conv_w, conv_b, dt_bias, a_log, d_skip, norm_g):
    z_w = w_in[:, :SSM_D_INNER]
    xbc_w = w_in[:, SSM_D_INNER:SSM_D_INNER + SSM_CONV_DIM]
    dt_w = w_in[:, SSM_D_INNER + SSM_CONV_DIM:]
    pad = NP_SSM - w_in.shape[1]
    w = jnp.concatenate([xbc_w, z_w, dt_w, jnp.zeros((w_in.shape[0], pad), w_in.dtype)], axis=1).astype(BF16)
    lane_pad = lambda v: jnp.zeros((1, LANES), F32).at[0, :SSM_HEADS].set(v.astype(F32))
    return dict(w_in=w, conv_w=conv_w.astype(F32), conv_b=conv_b.astype(F32).reshape(1, -1),
                dt_bias=lane_pad(dt_bias), a_log=lane_pad(a_log),
                d_lane=jnp.repeat(d_skip.astype(F32), SSM_HEAD_DIM).reshape(1, -1),
                norm_g=norm_g.astype(F32).reshape(1, -1))


ROW_TILE = 512
MLP_ROW_TILE = 1024
FF_TILE = 1024


def kernel(x_prompt, x_sample, cache_kv, cache_win, state_ssm, state_conv, page_table, c_prompt, c_sample, rel_bias,
           ada_w, ada_b, norm_g, mlp_w1, mlp_w2, attn_w_in, attn_w_out, cmp_pos, cmp_w1, cmp_w2, ssm_w_in,
           ssm_conv_w, ssm_conv_b, ssm_dt_bias, ssm_a_log, ssm_d, ssm_norm_g, ssm_w_out):
    nb, t, d = x_prompt.shape
    db = x_sample.shape[0]
    assert x_sample.shape[1] == 1 and t % SSM_CHUNK == 0 and t % Q_BLOCK == 0
    n_pool, page = cache_kv.shape[1], cache_kv.shape[2]
    past = page_table.shape[1] * page
    depth = ada_w.shape[0]
    tm = _row_tile(t, ROW_TILE)

    xp = x_prompt.reshape(nb * t, d).astype(F32)
    xs = x_sample.reshape(db, d).astype(F32)
    c_all = jnp.concatenate([c_prompt, c_sample], axis=0).astype(F32)
    c_all = jnp.pad(c_all, ((0, (-c_all.shape[0]) % SUB), (0, 0)))
    bvz = _bias_vector(rel_bias)
    cache_t = _cache_tiles(cache_kv)
    win_t = jnp.transpose(cache_win, (0, 1, 3, 4, 5, 2))

    kv_p, win_p, ssm_p, conv_p, kv_s, win_s, ssm_s, conv_s = ([] for _ in range(8))
    for i in range(depth):
        mod = _ada(c_all, ada_w[i], ada_b[i])
        mp = [mod[:nb, j * d:(j + 1) * d].reshape(nb, 1, d) for j in range(N_MOD)]
        ms = [mod[nb:nb + db, j * d:(j + 1) * d].reshape(1, db, d) for j in range(N_MOD)]
        g = norm_g[i].astype(F32)
        if i % 2 == 0:
            a = i // 2
            w_in = _attn_in_weights(attn_w_in[a])
            w_out = attn_w_out[a].astype(BF16)
            cw = _compress_weights(cmp_pos[a], cmp_w1[a], cmp_w2[a], [(0, 1)])
            o_p, kv6 = _nsa_prompt_mixer(xp, g[0], mp[0], mp[1], w_in, cw, rel_bias, nb, t)
            kv_p.append(kv6[:, :, :N_KV_SLOTS])
            win_p.append(kv6[:, -min(WINDOW, t):, N_KV_SLOTS:])
            xp = _heads_out(o_p, _attn_out_weights(attn_w_out[a]), g[1], mp[2], xp, tm)

            proj_s = _nm_matmul(xs, g[0], ms[0], ms[1], w_in, db, NP_ATTN)
            cw_s = _compress_weights(cmp_pos[a], cmp_w1[a], cmp_w2[a], [(0, 0), (1, 1)])
            kvc_s = _pack_cmp(_compress_sample(page_table, cache_t, cw_s, page, a * n_pool))
            o_s, win_new = _nsa_sample(proj_s, kvc_s, page_table, cache_t, win_t[a], bvz, past, page, a * n_pool)
            kv_s.append(_unpermute_kv(proj_s[:, N_Q_COLS:GATE_COL0], (db, 1))[:, :, :N_KV_SLOTS])
            win_s.append(jnp.transpose(win_new, (0, 4, 1, 2, 3)))
            xs = _mm_norm_res(o_s, w_out, g[1], ms[2], xs, db)
        else:
            m = i // 2
            sw = _ssm_weights(ssm_w_in[m], ssm_conv_w[m], ssm_conv_b[m], ssm_dt_bias[m], ssm_a_log[m], ssm_d[m],
                              ssm_norm_g[m])
            w_out = ssm_w_out[m].astype(BF16)
            proj = _nm_matmul(xp, g[0], mp[0], mp[1], sw['w_in'], tm, SSM_COL_TILE)
            h0 = jnp.zeros((nb, N_PAIRS, LANES, SSM_STATE), F32)
            cinit = jnp.zeros((nb, CONV_PAD, SSM_CONV_DIM), F32)
            yn, hfin = _ssd(proj, h0, cinit, sw, nb, t)
            ssm_p.append(hfin.reshape(nb, SSM_HEADS, SSM_HEAD_DIM, SSM_STATE).astype(state_ssm.dtype))
            conv_p.append(proj.reshape(nb, t, NP_SSM)[:, t - (SSM_CONV - 1):, :SSM_CONV_DIM])
            xp = _mm_norm_res(yn, w_out, g[1], mp[2], xp, tm)

            proj_s = _nm_matmul(xs, g[0], ms[0], ms[1], sw['w_in'], db, SSM_COL_TILE)
            yn_s, h_s, conv_new = _ssm_sample_step(proj_s, state_ssm[m], state_conv[m], sw)
            ssm_s.append(h_s.astype(state_ssm.dtype))
            conv_s.append(conv_new)
            xs = _mm_norm_res(yn_s, w_out, g[1], ms[2], xs, db)
        w1 = mlp_w1[i].astype(BF16)
        w2 = mlp_w2[i].astype(BF16)
        xp = _mlp(xp, g[2], mp[3], mp[4], w1, w2, g[3], mp[5], _row_tile(t, MLP_ROW_TILE), FF_TILE)
        xs = _mlp(xs, g[2], ms[3], ms[4], w1, w2, g[3], ms[5], db, FF_TILE)
    return (xp.reshape(nb, t, d), xs.reshape(db, 1, d), jnp.stack(kv_p), jnp.stack(win_p), jnp.stack(ssm_p),
            jnp.stack(conv_p), jnp.stack(kv_s), jnp.stack(win_s), jnp.stack(ssm_s), jnp.stack(conv_s))
```

```python
import functools
import math

import numpy as np
import jax
import jax.numpy as jnp
from jax import lax
from jax.experimental import pallas as pl
from jax.experimental.pallas import tpu as pltpu

F32 = jnp.float32
BF16 = jnp.bfloat16

D_MODEL = 1024
N_HEADS = 16
HEAD_DIM = 64
N_KV_HEADS = 4
GROUP = N_HEADS // N_KV_HEADS
CMP_BLOCK = 32
CMP_STRIDE = 16
CMP_RATIO = CMP_BLOCK // CMP_STRIDE
CMP_HIDDEN = 2 * HEAD_DIM
SEL_BLOCK = 64
SEL_RATIO = SEL_BLOCK // CMP_STRIDE
N_SELECT = 16
WINDOW = 512
FORCE_BONUS = 1e4
Q_BLOCK = 128
N_KV_SLOTS = 4
N_WIN_SLOTS = 2
N_Q_COLS = N_HEADS * HEAD_DIM
N_KV_COLS = (N_KV_SLOTS + N_WIN_SLOTS) * N_KV_HEADS * HEAD_DIM
N_GATE_COLS = 3 * N_HEADS
KV_ROW = N_KV_HEADS * HEAD_DIM
KV_HEAD_COLS = (N_KV_SLOTS + N_WIN_SLOTS) * HEAD_DIM
N_BUCKETS = 32
MAX_DISTANCE = 128
SSM_D_INNER = 2 * D_MODEL
SSM_HEAD_DIM = 64
SSM_HEADS = SSM_D_INNER // SSM_HEAD_DIM
SSM_GROUPS = 8
SSM_STATE = 128
SSM_CONV = 4
SSM_CONV_DIM = SSM_D_INNER + 2 * SSM_GROUPS * SSM_STATE
SSM_CHUNK = 128
D_FF = 4 * D_MODEL
N_MOD = 6
EPS = 1e-6

LANES = 128
NEG = -1e30
LOG2E = math.log2(math.e)
VMEM_LIMIT = 56 * 1024 * 1024
NEAR = 2 * LANES

_NT = (((1,), (1,)), ((), ()))


def _cparams(sem):
    return pltpu.CompilerParams(dimension_semantics=sem, vmem_limit_bytes=VMEM_LIMIT)


def _rms(x, g):
    return x * lax.rsqrt(jnp.mean(x * x, axis=-1, keepdims=True) + EPS) * g


def _silu(x):
    return x * jax.nn.sigmoid(x)


def _split_bf16(x, n):
    parts = []
    for _ in range(n - 1):
        p = x.astype(BF16)
        parts.append(p)
        x = x - p.astype(F32)
    parts.append(x.astype(BF16))
    return parts


def _ada_kernel(c_ref, w_ref, b_ref, o_ref):
    s = _silu(c_ref[...]).astype(BF16)
    o_ref[...] = jnp.dot(s, w_ref[...].astype(BF16), preferred_element_type=F32) + b_ref[...]


def _ada(c, w, b, tn=1024):
    m, d = c.shape
    n = w.shape[1]
    return pl.pallas_call(
        _ada_kernel,
        grid=(n // tn,),
        in_specs=[pl.BlockSpec((m, d), lambda j: (0, 0)),
                  pl.BlockSpec((d, tn), lambda j: (0, j)),
                  pl.BlockSpec((1, tn), lambda j: (0, j))],
        out_specs=pl.BlockSpec((m, tn), lambda j: (0, j)),
        out_shape=jax.ShapeDtypeStruct((m, n), F32),
        compiler_params=_cparams(("arbitrary",)),
    )(c, w, b.reshape(1, n))


def _nm_matmul_kernel(x_ref, g_ref, sh_ref, sc_ref, w_ref, o_ref, h_ref):
    @pl.when(pl.program_id(1) == 0)
    def _():
        h = _rms(x_ref[...], g_ref[...]) * (1.0 + sc_ref[0]) + sh_ref[0]
        h_ref[...] = h.astype(BF16)

    o_ref[...] = jnp.dot(h_ref[...], w_ref[...], preferred_element_type=F32)


def _nm_matmul(x, g, shift, scale, w, tm, tn):
    m, d = x.shape
    n = w.shape[1]
    nb = shift.shape[0]
    tpb = m // nb // tm
    mod_spec = pl.BlockSpec((1,) + shift.shape[1:], lambda i, j: (i // tpb, 0, 0))
    return pl.pallas_call(
        _nm_matmul_kernel,
        grid=(m // tm, n // tn),
        in_specs=[pl.BlockSpec((tm, d), lambda i, j: (i, 0)),
                  pl.BlockSpec((1, d), lambda i, j: (0, 0)),
                  mod_spec, mod_spec,
                  pl.BlockSpec((d, tn), lambda i, j: (0, j))],
        out_specs=pl.BlockSpec((tm, tn), lambda i, j: (i, j)),
        out_shape=jax.ShapeDtypeStruct((m, n), F32),
        scratch_shapes=[pltpu.VMEM((tm, d), BF16)],
        compiler_params=_cparams(("arbitrary", "arbitrary")),
    )(x, g.reshape(1, d), shift, scale, w)


def _mm_norm_res_kernel(a_ref, w_ref, g_ref, gate_ref, x_ref, o_ref):
    y = jnp.dot(a_ref[...].astype(BF16), w_ref[...], preferred_element_type=F32)
    o_ref[...] = x_ref[...] + gate_ref[0] * _rms(y, g_ref[...])


def _mm_norm_res(a, w, g, gate, x, tm):
    m, k = a.shape
    d = w.shape[1]
    nb = gate.shape[0]
    tpb = m // nb // tm
    return pl.pallas_call(
        _mm_norm_res_kernel,
        grid=(m // tm,),
        in_specs=[pl.BlockSpec((tm, k), lambda i: (i, 0)),
                  pl.BlockSpec((k, d), lambda i: (0, 0)),
                  pl.BlockSpec((1, d), lambda i: (0, 0)),
                  pl.BlockSpec((1,) + gate.shape[1:], lambda i: (i // tpb, 0, 0)),
                  pl.BlockSpec((tm, d), lambda i: (i, 0))],
        out_specs=pl.BlockSpec((tm, d), lambda i: (i, 0)),
        out_shape=jax.ShapeDtypeStruct((m, d), F32),
        compiler_params=_cparams(("arbitrary",)),
    )(a, w, g.reshape(1, d), gate, x)


def _mlp_kernel(x_ref, g1_ref, sh_ref, sc_ref, w1_ref, w2_ref, g2_ref, gate_ref, o_ref, h_ref, acc_ref):
    c = pl.program_id(1)

    @pl.when(c == 0)
    def _():
        h = _rms(x_ref[...], g1_ref[...]) * (1.0 + sc_ref[0]) + sh_ref[0]
        h_ref[...] = h.astype(BF16)
        acc_ref[...] = jnp.zeros_like(acc_ref)

    a = jnp.dot(h_ref[...], w1_ref[...], preferred_element_type=F32)
    a = jnp.square(jnp.maximum(a, 0.0)).astype(BF16)
    acc_ref[...] += jnp.dot(a, w2_ref[...], preferred_element_type=F32)

    @pl.when(c == pl.num_programs(1) - 1)
    def _():
        o_ref[...] = x_ref[...] + gate_ref[0] * _rms(acc_ref[...], g2_ref[...])


def _mlp(x, g1, shift, scale, w1, w2, g2, gate, tm, tf):
    m, d = x.shape
    f = w1.shape[1]
    nb = shift.shape[0]
    tpb = m // nb // tm
    mod_spec = pl.BlockSpec((1,) + shift.shape[1:], lambda i, c: (i // tpb, 0, 0))
    vec_spec = pl.BlockSpec((1, d), lambda i, c: (0, 0))
    return pl.pallas_call(
        _mlp_kernel,
        grid=(m // tm, f // tf),
        in_specs=[pl.BlockSpec((tm, d), lambda i, c: (i, 0)), vec_spec, mod_spec, mod_spec,
                  pl.BlockSpec((d, tf), lambda i, c: (0, c)),
                  pl.BlockSpec((tf, d), lambda i, c: (c, 0)),
                  vec_spec, mod_spec],
        out_specs=pl.BlockSpec((tm, d), lambda i, c: (i, 0)),
        out_shape=jax.ShapeDtypeStruct((m, d), F32),
        scratch_shapes=[pltpu.VMEM((tm, d), BF16), pltpu.VMEM((tm, d), F32)],
        compiler_params=_cparams(("arbitrary", "arbitrary")),
    )(x, g1.reshape(1, d), shift, scale, w1, w2, g2.reshape(1, d), gate)


def _bias_kernel(oh_ref, t_ref, o_ref):
    t = t_ref[...]
    t = t - t[N_BUCKETS - 1:N_BUCKETS, :]
    oh = oh_ref[...]
    o_ref[...] = sum(jnp.dot(oh, p, preferred_element_type=F32) for p in _split_bf16(t, 3))


def _rel_bucket_np(dist):
    n = np.maximum(dist, 0)
    exact = N_BUCKETS // 2
    nf = np.maximum(n, 1).astype(np.float32)
    large = exact + (np.log(nf / exact) / math.log(MAX_DISTANCE / exact) * (N_BUCKETS - exact)).astype(np.int32)
    return np.where(n < exact, n, np.minimum(large, N_BUCKETS - 1))


def _bias_vector(rel_bias):
    assert _rel_bucket_np(np.array([NEAR - 1]))[0] == N_BUCKETS - 1 == _rel_bucket_np(np.array([MAX_DISTANCE]))[0]
    oh = np.zeros((NEAR + 8, LANES), np.float32)
    oh[np.arange(NEAR), _rel_bucket_np(np.arange(NEAR))] = 1.0
    oh[NEAR:, N_BUCKETS - 1] = 1.0
    table = jnp.zeros((LANES, LANES), F32).at[:N_BUCKETS, :N_HEADS].set(rel_bias.astype(F32))
    bv = pl.pallas_call(
        _bias_kernel,
        out_shape=jax.ShapeDtypeStruct((NEAR + 8, LANES), F32),
    )(jnp.asarray(oh, BF16), table)
    return bv[:NEAR + 1, :N_HEADS]


SUB = 8
HEADS_PER_CG = LANES // HEAD_DIM
N_CG = 2 * KV_ROW // LANES
CG_PER_SLOT = KV_ROW // LANES
CG_HIDDEN = HEADS_PER_CG * CMP_HIDDEN


def _compress_parts(rows_s, wbd_ref, nh):
    part0 = jnp.zeros((nh, CG_HIDDEN), F32)
    part1 = jnp.zeros((nh, CG_HIDDEN), F32)
    for s in range(0, CMP_STRIDE, 2):
        xs = jnp.concatenate([rows_s(s), rows_s(s + 1)], axis=1).astype(BF16)
        part0 = part0 + jnp.dot(xs, wbd_ref[0, 0, s // 2], preferred_element_type=F32)
        part1 = part1 + jnp.dot(xs, wbd_ref[0, 1, s // 2], preferred_element_type=F32)
    return part0, part1


def _compress_finish(part0, part1, pos_ref, w1_ref, w2bd_ref, nh):
    pre0 = jnp.dot(pos_ref[0], w1_ref[0], preferred_element_type=F32)[0:1]
    pre = pre0 + part0 + pltpu.roll(part1, nh - 1, axis=0)
    return jnp.dot(_silu(pre).astype(BF16), w2bd_ref[0], preferred_element_type=F32)


def _compress_kernel(x_ref, pos_ref, w1_ref, wbd_ref, w2bd_ref, o_ref, *, nh):
    part0, part1 = _compress_parts(lambda s: x_ref[pl.ds(s, nh, stride=CMP_STRIDE), :], wbd_ref, nh)
    o_ref[0, 0] = _compress_finish(part0, part1, pos_ref, w1_ref, w2bd_ref, nh)


def _block_diag(a, b):
    za = jnp.zeros(a.shape[:-1] + (b.shape[-1],), a.dtype)
    zb = jnp.zeros(b.shape[:-1] + (a.shape[-1],), b.dtype)
    return jnp.concatenate([jnp.concatenate([a, za], axis=-1), jnp.concatenate([zb, b], axis=-1)], axis=-2)


def _compress_weight_set(pos, w1, w2, a, b):
    w1r = w1.reshape(2, CMP_RATIO, CMP_STRIDE, HEAD_DIM, CMP_HIDDEN)
    wbd = _block_diag(w1r[a], w1r[b]).reshape(CMP_RATIO, CMP_STRIDE // 2, 2 * LANES, CG_HIDDEN)
    posx = jnp.zeros((SUB, 2 * CMP_BLOCK * HEAD_DIM), F32).at[0].set(
        jnp.concatenate([pos[a].reshape(-1), pos[b].reshape(-1)]))
    return [x.astype(BF16) for x in (posx, _block_diag(w1[a], w1[b]), wbd, _block_diag(w2[a], w2[b]))]


def _compress_weights(cmp_pos, cmp_w1, cmp_w2, pairs):
    sets = [_compress_weight_set(cmp_pos, cmp_w1, cmp_w2, a, b) for a, b in pairs]
    return [jnp.stack(x) for x in zip(*sets)]


def _compress_specs(cw, imap):
    def spec(a):
        return pl.BlockSpec((1,) + a.shape[1:], lambda *g: (imap(*g),) + (0,) * (a.ndim - 1))
    return [spec(a) for a in cw]


def _pack_cmp(cmp, dtype=None):
    nb, _, n, _ = cmp.shape
    c = cmp.reshape(nb, 2, CG_PER_SLOT, n, HEADS_PER_CG, HEAD_DIM)
    c = jnp.transpose(c, (0, 2, 4, 3, 1, 5)).reshape(nb, N_KV_HEADS, n, 2 * HEAD_DIM)
    return c.astype(BF16 if dtype is None else dtype)


def _compress_prompt(proj, cw, nb, t):
    nh = t // CMP_STRIDE
    return pl.pallas_call(
        functools.partial(_compress_kernel, nh=nh),
        grid=(nb, N_KV_HEADS),
        in_specs=[pl.BlockSpec((t, LANES), lambda b, k: (b, (N_Q_COLS + k * KV_HEAD_COLS) // LANES))]
        + _compress_specs(cw, lambda b, k: 0),
        out_specs=pl.BlockSpec((1, 1, nh, LANES), lambda b, k: (b, k, 0, 0)),
        out_shape=jax.ShapeDtypeStruct((nb, N_KV_HEADS, nh, LANES), F32),
        compiler_params=_cparams(("arbitrary", "arbitrary")),
    )(proj, *cw)


KEY_TILE = 4 * Q_BLOCK
FRONT_PAD = KEY_TILE
SUBTILES = KEY_TILE // Q_BLOCK
BAND_OFF = 2 * Q_BLOCK // CMP_STRIDE
BAND_W = 32
Q_PER_CMP = Q_BLOCK // CMP_STRIDE


def _nsa_consts(t):
    ncp = t // CMP_STRIDE
    nblk = t // SEL_BLOCK
    nbp = -(-nblk // LANES) * LANES
    at = np.zeros((nbp, ncp), np.float32)
    for j in range(nblk):
        for n in range(SEL_RATIO * j - (CMP_RATIO - 1), SEL_RATIO * (j + 1)):
            if 0 <= n < ncp - CMP_RATIO + 1:
                at[j, n] = 1.0
    key_blk = np.arange(t) // SEL_BLOCK
    e = (np.arange(nbp)[:, None] == key_blk[None, :]).astype(np.float32)
    e1 = e.reshape(nbp, t // LANES, LANES).transpose(1, 0, 2)
    e1 = np.concatenate([np.zeros((SUBTILES,) + e1.shape[1:], np.float32), e1], axis=0)
    band = np.zeros((2 * ncp + 2 * Q_PER_CMP, LANES), np.float32)
    u = np.arange(BAND_W)
    band[u + ncp, HEAD_DIM + u] = 1.0
    band[u + ncp, HEAD_DIM + BAND_W + u] = 1.0
    r = np.arange(Q_BLOCK)[:, None]
    c = np.arange(LANES)[None, :]
    d0 = r - c
    idx0 = np.where(d0 >= 0, d0, NEAR)
    idx1 = Q_BLOCK + r - c
    dc = r - CMP_STRIDE * c + (CMP_STRIDE * BAND_OFF - CMP_BLOCK + 1)
    idxc = np.where((dc >= 0) & (dc < NEAR), dc, NEAR)
    assert np.all(idxc[:, BAND_W:] == NEAR)
    bf = lambda a: jnp.asarray(a, BF16)
    return dict(at=bf(at), e1=bf(e1), band=jnp.asarray(band),
                idx0=idx0, idx1=idx1, idxc=idxc, ncp=ncp, nbp=nbp, nblk=nblk)


def _nsa_kernel(q_ref, gl_ref, kck_ref, kvc_ref, kvs_ref, kvw_ref, at_ref, e1_ref, tb_ref, cbq_ref, band_ref,
                o_ref, *, ncp, nbp, n_sel):
    i = pl.program_id(2)
    rows = GROUP * Q_BLOCK
    lane = lax.broadcasted_iota(jnp.int32, (1, LANES), 1)
    low = lane < HEAD_DIM
    r_col = lax.broadcasted_iota(jnp.int32, (rows, 1), 0) % Q_BLOCK
    q_pos = i * Q_BLOCK + r_col

    q = q_ref[...] * (HEAD_DIM ** -0.5 * LOG2E)
    parts = []
    for h in range(GROUP * HEAD_DIM // LANES):
        qh = q[:, h * LANES:(h + 1) * LANES]
        parts.append(jnp.where(low, qh, 0.0))
        parts.append(jnp.where(low, pltpu.roll(qh, HEAD_DIM, axis=1), 0.0))
    qs = jnp.concatenate(parts, axis=0).astype(BF16)

    qc = jnp.where(low, qs, cbq_ref[0])
    start = pl.multiple_of(ncp + BAND_OFF - Q_PER_CMP * i, 8)
    kq = (kck_ref[0, 0] + band_ref[pl.ds(start, ncp), :]).astype(BF16)
    s = lax.dot_general(qc, kq, _NT, preferred_element_type=F32)
    cmp_end = lax.broadcasted_iota(jnp.int32, (1, ncp), 1) * CMP_STRIDE + (CMP_BLOCK - 1)
    mask = cmp_end <= q_pos
    m = jnp.max(jnp.where(mask, s, NEG), axis=-1, keepdims=True)
    e = jnp.where(mask, jnp.exp2(s - m), 0.0)
    p_cmp = e * (1.0 / jnp.maximum(jnp.sum(e, axis=-1, keepdims=True), 1e-30))
    o_cmp = jnp.dot(p_cmp.astype(BF16), kvc_ref[0, 0], preferred_element_type=F32)
    imp = p_cmp[0:Q_BLOCK]
    for g in range(1, GROUP):
        imp = imp + p_cmp[g * Q_BLOCK:(g + 1) * Q_BLOCK]
    bst = sum(lax.dot_general(at_ref[...], part, _NT, preferred_element_type=F32)
              for part in _split_bf16(imp, 2))

    jb = lax.broadcasted_iota(jnp.int32, (nbp, Q_BLOCK), 0)
    q_blk = (i * Q_BLOCK + lax.broadcasted_iota(jnp.int32, (nbp, Q_BLOCK), 1)) // SEL_BLOCK
    causal = jb <= q_blk
    forced = (jb == 0) | (jb == q_blk) | (jb == q_blk - 1)
    work = jnp.where(causal, bst + jnp.where(forced, FORCE_BONUS, 0.0), -jnp.inf)
    jbf = jb.astype(F32)
    sel_t = jnp.zeros((nbp, Q_BLOCK), F32)
    for _ in range(n_sel):
        top = jnp.max(work, axis=0, keepdims=True)
        first = jnp.min(jnp.where(work == top, jbf, float(nbp)), axis=0, keepdims=True)
        hit = jbf == first
        sel_t = jnp.where(hit, 1.0, sel_t)
        work = jnp.where(hit, -jnp.inf, work)
    sel = jnp.where(causal, sel_t, 0.0).T.astype(BF16)

    def ones_and_values(kv):
        return jnp.where(low, jnp.ones_like(kv), kv)

    def normalise(acc):
        return jnp.where(low, 0.0, acc / jnp.maximum(pltpu.roll(acc, HEAD_DIM, axis=1), 1e-30))

    n_wt = WINDOW // Q_BLOCK
    kw = kvw_ref[0, 0, pl.ds(pl.multiple_of((i + SUBTILES - n_wt) * Q_BLOCK, Q_BLOCK), WINDOW + Q_BLOCK), :]
    s = lax.dot_general(qs, kw, _NT, preferred_element_type=F32)
    pieces = []
    for u in range(n_wt + 1):
        su = s[:, u * Q_BLOCK:(u + 1) * Q_BLOCK]
        if u == n_wt:
            su = jnp.where(lane <= r_col, su + tb_ref[0, 0], NEG)
        else:
            if u == n_wt - 1:
                su = su + tb_ref[0, 1]
            su = su + jnp.where(i - n_wt + u >= 0, 0.0, NEG)
            if u == 0:
                su = jnp.where(lane > r_col, su, NEG)
        pieces.append(su)
    sm = jnp.concatenate(pieces, axis=1).astype(BF16)
    e = jnp.exp2(sm - jnp.max(sm, axis=-1, keepdims=True))
    o_win = normalise(jnp.dot(e, ones_and_values(kw), preferred_element_type=F32))

    n_tiles = (i + SUBTILES) // SUBTILES

    def keys(u):
        return kvs_ref[0, 0, pl.ds(pl.multiple_of((i + 1 - SUBTILES * u) * Q_BLOCK, Q_BLOCK), KEY_TILE), :]

    def masked_scores(u, diagonal=False):
        t0p = i + 1 - SUBTILES * u
        mk = jnp.concatenate([jnp.dot(sel, e1_ref[t0p + v], preferred_element_type=F32) for v in range(SUBTILES)],
                             axis=1).astype(BF16)
        s3 = lax.dot_general(qs, keys(u), _NT, preferred_element_type=F32).reshape(GROUP, Q_BLOCK, KEY_TILE)
        if diagonal:
            r3 = lax.broadcasted_iota(jnp.int32, (1, Q_BLOCK, LANES), 1)
            c3 = lax.broadcasted_iota(jnp.int32, (1, Q_BLOCK, LANES), 2)
            tb0 = tb_ref[0, 0].reshape(GROUP, Q_BLOCK, LANES)
            tb1 = tb_ref[0, 1].reshape(GROUP, Q_BLOCK, LANES)
            s3 = jnp.concatenate([s3[:, :, :KEY_TILE - 2 * Q_BLOCK],
                                  s3[:, :, KEY_TILE - 2 * Q_BLOCK:KEY_TILE - Q_BLOCK] + tb1,
                                  jnp.where(c3 <= r3, s3[:, :, KEY_TILE - Q_BLOCK:] + tb0, NEG)], axis=2)
        return jnp.where((mk > 0.5)[None], s3.astype(BF16), NEG).reshape(rows, KEY_TILE)

    def sel_body(u, carry):
        m, acc, sm, e_prev = carry
        pv = jnp.dot(e_prev, ones_and_values(keys(jnp.maximum(u - 1, 0))), preferred_element_type=F32)
        sm_next = masked_scores(jnp.minimum(u + 1, n_tiles - 1))
        m_new = jnp.maximum(m, jnp.max(sm, axis=-1, keepdims=True).astype(F32))
        alpha = jnp.exp2(m - m_new)
        e = jnp.exp2(sm - m_new.astype(BF16))
        return m_new, alpha * (acc + pv), sm_next, e

    init = (jnp.full((rows, 1), NEG, F32), jnp.zeros((rows, LANES), F32),
            masked_scores(0, diagonal=True), jnp.zeros((rows, KEY_TILE), BF16))
    m, acc, _, e_last = lax.fori_loop(0, n_tiles, sel_body, init)
    o_sel = normalise(acc + jnp.dot(e_last, ones_and_values(keys(n_tiles - 1)), preferred_element_type=F32))

    sg = jax.nn.sigmoid(gl_ref[...])
    for g in range(GROUP):
        out = jnp.zeros((Q_BLOCK, LANES), F32)
        for j, ob in enumerate((o_cmp, o_sel, o_win)):
            out = out + sg[:, 3 * g + j:3 * g + j + 1] * ob[g * Q_BLOCK:(g + 1) * Q_BLOCK]
        o_ref[0, 0, g] = out.astype(o_ref.dtype)


GATE_COL0 = N_Q_COLS + N_KV_COLS
NP_ATTN = GATE_COL0 + N_KV_HEADS * LANES


def _nsa_prompt(proj, kck, kvc, kvs, kvw, tb, cbq, consts, nb, t):
    ni = t // Q_BLOCK
    ncp, nbp = consts['ncp'], consts['nbp']
    gcol = GATE_COL0 // LANES
    qw = GROUP * HEAD_DIM
    full = lambda a: pl.BlockSpec(a.shape, lambda b, k, i: (0,) * a.ndim)
    per_kvh = lambda a: pl.BlockSpec((1,) + a.shape[1:], lambda b, k, i: (k,) + (0,) * (a.ndim - 1))
    per_bk = lambda a: pl.BlockSpec((1, 1) + a.shape[2:], lambda b, k, i: (b, k, 0, 0))
    c = consts
    n_sel = min(N_SELECT, c['nblk'])
    assert n_sel >= 3
    return pl.pallas_call(
        functools.partial(_nsa_kernel, ncp=ncp, nbp=nbp, n_sel=n_sel),
        grid=(nb, N_KV_HEADS, ni),
        in_specs=[pl.BlockSpec((Q_BLOCK, qw), lambda b, k, i: (b * ni + i, k)),
                  pl.BlockSpec((Q_BLOCK, LANES), lambda b, k, i: (b * ni + i, gcol + k)),
                  per_bk(kck), per_bk(kvc), per_bk(kvs), per_bk(kvw),
                  full(c['at']), full(c['e1']), per_kvh(tb), per_kvh(cbq), full(c['band'])],
        out_specs=pl.BlockSpec((1, 1, GROUP, Q_BLOCK, LANES), lambda b, k, i: (b, k, 0, i, 0)),
        out_shape=jax.ShapeDtypeStruct((nb, N_KV_HEADS, GROUP, t, LANES), BF16),
        compiler_params=_cparams(("arbitrary", "arbitrary", "arbitrary")),
    )(proj, proj, kck, kvc, kvs, kvw, c['at'], c['e1'], tb, cbq, c['band'])


def _heads_out_kernel(a_ref, w_ref, g_ref, gate_ref, x_ref, o_ref):
    y = jnp.zeros(o_ref.shape, F32)
    for k in range(N_KV_HEADS):
        for g in range(0, GROUP, 2):
            a = jnp.concatenate([a_ref[0, k, g], a_ref[0, k, g + 1]], axis=1)
            h = k * GROUP + g
            w = w_ref[h * LANES:(h + 2) * LANES, :]
            y = y + jnp.dot(a, w, preferred_element_type=F32)
    o_ref[...] = x_ref[...] + gate_ref[0] * _rms(y, g_ref[...])


def _heads_out(a, w_pad, g, gate, x, tm):
    nb, _, _, t, _ = a.shape
    d = w_pad.shape[1]
    tpb = t // tm
    return pl.pallas_call(
        _heads_out_kernel,
        grid=(nb * tpb,),
        in_specs=[pl.BlockSpec((1, N_KV_HEADS, GROUP, tm, LANES), lambda i: (i // tpb, 0, 0, i % tpb, 0)),
                  pl.BlockSpec(w_pad.shape, lambda i: (0, 0)),
                  pl.BlockSpec((1, d), lambda i: (0, 0)),
                  pl.BlockSpec((1,) + gate.shape[1:], lambda i: (i // tpb, 0, 0)),
                  pl.BlockSpec((tm, d), lambda i: (i, 0))],
        out_specs=pl.BlockSpec((tm, d), lambda i: (i, 0)),
        out_shape=jax.ShapeDtypeStruct((nb * t, d), F32),
        compiler_params=_cparams(("arbitrary",)),
    )(a, w_pad, g.reshape(1, d), gate, x)


def _row_tile(m, cap):
    return m if m <= cap else cap


def _attn_in_weights(w_in):
    d = w_in.shape[0]
    kv = w_in[:, N_Q_COLS:GATE_COL0].reshape(d, N_KV_SLOTS + N_WIN_SLOTS, N_KV_HEADS, HEAD_DIM)
    kv = jnp.swapaxes(kv, 1, 2).reshape(d, N_KV_COLS)
    gates = w_in[:, GATE_COL0:].reshape(d, N_KV_HEADS, GROUP * 3)
    gates = jnp.pad(gates, ((0, 0), (0, 0), (0, LANES - GROUP * 3))).reshape(d, -1)
    return jnp.concatenate([w_in[:, :N_Q_COLS], kv, gates], axis=1).astype(BF16)


def _attn_out_weights(w_out):
    w = w_out.reshape(N_HEADS, HEAD_DIM, w_out.shape[1])
    return jnp.pad(w, ((0, 0), (LANES - HEAD_DIM, 0), (0, 0))).reshape(N_HEADS * LANES, -1).astype(BF16)


def _attn_in_kernel(x_ref, g_ref, sh_ref, sc_ref, w_ref, zs_ref, zw_ref, o_ref, kvs_ref, kvw_ref, kvt_ref, wint_ref):
    del zs_ref, zw_ref
    h = _rms(x_ref[...], g_ref[...]) * (1.0 + sc_ref[0]) + sh_ref[0]
    r = jnp.dot(h.astype(BF16), w_ref[...], preferred_element_type=F32)
    o_ref[...] = r
    for k in range(N_KV_HEADS):
        c0 = N_Q_COLS + k * KV_HEAD_COLS
        kvs_ref[0, k] = r[:, c0 + LANES:c0 + 2 * LANES].astype(BF16)
        kvw_ref[0, k] = r[:, c0 + 2 * LANES:c0 + 3 * LANES].astype(BF16)
        for p in range(KV_HEAD_COLS // LANES):
            tt = r[:, c0 + p * LANES:c0 + (p + 1) * LANES].T
            for j in range(2):
                slot = 2 * p + j
                dst, s0 = (kvt_ref, slot) if slot < N_KV_SLOTS else (wint_ref, slot - N_KV_SLOTS)
                dst[0, (s0 * N_KV_HEADS + k) * HEAD_DIM:(s0 * N_KV_HEADS + k + 1) * HEAD_DIM, :] = (
                    tt[j * HEAD_DIM:(j + 1) * HEAD_DIM])


def _attn_in_prompt(x, g, shift, scale, w, nb, t):
    m, d = x.shape
    tm = FRONT_PAD
    assert t % tm == 0
    tpb = t // tm
    mod_spec = pl.BlockSpec((1,) + shift.shape[1:], lambda i: (i // tpb, 0, 0))
    pad_shape = (nb, N_KV_HEADS, FRONT_PAD + t, LANES)
    pad_spec = pl.BlockSpec((1, N_KV_HEADS, tm, LANES), lambda i: (i // tpb, 0, 1 + i % tpb, 0))
    zeros = jnp.zeros(pad_shape, BF16)
    return pl.pallas_call(
        _attn_in_kernel,
        grid=(m // tm,),
        in_specs=[pl.BlockSpec((tm, d), lambda i: (i, 0)),
                  pl.BlockSpec((1, d), lambda i: (0, 0)),
                  mod_spec, mod_spec,
                  pl.BlockSpec(w.shape, lambda i: (0, 0)),
                  pl.BlockSpec(memory_space=pl.ANY), pl.BlockSpec(memory_space=pl.ANY)],
        out_specs=[pl.BlockSpec((tm, NP_ATTN), lambda i: (i, 0)), pad_spec, pad_spec,
                   pl.BlockSpec((1, N_KV_SLOTS * KV_ROW, tm), lambda i: (i // tpb, 0, i % tpb)),
                   pl.BlockSpec((1, N_WIN_SLOTS * KV_ROW, tm), lambda i: (i // tpb, 0, i % tpb))],
        out_shape=[jax.ShapeDtypeStruct((m, NP_ATTN), F32),
                   jax.ShapeDtypeStruct(pad_shape, BF16), jax.ShapeDtypeStruct(pad_shape, BF16),
                   jax.ShapeDtypeStruct((nb, N_KV_SLOTS * KV_ROW, t), F32),
                   jax.ShapeDtypeStruct((nb, N_WIN_SLOTS * KV_ROW, t), F32)],
        input_output_aliases={5: 1, 6: 2},
        compiler_params=_cparams(("arbitrary",)),
    )(x, g.reshape(1, d), shift, scale, w, zeros, zeros)


def _unpermute_kv(proj_kv, lead):
    kv = proj_kv.reshape(lead + (N_KV_HEADS, N_KV_SLOTS + N_WIN_SLOTS, HEAD_DIM))
    return jnp.swapaxes(kv, -3, -2)


def _bias_tiles_kernel(t_ref, oh_ref, o_ref):
    oh = oh_ref[...]
    o_ref[...] = sum(jnp.dot(p, oh, preferred_element_type=F32) for p in _split_bf16(t_ref[...], 3))


def _bias_tiles(rel_bias, idxs):
    n = sum(ix.size for ix in idxs)
    d = np.concatenate([ix.reshape(-1) for ix in idxs])
    oh = np.zeros((LANES, n), np.float32)
    near = d < NEAR
    oh[_rel_bucket_np(d[near]), np.nonzero(near)[0]] = 1.0
    table = (rel_bias.astype(F32) - rel_bias[N_BUCKETS - 1].astype(F32)).T * LOG2E
    table = jnp.zeros((N_HEADS, LANES), F32).at[:, :N_BUCKETS].set(table)
    tn = 8192
    assert n % tn == 0
    out = pl.pallas_call(
        _bias_tiles_kernel,
        grid=(n // tn,),
        in_specs=[pl.BlockSpec((N_HEADS, LANES), lambda j: (0, 0)), pl.BlockSpec((LANES, tn), lambda j: (0, j))],
        out_specs=pl.BlockSpec((N_HEADS, tn), lambda j: (0, j)),
        out_shape=jax.ShapeDtypeStruct((N_HEADS, n), F32),
        compiler_params=_cparams(("arbitrary",)),
    )(table, jnp.asarray(oh, BF16))
    tiles, off = [], 0
    for ix in idxs:
        r, c = ix.shape
        tiles.append(out[:, off:off + r * c].reshape(N_KV_HEADS, GROUP * r, c))
        off += r * c
    return tiles


def _nsa_prompt_mixer(xp, g, shift, scale, w_in, cw, rel_bias, nb, t):
    proj, kvs, kvw, kvt, wint = _attn_in_prompt(xp, g, shift, scale, w_in, nb, t)
    as_rows = lambda a: jnp.transpose(a.reshape(nb, -1, N_KV_HEADS, HEAD_DIM, a.shape[-1]), (0, 4, 1, 2, 3))
    kv_new = as_rows(kvt)
    win_new = as_rows(wint[:, :, t - min(WINDOW, t):])
    kck = _compress_prompt(proj, cw, nb, t)
    kvc = kck.astype(BF16)
    kck = jnp.where(jnp.arange(LANES) < HEAD_DIM, kck, 0.0)
    c = _nsa_consts(t)
    tiles = _bias_tiles(rel_bias, [c['idx0'], c['idx1'], c['idxc']])
    tb = jnp.stack(tiles[:2], axis=1)
    hi, lo = _split_bf16(tiles[2], 2)
    cbq = jnp.concatenate([jnp.zeros(hi.shape[:2] + (HEAD_DIM,), BF16), hi[..., :BAND_W], lo[..., :BAND_W]], axis=-1)
    o = _nsa_prompt(proj, kck, kvc, kvs, kvw, tb, cbq, c, nb, t)
    return o, kv_new, win_new


SC_PAGES = 32
CG_PER_PAGE = N_KV_SLOTS * KV_ROW // LANES


def _cache_tiles(cache_kv):
    page = cache_kv.shape[2]
    assert page == LANES
    return jnp.transpose(cache_kv, (0, 1, 3, 4, 5, 2)).reshape(-1, page)


def _compress_sample_kernel(pt_ref, cache_ref, pos_ref, w1_ref, wbd_ref, w2bd_ref, o_ref, tbuf_ref, rows0_ref,
                            rows1_ref, sem_ref, p0_ref, p1_ref, *, n_pages, page, row_base):
    row_bufs = (rows0_ref, rows1_ref)
    step = pl.program_id(0) * N_CG + pl.program_id(1)
    nsteps = pl.num_programs(0) * N_CG
    chunk_pages = min(SC_PAGES, n_pages // 2)
    nch = n_pages // chunk_pages
    hpc = chunk_pages * page // CMP_STRIDE
    nh = nch * hpc

    last = nsteps * nch - 1

    def copies(chunk, slot):
        chunk = jnp.minimum(chunk, last)
        step_, ch = chunk // nch, chunk % nch
        b_, c_ = step_ // N_CG, step_ % N_CG
        out = []
        for p in range(chunk_pages):
            pg = pt_ref[b_, ch * chunk_pages + p]
            out.append(pltpu.make_async_copy(
                cache_ref.at[pl.ds(((row_base + pg) * CG_PER_PAGE + c_) * LANES, LANES), :],
                tbuf_ref.at[slot, p], sem_ref.at[slot]))
        return out

    def to_rows(slot):
        for p in range(chunk_pages):
            row_bufs[slot][p * page:(p + 1) * page, :] = tbuf_ref[slot, p].T

    @pl.when(step == 0)
    def _():
        for slot in range(2):
            for cp in copies(slot, slot):
                cp.start()
        for cp in copies(0, 0):
            cp.wait()
        to_rows(0)

    for ch in range(nch):
        slot = ch % 2
        chunk = step * nch + ch
        for cp in copies(chunk + 1, 1 - slot):
            cp.wait()
        for cp in copies(chunk + 2, slot):
            cp.start()
        to_rows(1 - slot)
        rows = row_bufs[slot]
        p0, p1 = _compress_parts(lambda s: rows[pl.ds(s, hpc, stride=CMP_STRIDE), :], wbd_ref, hpc)
        p0_ref[ch * hpc:(ch + 1) * hpc] = p0
        p1_ref[ch * hpc:(ch + 1) * hpc] = p1

    @pl.when(step == nsteps - 1)
    def _():
        for cp in copies(last, (nch - 1) % 2):
            cp.wait()

    o_ref[0, 0] = _compress_finish(p0_ref[...], p1_ref[...], pos_ref, w1_ref, w2bd_ref, nh)


def _compress_sample(page_table, cache_t, cw, page, row_base):
    nb, n_pages = page_table.shape
    chunk_pages = min(SC_PAGES, n_pages // 2)
    assert n_pages % (2 * chunk_pages) == 0
    nh = n_pages * page // CMP_STRIDE
    grid_spec = pltpu.PrefetchScalarGridSpec(
        num_scalar_prefetch=1,
        grid=(nb, N_CG),
        in_specs=[pl.BlockSpec(memory_space=pl.ANY)] + _compress_specs(cw, lambda b, c, pt: c // CG_PER_SLOT),
        out_specs=pl.BlockSpec((1, 1, nh, LANES), lambda b, c, pt: (b, c, 0, 0)),
        scratch_shapes=[pltpu.VMEM((2, chunk_pages, LANES, page), F32),
                        pltpu.VMEM((chunk_pages * page, LANES), F32),
                        pltpu.VMEM((chunk_pages * page, LANES), F32),
                        pltpu.SemaphoreType.DMA((2,)),
                        pltpu.VMEM((nh, CG_HIDDEN), F32),
                        pltpu.VMEM((nh, CG_HIDDEN), F32)])
    return pl.pallas_call(
        functools.partial(_compress_sample_kernel, n_pages=n_pages, page=page, row_base=row_base),
        grid_spec=grid_spec,
        out_shape=jax.ShapeDtypeStruct((nb, N_CG, nh, LANES), F32),
        compiler_params=_cparams(("arbitrary", "arbitrary")),
    )(page_table, cache_t, *cw)


def _softmax_with_new_key(s, s_new):
    m = jnp.maximum(jnp.max(s, axis=-1, keepdims=True), s_new)
    e = jnp.exp(s - m)
    e_new = jnp.exp(s_new - m)
    inv = 1.0 / jnp.maximum(jnp.sum(e, axis=-1, keepdims=True) + e_new, 1e-30)
    return e * inv, e_new * inv


def _nsa_sample_kernel(pt_ref, qs_ref, q_ref, gl_ref, new_ref, ncol_ref, kvc_ref, cache_ref, win_ref, cbs_ref,
                       sbz_ref, wb_ref, b0_ref, as_ref, o_ref, wout_ref, gk_ref, gv_ref, sem_ref, idx_ref,
                       *, past, page, row_base, n_pick):
    b = pl.program_id(0)
    blk_per_page = page // SEL_BLOCK
    nbs = past // SEL_BLOCK
    ncs = kvc_ref.shape[2]
    row = lax.broadcasted_iota(jnp.int32, (SUB, 1), 0)

    cmp_ok = lax.broadcasted_iota(jnp.int32, (SUB, ncs), 1) * CMP_STRIDE + (CMP_BLOCK - 1) <= past
    imp = jnp.zeros((SUB, ncs), F32)
    o_cmp = []
    for k in range(N_KV_HEADS):
        kc = kvc_ref[0, k]
        s = lax.dot_general(qs_ref[0, k], kc, _NT, preferred_element_type=F32) + cbs_ref[k]
        m = jnp.max(jnp.where(cmp_ok, s, NEG), axis=-1, keepdims=True)
        e = jnp.where(cmp_ok, jnp.exp(s - m), 0.0)
        p = e / jnp.maximum(jnp.sum(e, axis=-1, keepdims=True), 1e-30)
        o_cmp.append(jnp.dot(p.astype(BF16), kc, preferred_element_type=F32))
        imp_k = p[0:1]
        for g in range(1, GROUP):
            imp_k = imp_k + p[g:g + 1]
        imp = imp + jnp.where(row == k, imp_k, 0.0)
    bs = sum(jnp.dot(part, as_ref[...], preferred_element_type=F32) for part in _split_bf16(imp, 3))

    lane_b = lax.broadcasted_iota(jnp.int32, (SUB, nbs), 1)
    lane_f = lane_b.astype(F32)
    score = bs + jnp.where((lane_b == 0) | (lane_b == nbs - 1), FORCE_BONUS, 0.0)

    def gathers(k, r):
        j = idx_ref[k * n_pick + r]
        pg = pt_ref[b, j // blk_per_page]
        tile0 = (row_base + pg) * N_KV_SLOTS
        return [pltpu.make_async_copy(
            cache_ref.at[pl.ds(((tile0 + slot) * N_KV_HEADS + k) * HEAD_DIM, HEAD_DIM), :],
            buf.at[k * n_pick + r], sem_ref.at[0]) for slot, buf in ((2, gk_ref), (3, gv_ref))]

    for r in range(n_pick):
        m = jnp.max(score, axis=-1, keepdims=True)
        pick = jnp.min(jnp.where(score == m, lane_f, 1e9), axis=-1, keepdims=True)
        score = jnp.where(lane_f == pick, -jnp.inf, score)
        for k in range(N_KV_HEADS):
            idx_ref[k * n_pick + r] = jnp.sum(jnp.where(row == k, pick, 0.0)).astype(jnp.int32)
            for cp in gathers(k, r):
                cp.start()

    def new_row(k, j):
        return new_ref[0, k][j:j + 1].astype(BF16).astype(F32)

    tok = lax.broadcasted_iota(jnp.int32, (1, WINDOW), 1)
    o_win = []
    for k in range(N_KV_HEADS):
        for slot in range(N_WIN_SLOTS):
            shifted = pltpu.roll(win_ref[0, slot, k], WINDOW - 1, axis=1)
            wout_ref[0, slot, k] = jnp.where(tok == WINDOW - 1, ncol_ref[0, slot, k], shifted)
        q = q_ref[0, k]
        s = jnp.dot(q, win_ref[0, 0, k].astype(BF16), preferred_element_type=F32) + wb_ref[k]
        s = jnp.where(tok >= 1, s, NEG)
        s_new = jnp.sum(q.astype(F32) * new_row(k, 2), axis=-1, keepdims=True) + b0_ref[k][:, 0:1]
        p, p_new = _softmax_with_new_key(s, s_new)
        o_win.append(lax.dot_general(p.astype(BF16), win_ref[0, 1, k].astype(BF16), _NT,
                                     preferred_element_type=F32) + p_new * new_row(k, 3))

    for k in range(N_KV_HEADS):
        for r in range(n_pick):
            for cp in gathers(k, r):
                cp.wait()

    upper = lax.broadcasted_iota(jnp.int32, (1, LANES), 1) // SEL_BLOCK
    for k in range(N_KV_HEADS):
        q = q_ref[0, k]
        tiles = []
        for r in range(n_pick):
            j = idx_ref[k * n_pick + r]
            near = j - (nbs - NEAR // SEL_BLOCK)
            s = jnp.dot(q, gk_ref[k * n_pick + r].astype(BF16), preferred_element_type=F32)
            s = s + sbz_ref[k, jnp.where(near >= 0, near, NEAR // SEL_BLOCK)]
            tiles.append(jnp.where(upper == j % blk_per_page, s, NEG))
        s_new = jnp.sum(q.astype(F32) * new_row(k, 0), axis=-1, keepdims=True) + b0_ref[k][:, 0:1]
        p, p_new = _softmax_with_new_key(jnp.concatenate(tiles, axis=1), s_new)
        o_sel = p_new * new_row(k, 1)
        for r in range(n_pick):
            o_sel = o_sel + lax.dot_general(p[:, r * LANES:(r + 1) * LANES].astype(BF16),
                                            gv_ref[k * n_pick + r].astype(BF16), _NT, preferred_element_type=F32)
        g_cmp, g_sel, g_win = (jax.nn.sigmoid(gl_ref[0, k, j])[:, :HEAD_DIM] for j in range(3))
        o_ref[0, k] = g_cmp * o_cmp[k][:, HEAD_DIM:] + g_sel * o_sel + g_win * o_win[k]


def _bias_rows(bvz, idx):
    t = jnp.take(bvz, jnp.asarray(idx, jnp.int32), axis=0).reshape(len(idx), N_KV_HEADS, GROUP)
    t = jnp.transpose(t, (1, 2, 0))
    return jnp.pad(t, ((0, 0), (0, SUB - GROUP), (0, 0)))


def _nsa_sample(proj_s, kvc, page_table, cache_t, win_t, bvz, past, page, row_base):
    nb = proj_s.shape[0]
    n_pick = N_SELECT - 1
    nbs = past // SEL_BLOCK
    ncs = past // CMP_STRIDE
    n_near = NEAR // SEL_BLOCK
    blk_per_page = page // SEL_BLOCK
    assert past % page == 0 and page == LANES and nbs > n_pick + n_near and (nbs - n_near) % blk_per_page == 0
    assert win_t.shape[-1] == WINDOW < past
    q = (proj_s[:, :N_Q_COLS] * HEAD_DIM ** -0.5).astype(BF16).reshape(nb, N_KV_HEADS, GROUP, HEAD_DIM)
    q = jnp.pad(q, ((0, 0), (0, 0), (0, SUB - GROUP), (0, 0)))
    qs = jnp.pad(q, ((0, 0), (0, 0), (0, 0), (0, LANES - HEAD_DIM)))
    gl = proj_s[:, GATE_COL0:].reshape(nb, N_KV_HEADS, LANES)[:, :, :GROUP * 3].reshape(nb, N_KV_HEADS, GROUP, 3)
    gl = jnp.pad(jnp.transpose(gl, (0, 1, 3, 2)), ((0, 0), (0, 0), (0, 0), (0, SUB - GROUP)))
    gl = jnp.broadcast_to(gl[..., None], gl.shape + (LANES,))
    new = _unpermute_kv(proj_s[:, N_Q_COLS:GATE_COL0], (nb,))[:, 2:]
    new_rows = jnp.transpose(new, (0, 2, 1, 3))
    new_cols = new[:, 2:].reshape(nb, N_WIN_SLOTS, N_KV_HEADS, HEAD_DIM, 1)

    dc = past - (np.arange(ncs) * CMP_STRIDE + CMP_BLOCK - 1)
    cbs = _bias_rows(bvz, np.where((dc >= 0) & (dc < NEAR), dc, NEAR))
    lane = np.arange(LANES)
    sb = []
    for jj in range(n_near + 1):
        d = NEAR - SEL_BLOCK * jj - lane % SEL_BLOCK
        ok = (lane // SEL_BLOCK == jj % blk_per_page) & (jj < n_near) & (d < NEAR)
        sb.append(_bias_rows(bvz, np.where(ok, d, NEAR)))
    sbz = jnp.stack(sb, axis=1)
    dw = WINDOW - np.arange(WINDOW)
    wb = _bias_rows(bvz, np.where(dw < NEAR, dw, NEAR))
    b0 = _bias_rows(bvz, np.zeros(LANES, np.int64))
    a_s = np.zeros((ncs, nbs), np.float32)
    for j in range(nbs):
        for n in range(SEL_RATIO * j - (CMP_RATIO - 1), SEL_RATIO * (j + 1)):
            if 0 <= n < ncs - CMP_RATIO + 1:
                a_s[n, j] = 1.0
    a_s = jnp.asarray(a_s, BF16)

    per_b = lambda a: pl.BlockSpec((1,) + a.shape[1:], lambda b, pt: (b,) + (0,) * (a.ndim - 1))
    full = lambda a: pl.BlockSpec(a.shape, lambda b, pt: (0,) * a.ndim)
    grid_spec = pltpu.PrefetchScalarGridSpec(
        num_scalar_prefetch=1,
        grid=(nb,),
        in_specs=[per_b(qs), per_b(q), per_b(gl), per_b(new_rows), per_b(new_cols), per_b(kvc),
                  pl.BlockSpec(memory_space=pl.ANY), per_b(win_t),
                  full(cbs), full(sbz), full(wb), full(b0), full(a_s)],
        out_specs=[pl.BlockSpec((1, N_KV_HEADS, SUB, HEAD_DIM), lambda b, pt: (b, 0, 0, 0)), per_b(win_t)],
        scratch_shapes=[pltpu.VMEM((N_KV_HEADS * n_pick, HEAD_DIM, page), F32),
                        pltpu.VMEM((N_KV_HEADS * n_pick, HEAD_DIM, page), F32),
                        pltpu.SemaphoreType.DMA((1,)),
                        pltpu.SMEM((N_KV_HEADS * n_pick,), jnp.int32)])
    o, win_out = pl.pallas_call(
        functools.partial(_nsa_sample_kernel, past=past, page=page, row_base=row_base, n_pick=n_pick),
        grid_spec=grid_spec,
        out_shape=[jax.ShapeDtypeStruct((nb, N_KV_HEADS, SUB, HEAD_DIM), F32),
                   jax.ShapeDtypeStruct(win_t.shape, F32)],
        compiler_params=_cparams(("arbitrary",)),
    )(page_table, qs, q, gl, new_rows, new_cols, kvc, cache_t, win_t, cbs, sbz, wb, b0, a_s)
    return o[:, :, :GROUP].reshape(nb, N_Q_COLS), win_out


MXU_WIDTH = 256
SSM_COL_TILE = 5 * MXU_WIDTH
NP_SSM = -(-(SSM_CONV_DIM + SSM_D_INNER + SSM_HEADS) // SSM_COL_TILE) * SSM_COL_TILE
N_BC = SSM_GROUPS * SSM_STATE
HEADS_PER_GROUP = SSM_HEADS // SSM_GROUPS
PAIR = LANES // SSM_HEAD_DIM
N_PAIRS = SSM_HEADS // PAIR
CONV_PAD = 8


def _softplus(x):
    return jnp.maximum(x, 0.0) + jnp.log1p(jnp.exp(-jnp.abs(x)))


def _cumsum_rows(x):
    n = x.shape[0]
    row = lax.broadcasted_iota(jnp.int32, x.shape, 0)
    s = 1
    while s < n:
        x = x + jnp.where(row >= s, pltpu.roll(x, s, axis=0), 0.0)
        s *= 2
    return x


def _ssd_kernel(xbc_ref, z_ref, dt_ref, h0_ref, cinit_ref, cw_ref, cb_ref, dtb_ref, alog_ref, dsk_ref, ng_ref,
                y_ref, hout_ref, xs_ref, act_ref, ybuf_ref, h_ref, *, nc):
    c = pl.program_id(1)
    q = SSM_CHUNK

    @pl.when(c == 0)
    def _():
        xs_ref[0:CONV_PAD] = cinit_ref[0]
        h_ref[...] = h0_ref[0]

    xs_ref[CONV_PAD:CONV_PAD + q] = xbc_ref[...]
    conv = cb_ref[...] + cw_ref[0:1] * xs_ref[CONV_PAD - 3:CONV_PAD - 3 + q]
    for k in range(1, SSM_CONV):
        conv = conv + cw_ref[k:k + 1] * xs_ref[CONV_PAD - 3 + k:CONV_PAD - 3 + k + q]
    xs_ref[0:CONV_PAD] = xs_ref[q:q + CONV_PAD]
    act_ref[...] = _silu(conv)

    dt = _softplus(dt_ref[...] + dtb_ref[...])
    acum = _cumsum_rows(dt * (-jnp.exp(alog_ref[...])))
    acum_t = acum.T
    dt_t = dt.T
    last = acum[q - 1:q, :]
    ea = jnp.exp(acum)
    te = jnp.exp(last - acum) * dt
    cd = jnp.exp(last)
    ii = lax.broadcasted_iota(jnp.int32, (q, q), 0)
    jj = lax.broadcasted_iota(jnp.int32, (q, q), 1)
    tri = ii >= jj
    low = jj < SSM_HEAD_DIM
    low_rows = ii < SSM_HEAD_DIM

    def col(a, h):
        return a[:, h:h + 1]

    for g in range(SSM_GROUPS):
        bg = act_ref[:, SSM_D_INNER + g * SSM_STATE:SSM_D_INNER + (g + 1) * SSM_STATE].astype(BF16)
        cg = act_ref[:, SSM_D_INNER + N_BC + g * SSM_STATE:SSM_D_INNER + N_BC + (g + 1) * SSM_STATE].astype(BF16)
        cbg = lax.dot_general(cg, bg, _NT, preferred_element_type=F32)
        for pr in range(HEADS_PER_GROUP // PAIR):
            k = g * (HEADS_PER_GROUP // PAIR) + pr
            ha, hb = PAIR * k, PAIR * k + 1
            xp = act_ref[:, k * LANES:(k + 1) * LANES]
            xpb = xp.astype(BF16)
            ys = []
            for h in (ha, hb):
                decay = jnp.exp(jnp.where(tri, col(acum, h) - acum_t[h:h + 1, :], NEG))
                w = cbg * decay * dt_t[h:h + 1, :]
                ys.append(jnp.dot(w.astype(BF16), xpb, preferred_element_type=F32))
            y = jnp.where(low, ys[0], ys[1])
            xs_pair = xp * jnp.where(low, col(te, ha), col(te, hb))
            st = jnp.dot(xs_pair.T.astype(BF16), bg, preferred_element_type=F32)
            hprev = h_ref[k]
            yoff = lax.dot_general(cg, hprev.astype(BF16), _NT, preferred_element_type=F32)
            y = y + yoff * jnp.where(low, col(ea, ha), col(ea, hb)) + dsk_ref[:, k * LANES:(k + 1) * LANES] * xp
            h_ref[k] = hprev * jnp.where(low_rows, cd[:, ha:ha + 1], cd[:, hb:hb + 1]) + st
            ybuf_ref[:, k * LANES:(k + 1) * LANES] = y

    yz = ybuf_ref[...] * _silu(z_ref[...])
    gw = SSM_D_INNER // SSM_GROUPS
    outs = []
    for g in range(SSM_GROUPS):
        seg = yz[:, g * gw:(g + 1) * gw]
        outs.append(seg * lax.rsqrt(jnp.mean(seg * seg, axis=-1, keepdims=True) + EPS))
    y_ref[...] = (jnp.concatenate(outs, axis=1) * ng_ref[...]).astype(y_ref.dtype)

    @pl.when(c == nc - 1)
    def _():
        hout_ref[0] = h_ref[...]


def _ssd(proj, h0, cinit, sw, nb, t):
    q = SSM_CHUNK
    nc = t // q
    vec = lambda a: pl.BlockSpec(a.shape, lambda b, c: (0, 0))
    return pl.pallas_call(
        functools.partial(_ssd_kernel, nc=nc),
        grid=(nb, nc),
        in_specs=[pl.BlockSpec((q, SSM_CONV_DIM), lambda b, c: (b * nc + c, 0)),
                  pl.BlockSpec((q, SSM_D_INNER), lambda b, c: (b * nc + c, SSM_CONV_DIM // SSM_D_INNER)),
                  pl.BlockSpec((q, LANES), lambda b, c: (b * nc + c, (SSM_CONV_DIM + SSM_D_INNER) // LANES)),
                  pl.BlockSpec((1,) + h0.shape[1:], lambda b, c: (b, 0, 0, 0)),
                  pl.BlockSpec((1,) + cinit.shape[1:], lambda b, c: (b, 0, 0)),
                  vec(sw['conv_w']), vec(sw['conv_b']), vec(sw['dt_bias']), vec(sw['a_log']), vec(sw['d_lane']),
                  vec(sw['norm_g'])],
        out_specs=[pl.BlockSpec((q, SSM_D_INNER), lambda b, c: (b * nc + c, 0)),
                   pl.BlockSpec((1,) + h0.shape[1:], lambda b, c: (b, 0, 0, 0))],
        out_shape=[jax.ShapeDtypeStruct((nb * t, SSM_D_INNER), BF16),
                   jax.ShapeDtypeStruct(h0.shape, F32)],
        scratch_shapes=[pltpu.VMEM((CONV_PAD + q, SSM_CONV_DIM), F32),
                        pltpu.VMEM((q, SSM_CONV_DIM), F32),
                        pltpu.VMEM((q, SSM_D_INNER), F32),
                        pltpu.VMEM(h0.shape[1:], F32)],
        compiler_params=_cparams(("arbitrary", "arbitrary")),
    )(proj, proj, proj, h0, cinit, sw['conv_w'], sw['conv_b'], sw['dt_bias'], sw['a_log'], sw['d_lane'],
      sw['norm_g'])


def _ssm_sample_step(proj_s, state_ssm, state_conv, sw):
    nb = proj_s.shape[0]
    q = SSM_CHUNK
    xbc_new = proj_s[:, :SSM_CONV_DIM]
    rows = jnp.zeros((nb, q, NP_SSM), F32)
    rows = rows.at[:, :, SSM_CONV_DIM + SSM_D_INNER:].set(NEG)
    rows = rows.at[:, q - SSM_CONV:q - 1, :SSM_CONV_DIM].set(state_conv.astype(F32))
    rows = rows.at[:, q - 1].set(proj_s)
    h0 = state_ssm.astype(F32).reshape(nb, N_PAIRS, LANES, SSM_STATE)
    cinit = jnp.zeros((nb, CONV_PAD, SSM_CONV_DIM), F32)
    yn, hfin = _ssd(rows.reshape(nb * q, NP_SSM), h0, cinit, sw, nb, q)
    conv_new = jnp.concatenate([state_conv[:, 1:].astype(F32), xbc_new[:, None]], axis=1)
    return yn.reshape(nb, q, SSM_D_INNER)[:, q - 1], hfin.reshape(state_ssm.shape), conv_new


def _ssm_weights(w_in, conv_w, conv_b, dt_bias, a_log, d_skip, norm_g):
    z_w = w_in[:, :SSM_D_INNER]
    xbc_w = w_in[:, SSM_D_INNER:SSM_D_INNER + SSM_CONV_DIM]
    dt_w = w_in[:, SSM_D_INNER + SSM_CONV_DIM:]
    pad = NP_SSM - w_in.shape[1]
    w = jnp.concatenate([xbc_w, z_w, dt_w, jnp.zeros((w_in.shape[0], pad), w_in.dtype)], axis=1).astype(BF16)
    lane_pad = lambda v: jnp.zeros((1, LANES), F32).at[0, :SSM_HEADS].set(v.astype(F32))
    return dict(w_in=w, conv_w=conv_w.astype(F32), conv_b=conv_b.astype(F32).reshape(1, -1),
                dt_bias=lane_pad(dt_bias), a_log=lane_pad(a_log),
                d_lane=jnp.repeat(d_skip.astype(F32), SSM_HEAD_DIM).reshape(1, -1),
                norm_g=norm_g.astype(F32).reshape(1, -1))


ROW_TILE = 512
MLP_ROW_TILE = 1024
FF_TILE = 1024


def kernel(x_prompt, x_sample, cache_kv, cache_win, state_ssm, state_conv, page_table, c_prompt, c_sample, rel_bias,
           ada_w, ada_b, norm_g, mlp_w1, mlp_w2, attn_w_in, attn_w_out, cmp_pos, cmp_w1, cmp_w2, ssm_w_in,
           ssm_conv_w, ssm_conv_b, ssm_dt_bias, ssm_a_log, ssm_d, ssm_norm_g, ssm_w_out):
    nb, t, d = x_prompt.shape
    db = x_sample.shape[0]
    assert x_sample.shape[1] == 1 and t % SSM_CHUNK == 0 and t % Q_BLOCK == 0
    n_pool, page = cache_kv.shape[1], cache_kv.shape[2]
    past = page_table.shape[1] * page
    depth = ada_w.shape[0]
    tm = _row_tile(t, ROW_TILE)

    xp = x_prompt.reshape(nb * t, d).astype(F32)
    xs = x_sample.reshape(db, d).astype(F32)
    c_all = jnp.concatenate([c_prompt, c_sample], axis=0).astype(F32)
    c_all = jnp.pad(c_all, ((0, (-c_all.shape[0]) % SUB), (0, 0)))
    bvz = _bias_vector(rel_bias)
    cache_t = _cache_tiles(cache_kv)
    win_t = jnp.transpose(cache_win, (0, 1, 3, 4, 5, 2))

    kv_p, win_p, ssm_p, conv_p, kv_s, win_s, ssm_s, conv_s = ([] for _ in range(8))
    for i in range(depth):
        mod = _ada(c_all, ada_w[i], ada_b[i])
        mp = [mod[:nb, j * d:(j + 1) * d].reshape(nb, 1, d) for j in range(N_MOD)]
        ms = [mod[nb:nb + db, j * d:(j + 1) * d].reshape(1, db, d) for j in range(N_MOD)]
        g = norm_g[i].astype(F32)
        if i % 2 == 0:
            a = i // 2
            w_in = _attn_in_weights(attn_w_in[a])
            w_out = attn_w_out[a].astype(BF16)
            cw = _compress_weights(cmp_pos[a], cmp_w1[a], cmp_w2[a], [(0, 1)])
            o_p, kv_new, win_new_p = _nsa_prompt_mixer(xp, g[0], mp[0], mp[1], w_in, cw, rel_bias, nb, t)
            kv_p.append(kv_new)
            win_p.append(win_new_p)
            xp = _heads_out(o_p, _attn_out_weights(attn_w_out[a]), g[1], mp[2], xp, tm)

            proj_s = _nm_matmul(xs, g[0], ms[0], ms[1], w_in, db, NP_ATTN)
            cw_s = _compress_weights(cmp_pos[a], cmp_w1[a], cmp_w2[a], [(0, 0), (1, 1)])
            kvc_s = _pack_cmp(_compress_sample(page_table, cache_t, cw_s, page, a * n_pool))
            o_s, win_new = _nsa_sample(proj_s, kvc_s, page_table, cache_t, win_t[a], bvz, past, page, a * n_pool)
            kv_s.append(_unpermute_kv(proj_s[:, N_Q_COLS:GATE_COL0], (db, 1))[:, :, :N_KV_SLOTS])
            win_s.append(jnp.transpose(win_new, (0, 4, 1, 2, 3)))
            xs = _mm_norm_res(o_s, w_out, g[1], ms[2], xs, db)
        else:
            m = i // 2
            sw = _ssm_weights(ssm_w_in[m], ssm_conv_w[m], ssm_conv_b[m], ssm_dt_bias[m], ssm_a_log[m], ssm_d[m],
                              ssm_norm_g[m])
            w_out = ssm_w_out[m].astype(BF16)
            proj = _nm_matmul(xp, g[0], mp[0], mp[1], sw['w_in'], tm, SSM_COL_TILE)
            h0 = jnp.zeros((nb, N_PAIRS, LANES, SSM_STATE), F32)
            cinit = jnp.zeros((nb, CONV_PAD, SSM_CONV_DIM), F32)
            yn, hfin = _ssd(proj, h0, cinit, sw, nb, t)
            ssm_p.append(hfin.reshape(nb, SSM_HEADS, SSM_HEAD_DIM, SSM_STATE).astype(state_ssm.dtype))
            conv_p.append(proj.reshape(nb, t, NP_SSM)[:, t - (SSM_CONV - 1):, :SSM_CONV_DIM])
            xp = _mm_norm_res(yn, w_out, g[1], mp[2], xp, tm)

            proj_s = _nm_matmul(xs, g[0], ms[0], ms[1], sw['w_in'], db, SSM_COL_TILE)
            yn_s, h_s, conv_new = _ssm_sample_step(proj_s, state_ssm[m], state_conv[m], sw)
            ssm_s.append(h_s.astype(state_ssm.dtype))
            conv_s.append(conv_new)
            xs = _mm_norm_res(yn_s, w_out, g[1], ms[2], xs, db)
        w1 = mlp_w1[i].astype(BF16)
        w2 = mlp_w2[i].astype(BF16)
        xp = _mlp(xp, g[2], mp[3], mp[4], w1, w2, g[3], mp[5], _row_tile(t, MLP_ROW_TILE), FF_TILE)
        xs = _mlp(xs, g[2], ms[3], ms[4], w1, w2, g[3], ms[5], db, FF_TILE)
    return (xp.reshape(nb, t, d), xs.reshape(db, 1, d), jnp.stack(kv_p), jnp.stack(win_p), jnp.stack(ssm_p),
            jnp.stack(conv_p), jnp.stack(kv_s), jnp.stack(win_s), jnp.stack(ssm_s), jnp.stack(conv_s))
```

```python
import functools
import math

import numpy as np
import jax
import jax.numpy as jnp
from jax import lax
from jax.experimental import pallas as pl
from jax.experimental.pallas import tpu as pltpu

F32 = jnp.float32
BF16 = jnp.bfloat16

D_MODEL = 1024
N_HEADS = 16
HEAD_DIM = 64
N_KV_HEADS = 4
GROUP = N_HEADS // N_KV_HEADS
CMP_BLOCK = 32
CMP_STRIDE = 16
CMP_RATIO = CMP_BLOCK // CMP_STRIDE
CMP_HIDDEN = 2 * HEAD_DIM
SEL_BLOCK = 64
SEL_RATIO = SEL_BLOCK // CMP_STRIDE
N_SELECT = 16
WINDOW = 512
FORCE_BONUS = 1e4
Q_BLOCK = 128
N_KV_SLOTS = 4
N_WIN_SLOTS = 2
N_Q_COLS = N_HEADS * HEAD_DIM
N_KV_COLS = (N_KV_SLOTS + N_WIN_SLOTS) * N_KV_HEADS * HEAD_DIM
N_GATE_COLS = 3 * N_HEADS
KV_ROW = N_KV_HEADS * HEAD_DIM
KV_HEAD_COLS = (N_KV_SLOTS + N_WIN_SLOTS) * HEAD_DIM
N_BUCKETS = 32
MAX_DISTANCE = 128
SSM_D_INNER = 2 * D_MODEL
SSM_HEAD_DIM = 64
SSM_HEADS = SSM_D_INNER // SSM_HEAD_DIM
SSM_GROUPS = 8
SSM_STATE = 128
SSM_CONV = 4
SSM_CONV_DIM = SSM_D_INNER + 2 * SSM_GROUPS * SSM_STATE
SSM_CHUNK = 128
D_FF = 4 * D_MODEL
N_MOD = 6
EPS = 1e-6

LANES = 128
NEG = -1e30
LOG2E = math.log2(math.e)
VMEM_LIMIT = 56 * 1024 * 1024
NEAR = 2 * LANES

_NT = (((1,), (1,)), ((), ()))


def _cparams(sem):
    return pltpu.CompilerParams(dimension_semantics=sem, vmem_limit_bytes=VMEM_LIMIT)


def _rms(x, g):
    return x * lax.rsqrt(jnp.mean(x * x, axis=-1, keepdims=True) + EPS) * g


def _silu(x):
    return x * jax.nn.sigmoid(x)


def _split_bf16(x, n):
    parts = []
    for _ in range(n - 1):
        p = x.astype(BF16)
        parts.append(p)
        x = x - p.astype(F32)
    parts.append(x.astype(BF16))
    return parts


def _ada_kernel(c_ref, w_ref, b_ref, o_ref):
    s = _silu(c_ref[...]).astype(BF16)
    o_ref[...] = jnp.dot(s, w_ref[...].astype(BF16), preferred_element_type=F32) + b_ref[...]


def _ada(c, w, b, tn=1024):
    m, d = c.shape
    n = w.shape[1]
    return pl.pallas_call(
        _ada_kernel,
        grid=(n // tn,),
        in_specs=[pl.BlockSpec((m, d), lambda j: (0, 0)),
                  pl.BlockSpec((d, tn), lambda j: (0, j)),
                  pl.BlockSpec((1, tn), lambda j: (0, j))],
        out_specs=pl.BlockSpec((m, tn), lambda j: (0, j)),
        out_shape=jax.ShapeDtypeStruct((m, n), F32),
        compiler_params=_cparams(("arbitrary",)),
    )(c, w, b.reshape(1, n))


def _nm_matmul_kernel(x_ref, g_ref, sh_ref, sc_ref, w_ref, o_ref, h_ref):
    @pl.when(pl.program_id(1) == 0)
    def _():
        h = _rms(x_ref[...], g_ref[...]) * (1.0 + sc_ref[0]) + sh_ref[0]
        h_ref[...] = h.astype(BF16)

    o_ref[...] = jnp.dot(h_ref[...], w_ref[...], preferred_element_type=F32)


def _nm_matmul(x, g, shift, scale, w, tm, tn):
    m, d = x.shape
    n = w.shape[1]
    nb = shift.shape[0]
    tpb = m // nb // tm
    mod_spec = pl.BlockSpec((1,) + shift.shape[1:], lambda i, j: (i // tpb, 0, 0))
    return pl.pallas_call(
        _nm_matmul_kernel,
        grid=(m // tm, n // tn),
        in_specs=[pl.BlockSpec((tm, d), lambda i, j: (i, 0)),
                  pl.BlockSpec((1, d), lambda i, j: (0, 0)),
                  mod_spec, mod_spec,
                  pl.BlockSpec((d, tn), lambda i, j: (0, j))],
        out_specs=pl.BlockSpec((tm, tn), lambda i, j: (i, j)),
        out_shape=jax.ShapeDtypeStruct((m, n), F32),
        scratch_shapes=[pltpu.VMEM((tm, d), BF16)],
        compiler_params=_cparams(("arbitrary", "arbitrary")),
    )(x, g.reshape(1, d), shift, scale, w)


def _mm_norm_res_kernel(a_ref, w_ref, g_ref, gate_ref, x_ref, o_ref):
    y = jnp.dot(a_ref[...].astype(BF16), w_ref[...], preferred_element_type=F32)
    o_ref[...] = x_ref[...] + gate_ref[0] * _rms(y, g_ref[...])


def _mm_norm_res(a, w, g, gate, x, tm):
    m, k = a.shape
    d = w.shape[1]
    nb = gate.shape[0]
    tpb = m // nb // tm
    return pl.pallas_call(
        _mm_norm_res_kernel,
        grid=(m // tm,),
        in_specs=[pl.BlockSpec((tm, k), lambda i: (i, 0)),
                  pl.BlockSpec((k, d), lambda i: (0, 0)),
                  pl.BlockSpec((1, d), lambda i: (0, 0)),
                  pl.BlockSpec((1,) + gate.shape[1:], lambda i: (i // tpb, 0, 0)),
                  pl.BlockSpec((tm, d), lambda i: (i, 0))],
        out_specs=pl.BlockSpec((tm, d), lambda i: (i, 0)),
        out_shape=jax.ShapeDtypeStruct((m, d), F32),
        compiler_params=_cparams(("arbitrary",)),
    )(a, w, g.reshape(1, d), gate, x)


def _mlp_kernel(x_ref, g1_ref, sh_ref, sc_ref, w1_ref, w2_ref, g2_ref, gate_ref, o_ref, h_ref, acc_ref):
    c = pl.program_id(1)

    @pl.when(c == 0)
    def _():
        h = _rms(x_ref[...], g1_ref[...]) * (1.0 + sc_ref[0]) + sh_ref[0]
        h_ref[...] = h.astype(BF16)
        acc_ref[...] = jnp.zeros_like(acc_ref)

    a = jnp.dot(h_ref[...], w1_ref[...], preferred_element_type=F32)
    a = jnp.square(jnp.maximum(a, 0.0)).astype(BF16)
    acc_ref[...] += jnp.dot(a, w2_ref[...], preferred_element_type=F32)

    @pl.when(c == pl.num_programs(1) - 1)
    def _():
        o_ref[...] = x_ref[...] + gate_ref[0] * _rms(acc_ref[...], g2_ref[...])


def _mlp(x, g1, shift, scale, w1, w2, g2, gate, tm, tf):
    m, d = x.shape
    f = w1.shape[1]
    nb = shift.shape[0]
    tpb = m // nb // tm
    mod_spec = pl.BlockSpec((1,) + shift.shape[1:], lambda i, c: (i // tpb, 0, 0))
    vec_spec = pl.BlockSpec((1, d), lambda i, c: (0, 0))
    return pl.pallas_call(
        _mlp_kernel,
        grid=(m // tm, f // tf),
        in_specs=[pl.BlockSpec((tm, d), lambda i, c: (i, 0)), vec_spec, mod_spec, mod_spec,
                  pl.BlockSpec((d, tf), lambda i, c: (0, c)),
                  pl.BlockSpec((tf, d), lambda i, c: (c, 0)),
                  vec_spec, mod_spec],
        out_specs=pl.BlockSpec((tm, d), lambda i, c: (i, 0)),
        out_shape=jax.ShapeDtypeStruct((m, d), F32),
        scratch_shapes=[pltpu.VMEM((tm, d), BF16), pltpu.VMEM((tm, d), F32)],
        compiler_params=_cparams(("arbitrary", "arbitrary")),
    )(x, g1.reshape(1, d), shift, scale, w1, w2, g2.reshape(1, d), gate)


def _bias_kernel(oh_ref, t_ref, o_ref):
    t = t_ref[...]
    t = t - t[N_BUCKETS - 1:N_BUCKETS, :]
    oh = oh_ref[...]
    o_ref[...] = sum(jnp.dot(oh, p, preferred_element_type=F32) for p in _split_bf16(t, 3))


def _rel_bucket_np(dist):
    n = np.maximum(dist, 0)
    exact = N_BUCKETS // 2
    nf = np.maximum(n, 1).astype(np.float32)
    large = exact + (np.log(nf / exact) / math.log(MAX_DISTANCE / exact) * (N_BUCKETS - exact)).astype(np.int32)
    return np.where(n < exact, n, np.minimum(large, N_BUCKETS - 1))


def _bias_vector(rel_bias):
    assert _rel_bucket_np(np.array([NEAR - 1]))[0] == N_BUCKETS - 1 == _rel_bucket_np(np.array([MAX_DISTANCE]))[0]
    oh = np.zeros((NEAR + 8, LANES), np.float32)
    oh[np.arange(NEAR), _rel_bucket_np(np.arange(NEAR))] = 1.0
    oh[NEAR:, N_BUCKETS - 1] = 1.0
    table = jnp.zeros((LANES, LANES), F32).at[:N_BUCKETS, :N_HEADS].set(rel_bias.astype(F32))
    bv = pl.pallas_call(
        _bias_kernel,
        out_shape=jax.ShapeDtypeStruct((NEAR + 8, LANES), F32),
    )(jnp.asarray(oh, BF16), table)
    return bv[:NEAR + 1, :N_HEADS]


SUB = 8
HEADS_PER_CG = LANES // HEAD_DIM
N_CG = 2 * KV_ROW // LANES
CG_PER_SLOT = KV_ROW // LANES
CG_HIDDEN = HEADS_PER_CG * CMP_HIDDEN


def _compress_parts(rows_s, wbd_ref, nh):
    part0 = jnp.zeros((nh, CG_HIDDEN), F32)
    part1 = jnp.zeros((nh, CG_HIDDEN), F32)
    for s in range(0, CMP_STRIDE, 2):
        xs = jnp.concatenate([rows_s(s), rows_s(s + 1)], axis=1).astype(BF16)
        part0 = part0 + jnp.dot(xs, wbd_ref[0, 0, s // 2], preferred_element_type=F32)
        part1 = part1 + jnp.dot(xs, wbd_ref[0, 1, s // 2], preferred_element_type=F32)
    return part0, part1


def _compress_finish(part0, part1, pos_ref, w1_ref, w2bd_ref, nh):
    pre0 = jnp.dot(pos_ref[0], w1_ref[0], preferred_element_type=F32)[0:1]
    pre = pre0 + part0 + pltpu.roll(part1, nh - 1, axis=0)
    return jnp.dot(_silu(pre).astype(BF16), w2bd_ref[0], preferred_element_type=F32)


def _compress_kernel(x_ref, pos_ref, w1_ref, wbd_ref, w2bd_ref, o_ref, *, nh):
    part0, part1 = _compress_parts(lambda s: x_ref[pl.ds(s, nh, stride=CMP_STRIDE), :], wbd_ref, nh)
    o_ref[0, 0] = _compress_finish(part0, part1, pos_ref, w1_ref, w2bd_ref, nh)


def _block_diag(a, b):
    za = jnp.zeros(a.shape[:-1] + (b.shape[-1],), a.dtype)
    zb = jnp.zeros(b.shape[:-1] + (a.shape[-1],), b.dtype)
    return jnp.concatenate([jnp.concatenate([a, za], axis=-1), jnp.concatenate([zb, b], axis=-1)], axis=-2)


def _compress_weight_set(pos, w1, w2, a, b):
    w1r = w1.reshape(2, CMP_RATIO, CMP_STRIDE, HEAD_DIM, CMP_HIDDEN)
    wbd = _block_diag(w1r[a], w1r[b]).reshape(CMP_RATIO, CMP_STRIDE // 2, 2 * LANES, CG_HIDDEN)
    posx = jnp.zeros((SUB, 2 * CMP_BLOCK * HEAD_DIM), F32).at[0].set(
        jnp.concatenate([pos[a].reshape(-1), pos[b].reshape(-1)]))
    return [x.astype(BF16) for x in (posx, _block_diag(w1[a], w1[b]), wbd, _block_diag(w2[a], w2[b]))]


def _compress_weights(cmp_pos, cmp_w1, cmp_w2, pairs):
    sets = [_compress_weight_set(cmp_pos, cmp_w1, cmp_w2, a, b) for a, b in pairs]
    return [jnp.stack(x) for x in zip(*sets)]


def _compress_specs(cw, imap):
    def spec(a):
        return pl.BlockSpec((1,) + a.shape[1:], lambda *g: (imap(*g),) + (0,) * (a.ndim - 1))
    return [spec(a) for a in cw]


def _compress_prompt(proj, cw, nb, t):
    nh = t // CMP_STRIDE
    return pl.pallas_call(
        functools.partial(_compress_kernel, nh=nh),
        grid=(nb, N_KV_HEADS),
        in_specs=[pl.BlockSpec((t, LANES), lambda b, k: (b, (N_Q_COLS + k * KV_HEAD_COLS) // LANES))]
        + _compress_specs(cw, lambda b, k: 0),
        out_specs=pl.BlockSpec((1, 1, nh, LANES), lambda b, k: (b, k, 0, 0)),
        out_shape=jax.ShapeDtypeStruct((nb, N_KV_HEADS, nh, LANES), F32),
        compiler_params=_cparams(("arbitrary", "arbitrary")),
    )(proj, *cw)


KEY_TILE = 4 * Q_BLOCK
FRONT_PAD = KEY_TILE
SUBTILES = KEY_TILE // Q_BLOCK
BAND_OFF = 2 * Q_BLOCK // CMP_STRIDE
BAND_W = 32
Q_PER_CMP = Q_BLOCK // CMP_STRIDE


def _nsa_consts(t):
    ncp = t // CMP_STRIDE
    nblk = t // SEL_BLOCK
    nbp = -(-nblk // LANES) * LANES
    at = np.zeros((nbp, ncp), np.float32)
    for j in range(nblk):
        for n in range(SEL_RATIO * j - (CMP_RATIO - 1), SEL_RATIO * (j + 1)):
            if 0 <= n < ncp - CMP_RATIO + 1:
                at[j, n] = 1.0
    key_blk = np.arange(t) // SEL_BLOCK
    e = (np.arange(nbp)[:, None] == key_blk[None, :]).astype(np.float32)
    e1 = e.reshape(nbp, t // LANES, LANES).transpose(1, 0, 2)
    e1 = np.concatenate([np.zeros((SUBTILES,) + e1.shape[1:], np.float32), e1], axis=0)
    band = np.zeros((2 * ncp + 2 * Q_PER_CMP, LANES), np.float32)
    u = np.arange(BAND_W)
    band[u + ncp, HEAD_DIM + u] = 1.0
    band[u + ncp, HEAD_DIM + BAND_W + u] = 1.0
    r = np.arange(Q_BLOCK)[:, None]
    c = np.arange(LANES)[None, :]
    d0 = r - c
    idx0 = np.where(d0 >= 0, d0, NEAR)
    idx1 = Q_BLOCK + r - c
    dc = r - CMP_STRIDE * c + (CMP_STRIDE * BAND_OFF - CMP_BLOCK + 1)
    idxc = np.where((dc >= 0) & (dc < NEAR), dc, NEAR)
    assert np.all(idxc[:, BAND_W:] == NEAR)
    bf = lambda a: jnp.asarray(a, BF16)
    return dict(at=bf(at), e1=bf(e1), band=jnp.asarray(band),
                idx0=idx0, idx1=idx1, idxc=idxc, ncp=ncp, nbp=nbp, nblk=nblk)


def _nsa_kernel(q_ref, gl_ref, kck_ref, kvc_ref, kvs_ref, kvw_ref, at_ref, e1_ref, tb_ref, cbq_ref, band_ref,
                o_ref, *, ncp, nbp, n_sel):
    i = pl.program_id(2)
    rows = GROUP * Q_BLOCK
    lane = lax.broadcasted_iota(jnp.int32, (1, LANES), 1)
    low = lane < HEAD_DIM
    r_col = lax.broadcasted_iota(jnp.int32, (rows, 1), 0) % Q_BLOCK
    q_pos = i * Q_BLOCK + r_col

    q = q_ref[...] * (HEAD_DIM ** -0.5 * LOG2E)
    parts = []
    for h in range(GROUP * HEAD_DIM // LANES):
        qh = q[:, h * LANES:(h + 1) * LANES]
        parts.append(jnp.where(low, qh, 0.0))
        parts.append(jnp.where(low, pltpu.roll(qh, HEAD_DIM, axis=1), 0.0))
    qs = jnp.concatenate(parts, axis=0).astype(BF16)

    qc = jnp.where(low, qs, cbq_ref[0])
    start = pl.multiple_of(ncp + BAND_OFF - Q_PER_CMP * i, 8)
    kq = (kck_ref[0, 0] + band_ref[pl.ds(start, ncp), :]).astype(BF16)
    s = lax.dot_general(qc, kq, _NT, preferred_element_type=F32)
    cmp_end = lax.broadcasted_iota(jnp.int32, (1, ncp), 1) * CMP_STRIDE + (CMP_BLOCK - 1)
    mask = cmp_end <= q_pos
    m = jnp.max(jnp.where(mask, s, NEG), axis=-1, keepdims=True)
    e = jnp.where(mask, jnp.exp2(s - m), 0.0)
    p_cmp = e * (1.0 / jnp.maximum(jnp.sum(e, axis=-1, keepdims=True), 1e-30))
    o_cmp = jnp.dot(p_cmp.astype(BF16), kvc_ref[0, 0], preferred_element_type=F32)
    imp = p_cmp[0:Q_BLOCK]
    for g in range(1, GROUP):
        imp = imp + p_cmp[g * Q_BLOCK:(g + 1) * Q_BLOCK]
    bst = sum(lax.dot_general(at_ref[...], part, _NT, preferred_element_type=F32)
              for part in _split_bf16(imp, 2))

    jb = lax.broadcasted_iota(jnp.int32, (nbp, Q_BLOCK), 0)
    q_blk = (i * Q_BLOCK + lax.broadcasted_iota(jnp.int32, (nbp, Q_BLOCK), 1)) // SEL_BLOCK
    causal = jb <= q_blk
    forced = (jb == 0) | (jb == q_blk) | (jb == q_blk - 1)
    work = jnp.where(causal, bst + jnp.where(forced, FORCE_BONUS, 0.0), -jnp.inf)
    jbf = jb.astype(F32)
    sel_t = jnp.zeros((nbp, Q_BLOCK), F32)
    for _ in range(n_sel):
        top = jnp.max(work, axis=0, keepdims=True)
        first = jnp.min(jnp.where(work == top, jbf, float(nbp)), axis=0, keepdims=True)
        hit = jbf == first
        sel_t = jnp.where(hit, 1.0, sel_t)
        work = jnp.where(hit, -jnp.inf, work)
    sel = jnp.where(causal, sel_t, 0.0).T.astype(BF16)

    def ones_and_values(kv):
        return jnp.where(low, jnp.ones_like(kv), kv)

    def normalise(acc):
        return jnp.where(low, 0.0, acc / jnp.maximum(pltpu.roll(acc, HEAD_DIM, axis=1), 1e-30))

    n_wt = WINDOW // Q_BLOCK
    kw = kvw_ref[0, 0, pl.ds(pl.multiple_of((i + SUBTILES - n_wt) * Q_BLOCK, Q_BLOCK), WINDOW + Q_BLOCK), :]
    s = lax.dot_general(qs, kw, _NT, preferred_element_type=F32)
    pieces = []
    for u in range(n_wt + 1):
        su = s[:, u * Q_BLOCK:(u + 1) * Q_BLOCK]
        if u == n_wt:
            su = jnp.where(lane <= r_col, su + tb_ref[0, 0], NEG)
        else:
            if u == n_wt - 1:
                su = su + tb_ref[0, 1]
            su = su + jnp.where(i - n_wt + u >= 0, 0.0, NEG)
            if u == 0:
                su = jnp.where(lane > r_col, su, NEG)
        pieces.append(su)
    sm = jnp.concatenate(pieces, axis=1).astype(BF16)
    e = jnp.exp2(sm - jnp.max(sm, axis=-1, keepdims=True))
    o_win = normalise(jnp.dot(e, ones_and_values(kw), preferred_element_type=F32))

    n_tiles = (i + SUBTILES) // SUBTILES

    def keys(u):
        return kvs_ref[0, 0, pl.ds(pl.multiple_of((i + 1 - SUBTILES * u) * Q_BLOCK, Q_BLOCK), KEY_TILE), :]

    def masked_scores(u, diagonal=False):
        t0p = i + 1 - SUBTILES * u
        mk = jnp.concatenate([jnp.dot(sel, e1_ref[t0p + v], preferred_element_type=F32) for v in range(SUBTILES)],
                             axis=1).astype(BF16)
        s3 = lax.dot_general(qs, keys(u), _NT, preferred_element_type=F32).reshape(GROUP, Q_BLOCK, KEY_TILE)
        if diagonal:
            r3 = lax.broadcasted_iota(jnp.int32, (1, Q_BLOCK, LANES), 1)
            c3 = lax.broadcasted_iota(jnp.int32, (1, Q_BLOCK, LANES), 2)
            tb0 = tb_ref[0, 0].reshape(GROUP, Q_BLOCK, LANES)
            tb1 = tb_ref[0, 1].reshape(GROUP, Q_BLOCK, LANES)
            s3 = jnp.concatenate([s3[:, :, :KEY_TILE - 2 * Q_BLOCK],
                                  s3[:, :, KEY_TILE - 2 * Q_BLOCK:KEY_TILE - Q_BLOCK] + tb1,
                                  jnp.where(c3 <= r3, s3[:, :, KEY_TILE - Q_BLOCK:] + tb0, NEG)], axis=2)
        return jnp.where((mk > 0.5)[None], s3.astype(BF16), NEG).reshape(rows, KEY_TILE)

    def sel_body(u, carry):
        m, acc, sm, e_prev = carry
        pv = jnp.dot(e_prev, ones_and_values(keys(jnp.maximum(u - 1, 0))), preferred_element_type=F32)
        sm_next = masked_scores(jnp.minimum(u + 1, n_tiles - 1))
        m_new = jnp.maximum(m, jnp.max(sm, axis=-1, keepdims=True).astype(F32))
        alpha = jnp.exp2(m - m_new)
        e = jnp.exp2(sm - m_new.astype(BF16))
        return m_new, alpha * (acc + pv), sm_next, e

    init = (jnp.full((rows, 1), NEG, F32), jnp.zeros((rows, LANES), F32),
            masked_scores(0, diagonal=True), jnp.zeros((rows, KEY_TILE), BF16))
    m, acc, _, e_last = lax.fori_loop(0, n_tiles, sel_body, init)
    o_sel = normalise(acc + jnp.dot(e_last, ones_and_values(keys(n_tiles - 1)), preferred_element_type=F32))

    sg = jax.nn.sigmoid(gl_ref[...])
    for g in range(GROUP):
        out = jnp.zeros((Q_BLOCK, LANES), F32)
        for j, ob in enumerate((o_cmp, o_sel, o_win)):
            out = out + sg[:, 3 * g + j:3 * g + j + 1] * ob[g * Q_BLOCK:(g + 1) * Q_BLOCK]
        o_ref[0, 0, g] = out.astype(o_ref.dtype)


GATE_COL0 = N_Q_COLS + N_KV_COLS
NP_ATTN = GATE_COL0 + N_KV_HEADS * LANES


def _nsa_prompt(proj, kck, kvc, kvs, kvw, tb, cbq, consts, nb, t):
    ni = t // Q_BLOCK
    ncp, nbp = consts['ncp'], consts['nbp']
    gcol = GATE_COL0 // LANES
    qw = GROUP * HEAD_DIM
    full = lambda a: pl.BlockSpec(a.shape, lambda b, k, i: (0,) * a.ndim)
    per_kvh = lambda a: pl.BlockSpec((1,) + a.shape[1:], lambda b, k, i: (k,) + (0,) * (a.ndim - 1))
    per_bk = lambda a: pl.BlockSpec((1, 1) + a.shape[2:], lambda b, k, i: (b, k, 0, 0))
    c = consts
    n_sel = min(N_SELECT, c['nblk'])
    assert n_sel >= 3
    return pl.pallas_call(
        functools.partial(_nsa_kernel, ncp=ncp, nbp=nbp, n_sel=n_sel),
        grid=(nb, N_KV_HEADS, ni),
        in_specs=[pl.BlockSpec((Q_BLOCK, qw), lambda b, k, i: (b * ni + i, k)),
                  pl.BlockSpec((Q_BLOCK, LANES), lambda b, k, i: (b * ni + i, gcol + k)),
                  per_bk(kck), per_bk(kvc), per_bk(kvs), per_bk(kvw),
                  full(c['at']), full(c['e1']), per_kvh(tb), per_kvh(cbq), full(c['band'])],
        out_specs=pl.BlockSpec((1, 1, GROUP, Q_BLOCK, LANES), lambda b, k, i: (b, k, 0, i, 0)),
        out_shape=jax.ShapeDtypeStruct((nb, N_KV_HEADS, GROUP, t, LANES), BF16),
        compiler_params=_cparams(("arbitrary", "arbitrary", "arbitrary")),
    )(proj, proj, kck, kvc, kvs, kvw, c['at'], c['e1'], tb, cbq, c['band'])


def _heads_out_kernel(a_ref, w_ref, g_ref, gate_ref, x_ref, o_ref):
    y = jnp.zeros(o_ref.shape, F32)
    for k in range(N_KV_HEADS):
        for g in range(0, GROUP, 2):
            a = jnp.concatenate([a_ref[0, k, g], a_ref[0, k, g + 1]], axis=1)
            h = k * GROUP + g
            w = w_ref[h * LANES:(h + 2) * LANES, :]
            y = y + jnp.dot(a, w, preferred_element_type=F32)
    o_ref[...] = x_ref[...] + gate_ref[0] * _rms(y, g_ref[...])


def _heads_out(a, w_pad, g, gate, x, tm):
    nb, _, _, t, _ = a.shape
    d = w_pad.shape[1]
    tpb = t // tm
    return pl.pallas_call(
        _heads_out_kernel,
        grid=(nb * tpb,),
        in_specs=[pl.BlockSpec((1, N_KV_HEADS, GROUP, tm, LANES), lambda i: (i // tpb, 0, 0, i % tpb, 0)),
                  pl.BlockSpec(w_pad.shape, lambda i: (0, 0)),
                  pl.BlockSpec((1, d), lambda i: (0, 0)),
                  pl.BlockSpec((1,) + gate.shape[1:], lambda i: (i // tpb, 0, 0)),
                  pl.BlockSpec((tm, d), lambda i: (i, 0))],
        out_specs=pl.BlockSpec((tm, d), lambda i: (i, 0)),
        out_shape=jax.ShapeDtypeStruct((nb * t, d), F32),
        compiler_params=_cparams(("arbitrary",)),
    )(a, w_pad, g.reshape(1, d), gate, x)


def _row_tile(m, cap):
    return m if m <= cap else cap


def _attn_in_weights(w_in):
    d = w_in.shape[0]
    kv = w_in[:, N_Q_COLS:GATE_COL0].reshape(d, N_KV_SLOTS + N_WIN_SLOTS, N_KV_HEADS, HEAD_DIM)
    kv = jnp.swapaxes(kv, 1, 2).reshape(d, N_KV_COLS)
    gates = w_in[:, GATE_COL0:].reshape(d, N_KV_HEADS, GROUP * 3)
    gates = jnp.pad(gates, ((0, 0), (0, 0), (0, LANES - GROUP * 3))).reshape(d, -1)
    return jnp.concatenate([w_in[:, :N_Q_COLS], kv, gates], axis=1).astype(BF16)


def _attn_out_weights(w_out):
    w = w_out.reshape(N_HEADS, HEAD_DIM, w_out.shape[1])
    return jnp.pad(w, ((0, 0), (LANES - HEAD_DIM, 0), (0, 0))).reshape(N_HEADS * LANES, -1).astype(BF16)


def _attn_in_kernel(x_ref, g_ref, sh_ref, sc_ref, w_ref, zs_ref, zw_ref, o_ref, kvs_ref, kvw_ref, kvt_ref, wint_ref):
    del zs_ref, zw_ref
    h = _rms(x_ref[...], g_ref[...]) * (1.0 + sc_ref[0]) + sh_ref[0]
    r = jnp.dot(h.astype(BF16), w_ref[...], preferred_element_type=F32)
    o_ref[...] = r
    for k in range(N_KV_HEADS):
        c0 = N_Q_COLS + k * KV_HEAD_COLS
        kvs_ref[0, k] = r[:, c0 + LANES:c0 + 2 * LANES].astype(BF16)
        kvw_ref[0, k] = r[:, c0 + 2 * LANES:c0 + 3 * LANES].astype(BF16)
        for p in range(KV_HEAD_COLS // LANES):
            tt = r[:, c0 + p * LANES:c0 + (p + 1) * LANES].T
            for j in range(2):
                slot = 2 * p + j
                dst, s0 = (kvt_ref, slot) if slot < N_KV_SLOTS else (wint_ref, slot - N_KV_SLOTS)
                dst[0, (s0 * N_KV_HEADS + k) * HEAD_DIM:(s0 * N_KV_HEADS + k + 1) * HEAD_DIM, :] = (
                    tt[j * HEAD_DIM:(j + 1) * HEAD_DIM])


def _attn_in_prompt(x, g, shift, scale, w, nb, t):
    m, d = x.shape
    tm = FRONT_PAD
    assert t % tm == 0
    tpb = t // tm
    mod_spec = pl.BlockSpec((1,) + shift.shape[1:], lambda i: (i // tpb, 0, 0))
    pad_shape = (nb, N_KV_HEADS, FRONT_PAD + t, LANES)
    pad_spec = pl.BlockSpec((1, N_KV_HEADS, tm, LANES), lambda i: (i // tpb, 0, 1 + i % tpb, 0))
    zeros = jnp.zeros(pad_shape, BF16)
    return pl.pallas_call(
        _attn_in_kernel,
        grid=(m // tm,),
        in_specs=[pl.BlockSpec((tm, d), lambda i: (i, 0)),
                  pl.BlockSpec((1, d), lambda i: (0, 0)),
                  mod_spec, mod_spec,
                  pl.BlockSpec(w.shape, lambda i: (0, 0)),
                  pl.BlockSpec(memory_space=pl.ANY), pl.BlockSpec(memory_space=pl.ANY)],
        out_specs=[pl.BlockSpec((tm, NP_ATTN), lambda i: (i, 0)), pad_spec, pad_spec,
                   pl.BlockSpec((1, N_KV_SLOTS * KV_ROW, tm), lambda i: (i // tpb, 0, i % tpb)),
                   pl.BlockSpec((1, N_WIN_SLOTS * KV_ROW, tm), lambda i: (i // tpb, 0, i % tpb))],
        out_shape=[jax.ShapeDtypeStruct((m, NP_ATTN), F32),
                   jax.ShapeDtypeStruct(pad_shape, BF16), jax.ShapeDtypeStruct(pad_shape, BF16),
                   jax.ShapeDtypeStruct((nb, N_KV_SLOTS * KV_ROW, t), F32),
                   jax.ShapeDtypeStruct((nb, N_WIN_SLOTS * KV_ROW, t), F32)],
        input_output_aliases={5: 1, 6: 2},
        compiler_params=_cparams(("arbitrary",)),
    )(x, g.reshape(1, d), shift, scale, w, zeros, zeros)


def _unpermute_kv(proj_kv, lead):
    kv = proj_kv.reshape(lead + (N_KV_HEADS, N_KV_SLOTS + N_WIN_SLOTS, HEAD_DIM))
    return jnp.swapaxes(kv, -3, -2)


def _bias_tiles_kernel(t_ref, oh_ref, o_ref):
    oh = oh_ref[...]
    o_ref[...] = sum(jnp.dot(p, oh, preferred_element_type=F32) for p in _split_bf16(t_ref[...], 3))


def _bias_tiles(rel_bias, idxs):
    n = sum(ix.size for ix in idxs)
    d = np.concatenate([ix.reshape(-1) for ix in idxs])
    oh = np.zeros((LANES, n), np.float32)
    near = d < NEAR
    oh[_rel_bucket_np(d[near]), np.nonzero(near)[0]] = 1.0
    table = (rel_bias.astype(F32) - rel_bias[N_BUCKETS - 1].astype(F32)).T * LOG2E
    table = jnp.zeros((N_HEADS, LANES), F32).at[:, :N_BUCKETS].set(table)
    tn = 8192
    assert n % tn == 0
    out = pl.pallas_call(
        _bias_tiles_kernel,
        grid=(n // tn,),
        in_specs=[pl.BlockSpec((N_HEADS, LANES), lambda j: (0, 0)), pl.BlockSpec((LANES, tn), lambda j: (0, j))],
        out_specs=pl.BlockSpec((N_HEADS, tn), lambda j: (0, j)),
        out_shape=jax.ShapeDtypeStruct((N_HEADS, n), F32),
        compiler_params=_cparams(("arbitrary",)),
    )(table, jnp.asarray(oh, BF16))
    tiles, off = [], 0
    for ix in idxs:
        r, c = ix.shape
        tiles.append(out[:, off:off + r * c].reshape(N_KV_HEADS, GROUP * r, c))
        off += r * c
    return tiles


def _nsa_prompt_mixer(xp, g, shift, scale, w_in, cw, rel_bias, nb, t):
    proj, kvs, kvw, kvt, wint = _attn_in_prompt(xp, g, shift, scale, w_in, nb, t)
    as_rows = lambda a: jnp.transpose(a.reshape(nb, -1, N_KV_HEADS, HEAD_DIM, a.shape[-1]), (0, 4, 1, 2, 3))
    kv_new = as_rows(kvt)
    win_new = as_rows(wint[:, :, t - min(WINDOW, t):])
    kck = _compress_prompt(proj, cw, nb, t)
    kvc = kck.astype(BF16)
    kck = jnp.where(jnp.arange(LANES) < HEAD_DIM, kck, 0.0)
    c = _nsa_consts(t)
    tiles = _bias_tiles(rel_bias, [c['idx0'], c['idx1'], c['idxc']])
    tb = jnp.stack(tiles[:2], axis=1)
    hi, lo = _split_bf16(tiles[2], 2)
    cbq = jnp.concatenate([jnp.zeros(hi.shape[:2] + (HEAD_DIM,), BF16), hi[..., :BAND_W], lo[..., :BAND_W]], axis=-1)
    o = _nsa_prompt(proj, kck, kvc, kvs, kvw, tb, cbq, c, nb, t)
    return o, kv_new, win_new


SC_PAGES = 32
CG_PER_PAGE = N_KV_SLOTS * KV_ROW // LANES


def _cache_tiles(cache_kv):
    page = cache_kv.shape[2]
    assert page == LANES
    return jnp.transpose(cache_kv, (0, 1, 3, 4, 5, 2)).reshape(-1, page)


def _compress_sample_kernel(pt_ref, cache_ref, pos_ref, w1_ref, wbd_ref, w2bd_ref, o_ref, tbuf_ref, rows0_ref,
                            rows1_ref, sem_ref, p0_ref, p1_ref, *, n_pages, page, row_base):
    row_bufs = (rows0_ref, rows1_ref)
    step = pl.program_id(0) * N_CG + pl.program_id(1)
    nsteps = pl.num_programs(0) * N_CG
    chunk_pages = min(SC_PAGES, n_pages // 2)
    nch = n_pages // chunk_pages
    hpc = chunk_pages * page // CMP_STRIDE
    nh = nch * hpc

    last = nsteps * nch - 1

    def copies(chunk, slot):
        chunk = jnp.minimum(chunk, last)
        step_, ch = chunk // nch, chunk % nch
        b_, c_ = step_ // N_CG, step_ % N_CG
        out = []
        for p in range(chunk_pages):
            pg = pt_ref[b_, ch * chunk_pages + p]
            out.append(pltpu.make_async_copy(
                cache_ref.at[pl.ds(((row_base + pg) * CG_PER_PAGE + c_) * LANES, LANES), :],
                tbuf_ref.at[slot, p], sem_ref.at[slot]))
        return out

    def to_rows(slot):
        for p in range(chunk_pages):
            row_bufs[slot][p * page:(p + 1) * page, :] = tbuf_ref[slot, p].T

    @pl.when(step == 0)
    def _():
        for slot in range(2):
            for cp in copies(slot, slot):
                cp.start()
        for cp in copies(0, 0):
            cp.wait()
        to_rows(0)

    for ch in range(nch):
        slot = ch % 2
        chunk = step * nch + ch
        for cp in copies(chunk + 2, slot):
            cp.start()
        for cp in copies(chunk + 1, 1 - slot):
            cp.wait()
        to_rows(1 - slot)
        rows = row_bufs[slot]
        p0, p1 = _compress_parts(lambda s: rows[pl.ds(s, hpc, stride=CMP_STRIDE), :], wbd_ref, hpc)
        p0_ref[ch * hpc:(ch + 1) * hpc] = p0
        p1_ref[ch * hpc:(ch + 1) * hpc] = p1

    @pl.when(step == nsteps - 1)
    def _():
        for cp in copies(last, (nch - 1) % 2):
            cp.wait()

    o_ref[0, 0] = _compress_finish(p0_ref[...], p1_ref[...], pos_ref, w1_ref, w2bd_ref, nh)


def _compress_sample(page_table, cache_t, cw, page, row_base):
    nb, n_pages = page_table.shape
    chunk_pages = min(SC_PAGES, n_pages // 2)
    assert n_pages % (2 * chunk_pages) == 0
    nh = n_pages * page // CMP_STRIDE
    grid_spec = pltpu.PrefetchScalarGridSpec(
        num_scalar_prefetch=1,
        grid=(nb, N_CG),
        in_specs=[pl.BlockSpec(memory_space=pl.ANY)] + _compress_specs(cw, lambda b, c, pt: c // CG_PER_SLOT),
        out_specs=pl.BlockSpec((1, 1, nh, LANES), lambda b, c, pt: (b, c, 0, 0)),
        scratch_shapes=[pltpu.VMEM((2, chunk_pages, LANES, page), F32),
                        pltpu.VMEM((chunk_pages * page, LANES), F32),
                        pltpu.VMEM((chunk_pages * page, LANES), F32),
                        pltpu.SemaphoreType.DMA((2,)),
                        pltpu.VMEM((nh, CG_HIDDEN), F32),
                        pltpu.VMEM((nh, CG_HIDDEN), F32)])
    return pl.pallas_call(
        functools.partial(_compress_sample_kernel, n_pages=n_pages, page=page, row_base=row_base),
        grid_spec=grid_spec,
        out_shape=jax.ShapeDtypeStruct((nb, N_CG, nh, LANES), F32),
        compiler_params=_cparams(("arbitrary", "arbitrary")),
    )(page_table, cache_t, *cw)


def _softmax_with_new_key(s, s_new):
    m = jnp.maximum(jnp.max(s, axis=-1, keepdims=True), s_new)
    e = jnp.exp(s - m)
    e_new = jnp.exp(s_new - m)
    inv = 1.0 / jnp.maximum(jnp.sum(e, axis=-1, keepdims=True) + e_new, 1e-30)
    return e * inv, e_new * inv


def _nsa_sample_kernel(pt_ref, qs_ref, q_ref, gl_ref, new_ref, ncol_ref, kvc_ref, cache_ref, win_ref, cbs_ref,
                       sbz_ref, wb_ref, b0_ref, as_ref, o_ref, wout_ref, gk_ref, gv_ref, sem_ref, idx_ref,
                       *, past, page, row_base, n_pick):
    b = pl.program_id(0)
    blk_per_page = page // SEL_BLOCK
    nbs = past // SEL_BLOCK
    ncs = kvc_ref.shape[2]
    row = lax.broadcasted_iota(jnp.int32, (SUB, 1), 0)

    cmp_ok = lax.broadcasted_iota(jnp.int32, (SUB, ncs), 1) * CMP_STRIDE + (CMP_BLOCK - 1) <= past
    imp = jnp.zeros((SUB, ncs), F32)
    o_cmp = []
    for k in range(N_KV_HEADS):
        half = k % HEADS_PER_CG
        ck = kvc_ref[0, k // HEADS_PER_CG].astype(BF16)
        cv = kvc_ref[0, CG_PER_SLOT + k // HEADS_PER_CG].astype(BF16)
        s = lax.dot_general(qs_ref[0, k], ck, _NT, preferred_element_type=F32) + cbs_ref[k]
        m = jnp.max(jnp.where(cmp_ok, s, NEG), axis=-1, keepdims=True)
        e = jnp.where(cmp_ok, jnp.exp(s - m), 0.0)
        p = e / jnp.maximum(jnp.sum(e, axis=-1, keepdims=True), 1e-30)
        o_cmp.append(jnp.dot(p.astype(BF16), cv, preferred_element_type=F32)[:, half * HEAD_DIM:(half + 1) * HEAD_DIM])
        imp_k = p[0:1]
        for g in range(1, GROUP):
            imp_k = imp_k + p[g:g + 1]
        imp = imp + jnp.where(row == k, imp_k, 0.0)
    bs = sum(jnp.dot(part, as_ref[...], preferred_element_type=F32) for part in _split_bf16(imp, 3))

    lane_b = lax.broadcasted_iota(jnp.int32, (SUB, nbs), 1)
    lane_f = lane_b.astype(F32)
    score = bs + jnp.where((lane_b == 0) | (lane_b == nbs - 1), FORCE_BONUS, 0.0)

    def gathers(k, r):
        j = idx_ref[k * n_pick + r]
        pg = pt_ref[b, j // blk_per_page]
        tile0 = (row_base + pg) * N_KV_SLOTS
        return [pltpu.make_async_copy(
            cache_ref.at[pl.ds(((tile0 + slot) * N_KV_HEADS + k) * HEAD_DIM, HEAD_DIM), :],
            buf.at[k * n_pick + r], sem_ref.at[0]) for slot, buf in ((2, gk_ref), (3, gv_ref))]

    for r in range(n_pick):
        m = jnp.max(score, axis=-1, keepdims=True)
        pick = jnp.min(jnp.where(score == m, lane_f, 1e9), axis=-1, keepdims=True)
        score = jnp.where(lane_f == pick, -jnp.inf, score)
        for k in range(N_KV_HEADS):
            idx_ref[k * n_pick + r] = jnp.sum(jnp.where(row == k, pick, 0.0)).astype(jnp.int32)
            for cp in gathers(k, r):
                cp.start()

    def new_row(k, j):
        return new_ref[0, k][j:j + 1].astype(BF16).astype(F32)

    tok = lax.broadcasted_iota(jnp.int32, (1, WINDOW), 1)
    o_win = []
    for k in range(N_KV_HEADS):
        for slot in range(N_WIN_SLOTS):
            shifted = pltpu.roll(win_ref[0, slot, k], WINDOW - 1, axis=1)
            wout_ref[0, slot, k] = jnp.where(tok == WINDOW - 1, ncol_ref[0, slot, k], shifted)
        q = q_ref[0, k]
        s = jnp.dot(q, win_ref[0, 0, k].astype(BF16), preferred_element_type=F32) + wb_ref[k]
        s = jnp.where(tok >= 1, s, NEG)
        s_new = jnp.sum(q.astype(F32) * new_row(k, 2), axis=-1, keepdims=True) + b0_ref[k][:, 0:1]
        p, p_new = _softmax_with_new_key(s, s_new)
        o_win.append(lax.dot_general(p.astype(BF16), win_ref[0, 1, k].astype(BF16), _NT,
                                     preferred_element_type=F32) + p_new * new_row(k, 3))

    for k in range(N_KV_HEADS):
        for r in range(n_pick):
            for cp in gathers(k, r):
                cp.wait()

    upper = lax.broadcasted_iota(jnp.int32, (1, LANES), 1) // SEL_BLOCK
    for k in range(N_KV_HEADS):
        q = q_ref[0, k]
        tiles = []
        for r in range(n_pick):
            j = idx_ref[k * n_pick + r]
            near = j - (nbs - NEAR // SEL_BLOCK)
            s = jnp.dot(q, gk_ref[k * n_pick + r].astype(BF16), preferred_element_type=F32)
            s = s + sbz_ref[k, jnp.where(near >= 0, near, NEAR // SEL_BLOCK)]
            tiles.append(jnp.where(upper == j % blk_per_page, s, NEG))
        s_new = jnp.sum(q.astype(F32) * new_row(k, 0), axis=-1, keepdims=True) + b0_ref[k][:, 0:1]
        p, p_new = _softmax_with_new_key(jnp.concatenate(tiles, axis=1), s_new)
        o_sel = p_new * new_row(k, 1)
        for r in range(n_pick):
            o_sel = o_sel + lax.dot_general(p[:, r * LANES:(r + 1) * LANES].astype(BF16),
                                            gv_ref[k * n_pick + r].astype(BF16), _NT, preferred_element_type=F32)
        g_cmp, g_sel, g_win = (jax.nn.sigmoid(gl_ref[0, k, j])[:, :HEAD_DIM] for j in range(3))
        o_ref[0, k] = g_cmp * o_cmp[k] + g_sel * o_sel + g_win * o_win[k]


def _bias_rows(bvz, idx):
    t = jnp.take(bvz, jnp.asarray(idx, jnp.int32), axis=0).reshape(len(idx), N_KV_HEADS, GROUP)
    t = jnp.transpose(t, (1, 2, 0))
    return jnp.pad(t, ((0, 0), (0, SUB - GROUP), (0, 0)))


def _nsa_sample(proj_s, kvc, page_table, cache_t, win_t, bvz, past, page, row_base):
    nb = proj_s.shape[0]
    n_pick = N_SELECT - 1
    nbs = past // SEL_BLOCK
    ncs = past // CMP_STRIDE
    n_near = NEAR // SEL_BLOCK
    blk_per_page = page // SEL_BLOCK
    assert past % page == 0 and page == LANES and nbs > n_pick + n_near and (nbs - n_near) % blk_per_page == 0
    assert win_t.shape[-1] == WINDOW < past
    q = (proj_s[:, :N_Q_COLS] * HEAD_DIM ** -0.5).astype(BF16).reshape(nb, N_KV_HEADS, GROUP, HEAD_DIM)
    q = jnp.pad(q, ((0, 0), (0, 0), (0, SUB - GROUP), (0, 0)))
    qs = jnp.stack([jnp.pad(q[:, k], ((0, 0), (0, 0), ((k % HEADS_PER_CG) * HEAD_DIM,
                                                       LANES - (k % HEADS_PER_CG + 1) * HEAD_DIM)))
                    for k in range(N_KV_HEADS)], axis=1)
    gl = proj_s[:, GATE_COL0:].reshape(nb, N_KV_HEADS, LANES)[:, :, :GROUP * 3].reshape(nb, N_KV_HEADS, GROUP, 3)
    gl = jnp.pad(jnp.transpose(gl, (0, 1, 3, 2)), ((0, 0), (0, 0), (0, 0), (0, SUB - GROUP)))
    gl = jnp.broadcast_to(gl[..., None], gl.shape + (LANES,))
    new = _unpermute_kv(proj_s[:, N_Q_COLS:GATE_COL0], (nb,))[:, 2:]
    new_rows = jnp.transpose(new, (0, 2, 1, 3))
    new_cols = new[:, 2:].reshape(nb, N_WIN_SLOTS, N_KV_HEADS, HEAD_DIM, 1)

    dc = past - (np.arange(ncs) * CMP_STRIDE + CMP_BLOCK - 1)
    cbs = _bias_rows(bvz, np.where((dc >= 0) & (dc < NEAR), dc, NEAR))
    lane = np.arange(LANES)
    sb = []
    for jj in range(n_near + 1):
        d = NEAR - SEL_BLOCK * jj - lane % SEL_BLOCK
        ok = (lane // SEL_BLOCK == jj % blk_per_page) & (jj < n_near) & (d < NEAR)
        sb.append(_bias_rows(bvz, np.where(ok, d, NEAR)))
    sbz = jnp.stack(sb, axis=1)
    dw = WINDOW - np.arange(WINDOW)
    wb = _bias_rows(bvz, np.where(dw < NEAR, dw, NEAR))
    b0 = _bias_rows(bvz, np.zeros(LANES, np.int64))
    a_s = np.zeros((ncs, nbs), np.float32)
    for j in range(nbs):
        for n in range(SEL_RATIO * j - (CMP_RATIO - 1), SEL_RATIO * (j + 1)):
            if 0 <= n < ncs - CMP_RATIO + 1:
                a_s[n, j] = 1.0
    a_s = jnp.asarray(a_s, BF16)

    per_b = lambda a: pl.BlockSpec((1,) + a.shape[1:], lambda b, pt: (b,) + (0,) * (a.ndim - 1))
    full = lambda a: pl.BlockSpec(a.shape, lambda b, pt: (0,) * a.ndim)
    grid_spec = pltpu.PrefetchScalarGridSpec(
        num_scalar_prefetch=1,
        grid=(nb,),
        in_specs=[per_b(qs), per_b(q), per_b(gl), per_b(new_rows), per_b(new_cols), per_b(kvc),
                  pl.BlockSpec(memory_space=pl.ANY), per_b(win_t),
                  full(cbs), full(sbz), full(wb), full(b0), full(a_s)],
        out_specs=[pl.BlockSpec((1, N_KV_HEADS, SUB, HEAD_DIM), lambda b, pt: (b, 0, 0, 0)), per_b(win_t)],
        scratch_shapes=[pltpu.VMEM((N_KV_HEADS * n_pick, HEAD_DIM, page), F32),
                        pltpu.VMEM((N_KV_HEADS * n_pick, HEAD_DIM, page), F32),
                        pltpu.SemaphoreType.DMA((1,)),
                        pltpu.SMEM((N_KV_HEADS * n_pick,), jnp.int32)])
    o, win_out = pl.pallas_call(
        functools.partial(_nsa_sample_kernel, past=past, page=page, row_base=row_base, n_pick=n_pick),
        grid_spec=grid_spec,
        out_shape=[jax.ShapeDtypeStruct((nb, N_KV_HEADS, SUB, HEAD_DIM), F32),
                   jax.ShapeDtypeStruct(win_t.shape, F32)],
        compiler_params=_cparams(("arbitrary",)),
    )(page_table, qs, q, gl, new_rows, new_cols, kvc, cache_t, win_t, cbs, sbz, wb, b0, a_s)
    return o[:, :, :GROUP].reshape(nb, N_Q_COLS), win_out


MXU_WIDTH = 256
SSM_COL_TILE = 5 * MXU_WIDTH
NP_SSM = -(-(SSM_CONV_DIM + SSM_D_INNER + SSM_HEADS) // SSM_COL_TILE) * SSM_COL_TILE
N_BC = SSM_GROUPS * SSM_STATE
HEADS_PER_GROUP = SSM_HEADS // SSM_GROUPS
PAIR = LANES // SSM_HEAD_DIM
N_PAIRS = SSM_HEADS // PAIR
CONV_PAD = 8


def _softplus(x):
    return jnp.maximum(x, 0.0) + jnp.log(1.0 + jnp.exp(-jnp.abs(x)))


def _cumsum_rows(x):
    n = x.shape[0]
    row = lax.broadcasted_iota(jnp.int32, x.shape, 0)
    s = 1
    while s < n:
        x = x + jnp.where(row >= s, pltpu.roll(x, s, axis=0), 0.0)
        s *= 2
    return x


def _ssd_kernel(xbc_ref, z_ref, dt_ref, h0_ref, cinit_ref, cw_ref, cb_ref, dtb_ref, alog_ref, dsk_ref, ng_ref,
                y_ref, hout_ref, xs_ref, act_ref, ybuf_ref, h_ref, *, nc):
    c = pl.program_id(1)
    q = SSM_CHUNK

    @pl.when(c == 0)
    def _():
        xs_ref[...] = cinit_ref[0]
        h_ref[...] = h0_ref[0]

    x = xbc_ref[...]
    prev = xs_ref[...]
    head_row = lax.broadcasted_iota(jnp.int32, (CONV_PAD, 1), 0)
    conv = cb_ref[...] + cw_ref[SSM_CONV - 1:SSM_CONV] * x
    for j in range(1, SSM_CONV):
        xj = pltpu.roll(x, j, axis=0)
        head = jnp.where(head_row < j, pltpu.roll(prev, j, axis=0), xj[:CONV_PAD])
        xj = jnp.concatenate([head, xj[CONV_PAD:]], axis=0)
        conv = conv + cw_ref[SSM_CONV - 1 - j:SSM_CONV - j] * xj
    xs_ref[...] = x[q - CONV_PAD:]
    act_ref[...] = _silu(conv)

    dt = _softplus(dt_ref[...] + dtb_ref[...])
    acum = _cumsum_rows(dt * (-jnp.exp(alog_ref[...])))
    acum_t = acum.T
    dt_t = dt.T
    last = acum[q - 1:q, :]
    ea = jnp.exp(acum)
    te = jnp.exp(last - acum) * dt
    cd = jnp.exp(last)
    ii = lax.broadcasted_iota(jnp.int32, (q, q), 0)
    jj = lax.broadcasted_iota(jnp.int32, (q, q), 1)
    tri = ii >= jj
    low = jj < SSM_HEAD_DIM
    low_rows = ii < SSM_HEAD_DIM

    def col(a, h):
        return a[:, h:h + 1]

    for g in range(SSM_GROUPS):
        bg = act_ref[:, SSM_D_INNER + g * SSM_STATE:SSM_D_INNER + (g + 1) * SSM_STATE].astype(BF16)
        cg = act_ref[:, SSM_D_INNER + N_BC + g * SSM_STATE:SSM_D_INNER + N_BC + (g + 1) * SSM_STATE].astype(BF16)
        cbg = lax.dot_general(cg, bg, _NT, preferred_element_type=F32)
        for pr in range(HEADS_PER_GROUP // PAIR):
            k = g * (HEADS_PER_GROUP // PAIR) + pr
            ha, hb = PAIR * k, PAIR * k + 1
            xp = act_ref[:, k * LANES:(k + 1) * LANES]
            xpb = xp.astype(BF16)
            ys = []
            for h in (ha, hb):
                decay = jnp.exp(jnp.where(tri, col(acum, h) - acum_t[h:h + 1, :], NEG))
                w = cbg * decay * dt_t[h:h + 1, :]
                ys.append(jnp.dot(w.astype(BF16), xpb, preferred_element_type=F32))
            y = jnp.where(low, ys[0], ys[1])
            xs_pair = xp * jnp.where(low, col(te, ha), col(te, hb))
            st = jnp.dot(xs_pair.T.astype(BF16), bg, preferred_element_type=F32)
            hprev = h_ref[k]
            yoff = lax.dot_general(cg, hprev.astype(BF16), _NT, preferred_element_type=F32)
            y = y + yoff * jnp.where(low, col(ea, ha), col(ea, hb)) + dsk_ref[:, k * LANES:(k + 1) * LANES] * xp
            h_ref[k] = hprev * jnp.where(low_rows, cd[:, ha:ha + 1], cd[:, hb:hb + 1]) + st
            ybuf_ref[:, k * LANES:(k + 1) * LANES] = y

    yz = ybuf_ref[...] * _silu(z_ref[...])
    gw = SSM_D_INNER // SSM_GROUPS
    outs = []
    for g in range(SSM_GROUPS):
        seg = yz[:, g * gw:(g + 1) * gw]
        outs.append(seg * lax.rsqrt(jnp.mean(seg * seg, axis=-1, keepdims=True) + EPS))
    y_ref[...] = (jnp.concatenate(outs, axis=1) * ng_ref[...]).astype(y_ref.dtype)

    @pl.when(c == nc - 1)
    def _():
        hout_ref[0] = h_ref[...]


def _ssd(proj, h0, cinit, sw, nb, t):
    q = SSM_CHUNK
    nc = t // q
    vec = lambda a: pl.BlockSpec(a.shape, lambda b, c: (0, 0))
    return pl.pallas_call(
        functools.partial(_ssd_kernel, nc=nc),
        grid=(nb, nc),
        in_specs=[pl.BlockSpec((q, SSM_CONV_DIM), lambda b, c: (b * nc + c, 0)),
                  pl.BlockSpec((q, SSM_D_INNER), lambda b, c: (b * nc + c, SSM_CONV_DIM // SSM_D_INNER)),
                  pl.BlockSpec((q, LANES), lambda b, c: (b * nc + c, (SSM_CONV_DIM + SSM_D_INNER) // LANES)),
                  pl.BlockSpec((1,) + h0.shape[1:], lambda b, c: (b, 0, 0, 0)),
                  pl.BlockSpec((1,) + cinit.shape[1:], lambda b, c: (b, 0, 0)),
                  vec(sw['conv_w']), vec(sw['conv_b']), vec(sw['dt_bias']), vec(sw['a_log']), vec(sw['d_lane']),
                  vec(sw['norm_g'])],
        out_specs=[pl.BlockSpec((q, SSM_D_INNER), lambda b, c: (b * nc + c, 0)),
                   pl.BlockSpec((1,) + h0.shape[1:], lambda b, c: (b, 0, 0, 0))],
        out_shape=[jax.ShapeDtypeStruct((nb * t, SSM_D_INNER), BF16),
                   jax.ShapeDtypeStruct(h0.shape, F32)],
        scratch_shapes=[pltpu.VMEM((CONV_PAD, SSM_CONV_DIM), F32),
                        pltpu.VMEM((q, SSM_CONV_DIM), F32),
                        pltpu.VMEM((q, SSM_D_INNER), F32),
                        pltpu.VMEM(h0.shape[1:], F32)],
        compiler_params=_cparams(("arbitrary", "arbitrary")),
    )(proj, proj, proj, h0, cinit, sw['conv_w'], sw['conv_b'], sw['dt_bias'], sw['a_log'], sw['d_lane'],
      sw['norm_g'])


def _ssm_sample_step(proj_s, state_ssm, state_conv, sw):
    nb = proj_s.shape[0]
    q = SSM_CHUNK
    xbc_new = proj_s[:, :SSM_CONV_DIM]
    n_tail = NP_SSM - SSM_CONV_DIM
    tail = jnp.where(jnp.arange(n_tail) < SSM_D_INNER, 0.0, NEG).astype(F32)
    rows = jnp.concatenate([
        jnp.concatenate([jnp.zeros((nb, q - SSM_CONV, SSM_CONV_DIM), F32), state_conv.astype(F32)], axis=1),
        jnp.broadcast_to(tail, (nb, q - 1, n_tail))], axis=2)
    rows = jnp.concatenate([rows, proj_s[:, None]], axis=1)
    h0 = state_ssm.astype(F32).reshape(nb, N_PAIRS, LANES, SSM_STATE)
    cinit = jnp.zeros((nb, CONV_PAD, SSM_CONV_DIM), F32)
    yn, hfin = _ssd(rows.reshape(nb * q, NP_SSM), h0, cinit, sw, nb, q)
    conv_new = jnp.concatenate([state_conv[:, 1:].astype(F32), xbc_new[:, None]], axis=1)
    return yn.reshape(nb, q, SSM_D_INNER)[:, q - 1], hfin.reshape(state_ssm.shape), conv_new


def _ssm_weights(w_in, conv_w, conv_b, dt_bias, a_log, d_skip, norm_g):
    z_w = w_in[:, :SSM_D_INNER]
    xbc_w = w_in[:, SSM_D_INNER:SSM_D_INNER + SSM_CONV_DIM]
    dt_w = w_in[:, SSM_D_INNER + SSM_CONV_DIM:]
    pad = NP_SSM - w_in.shape[1]
    w = jnp.concatenate([xbc_w, z_w, dt_w, jnp.zeros((w_in.shape[0], pad), w_in.dtype)], axis=1).astype(BF16)
    lane_pad = lambda v: jnp.zeros((1, LANES), F32).at[0, :SSM_HEADS].set(v.astype(F32))
    return dict(w_in=w, conv_w=conv_w.astype(F32), conv_b=conv_b.astype(F32).reshape(1, -1),
                dt_bias=lane_pad(dt_bias), a_log=lane_pad(a_log),
                d_lane=jnp.repeat(d_skip.astype(F32), SSM_HEAD_DIM).reshape(1, -1),
                norm_g=norm_g.astype(F32).reshape(1, -1))


ROW_TILE = 512
MLP_ROW_TILE = 1024
FF_TILE = 1024


def kernel(x_prompt, x_sample, cache_kv, cache_win, state_ssm, state_conv, page_table, c_prompt, c_sample, rel_bias,
           ada_w, ada_b, norm_g, mlp_w1, mlp_w2, attn_w_in, attn_w_out, cmp_pos, cmp_w1, cmp_w2, ssm_w_in,
           ssm_conv_w, ssm_conv_b, ssm_dt_bias, ssm_a_log, ssm_d, ssm_norm_g, ssm_w_out):
    nb, t, d = x_prompt.shape
    db = x_sample.shape[0]
    assert x_sample.shape[1] == 1 and t % SSM_CHUNK == 0 and t % Q_BLOCK == 0
    n_pool, page = cache_kv.shape[1], cache_kv.shape[2]
    past = page_table.shape[1] * page
    depth = ada_w.shape[0]
    tm = _row_tile(t, ROW_TILE)

    xp = x_prompt.reshape(nb * t, d).astype(F32)
    xs = x_sample.reshape(db, d).astype(F32)
    c_all = jnp.concatenate([c_prompt, c_sample], axis=0).astype(F32)
    c_all = jnp.pad(c_all, ((0, (-c_all.shape[0]) % SUB), (0, 0)))
    bvz = _bias_vector(rel_bias)
    cache_t = _cache_tiles(cache_kv)
    win_t = jnp.transpose(cache_win, (0, 1, 3, 4, 5, 2))

    kv_p, win_p, ssm_p, conv_p, kv_s, win_s, ssm_s, conv_s = ([] for _ in range(8))
    for i in range(depth):
        mod = _ada(c_all, ada_w[i], ada_b[i])
        mp = [mod[:nb, j * d:(j + 1) * d].reshape(nb, 1, d) for j in range(N_MOD)]
        ms = [mod[nb:nb + db, j * d:(j + 1) * d].reshape(1, db, d) for j in range(N_MOD)]
        g = norm_g[i].astype(F32)
        if i % 2 == 0:
            a = i // 2
            w_in = _attn_in_weights(attn_w_in[a])
            w_out = attn_w_out[a].astype(BF16)
            cw = _compress_weights(cmp_pos[a], cmp_w1[a], cmp_w2[a], [(0, 1)])
            o_p, kv_new, win_new_p = _nsa_prompt_mixer(xp, g[0], mp[0], mp[1], w_in, cw, rel_bias, nb, t)
            kv_p.append(kv_new)
            win_p.append(win_new_p)
            xp = _heads_out(o_p, _attn_out_weights(attn_w_out[a]), g[1], mp[2], xp, tm)

            proj_s = _nm_matmul(xs, g[0], ms[0], ms[1], w_in, db, NP_ATTN)
            cw_s = _compress_weights(cmp_pos[a], cmp_w1[a], cmp_w2[a], [(0, 0), (1, 1)])
            kvc_s = _compress_sample(page_table, cache_t, cw_s, page, a * n_pool)
            o_s, win_new = _nsa_sample(proj_s, kvc_s, page_table, cache_t, win_t[a], bvz, past, page, a * n_pool)
            kv_s.append(_unpermute_kv(proj_s[:, N_Q_COLS:GATE_COL0], (db, 1))[:, :, :N_KV_SLOTS])
            win_s.append(jnp.transpose(win_new, (0, 4, 1, 2, 3)))
            xs = _mm_norm_res(o_s, w_out, g[1], ms[2], xs, db)
        else:
            m = i // 2
            sw = _ssm_weights(ssm_w_in[m], ssm_conv_w[m], ssm_conv_b[m], ssm_dt_bias[m], ssm_a_log[m], ssm_d[m],
                              ssm_norm_g[m])
            w_out = ssm_w_out[m].astype(BF16)
            proj = _nm_matmul(xp, g[0], mp[0], mp[1], sw['w_in'], tm, SSM_COL_TILE)
            h0 = jnp.zeros((nb, N_PAIRS, LANES, SSM_STATE), F32)
            cinit = jnp.zeros((nb, CONV_PAD, SSM_CONV_DIM), F32)
            yn, hfin = _ssd(proj, h0, cinit, sw, nb, t)
            ssm_p.append(hfin.reshape(nb, SSM_HEADS, SSM_HEAD_DIM, SSM_STATE).astype(state_ssm.dtype))
            conv_p.append(proj.reshape(nb, t, NP_SSM)[:, t - (SSM_CONV - 1):, :SSM_CONV_DIM])
            xp = _mm_norm_res(yn, w_out, g[1], mp[2], xp, tm)

            proj_s = _nm_matmul(xs, g[0], ms[0], ms[1], sw['w_in'], db, SSM_COL_TILE)
            yn_s, h_s, conv_new = _ssm_sample_step(proj_s, state_ssm[m], state_conv[m], sw)
            ssm_s.append(h_s.astype(state_ssm.dtype))
            conv_s.append(conv_new)
            xs = _mm_norm_res(yn_s, w_out, g[1], ms[2], xs, db)
        w1 = mlp_w1[i].astype(BF16)
        w2 = mlp_w2[i].astype(BF16)
        xp = _mlp(xp, g[2], mp[3], mp[4], w1, w2, g[3], mp[5], _row_tile(t, MLP_ROW_TILE), FF_TILE)
        xs = _mlp(xs, g[2], ms[3], ms[4], w1, w2, g[3], ms[5], db, FF_TILE)
    return (xp.reshape(nb, t, d), xs.reshape(db, 1, d), jnp.stack(kv_p), jnp.stack(win_p), jnp.stack(ssm_p),
            jnp.stack(conv_p), jnp.stack(kv_s), jnp.stack(win_s), jnp.stack(ssm_s), jnp.stack(conv_s))
```

```python
import functools
import math

import numpy as np
import jax
import jax.numpy as jnp
from jax import lax
from jax.experimental import pallas as pl
from jax.experimental.pallas import tpu as pltpu

F32 = jnp.float32
BF16 = jnp.bfloat16

D_MODEL = 1024
N_HEADS = 16
HEAD_DIM = 64
N_KV_HEADS = 4
GROUP = N_HEADS // N_KV_HEADS
CMP_BLOCK = 32
CMP_STRIDE = 16
CMP_RATIO = CMP_BLOCK // CMP_STRIDE
CMP_HIDDEN = 2 * HEAD_DIM
SEL_BLOCK = 64
SEL_RATIO = SEL_BLOCK // CMP_STRIDE
N_SELECT = 16
WINDOW = 512
FORCE_BONUS = 1e4
Q_BLOCK = 128
N_KV_SLOTS = 4
N_WIN_SLOTS = 2
N_Q_COLS = N_HEADS * HEAD_DIM
N_KV_COLS = (N_KV_SLOTS + N_WIN_SLOTS) * N_KV_HEADS * HEAD_DIM
N_GATE_COLS = 3 * N_HEADS
KV_ROW = N_KV_HEADS * HEAD_DIM
KV_HEAD_COLS = (N_KV_SLOTS + N_WIN_SLOTS) * HEAD_DIM
N_BUCKETS = 32
MAX_DISTANCE = 128
SSM_D_INNER = 2 * D_MODEL
SSM_HEAD_DIM = 64
SSM_HEADS = SSM_D_INNER // SSM_HEAD_DIM
SSM_GROUPS = 8
SSM_STATE = 128
SSM_CONV = 4
SSM_CONV_DIM = SSM_D_INNER + 2 * SSM_GROUPS * SSM_STATE
SSM_CHUNK = 128
D_FF = 4 * D_MODEL
N_MOD = 6
EPS = 1e-6

LANES = 128
NEG = -1e30
LOG2E = math.log2(math.e)
VMEM_LIMIT = 56 * 1024 * 1024
NEAR = 2 * LANES

_NT = (((1,), (1,)), ((), ()))


def _cparams(sem):
    return pltpu.CompilerParams(dimension_semantics=sem, vmem_limit_bytes=VMEM_LIMIT)


def _rms(x, g):
    return x * lax.rsqrt(jnp.mean(x * x, axis=-1, keepdims=True) + EPS) * g


def _silu(x):
    return x * jax.nn.sigmoid(x)


def _split_bf16(x, n):
    parts = []
    for _ in range(n - 1):
        p = x.astype(BF16)
        parts.append(p)
        x = x - p.astype(F32)
    parts.append(x.astype(BF16))
    return parts


def _ada_kernel(c_ref, w_ref, b_ref, o_ref):
    s = _silu(c_ref[...]).astype(BF16)
    o_ref[...] = jnp.dot(s, w_ref[...].astype(BF16), preferred_element_type=F32) + b_ref[...]


def _ada(c, w, b, tn=1024):
    m, d = c.shape
    n = w.shape[1]
    return pl.pallas_call(
        _ada_kernel,
        grid=(n // tn,),
        in_specs=[pl.BlockSpec((m, d), lambda j: (0, 0)),
                  pl.BlockSpec((d, tn), lambda j: (0, j)),
                  pl.BlockSpec((1, tn), lambda j: (0, j))],
        out_specs=pl.BlockSpec((m, tn), lambda j: (0, j)),
        out_shape=jax.ShapeDtypeStruct((m, n), F32),
        compiler_params=_cparams(("arbitrary",)),
    )(c, w, b.reshape(1, n))


def _nm_matmul_kernel(x_ref, g_ref, sh_ref, sc_ref, w_ref, o_ref, h_ref):
    @pl.when(pl.program_id(1) == 0)
    def _():
        h = _rms(x_ref[...], g_ref[...]) * (1.0 + sc_ref[0]) + sh_ref[0]
        h_ref[...] = h.astype(BF16)

    o_ref[...] = jnp.dot(h_ref[...], w_ref[...], preferred_element_type=F32)


def _nm_matmul(x, g, shift, scale, w, tm, tn):
    m, d = x.shape
    n = w.shape[1]
    nb = shift.shape[0]
    tpb = m // nb // tm
    mod_spec = pl.BlockSpec((1,) + shift.shape[1:], lambda i, j: (i // tpb, 0, 0))
    return pl.pallas_call(
        _nm_matmul_kernel,
        grid=(m // tm, n // tn),
        in_specs=[pl.BlockSpec((tm, d), lambda i, j: (i, 0)),
                  pl.BlockSpec((1, d), lambda i, j: (0, 0)),
                  mod_spec, mod_spec,
                  pl.BlockSpec((d, tn), lambda i, j: (0, j))],
        out_specs=pl.BlockSpec((tm, tn), lambda i, j: (i, j)),
        out_shape=jax.ShapeDtypeStruct((m, n), F32),
        scratch_shapes=[pltpu.VMEM((tm, d), BF16)],
        compiler_params=_cparams(("arbitrary", "arbitrary")),
    )(x, g.reshape(1, d), shift, scale, w)


def _mm_norm_res_kernel(a_ref, w_ref, g_ref, gate_ref, x_ref, o_ref):
    y = jnp.dot(a_ref[...].astype(BF16), w_ref[...], preferred_element_type=F32)
    o_ref[...] = x_ref[...] + gate_ref[0] * _rms(y, g_ref[...])


def _mm_norm_res(a, w, g, gate, x, tm):
    m, k = a.shape
    d = w.shape[1]
    nb = gate.shape[0]
    tpb = m // nb // tm
    return pl.pallas_call(
        _mm_norm_res_kernel,
        grid=(m // tm,),
        in_specs=[pl.BlockSpec((tm, k), lambda i: (i, 0)),
                  pl.BlockSpec((k, d), lambda i: (0, 0)),
                  pl.BlockSpec((1, d), lambda i: (0, 0)),
                  pl.BlockSpec((1,) + gate.shape[1:], lambda i: (i // tpb, 0, 0)),
                  pl.BlockSpec((tm, d), lambda i: (i, 0))],
        out_specs=pl.BlockSpec((tm, d), lambda i: (i, 0)),
        out_shape=jax.ShapeDtypeStruct((m, d), F32),
        compiler_params=_cparams(("arbitrary",)),
    )(a, w, g.reshape(1, d), gate, x)


def _mlp_kernel(x_ref, g1_ref, sh_ref, sc_ref, w1_ref, w2_ref, g2_ref, gate_ref, o_ref, h_ref, acc_ref):
    c = pl.program_id(1)

    @pl.when(c == 0)
    def _():
        h = _rms(x_ref[...], g1_ref[...]) * (1.0 + sc_ref[0]) + sh_ref[0]
        h_ref[...] = h.astype(BF16)
        acc_ref[...] = jnp.zeros_like(acc_ref)

    a = jnp.dot(h_ref[...], w1_ref[...], preferred_element_type=F32)
    a = jnp.square(jnp.maximum(a, 0.0)).astype(BF16)
    acc_ref[...] += jnp.dot(a, w2_ref[...], preferred_element_type=F32)

    @pl.when(c == pl.num_programs(1) - 1)
    def _():
        o_ref[...] = x_ref[...] + gate_ref[0] * _rms(acc_ref[...], g2_ref[...])


def _mlp(x, g1, shift, scale, w1, w2, g2, gate, tm, tf):
    m, d = x.shape
    f = w1.shape[1]
    nb = shift.shape[0]
    tpb = m // nb // tm
    mod_spec = pl.BlockSpec((1,) + shift.shape[1:], lambda i, c: (i // tpb, 0, 0))
    vec_spec = pl.BlockSpec((1, d), lambda i, c: (0, 0))
    return pl.pallas_call(
        _mlp_kernel,
        grid=(m // tm, f // tf),
        in_specs=[pl.BlockSpec((tm, d), lambda i, c: (i, 0)), vec_spec, mod_spec, mod_spec,
                  pl.BlockSpec((d, tf), lambda i, c: (0, c)),
                  pl.BlockSpec((tf, d), lambda i, c: (c, 0)),
                  vec_spec, mod_spec],
        out_specs=pl.BlockSpec((tm, d), lambda i, c: (i, 0)),
        out_shape=jax.ShapeDtypeStruct((m, d), F32),
        scratch_shapes=[pltpu.VMEM((tm, d), BF16), pltpu.VMEM((tm, d), F32)],
        compiler_params=_cparams(("arbitrary", "arbitrary")),
    )(x, g1.reshape(1, d), shift, scale, w1, w2, g2.reshape(1, d), gate)


def _bias_kernel(oh_ref, t_ref, o_ref):
    t = t_ref[...]
    t = t - t[N_BUCKETS - 1:N_BUCKETS, :]
    oh = oh_ref[...]
    o_ref[...] = sum(jnp.dot(oh, p, preferred_element_type=F32) for p in _split_bf16(t, 3))


def _rel_bucket_np(dist):
    n = np.maximum(dist, 0)
    exact = N_BUCKETS // 2
    nf = np.maximum(n, 1).astype(np.float32)
    large = exact + (np.log(nf / exact) / math.log(MAX_DISTANCE / exact) * (N_BUCKETS - exact)).astype(np.int32)
    return np.where(n < exact, n, np.minimum(large, N_BUCKETS - 1))


def _bias_vector(rel_bias):
    assert _rel_bucket_np(np.array([NEAR - 1]))[0] == N_BUCKETS - 1 == _rel_bucket_np(np.array([MAX_DISTANCE]))[0]
    oh = np.zeros((NEAR + 8, LANES), np.float32)
    oh[np.arange(NEAR), _rel_bucket_np(np.arange(NEAR))] = 1.0
    oh[NEAR:, N_BUCKETS - 1] = 1.0
    table = jnp.zeros((LANES, LANES), F32).at[:N_BUCKETS, :N_HEADS].set(rel_bias.astype(F32))
    bv = pl.pallas_call(
        _bias_kernel,
        out_shape=jax.ShapeDtypeStruct((NEAR + 8, LANES), F32),
    )(jnp.asarray(oh, BF16), table)
    return bv[:NEAR + 1, :N_HEADS]


SUB = 8
HEADS_PER_CG = LANES // HEAD_DIM
N_CG = 2 * KV_ROW // LANES
CG_PER_SLOT = KV_ROW // LANES
CG_HIDDEN = HEADS_PER_CG * CMP_HIDDEN


def _compress_parts(rows_s, wbd_ref, nh):
    part0 = jnp.zeros((nh, CG_HIDDEN), F32)
    part1 = jnp.zeros((nh, CG_HIDDEN), F32)
    for s in range(0, CMP_STRIDE, 2):
        xs = jnp.concatenate([rows_s(s), rows_s(s + 1)], axis=1).astype(BF16)
        part0 = part0 + jnp.dot(xs, wbd_ref[0, 0, s // 2], preferred_element_type=F32)
        part1 = part1 + jnp.dot(xs, wbd_ref[0, 1, s // 2], preferred_element_type=F32)
    return part0, part1


def _compress_finish(part0, part1, pos_ref, w1_ref, w2bd_ref, nh):
    pre0 = jnp.dot(pos_ref[0], w1_ref[0], preferred_element_type=F32)[0:1]
    pre = pre0 + part0 + pltpu.roll(part1, nh - 1, axis=0)
    return jnp.dot(_silu(pre).astype(BF16), w2bd_ref[0], preferred_element_type=F32)


def _compress_kernel(x_ref, pos_ref, w1_ref, wbd_ref, w2bd_ref, o_ref, *, nh):
    part0, part1 = _compress_parts(lambda s: x_ref[pl.ds(s, nh, stride=CMP_STRIDE), :], wbd_ref, nh)
    o_ref[0, 0] = _compress_finish(part0, part1, pos_ref, w1_ref, w2bd_ref, nh)


def _block_diag(a, b):
    za = jnp.zeros(a.shape[:-1] + (b.shape[-1],), a.dtype)
    zb = jnp.zeros(b.shape[:-1] + (a.shape[-1],), b.dtype)
    return jnp.concatenate([jnp.concatenate([a, za], axis=-1), jnp.concatenate([zb, b], axis=-1)], axis=-2)


def _compress_weight_set(pos, w1, w2, a, b):
    w1r = w1.reshape(2, CMP_RATIO, CMP_STRIDE, HEAD_DIM, CMP_HIDDEN)
    wbd = _block_diag(w1r[a], w1r[b]).reshape(CMP_RATIO, CMP_STRIDE // 2, 2 * LANES, CG_HIDDEN)
    posx = jnp.zeros((SUB, 2 * CMP_BLOCK * HEAD_DIM), F32).at[0].set(
        jnp.concatenate([pos[a].reshape(-1), pos[b].reshape(-1)]))
    return [x.astype(BF16) for x in (posx, _block_diag(w1[a], w1[b]), wbd, _block_diag(w2[a], w2[b]))]


def _compress_weights(cmp_pos, cmp_w1, cmp_w2, pairs):
    sets = [_compress_weight_set(cmp_pos, cmp_w1, cmp_w2, a, b) for a, b in pairs]
    return [jnp.stack(x) for x in zip(*sets)]


def _compress_specs(cw, imap):
    def spec(a):
        return pl.BlockSpec((1,) + a.shape[1:], lambda *g: (imap(*g),) + (0,) * (a.ndim - 1))
    return [spec(a) for a in cw]


def _compress_prompt(proj, cw, nb, t):
    nh = t // CMP_STRIDE
    return pl.pallas_call(
        functools.partial(_compress_kernel, nh=nh),
        grid=(nb, N_KV_HEADS),
        in_specs=[pl.BlockSpec((t, LANES), lambda b, k: (b, (N_Q_COLS + k * KV_HEAD_COLS) // LANES))]
        + _compress_specs(cw, lambda b, k: 0),
        out_specs=pl.BlockSpec((1, 1, nh, LANES), lambda b, k: (b, k, 0, 0)),
        out_shape=jax.ShapeDtypeStruct((nb, N_KV_HEADS, nh, LANES), F32),
        compiler_params=_cparams(("arbitrary", "arbitrary")),
    )(proj, *cw)


KEY_TILE = 4 * Q_BLOCK
FRONT_PAD = KEY_TILE
SUBTILES = KEY_TILE // Q_BLOCK
BAND_OFF = 2 * Q_BLOCK // CMP_STRIDE
BAND_W = 32
Q_PER_CMP = Q_BLOCK // CMP_STRIDE


def _nsa_consts(t):
    ncp = t // CMP_STRIDE
    nblk = t // SEL_BLOCK
    nbp = -(-nblk // LANES) * LANES
    at = np.zeros((nbp, ncp), np.float32)
    for j in range(nblk):
        for n in range(SEL_RATIO * j - (CMP_RATIO - 1), SEL_RATIO * (j + 1)):
            if 0 <= n < ncp - CMP_RATIO + 1:
                at[j, n] = 1.0
    key_blk = np.arange(t) // SEL_BLOCK
    e = (np.arange(nbp)[:, None] == key_blk[None, :]).astype(np.float32)
    e1 = e.reshape(nbp, t // LANES, LANES).transpose(1, 0, 2)
    e1 = np.concatenate([np.zeros((SUBTILES,) + e1.shape[1:], np.float32), e1], axis=0)
    band = np.zeros((2 * ncp + 2 * Q_PER_CMP, LANES), np.float32)
    u = np.arange(BAND_W)
    band[u + ncp, HEAD_DIM + u] = 1.0
    band[u + ncp, HEAD_DIM + BAND_W + u] = 1.0
    r = np.arange(Q_BLOCK)[:, None]
    c = np.arange(LANES)[None, :]
    d0 = r - c
    idx0 = np.where(d0 >= 0, d0, NEAR)
    idx1 = Q_BLOCK + r - c
    dc = r - CMP_STRIDE * c + (CMP_STRIDE * BAND_OFF - CMP_BLOCK + 1)
    idxc = np.where((dc >= 0) & (dc < NEAR), dc, NEAR)
    assert np.all(idxc[:, BAND_W:] == NEAR)
    bf = lambda a: jnp.asarray(a, BF16)
    return dict(at=bf(at), e1=bf(e1), band=jnp.asarray(band),
                idx0=idx0, idx1=idx1, idxc=idxc, ncp=ncp, nbp=nbp, nblk=nblk)


def _nsa_kernel(q_ref, gl_ref, kck_ref, kvc_ref, kvs_ref, kvw_ref, at_ref, e1_ref, tb_ref, cbq_ref, band_ref,
                o_ref, *, ncp, nbp, n_sel):
    i = pl.program_id(2)
    rows = GROUP * Q_BLOCK
    lane = lax.broadcasted_iota(jnp.int32, (1, LANES), 1)
    low = lane < HEAD_DIM
    r_col = lax.broadcasted_iota(jnp.int32, (rows, 1), 0) % Q_BLOCK
    q_pos = i * Q_BLOCK + r_col

    q = q_ref[...] * (HEAD_DIM ** -0.5 * LOG2E)
    parts = []
    for h in range(GROUP * HEAD_DIM // LANES):
        qh = q[:, h * LANES:(h + 1) * LANES]
        parts.append(jnp.where(low, qh, 0.0))
        parts.append(jnp.where(low, pltpu.roll(qh, HEAD_DIM, axis=1), 0.0))
    qs = jnp.concatenate(parts, axis=0).astype(BF16)

    qc = jnp.where(low, qs, cbq_ref[0])
    start = pl.multiple_of(ncp + BAND_OFF - Q_PER_CMP * i, 8)
    kq = (kck_ref[0, 0] + band_ref[pl.ds(start, ncp), :]).astype(BF16)
    s = lax.dot_general(qc, kq, _NT, preferred_element_type=F32)
    cmp_end = lax.broadcasted_iota(jnp.int32, (1, ncp), 1) * CMP_STRIDE + (CMP_BLOCK - 1)
    mask = cmp_end <= q_pos
    m = jnp.max(jnp.where(mask, s, NEG), axis=-1, keepdims=True)
    e = jnp.where(mask, jnp.exp2(s - m), 0.0)
    p_cmp = e * (1.0 / jnp.maximum(jnp.sum(e, axis=-1, keepdims=True), 1e-30))
    o_cmp = jnp.dot(p_cmp.astype(BF16), kvc_ref[0, 0], preferred_element_type=F32)
    imp = p_cmp[0:Q_BLOCK]
    for g in range(1, GROUP):
        imp = imp + p_cmp[g * Q_BLOCK:(g + 1) * Q_BLOCK]
    bst = sum(lax.dot_general(at_ref[...], part, _NT, preferred_element_type=F32)
              for part in _split_bf16(imp, 2))

    jb = lax.broadcasted_iota(jnp.int32, (nbp, Q_BLOCK), 0)
    q_blk = (i * Q_BLOCK + lax.broadcasted_iota(jnp.int32, (nbp, Q_BLOCK), 1)) // SEL_BLOCK
    causal = jb <= q_blk
    forced = (jb == 0) | (jb == q_blk) | (jb == q_blk - 1)
    work = jnp.where(causal, bst + jnp.where(forced, FORCE_BONUS, 0.0), -jnp.inf)
    jbf = jb.astype(F32)
    sel_t = jnp.zeros((nbp, Q_BLOCK), F32)
    for _ in range(n_sel):
        top = jnp.max(work, axis=0, keepdims=True)
        first = jnp.min(jnp.where(work == top, jbf, float(nbp)), axis=0, keepdims=True)
        hit = jbf == first
        sel_t = jnp.where(hit, 1.0, sel_t)
        work = jnp.where(hit, -jnp.inf, work)
    sel = jnp.where(causal, sel_t, 0.0).T.astype(BF16)

    def ones_and_values(kv):
        return jnp.where(low, jnp.ones_like(kv), kv)

    def normalise(acc):
        return jnp.where(low, 0.0, acc / jnp.maximum(pltpu.roll(acc, HEAD_DIM, axis=1), 1e-30))

    n_wt = WINDOW // Q_BLOCK
    kw = kvw_ref[0, 0, pl.ds(pl.multiple_of((i + SUBTILES - n_wt) * Q_BLOCK, Q_BLOCK), WINDOW + Q_BLOCK), :]
    s = lax.dot_general(qs, kw, _NT, preferred_element_type=F32)
    pieces = []
    for u in range(n_wt + 1):
        su = s[:, u * Q_BLOCK:(u + 1) * Q_BLOCK]
        if u == n_wt:
            su = jnp.where(lane <= r_col, su + tb_ref[0, 0], NEG)
        else:
            if u == n_wt - 1:
                su = su + tb_ref[0, 1]
            su = su + jnp.where(i - n_wt + u >= 0, 0.0, NEG)
            if u == 0:
                su = jnp.where(lane > r_col, su, NEG)
        pieces.append(su)
    sm = jnp.concatenate(pieces, axis=1).astype(BF16)
    e = jnp.exp2(sm - jnp.max(sm, axis=-1, keepdims=True))
    o_win = normalise(jnp.dot(e, ones_and_values(kw), preferred_element_type=F32))

    n_tiles = (i + SUBTILES) // SUBTILES

    def keys(u):
        return kvs_ref[0, 0, pl.ds(pl.multiple_of((i + 1 - SUBTILES * u) * Q_BLOCK, Q_BLOCK), KEY_TILE), :]

    def masked_scores(u, diagonal=False):
        t0p = i + 1 - SUBTILES * u
        mk = jnp.concatenate([jnp.dot(sel, e1_ref[t0p + v], preferred_element_type=F32) for v in range(SUBTILES)],
                             axis=1).astype(BF16)
        s3 = lax.dot_general(qs, keys(u), _NT, preferred_element_type=F32).reshape(GROUP, Q_BLOCK, KEY_TILE)
        if diagonal:
            r3 = lax.broadcasted_iota(jnp.int32, (1, Q_BLOCK, LANES), 1)
            c3 = lax.broadcasted_iota(jnp.int32, (1, Q_BLOCK, LANES), 2)
            tb0 = tb_ref[0, 0].reshape(GROUP, Q_BLOCK, LANES)
            tb1 = tb_ref[0, 1].reshape(GROUP, Q_BLOCK, LANES)
            s3 = jnp.concatenate([s3[:, :, :KEY_TILE - 2 * Q_BLOCK],
                                  s3[:, :, KEY_TILE - 2 * Q_BLOCK:KEY_TILE - Q_BLOCK] + tb1,
                                  jnp.where(c3 <= r3, s3[:, :, KEY_TILE - Q_BLOCK:] + tb0, NEG)], axis=2)
        return jnp.where((mk > 0.5)[None], s3.astype(BF16), NEG).reshape(rows, KEY_TILE)

    def sel_body(u, carry):
        m, acc, sm, e_prev = carry
        pv = jnp.dot(e_prev, ones_and_values(keys(jnp.maximum(u - 1, 0))), preferred_element_type=F32)
        sm_next = masked_scores(jnp.minimum(u + 1, n_tiles - 1))
        m_new = jnp.maximum(m, jnp.max(sm, axis=-1, keepdims=True).astype(F32))
        alpha = jnp.exp2(m - m_new)
        e = jnp.exp2(sm - m_new.astype(BF16))
        return m_new, alpha * (acc + pv), sm_next, e

    init = (jnp.full((rows, 1), NEG, F32), jnp.zeros((rows, LANES), F32),
            masked_scores(0, diagonal=True), jnp.zeros((rows, KEY_TILE), BF16))
    m, acc, _, e_last = lax.fori_loop(0, n_tiles, sel_body, init)
    o_sel = normalise(acc + jnp.dot(e_last, ones_and_values(keys(n_tiles - 1)), preferred_element_type=F32))

    sg = jax.nn.sigmoid(gl_ref[...])
    for g in range(GROUP):
        out = jnp.zeros((Q_BLOCK, LANES), F32)
        for j, ob in enumerate((o_cmp, o_sel, o_win)):
            out = out + sg[:, 3 * g + j:3 * g + j + 1] * ob[g * Q_BLOCK:(g + 1) * Q_BLOCK]
        o_ref[0, 0, g] = out.astype(o_ref.dtype)


GATE_COL0 = N_Q_COLS + N_KV_COLS
NP_ATTN = GATE_COL0 + N_KV_HEADS * LANES


def _nsa_prompt(proj, kck, kvc, kvs, kvw, tb, cbq, consts, nb, t):
    ni = t // Q_BLOCK
    ncp, nbp = consts['ncp'], consts['nbp']
    gcol = GATE_COL0 // LANES
    qw = GROUP * HEAD_DIM
    full = lambda a: pl.BlockSpec(a.shape, lambda b, k, i: (0,) * a.ndim)
    per_kvh = lambda a: pl.BlockSpec((1,) + a.shape[1:], lambda b, k, i: (k,) + (0,) * (a.ndim - 1))
    per_bk = lambda a: pl.BlockSpec((1, 1) + a.shape[2:], lambda b, k, i: (b, k, 0, 0))
    c = consts
    n_sel = min(N_SELECT, c['nblk'])
    assert n_sel >= 3
    return pl.pallas_call(
        functools.partial(_nsa_kernel, ncp=ncp, nbp=nbp, n_sel=n_sel),
        grid=(nb, N_KV_HEADS, ni),
        in_specs=[pl.BlockSpec((Q_BLOCK, qw), lambda b, k, i: (b * ni + i, k)),
                  pl.BlockSpec((Q_BLOCK, LANES), lambda b, k, i: (b * ni + i, gcol + k)),
                  per_bk(kck), per_bk(kvc), per_bk(kvs), per_bk(kvw),
                  full(c['at']), full(c['e1']), per_kvh(tb), per_kvh(cbq), full(c['band'])],
        out_specs=pl.BlockSpec((1, 1, GROUP, Q_BLOCK, LANES), lambda b, k, i: (b, k, 0, i, 0)),
        out_shape=jax.ShapeDtypeStruct((nb, N_KV_HEADS, GROUP, t, LANES), BF16),
        compiler_params=_cparams(("arbitrary", "arbitrary", "arbitrary")),
    )(proj, proj, kck, kvc, kvs, kvw, c['at'], c['e1'], tb, cbq, c['band'])


def _heads_out_kernel(a_ref, w_ref, g_ref, gate_ref, x_ref, o_ref):
    y = jnp.zeros(o_ref.shape, F32)
    for k in range(N_KV_HEADS):
        for g in range(0, GROUP, 2):
            a = jnp.concatenate([a_ref[0, k, g], a_ref[0, k, g + 1]], axis=1)
            h = k * GROUP + g
            w = w_ref[h * LANES:(h + 2) * LANES, :]
            y = y + jnp.dot(a, w, preferred_element_type=F32)
    o_ref[...] = x_ref[...] + gate_ref[0] * _rms(y, g_ref[...])


def _heads_out(a, w_pad, g, gate, x, tm):
    nb, _, _, t, _ = a.shape
    d = w_pad.shape[1]
    tpb = t // tm
    return pl.pallas_call(
        _heads_out_kernel,
        grid=(nb * tpb,),
        in_specs=[pl.BlockSpec((1, N_KV_HEADS, GROUP, tm, LANES), lambda i: (i // tpb, 0, 0, i % tpb, 0)),
                  pl.BlockSpec(w_pad.shape, lambda i: (0, 0)),
                  pl.BlockSpec((1, d), lambda i: (0, 0)),
                  pl.BlockSpec((1,) + gate.shape[1:], lambda i: (i // tpb, 0, 0)),
                  pl.BlockSpec((tm, d), lambda i: (i, 0))],
        out_specs=pl.BlockSpec((tm, d), lambda i: (i, 0)),
        out_shape=jax.ShapeDtypeStruct((nb * t, d), F32),
        compiler_params=_cparams(("arbitrary",)),
    )(a, w_pad, g.reshape(1, d), gate, x)


def _row_tile(m, cap):
    return m if m <= cap else cap


def _attn_in_weights(w_in):
    d = w_in.shape[0]
    kv = w_in[:, N_Q_COLS:GATE_COL0].reshape(d, N_KV_SLOTS + N_WIN_SLOTS, N_KV_HEADS, HEAD_DIM)
    kv = jnp.swapaxes(kv, 1, 2).reshape(d, N_KV_COLS)
    gates = w_in[:, GATE_COL0:].reshape(d, N_KV_HEADS, GROUP * 3)
    gates = jnp.pad(gates, ((0, 0), (0, 0), (0, LANES - GROUP * 3))).reshape(d, -1)
    return jnp.concatenate([w_in[:, :N_Q_COLS], kv, gates], axis=1).astype(BF16)


def _attn_out_weights(w_out):
    w = w_out.reshape(N_HEADS, HEAD_DIM, w_out.shape[1])
    return jnp.pad(w, ((0, 0), (LANES - HEAD_DIM, 0), (0, 0))).reshape(N_HEADS * LANES, -1).astype(BF16)


def _attn_in_kernel(x_ref, g_ref, sh_ref, sc_ref, w_ref, zs_ref, zw_ref, o_ref, kvs_ref, kvw_ref, kvt_ref, wint_ref):
    del zs_ref, zw_ref
    h = _rms(x_ref[...], g_ref[...]) * (1.0 + sc_ref[0]) + sh_ref[0]
    r = jnp.dot(h.astype(BF16), w_ref[...], preferred_element_type=F32)
    o_ref[...] = r
    for k in range(N_KV_HEADS):
        c0 = N_Q_COLS + k * KV_HEAD_COLS
        kvs_ref[0, k] = r[:, c0 + LANES:c0 + 2 * LANES].astype(BF16)
        kvw_ref[0, k] = r[:, c0 + 2 * LANES:c0 + 3 * LANES].astype(BF16)
        for p in range(KV_HEAD_COLS // LANES):
            tt = r[:, c0 + p * LANES:c0 + (p + 1) * LANES].T
            for j in range(2):
                slot = 2 * p + j
                dst, s0 = (kvt_ref, slot) if slot < N_KV_SLOTS else (wint_ref, slot - N_KV_SLOTS)
                dst[0, (s0 * N_KV_HEADS + k) * HEAD_DIM:(s0 * N_KV_HEADS + k + 1) * HEAD_DIM, :] = (
                    tt[j * HEAD_DIM:(j + 1) * HEAD_DIM])


def _attn_in_prompt(x, g, shift, scale, w, nb, t):
    m, d = x.shape
    tm = FRONT_PAD
    assert t % tm == 0
    tpb = t // tm
    mod_spec = pl.BlockSpec((1,) + shift.shape[1:], lambda i: (i // tpb, 0, 0))
    pad_shape = (nb, N_KV_HEADS, FRONT_PAD + t, LANES)
    pad_spec = pl.BlockSpec((1, N_KV_HEADS, tm, LANES), lambda i: (i // tpb, 0, 1 + i % tpb, 0))
    zeros = jnp.zeros(pad_shape, BF16)
    return pl.pallas_call(
        _attn_in_kernel,
        grid=(m // tm,),
        in_specs=[pl.BlockSpec((tm, d), lambda i: (i, 0)),
                  pl.BlockSpec((1, d), lambda i: (0, 0)),
                  mod_spec, mod_spec,
                  pl.BlockSpec(w.shape, lambda i: (0, 0)),
                  pl.BlockSpec(memory_space=pl.ANY), pl.BlockSpec(memory_space=pl.ANY)],
        out_specs=[pl.BlockSpec((tm, NP_ATTN), lambda i: (i, 0)), pad_spec, pad_spec,
                   pl.BlockSpec((1, N_KV_SLOTS * KV_ROW, tm), lambda i: (i // tpb, 0, i % tpb)),
                   pl.BlockSpec((1, N_WIN_SLOTS * KV_ROW, tm), lambda i: (i // tpb, 0, i % tpb))],
        out_shape=[jax.ShapeDtypeStruct((m, NP_ATTN), F32),
                   jax.ShapeDtypeStruct(pad_shape, BF16), jax.ShapeDtypeStruct(pad_shape, BF16),
                   jax.ShapeDtypeStruct((nb, N_KV_SLOTS * KV_ROW, t), F32),
                   jax.ShapeDtypeStruct((nb, N_WIN_SLOTS * KV_ROW, t), F32)],
        input_output_aliases={5: 1, 6: 2},
        compiler_params=_cparams(("arbitrary",)),
    )(x, g.reshape(1, d), shift, scale, w, zeros, zeros)


def _unpermute_kv(proj_kv, lead):
    kv = proj_kv.reshape(lead + (N_KV_HEADS, N_KV_SLOTS + N_WIN_SLOTS, HEAD_DIM))
    return jnp.swapaxes(kv, -3, -2)


def _bias_tiles_kernel(t_ref, oh_ref, o_ref):
    oh = oh_ref[...]
    o_ref[...] = sum(jnp.dot(p, oh, preferred_element_type=F32) for p in _split_bf16(t_ref[...], 3))


def _bias_tiles(rel_bias, idxs):
    n = sum(ix.size for ix in idxs)
    d = np.concatenate([ix.reshape(-1) for ix in idxs])
    oh = np.zeros((LANES, n), np.float32)
    near = d < NEAR
    oh[_rel_bucket_np(d[near]), np.nonzero(near)[0]] = 1.0
    table = (rel_bias.astype(F32) - rel_bias[N_BUCKETS - 1].astype(F32)).T * LOG2E
    table = jnp.zeros((N_HEADS, LANES), F32).at[:, :N_BUCKETS].set(table)
    tn = 8192
    assert n % tn == 0
    out = pl.pallas_call(
        _bias_tiles_kernel,
        grid=(n // tn,),
        in_specs=[pl.BlockSpec((N_HEADS, LANES), lambda j: (0, 0)), pl.BlockSpec((LANES, tn), lambda j: (0, j))],
        out_specs=pl.BlockSpec((N_HEADS, tn), lambda j: (0, j)),
        out_shape=jax.ShapeDtypeStruct((N_HEADS, n), F32),
        compiler_params=_cparams(("arbitrary",)),
    )(table, jnp.asarray(oh, BF16))
    tiles, off = [], 0
    for ix in idxs:
        r, c = ix.shape
        tiles.append(out[:, off:off + r * c].reshape(N_KV_HEADS, GROUP * r, c))
        off += r * c
    return tiles


def _nsa_prompt_mixer(xp, g, shift, scale, w_in, cw, rel_bias, nb, t):
    proj, kvs, kvw, kvt, wint = _attn_in_prompt(xp, g, shift, scale, w_in, nb, t)
    as_rows = lambda a: jnp.transpose(a.reshape(nb, -1, N_KV_HEADS, HEAD_DIM, a.shape[-1]), (0, 4, 1, 2, 3))
    kv_new = as_rows(kvt)
    win_new = as_rows(wint[:, :, t - min(WINDOW, t):])
    kck = _compress_prompt(proj, cw, nb, t)
    kvc = kck.astype(BF16)
    kck = jnp.where(jnp.arange(LANES) < HEAD_DIM, kck, 0.0)
    c = _nsa_consts(t)
    tiles = _bias_tiles(rel_bias, [c['idx0'], c['idx1'], c['idxc']])
    tb = jnp.stack(tiles[:2], axis=1)
    hi, lo = _split_bf16(tiles[2], 2)
    cbq = jnp.concatenate([jnp.zeros(hi.shape[:2] + (HEAD_DIM,), BF16), hi[..., :BAND_W], lo[..., :BAND_W]], axis=-1)
    o = _nsa_prompt(proj, kck, kvc, kvs, kvw, tb, cbq, c, nb, t)
    return o, kv_new, win_new


SC_PAGES = 32
CG_PER_PAGE = N_KV_SLOTS * KV_ROW // LANES
ROW_PITCH = CMP_STRIDE + SUB


def _cache_tiles(cache_kv):
    page = cache_kv.shape[2]
    assert page == LANES
    return jnp.transpose(cache_kv, (0, 1, 3, 4, 5, 2)).reshape(-1, page)


def _compress_sample_kernel(pt_ref, cache_ref, pos_ref, w1_ref, wbd_ref, w2bd_ref, o_ref, tbuf_ref, rows0_ref,
                            rows1_ref, sem_ref, p0_ref, p1_ref, *, n_pages, page, row_base):
    row_bufs = (rows0_ref, rows1_ref)
    step = pl.program_id(0) * N_CG + pl.program_id(1)
    nsteps = pl.num_programs(0) * N_CG
    chunk_pages = min(SC_PAGES, n_pages // 2)
    nch = n_pages // chunk_pages
    hpc = chunk_pages * page // CMP_STRIDE
    nh = nch * hpc

    last = nsteps * nch - 1

    def copies(chunk, slot):
        chunk = jnp.minimum(chunk, last)
        step_, ch = chunk // nch, chunk % nch
        b_, c_ = step_ // N_CG, step_ % N_CG
        out = []
        for p in range(chunk_pages):
            pg = pt_ref[b_, ch * chunk_pages + p]
            out.append(pltpu.make_async_copy(
                cache_ref.at[pl.ds(((row_base + pg) * CG_PER_PAGE + c_) * LANES, LANES), :],
                tbuf_ref.at[slot, p], sem_ref.at[slot]))
        return out

    def to_rows(slot):
        for p in range(chunk_pages):
            rows_t = tbuf_ref[slot, p].T
            for n in range(page // CMP_STRIDE):
                r0 = (p * (page // CMP_STRIDE) + n) * ROW_PITCH
                row_bufs[slot][r0:r0 + CMP_STRIDE, :] = rows_t[n * CMP_STRIDE:(n + 1) * CMP_STRIDE]

    @pl.when(step == 0)
    def _():
        for slot in range(2):
            for cp in copies(slot, slot):
                cp.start()
        for cp in copies(0, 0):
            cp.wait()
        to_rows(0)

    for ch in range(nch):
        slot = ch % 2
        chunk = step * nch + ch
        for cp in copies(chunk + 2, slot):
            cp.start()
        for cp in copies(chunk + 1, 1 - slot):
            cp.wait()
        to_rows(1 - slot)
        rows = row_bufs[slot]
        p0, p1 = _compress_parts(lambda s: rows[pl.ds(s, hpc, stride=ROW_PITCH), :], wbd_ref, hpc)
        p0_ref[ch * hpc:(ch + 1) * hpc] = p0
        p1_ref[ch * hpc:(ch + 1) * hpc] = p1

    @pl.when(step == nsteps - 1)
    def _():
        for cp in copies(last, (nch - 1) % 2):
            cp.wait()

    o_ref[0, 0] = _compress_finish(p0_ref[...], p1_ref[...], pos_ref, w1_ref, w2bd_ref, nh)


def _compress_sample(page_table, cache_t, cw, page, row_base):
    nb, n_pages = page_table.shape
    chunk_pages = min(SC_PAGES, n_pages // 2)
    assert n_pages % (2 * chunk_pages) == 0
    nh = n_pages * page // CMP_STRIDE
    grid_spec = pltpu.PrefetchScalarGridSpec(
        num_scalar_prefetch=1,
        grid=(nb, N_CG),
        in_specs=[pl.BlockSpec(memory_space=pl.ANY)] + _compress_specs(cw, lambda b, c, pt: c // CG_PER_SLOT),
        out_specs=pl.BlockSpec((1, 1, nh, LANES), lambda b, c, pt: (b, c, 0, 0)),
        scratch_shapes=[pltpu.VMEM((2, chunk_pages, LANES, page), F32),
                        pltpu.VMEM((chunk_pages * page // CMP_STRIDE * ROW_PITCH, LANES), F32),
                        pltpu.VMEM((chunk_pages * page // CMP_STRIDE * ROW_PITCH, LANES), F32),
                        pltpu.SemaphoreType.DMA((2,)),
                        pltpu.VMEM((nh, CG_HIDDEN), F32),
                        pltpu.VMEM((nh, CG_HIDDEN), F32)])
    return pl.pallas_call(
        functools.partial(_compress_sample_kernel, n_pages=n_pages, page=page, row_base=row_base),
        grid_spec=grid_spec,
        out_shape=jax.ShapeDtypeStruct((nb, N_CG, nh, LANES), F32),
        compiler_params=_cparams(("arbitrary", "arbitrary")),
    )(page_table, cache_t, *cw)


def _softmax_with_new_key(s, s_new):
    m = jnp.maximum(jnp.max(s, axis=-1, keepdims=True), s_new)
    e = jnp.exp(s - m)
    e_new = jnp.exp(s_new - m)
    inv = 1.0 / jnp.maximum(jnp.sum(e, axis=-1, keepdims=True) + e_new, 1e-30)
    return e * inv, e_new * inv


def _nsa_sample_kernel(pt_ref, qs_ref, q_ref, gl_ref, new_ref, ncol_ref, kvc_ref, cache_ref, win_ref, cbs_ref,
                       sbz_ref, wb_ref, b0_ref, as_ref, o_ref, wout_ref, gk_ref, gv_ref, sem_ref, idx_ref,
                       *, past, page, row_base, n_pick):
    b = pl.program_id(0)
    blk_per_page = page // SEL_BLOCK
    nbs = past // SEL_BLOCK
    ncs = kvc_ref.shape[2]
    row = lax.broadcasted_iota(jnp.int32, (SUB, 1), 0)

    cmp_ok = lax.broadcasted_iota(jnp.int32, (SUB, ncs), 1) * CMP_STRIDE + (CMP_BLOCK - 1) <= past
    imp = jnp.zeros((SUB, ncs), F32)
    o_cmp = []
    for k in range(N_KV_HEADS):
        half = k % HEADS_PER_CG
        ck = kvc_ref[0, k // HEADS_PER_CG].astype(BF16)
        cv = kvc_ref[0, CG_PER_SLOT + k // HEADS_PER_CG].astype(BF16)
        s = lax.dot_general(qs_ref[0, k], ck, _NT, preferred_element_type=F32) + cbs_ref[k]
        m = jnp.max(jnp.where(cmp_ok, s, NEG), axis=-1, keepdims=True)
        e = jnp.where(cmp_ok, jnp.exp(s - m), 0.0)
        p = e / jnp.maximum(jnp.sum(e, axis=-1, keepdims=True), 1e-30)
        o_cmp.append(jnp.dot(p.astype(BF16), cv, preferred_element_type=F32)[:, half * HEAD_DIM:(half + 1) * HEAD_DIM])
        imp_k = p[0:1]
        for g in range(1, GROUP):
            imp_k = imp_k + p[g:g + 1]
        imp = imp + jnp.where(row == k, imp_k, 0.0)
    bs = sum(jnp.dot(part, as_ref[...], preferred_element_type=F32) for part in _split_bf16(imp, 3))

    lane_b = lax.broadcasted_iota(jnp.int32, (SUB, nbs), 1)
    lane_f = lane_b.astype(F32)
    score = bs + jnp.where((lane_b == 0) | (lane_b == nbs - 1), FORCE_BONUS, 0.0)

    def gathers(k, r):
        j = idx_ref[k * n_pick + r]
        pg = pt_ref[b, j // blk_per_page]
        tile0 = (row_base + pg) * N_KV_SLOTS
        return [pltpu.make_async_copy(
            cache_ref.at[pl.ds(((tile0 + slot) * N_KV_HEADS + k) * HEAD_DIM, HEAD_DIM), :],
            buf.at[k * n_pick + r], sem_ref.at[0]) for slot, buf in ((2, gk_ref), (3, gv_ref))]

    for r in range(n_pick):
        m = jnp.max(score, axis=-1, keepdims=True)
        pick = jnp.min(jnp.where(score == m, lane_f, 1e9), axis=-1, keepdims=True)
        score = jnp.where(lane_f == pick, -jnp.inf, score)
        for k in range(N_KV_HEADS):
            idx_ref[k * n_pick + r] = jnp.sum(jnp.where(row == k, pick, 0.0)).astype(jnp.int32)
            for cp in gathers(k, r):
                cp.start()

    def new_row(k, j):
        return new_ref[0, k][j:j + 1].astype(BF16).astype(F32)

    tok = lax.broadcasted_iota(jnp.int32, (1, WINDOW), 1)
    o_win = []
    for k in range(N_KV_HEADS):
        for slot in range(N_WIN_SLOTS):
            shifted = pltpu.roll(win_ref[0, slot, k], WINDOW - 1, axis=1)
            wout_ref[0, slot, k] = jnp.where(tok == WINDOW - 1, ncol_ref[0, slot, k], shifted)
        q = q_ref[0, k]
        s = jnp.dot(q, win_ref[0, 0, k].astype(BF16), preferred_element_type=F32) + wb_ref[k]
        s = jnp.where(tok >= 1, s, NEG)
        s_new = jnp.sum(q.astype(F32) * new_row(k, 2), axis=-1, keepdims=True) + b0_ref[k][:, 0:1]
        p, p_new = _softmax_with_new_key(s, s_new)
        o_win.append(lax.dot_general(p.astype(BF16), win_ref[0, 1, k].astype(BF16), _NT,
                                     preferred_element_type=F32) + p_new * new_row(k, 3))

    for k in range(N_KV_HEADS):
        for r in range(n_pick):
            for cp in gathers(k, r):
                cp.wait()

    upper = lax.broadcasted_iota(jnp.int32, (1, LANES), 1) // SEL_BLOCK
    for k in range(N_KV_HEADS):
        q = q_ref[0, k]
        tiles = []
        for r in range(n_pick):
            j = idx_ref[k * n_pick + r]
            near = j - (nbs - NEAR // SEL_BLOCK)
            s = jnp.dot(q, gk_ref[k * n_pick + r].astype(BF16), preferred_element_type=F32)
            s = s + sbz_ref[k, jnp.where(near >= 0, near, NEAR // SEL_BLOCK)]
            tiles.append(jnp.where(upper == j % blk_per_page, s, NEG))
        s_new = jnp.sum(q.astype(F32) * new_row(k, 0), axis=-1, keepdims=True) + b0_ref[k][:, 0:1]
        p, p_new = _softmax_with_new_key(jnp.concatenate(tiles, axis=1), s_new)
        o_sel = p_new * new_row(k, 1)
        for r in range(n_pick):
            o_sel = o_sel + lax.dot_general(p[:, r * LANES:(r + 1) * LANES].astype(BF16),
                                            gv_ref[k * n_pick + r].astype(BF16), _NT, preferred_element_type=F32)
        g_cmp, g_sel, g_win = (jax.nn.sigmoid(gl_ref[0, k, j])[:, :HEAD_DIM] for j in range(3))
        o_ref[0, k] = g_cmp * o_cmp[k] + g_sel * o_sel + g_win * o_win[k]


def _bias_rows(bvz, idx):
    t = jnp.take(bvz, jnp.asarray(idx, jnp.int32), axis=0).reshape(len(idx), N_KV_HEADS, GROUP)
    t = jnp.transpose(t, (1, 2, 0))
    return jnp.pad(t, ((0, 0), (0, SUB - GROUP), (0, 0)))


def _nsa_sample(proj_s, kvc, page_table, cache_t, win_t, bvz, past, page, row_base):
    nb = proj_s.shape[0]
    n_pick = N_SELECT - 1
    nbs = past // SEL_BLOCK
    ncs = past // CMP_STRIDE
    n_near = NEAR // SEL_BLOCK
    blk_per_page = page // SEL_BLOCK
    assert past % page == 0 and page == LANES and nbs > n_pick + n_near and (nbs - n_near) % blk_per_page == 0
    assert win_t.shape[-1] == WINDOW < past
    q = (proj_s[:, :N_Q_COLS] * HEAD_DIM ** -0.5).astype(BF16).reshape(nb, N_KV_HEADS, GROUP, HEAD_DIM)
    q = jnp.pad(q, ((0, 0), (0, 0), (0, SUB - GROUP), (0, 0)))
    qs = jnp.stack([jnp.pad(q[:, k], ((0, 0), (0, 0), ((k % HEADS_PER_CG) * HEAD_DIM,
                                                       LANES - (k % HEADS_PER_CG + 1) * HEAD_DIM)))
                    for k in range(N_KV_HEADS)], axis=1)
    gl = proj_s[:, GATE_COL0:].reshape(nb, N_KV_HEADS, LANES)[:, :, :GROUP * 3].reshape(nb, N_KV_HEADS, GROUP, 3)
    gl = jnp.pad(jnp.transpose(gl, (0, 1, 3, 2)), ((0, 0), (0, 0), (0, 0), (0, SUB - GROUP)))
    gl = jnp.broadcast_to(gl[..., None], gl.shape + (LANES,))
    new = _unpermute_kv(proj_s[:, N_Q_COLS:GATE_COL0], (nb,))[:, 2:]
    new_rows = jnp.transpose(new, (0, 2, 1, 3))
    new_cols = new[:, 2:].reshape(nb, N_WIN_SLOTS, N_KV_HEADS, HEAD_DIM, 1)

    dc = past - (np.arange(ncs) * CMP_STRIDE + CMP_BLOCK - 1)
    cbs = _bias_rows(bvz, np.where((dc >= 0) & (dc < NEAR), dc, NEAR))
    lane = np.arange(LANES)
    sb = []
    for jj in range(n_near + 1):
        d = NEAR - SEL_BLOCK * jj - lane % SEL_BLOCK
        ok = (lane // SEL_BLOCK == jj % blk_per_page) & (jj < n_near) & (d < NEAR)
        sb.append(_bias_rows(bvz, np.where(ok, d, NEAR)))
    sbz = jnp.stack(sb, axis=1)
    dw = WINDOW - np.arange(WINDOW)
    wb = _bias_rows(bvz, np.where(dw < NEAR, dw, NEAR))
    b0 = _bias_rows(bvz, np.zeros(LANES, np.int64))
    a_s = np.zeros((ncs, nbs), np.float32)
    for j in range(nbs):
        for n in range(SEL_RATIO * j - (CMP_RATIO - 1), SEL_RATIO * (j + 1)):
            if 0 <= n < ncs - CMP_RATIO + 1:
                a_s[n, j] = 1.0
    a_s = jnp.asarray(a_s, BF16)

    per_b = lambda a: pl.BlockSpec((1,) + a.shape[1:], lambda b, pt: (b,) + (0,) * (a.ndim - 1))
    full = lambda a: pl.BlockSpec(a.shape, lambda b, pt: (0,) * a.ndim)
    grid_spec = pltpu.PrefetchScalarGridSpec(
        num_scalar_prefetch=1,
        grid=(nb,),
        in_specs=[per_b(qs), per_b(q), per_b(gl), per_b(new_rows), per_b(new_cols), per_b(kvc),
                  pl.BlockSpec(memory_space=pl.ANY), per_b(win_t),
                  full(cbs), full(sbz), full(wb), full(b0), full(a_s)],
        out_specs=[pl.BlockSpec((1, N_KV_HEADS, SUB, HEAD_DIM), lambda b, pt: (b, 0, 0, 0)), per_b(win_t)],
        scratch_shapes=[pltpu.VMEM((N_KV_HEADS * n_pick, HEAD_DIM, page), F32),
                        pltpu.VMEM((N_KV_HEADS * n_pick, HEAD_DIM, page), F32),
                        pltpu.SemaphoreType.DMA((1,)),
                        pltpu.SMEM((N_KV_HEADS * n_pick,), jnp.int32)])
    o, win_out = pl.pallas_call(
        functools.partial(_nsa_sample_kernel, past=past, page=page, row_base=row_base, n_pick=n_pick),
        grid_spec=grid_spec,
        out_shape=[jax.ShapeDtypeStruct((nb, N_KV_HEADS, SUB, HEAD_DIM), F32),
                   jax.ShapeDtypeStruct(win_t.shape, F32)],
        compiler_params=_cparams(("arbitrary",)),
    )(page_table, qs, q, gl, new_rows, new_cols, kvc, cache_t, win_t, cbs, sbz, wb, b0, a_s)
    return o[:, :, :GROUP].reshape(nb, N_Q_COLS), win_out


MXU_WIDTH = 256
SSM_COL_TILE = 5 * MXU_WIDTH
NP_SSM = -(-(SSM_CONV_DIM + SSM_D_INNER + SSM_HEADS) // SSM_COL_TILE) * SSM_COL_TILE
N_BC = SSM_GROUPS * SSM_STATE
HEADS_PER_GROUP = SSM_HEADS // SSM_GROUPS
PAIR = LANES // SSM_HEAD_DIM
N_PAIRS = SSM_HEADS // PAIR
CONV_PAD = 8


def _softplus(x):
    return jnp.maximum(x, 0.0) + jnp.log(1.0 + jnp.exp(-jnp.abs(x)))


def _cumsum_rows(x):
    n = x.shape[0]
    row = lax.broadcasted_iota(jnp.int32, x.shape, 0)
    s = 1
    while s < n:
        x = x + jnp.where(row >= s, pltpu.roll(x, s, axis=0), 0.0)
        s *= 2
    return x


def _ssd_kernel(xbc_ref, z_ref, dt_ref, h0_ref, cinit_ref, cw_ref, cb_ref, dtb_ref, alog_ref, dsk_ref, ng_ref,
                y_ref, hout_ref, xs_ref, act_ref, ybuf_ref, h_ref, *, nc):
    c = pl.program_id(1)
    q = SSM_CHUNK

    @pl.when(c == 0)
    def _():
        xs_ref[...] = cinit_ref[0]
        h_ref[...] = h0_ref[0]

    x = xbc_ref[...]
    prev = xs_ref[...]
    head_row = lax.broadcasted_iota(jnp.int32, (CONV_PAD, 1), 0)
    conv = cb_ref[...] + cw_ref[SSM_CONV - 1:SSM_CONV] * x
    for j in range(1, SSM_CONV):
        xj = pltpu.roll(x, j, axis=0)
        head = jnp.where(head_row < j, pltpu.roll(prev, j, axis=0), xj[:CONV_PAD])
        xj = jnp.concatenate([head, xj[CONV_PAD:]], axis=0)
        conv = conv + cw_ref[SSM_CONV - 1 - j:SSM_CONV - j] * xj
    xs_ref[...] = x[q - CONV_PAD:]
    act_ref[...] = _silu(conv)

    dt = _softplus(dt_ref[...] + dtb_ref[...])
    acum = _cumsum_rows(dt * (-jnp.exp(alog_ref[...])))
    acum_t = acum.T
    dt_t = dt.T
    last = acum[q - 1:q, :]
    ea = jnp.exp(acum)
    te = jnp.exp(last - acum) * dt
    cd = jnp.exp(last)
    ii = lax.broadcasted_iota(jnp.int32, (q, q), 0)
    jj = lax.broadcasted_iota(jnp.int32, (q, q), 1)
    tri = ii >= jj
    low = jj < SSM_HEAD_DIM
    low_rows = ii < SSM_HEAD_DIM

    def col(a, h):
        return a[:, h:h + 1]

    for g in range(SSM_GROUPS):
        bg = act_ref[:, SSM_D_INNER + g * SSM_STATE:SSM_D_INNER + (g + 1) * SSM_STATE].astype(BF16)
        cg = act_ref[:, SSM_D_INNER + N_BC + g * SSM_STATE:SSM_D_INNER + N_BC + (g + 1) * SSM_STATE].astype(BF16)
        cbg = lax.dot_general(cg, bg, _NT, preferred_element_type=F32)
        for pr in range(HEADS_PER_GROUP // PAIR):
            k = g * (HEADS_PER_GROUP // PAIR) + pr
            ha, hb = PAIR * k, PAIR * k + 1
            xp = act_ref[:, k * LANES:(k + 1) * LANES]
            xpb = xp.astype(BF16)
            ys = []
            for h in (ha, hb):
                decay = jnp.exp(jnp.where(tri, col(acum, h) - acum_t[h:h + 1, :], NEG))
                w = cbg * decay * dt_t[h:h + 1, :]
                ys.append(jnp.dot(w.astype(BF16), xpb, preferred_element_type=F32))
            y = jnp.where(low, ys[0], ys[1])
            xs_pair = xp * jnp.where(low, col(te, ha), col(te, hb))
            st = jnp.dot(xs_pair.T.astype(BF16), bg, preferred_element_type=F32)
            hprev = h_ref[k]
            yoff = lax.dot_general(cg, hprev.astype(BF16), _NT, preferred_element_type=F32)
            y = y + yoff * jnp.where(low, col(ea, ha), col(ea, hb)) + dsk_ref[:, k * LANES:(k + 1) * LANES] * xp
            h_ref[k] = hprev * jnp.where(low_rows, cd[:, ha:ha + 1], cd[:, hb:hb + 1]) + st
            ybuf_ref[:, k * LANES:(k + 1) * LANES] = y

    yz = ybuf_ref[...] * _silu(z_ref[...])
    gw = SSM_D_INNER // SSM_GROUPS
    outs = []
    for g in range(SSM_GROUPS):
        seg = yz[:, g * gw:(g + 1) * gw]
        outs.append(seg * lax.rsqrt(jnp.mean(seg * seg, axis=-1, keepdims=True) + EPS))
    y_ref[...] = (jnp.concatenate(outs, axis=1) * ng_ref[...]).astype(y_ref.dtype)

    @pl.when(c == nc - 1)
    def _():
        hout_ref[0] = h_ref[...]


def _ssd(proj, h0, cinit, sw, nb, t):
    q = SSM_CHUNK
    nc = t // q
    vec = lambda a: pl.BlockSpec(a.shape, lambda b, c: (0, 0))
    return pl.pallas_call(
        functools.partial(_ssd_kernel, nc=nc),
        grid=(nb, nc),
        in_specs=[pl.BlockSpec((q, SSM_CONV_DIM), lambda b, c: (b * nc + c, 0)),
                  pl.BlockSpec((q, SSM_D_INNER), lambda b, c: (b * nc + c, SSM_CONV_DIM // SSM_D_INNER)),
                  pl.BlockSpec((q, LANES), lambda b, c: (b * nc + c, (SSM_CONV_DIM + SSM_D_INNER) // LANES)),
                  pl.BlockSpec((1,) + h0.shape[1:], lambda b, c: (b, 0, 0, 0)),
                  pl.BlockSpec((1,) + cinit.shape[1:], lambda b, c: (b, 0, 0)),
                  vec(sw['conv_w']), vec(sw['conv_b']), vec(sw['dt_bias']), vec(sw['a_log']), vec(sw['d_lane']),
                  vec(sw['norm_g'])],
        out_specs=[pl.BlockSpec((q, SSM_D_INNER), lambda b, c: (b * nc + c, 0)),
                   pl.BlockSpec((1,) + h0.shape[1:], lambda b, c: (b, 0, 0, 0))],
        out_shape=[jax.ShapeDtypeStruct((nb * t, SSM_D_INNER), BF16),
                   jax.ShapeDtypeStruct(h0.shape, F32)],
        scratch_shapes=[pltpu.VMEM((CONV_PAD, SSM_CONV_DIM), F32),
                        pltpu.VMEM((q, SSM_CONV_DIM), F32),
                        pltpu.VMEM((q, SSM_D_INNER), F32),
                        pltpu.VMEM(h0.shape[1:], F32)],
        compiler_params=_cparams(("arbitrary", "arbitrary")),
    )(proj, proj, proj, h0, cinit, sw['conv_w'], sw['conv_b'], sw['dt_bias'], sw['a_log'], sw['d_lane'],
      sw['norm_g'])


def _ssm_sample_step(proj_s, state_ssm, state_conv, sw):
    nb = proj_s.shape[0]
    q = SSM_CHUNK
    xbc_new = proj_s[:, :SSM_CONV_DIM]
    n_tail = NP_SSM - SSM_CONV_DIM
    tail = jnp.where(jnp.arange(n_tail) < SSM_D_INNER, 0.0, NEG).astype(F32)
    rows = jnp.concatenate([
        jnp.concatenate([jnp.zeros((nb, q - SSM_CONV, SSM_CONV_DIM), F32), state_conv.astype(F32)], axis=1),
        jnp.broadcast_to(tail, (nb, q - 1, n_tail))], axis=2)
    rows = jnp.concatenate([rows, proj_s[:, None]], axis=1)
    h0 = state_ssm.astype(F32).reshape(nb, N_PAIRS, LANES, SSM_STATE)
    cinit = jnp.zeros((nb, CONV_PAD, SSM_CONV_DIM), F32)
    yn, hfin = _ssd(rows.reshape(nb * q, NP_SSM), h0, cinit, sw, nb, q)
    conv_new = jnp.concatenate([state_conv[:, 1:].astype(F32), xbc_new[:, None]], axis=1)
    return yn.reshape(nb, q, SSM_D_INNER)[:, q - 1], hfin.reshape(state_ssm.shape), conv_new


def _ssm_weights(w_in, conv_w, conv_b, dt_bias, a_log, d_skip, norm_g):
    z_w = w_in[:, :SSM_D_INNER]
    xbc_w = w_in[:, SSM_D_INNER:SSM_D_INNER + SSM_CONV_DIM]
    dt_w = w_in[:, SSM_D_INNER + SSM_CONV_DIM:]
    pad = NP_SSM - w_in.shape[1]
    w = jnp.concatenate([xbc_w, z_w, dt_w, jnp.zeros((w_in.shape[0], pad), w_in.dtype)], axis=1).astype(BF16)
    lane_pad = lambda v: jnp.zeros((1, LANES), F32).at[0, :SSM_HEADS].set(v.astype(F32))
    return dict(w_in=w, conv_w=conv_w.astype(F32), conv_b=conv_b.astype(F32).reshape(1, -1),
                dt_bias=lane_pad(dt_bias), a_log=lane_pad(a_log),
                d_lane=jnp.repeat(d_skip.astype(F32), SSM_HEAD_DIM).reshape(1, -1),
                norm_g=norm_g.astype(F32).reshape(1, -1))


ROW_TILE = 512
MLP_ROW_TILE = 1024
FF_TILE = 1024


def kernel(x_prompt, x_sample, cache_kv, cache_win, state_ssm, state_conv, page_table, c_prompt, c_sample, rel_bias,
           ada_w, ada_b, norm_g, mlp_w1, mlp_w2, attn_w_in, attn_w_out, cmp_pos, cmp_w1, cmp_w2, ssm_w_in,
           ssm_conv_w, ssm_conv_b, ssm_dt_bias, ssm_a_log, ssm_d, ssm_norm_g, ssm_w_out):
    nb, t, d = x_prompt.shape
    db = x_sample.shape[0]
    assert x_sample.shape[1] == 1 and t % SSM_CHUNK == 0 and t % Q_BLOCK == 0
    n_pool, page = cache_kv.shape[1], cache_kv.shape[2]
    past = page_table.shape[1] * page
    depth = ada_w.shape[0]
    tm = _row_tile(t, ROW_TILE)

    xp = x_prompt.reshape(nb * t, d).astype(F32)
    xs = x_sample.reshape(db, d).astype(F32)
    c_all = jnp.concatenate([c_prompt, c_sample], axis=0).astype(F32)
    c_all = jnp.pad(c_all, ((0, (-c_all.shape[0]) % SUB), (0, 0)))
    bvz = _bias_vector(rel_bias)
    cache_t = _cache_tiles(cache_kv)
    win_t = jnp.transpose(cache_win, (0, 1, 3, 4, 5, 2))

    kv_p, win_p, ssm_p, conv_p, kv_s, win_s, ssm_s, conv_s = ([] for _ in range(8))
    for i in range(depth):
        mod = _ada(c_all, ada_w[i], ada_b[i])
        mp = [mod[:nb, j * d:(j + 1) * d].reshape(nb, 1, d) for j in range(N_MOD)]
        ms = [mod[nb:nb + db, j * d:(j + 1) * d].reshape(1, db, d) for j in range(N_MOD)]
        g = norm_g[i].astype(F32)
        if i % 2 == 0:
            a = i // 2
            w_in = _attn_in_weights(attn_w_in[a])
            w_out = attn_w_out[a].astype(BF16)
            cw = _compress_weights(cmp_pos[a], cmp_w1[a], cmp_w2[a], [(0, 1)])
            o_p, kv_new, win_new_p = _nsa_prompt_mixer(xp, g[0], mp[0], mp[1], w_in, cw, rel_bias, nb, t)
            kv_p.append(kv_new)
            win_p.append(win_new_p)
            xp = _heads_out(o_p, _attn_out_weights(attn_w_out[a]), g[1], mp[2], xp, tm)

            proj_s = _nm_matmul(xs, g[0], ms[0], ms[1], w_in, db, NP_ATTN)
            cw_s = _compress_weights(cmp_pos[a], cmp_w1[a], cmp_w2[a], [(0, 0), (1, 1)])
            kvc_s = _compress_sample(page_table, cache_t, cw_s, page, a * n_pool)
            o_s, win_new = _nsa_sample(proj_s, kvc_s, page_table, cache_t, win_t[a], bvz, past, page, a * n_pool)
            kv_s.append(_unpermute_kv(proj_s[:, N_Q_COLS:GATE_COL0], (db, 1))[:, :, :N_KV_SLOTS])
            win_s.append(jnp.transpose(win_new, (0, 4, 1, 2, 3)))
            xs = _mm_norm_res(o_s, w_out, g[1], ms[2], xs, db)
        else:
            m = i // 2
            sw = _ssm_weights(ssm_w_in[m], ssm_conv_w[m], ssm_conv_b[m], ssm_dt_bias[m], ssm_a_log[m], ssm_d[m],
                              ssm_norm_g[m])
            w_out = ssm_w_out[m].astype(BF16)
            proj = _nm_matmul(xp, g[0], mp[0], mp[1], sw['w_in'], tm, SSM_COL_TILE)
            h0 = jnp.zeros((nb, N_PAIRS, LANES, SSM_STATE), F32)
            cinit = jnp.zeros((nb, CONV_PAD, SSM_CONV_DIM), F32)
            yn, hfin = _ssd(proj, h0, cinit, sw, nb, t)
            ssm_p.append(hfin.reshape(nb, SSM_HEADS, SSM_HEAD_DIM, SSM_STATE).astype(state_ssm.dtype))
            conv_p.append(proj.reshape(nb, t, NP_SSM)[:, t - (SSM_CONV - 1):, :SSM_CONV_DIM])
            xp = _mm_norm_res(yn, w_out, g[1], mp[2], xp, tm)

            proj_s = _nm_matmul(xs, g[0], ms[0], ms[1], sw['w_in'], db, SSM_COL_TILE)
            yn_s, h_s, conv_new = _ssm_sample_step(proj_s, state_ssm[m], state_conv[m], sw)
            ssm_s.append(h_s.astype(state_ssm.dtype))
            conv_s.append(conv_new)
            xs = _mm_norm_res(yn_s, w_out, g[1], ms[2], xs, db)
        w1 = mlp_w1[i].astype(BF16)
        w2 = mlp_w2[i].astype(BF16)
        xp = _mlp(xp, g[2], mp[3], mp[4], w1, w2, g[3], mp[5], _row_tile(t, MLP_ROW_TILE), FF_TILE)
        xs = _mlp(xs, g[2], ms[3], ms[4], w1, w2, g[3], ms[5], db, FF_TILE)
    return (xp.reshape(nb, t, d), xs.reshape(db, 1, d), jnp.stack(kv_p), jnp.stack(win_p), jnp.stack(ssm_p),
            jnp.stack(conv_p), jnp.stack(kv_s), jnp.stack(win_s), jnp.stack(ssm_s), jnp.stack(conv_s))
```

```python
import functools
import math

import numpy as np
import jax
import jax.numpy as jnp
from jax import lax
from jax.experimental import pallas as pl
from jax.experimental.pallas import tpu as pltpu

F32 = jnp.float32
BF16 = jnp.bfloat16

D_MODEL = 1024
N_HEADS = 16
HEAD_DIM = 64
N_KV_HEADS = 4
GROUP = N_HEADS // N_KV_HEADS
CMP_BLOCK = 32
CMP_STRIDE = 16
CMP_RATIO = CMP_BLOCK // CMP_STRIDE
CMP_HIDDEN = 2 * HEAD_DIM
SEL_BLOCK = 64
SEL_RATIO = SEL_BLOCK // CMP_STRIDE
N_SELECT = 16
WINDOW = 512
FORCE_BONUS = 1e4
Q_BLOCK = 128
N_KV_SLOTS = 4
N_WIN_SLOTS = 2
N_Q_COLS = N_HEADS * HEAD_DIM
N_KV_COLS = (N_KV_SLOTS + N_WIN_SLOTS) * N_KV_HEADS * HEAD_DIM
N_GATE_COLS = 3 * N_HEADS
KV_ROW = N_KV_HEADS * HEAD_DIM
KV_HEAD_COLS = (N_KV_SLOTS + N_WIN_SLOTS) * HEAD_DIM
N_BUCKETS = 32
MAX_DISTANCE = 128
SSM_D_INNER = 2 * D_MODEL
SSM_HEAD_DIM = 64
SSM_HEADS = SSM_D_INNER // SSM_HEAD_DIM
SSM_GROUPS = 8
SSM_STATE = 128
SSM_CONV = 4
SSM_CONV_DIM = SSM_D_INNER + 2 * SSM_GROUPS * SSM_STATE
SSM_CHUNK = 128
D_FF = 4 * D_MODEL
N_MOD = 6
EPS = 1e-6

LANES = 128
NEG = -1e30
LOG2E = math.log2(math.e)
VMEM_LIMIT = 56 * 1024 * 1024
NEAR = 2 * LANES

_NT = (((1,), (1,)), ((), ()))


def _cparams(sem):
    return pltpu.CompilerParams(dimension_semantics=sem, vmem_limit_bytes=VMEM_LIMIT)


def _rms(x, g):
    return x * lax.rsqrt(jnp.mean(x * x, axis=-1, keepdims=True) + EPS) * g


def _silu(x):
    return x * jax.nn.sigmoid(x)


def _split_bf16(x, n):
    parts = []
    for _ in range(n - 1):
        p = x.astype(BF16)
        parts.append(p)
        x = x - p.astype(F32)
    parts.append(x.astype(BF16))
    return parts


def _ada_kernel(c_ref, w_ref, b_ref, o_ref):
    s = _silu(c_ref[...]).astype(BF16)
    o_ref[...] = jnp.dot(s, w_ref[...].astype(BF16), preferred_element_type=F32) + b_ref[...]


def _ada(c, w, b, tn=1024):
    m, d = c.shape
    n = w.shape[1]
    return pl.pallas_call(
        _ada_kernel,
        grid=(n // tn,),
        in_specs=[pl.BlockSpec((m, d), lambda j: (0, 0)),
                  pl.BlockSpec((d, tn), lambda j: (0, j)),
                  pl.BlockSpec((1, tn), lambda j: (0, j))],
        out_specs=pl.BlockSpec((m, tn), lambda j: (0, j)),
        out_shape=jax.ShapeDtypeStruct((m, n), F32),
        compiler_params=_cparams(("arbitrary",)),
    )(c, w, b.reshape(1, n))


def _nm_matmul_kernel(x_ref, g_ref, sh_ref, sc_ref, w_ref, o_ref, h_ref):
    @pl.when(pl.program_id(1) == 0)
    def _():
        h = _rms(x_ref[...], g_ref[...]) * (1.0 + sc_ref[0]) + sh_ref[0]
        h_ref[...] = h.astype(BF16)

    o_ref[...] = jnp.dot(h_ref[...], w_ref[...], preferred_element_type=F32)


def _nm_matmul(x, g, shift, scale, w, tm, tn):
    m, d = x.shape
    n = w.shape[1]
    nb = shift.shape[0]
    tpb = m // nb // tm
    mod_spec = pl.BlockSpec((1,) + shift.shape[1:], lambda i, j: (i // tpb, 0, 0))
    return pl.pallas_call(
        _nm_matmul_kernel,
        grid=(m // tm, n // tn),
        in_specs=[pl.BlockSpec((tm, d), lambda i, j: (i, 0)),
                  pl.BlockSpec((1, d), lambda i, j: (0, 0)),
                  mod_spec, mod_spec,
                  pl.BlockSpec((d, tn), lambda i, j: (0, j))],
        out_specs=pl.BlockSpec((tm, tn), lambda i, j: (i, j)),
        out_shape=jax.ShapeDtypeStruct((m, n), F32),
        scratch_shapes=[pltpu.VMEM((tm, d), BF16)],
        compiler_params=_cparams(("arbitrary", "arbitrary")),
    )(x, g.reshape(1, d), shift, scale, w)


def _mm_norm_res_kernel(a_ref, w_ref, g_ref, gate_ref, x_ref, o_ref):
    y = jnp.dot(a_ref[...].astype(BF16), w_ref[...], preferred_element_type=F32)
    o_ref[...] = x_ref[...] + gate_ref[0] * _rms(y, g_ref[...])


def _mm_norm_res(a, w, g, gate, x, tm):
    m, k = a.shape
    d = w.shape[1]
    nb = gate.shape[0]
    tpb = m // nb // tm
    return pl.pallas_call(
        _mm_norm_res_kernel,
        grid=(m // tm,),
        in_specs=[pl.BlockSpec((tm, k), lambda i: (i, 0)),
                  pl.BlockSpec((k, d), lambda i: (0, 0)),
                  pl.BlockSpec((1, d), lambda i: (0, 0)),
                  pl.BlockSpec((1,) + gate.shape[1:], lambda i: (i // tpb, 0, 0)),
                  pl.BlockSpec((tm, d), lambda i: (i, 0))],
        out_specs=pl.BlockSpec((tm, d), lambda i: (i, 0)),
        out_shape=jax.ShapeDtypeStruct((m, d), F32),
        compiler_params=_cparams(("arbitrary",)),
    )(a, w, g.reshape(1, d), gate, x)


def _mlp_kernel(x_ref, g1_ref, sh_ref, sc_ref, w1_ref, w2_ref, g2_ref, gate_ref, o_ref, h_ref, acc_ref):
    c = pl.program_id(1)

    @pl.when(c == 0)
    def _():
        h = _rms(x_ref[...], g1_ref[...]) * (1.0 + sc_ref[0]) + sh_ref[0]
        h_ref[...] = h.astype(BF16)
        acc_ref[...] = jnp.zeros_like(acc_ref)

    a = jnp.dot(h_ref[...], w1_ref[...], preferred_element_type=F32)
    a = jnp.square(jnp.maximum(a, 0.0)).astype(BF16)
    acc_ref[...] += jnp.dot(a, w2_ref[...], preferred_element_type=F32)

    @pl.when(c == pl.num_programs(1) - 1)
    def _():
        o_ref[...] = x_ref[...] + gate_ref[0] * _rms(acc_ref[...], g2_ref[...])


def _mlp(x, g1, shift, scale, w1, w2, g2, gate, tm, tf):
    m, d = x.shape
    f = w1.shape[1]
    nb = shift.shape[0]
    tpb = m // nb // tm
    mod_spec = pl.BlockSpec((1,) + shift.shape[1:], lambda i, c: (i // tpb, 0, 0))
    vec_spec = pl.BlockSpec((1, d), lambda i, c: (0, 0))
    return pl.pallas_call(
        _mlp_kernel,
        grid=(m // tm, f // tf),
        in_specs=[pl.BlockSpec((tm, d), lambda i, c: (i, 0)), vec_spec, mod_spec, mod_spec,
                  pl.BlockSpec((d, tf), lambda i, c: (0, c)),
                  pl.BlockSpec((tf, d), lambda i, c: (c, 0)),
                  vec_spec, mod_spec],
        out_specs=pl.BlockSpec((tm, d), lambda i, c: (i, 0)),
        out_shape=jax.ShapeDtypeStruct((m, d), F32),
        scratch_shapes=[pltpu.VMEM((tm, d), BF16), pltpu.VMEM((tm, d), F32)],
        compiler_params=_cparams(("arbitrary", "arbitrary")),
    )(x, g1.reshape(1, d), shift, scale, w1, w2, g2.reshape(1, d), gate)


def _bias_kernel(oh_ref, t_ref, o_ref):
    t = t_ref[...]
    t = t - t[N_BUCKETS - 1:N_BUCKETS, :]
    oh = oh_ref[...]
    o_ref[...] = sum(jnp.dot(oh, p, preferred_element_type=F32) for p in _split_bf16(t, 3))


def _rel_bucket_np(dist):
    n = np.maximum(dist, 0)
    exact = N_BUCKETS // 2
    nf = np.maximum(n, 1).astype(np.float32)
    large = exact + (np.log(nf / exact) / math.log(MAX_DISTANCE / exact) * (N_BUCKETS - exact)).astype(np.int32)
    return np.where(n < exact, n, np.minimum(large, N_BUCKETS - 1))


def _bias_vector(rel_bias):
    assert _rel_bucket_np(np.array([NEAR - 1]))[0] == N_BUCKETS - 1 == _rel_bucket_np(np.array([MAX_DISTANCE]))[0]
    oh = np.zeros((NEAR + 8, LANES), np.float32)
    oh[np.arange(NEAR), _rel_bucket_np(np.arange(NEAR))] = 1.0
    oh[NEAR:, N_BUCKETS - 1] = 1.0
    table = jnp.zeros((LANES, LANES), F32).at[:N_BUCKETS, :N_HEADS].set(rel_bias.astype(F32))
    bv = pl.pallas_call(
        _bias_kernel,
        out_shape=jax.ShapeDtypeStruct((NEAR + 8, LANES), F32),
    )(jnp.asarray(oh, BF16), table)
    return bv[:NEAR + 1, :N_HEADS]


SUB = 8
HEADS_PER_CG = LANES // HEAD_DIM
N_CG = 2 * KV_ROW // LANES
CG_PER_SLOT = KV_ROW // LANES
CG_HIDDEN = HEADS_PER_CG * CMP_HIDDEN


def _compress_parts(rows_s, wbd_ref, nh):
    part0 = jnp.zeros((nh, CG_HIDDEN), F32)
    part1 = jnp.zeros((nh, CG_HIDDEN), F32)
    for s in range(0, CMP_STRIDE, 2):
        xs = jnp.concatenate([rows_s(s), rows_s(s + 1)], axis=1).astype(BF16)
        part0 = part0 + jnp.dot(xs, wbd_ref[0, 0, s // 2], preferred_element_type=F32)
        part1 = part1 + jnp.dot(xs, wbd_ref[0, 1, s // 2], preferred_element_type=F32)
    return part0, part1


def _compress_finish(part0, part1, pos_ref, w1_ref, w2bd_ref, nh):
    pre0 = jnp.dot(pos_ref[0], w1_ref[0], preferred_element_type=F32)[0:1]
    pre = pre0 + part0 + pltpu.roll(part1, nh - 1, axis=0)
    return jnp.dot(_silu(pre).astype(BF16), w2bd_ref[0], preferred_element_type=F32)


def _compress_kernel(x_ref, pos_ref, w1_ref, wbd_ref, w2bd_ref, o_ref, *, nh):
    part0, part1 = _compress_parts(lambda s: x_ref[pl.ds(s, nh, stride=CMP_STRIDE), :], wbd_ref, nh)
    o_ref[0, 0] = _compress_finish(part0, part1, pos_ref, w1_ref, w2bd_ref, nh)


def _block_diag(a, b):
    za = jnp.zeros(a.shape[:-1] + (b.shape[-1],), a.dtype)
    zb = jnp.zeros(b.shape[:-1] + (a.shape[-1],), b.dtype)
    return jnp.concatenate([jnp.concatenate([a, za], axis=-1), jnp.concatenate([zb, b], axis=-1)], axis=-2)


def _compress_weight_set(pos, w1, w2, a, b):
    w1r = w1.reshape(2, CMP_RATIO, CMP_STRIDE, HEAD_DIM, CMP_HIDDEN)
    wbd = _block_diag(w1r[a], w1r[b]).reshape(CMP_RATIO, CMP_STRIDE // 2, 2 * LANES, CG_HIDDEN)
    posx = jnp.zeros((SUB, 2 * CMP_BLOCK * HEAD_DIM), F32).at[0].set(
        jnp.concatenate([pos[a].reshape(-1), pos[b].reshape(-1)]))
    return [x.astype(BF16) for x in (posx, _block_diag(w1[a], w1[b]), wbd, _block_diag(w2[a], w2[b]))]


def _compress_weights(cmp_pos, cmp_w1, cmp_w2, pairs):
    sets = [_compress_weight_set(cmp_pos, cmp_w1, cmp_w2, a, b) for a, b in pairs]
    return [jnp.stack(x) for x in zip(*sets)]


def _compress_specs(cw, imap):
    def spec(a):
        return pl.BlockSpec((1,) + a.shape[1:], lambda *g: (imap(*g),) + (0,) * (a.ndim - 1))
    return [spec(a) for a in cw]


def _compress_prompt(proj, cw, nb, t):
    nh = t // CMP_STRIDE
    return pl.pallas_call(
        functools.partial(_compress_kernel, nh=nh),
        grid=(nb, N_KV_HEADS),
        in_specs=[pl.BlockSpec((t, LANES), lambda b, k: (b, (N_Q_COLS + k * KV_HEAD_COLS) // LANES))]
        + _compress_specs(cw, lambda b, k: 0),
        out_specs=pl.BlockSpec((1, 1, nh, LANES), lambda b, k: (b, k, 0, 0)),
        out_shape=jax.ShapeDtypeStruct((nb, N_KV_HEADS, nh, LANES), F32),
        compiler_params=_cparams(("arbitrary", "arbitrary")),
    )(proj, *cw)


KEY_TILE = 4 * Q_BLOCK
FRONT_PAD = KEY_TILE
SUBTILES = KEY_TILE // Q_BLOCK
BAND_OFF = 2 * Q_BLOCK // CMP_STRIDE
BAND_W = 32
Q_PER_CMP = Q_BLOCK // CMP_STRIDE


def _nsa_consts(t):
    ncp = t // CMP_STRIDE
    nblk = t // SEL_BLOCK
    nbp = -(-nblk // LANES) * LANES
    at = np.zeros((nbp, ncp), np.float32)
    for j in range(nblk):
        for n in range(SEL_RATIO * j - (CMP_RATIO - 1), SEL_RATIO * (j + 1)):
            if 0 <= n < ncp - CMP_RATIO + 1:
                at[j, n] = 1.0
    key_blk = np.arange(t) // SEL_BLOCK
    e = (np.arange(nbp)[:, None] == key_blk[None, :]).astype(np.float32)
    e1 = e.reshape(nbp, t // LANES, LANES).transpose(1, 0, 2)
    e1 = np.concatenate([np.zeros((SUBTILES,) + e1.shape[1:], np.float32), e1], axis=0)
    band = np.zeros((2 * ncp + 2 * Q_PER_CMP, LANES), np.float32)
    u = np.arange(BAND_W)
    band[u + ncp, HEAD_DIM + u] = 1.0
    band[u + ncp, HEAD_DIM + BAND_W + u] = 1.0
    r = np.arange(Q_BLOCK)[:, None]
    c = np.arange(LANES)[None, :]
    d0 = r - c
    idx0 = np.where(d0 >= 0, d0, NEAR)
    idx1 = Q_BLOCK + r - c
    dc = r - CMP_STRIDE * c + (CMP_STRIDE * BAND_OFF - CMP_BLOCK + 1)
    idxc = np.where((dc >= 0) & (dc < NEAR), dc, NEAR)
    assert np.all(idxc[:, BAND_W:] == NEAR)
    bf = lambda a: jnp.asarray(a, BF16)
    return dict(at=bf(at), e1=bf(e1), band=jnp.asarray(band),
                idx0=idx0, idx1=idx1, idxc=idxc, ncp=ncp, nbp=nbp, nblk=nblk)


def _nsa_kernel(q_ref, gl_ref, kck_ref, kvc_ref, kvs_ref, kvw_ref, at_ref, e1_ref, tb_ref, cbq_ref, band_ref,
                o_ref, *, ncp, nbp, n_sel):
    i = pl.program_id(2)
    rows = GROUP * Q_BLOCK
    lane = lax.broadcasted_iota(jnp.int32, (1, LANES), 1)
    low = lane < HEAD_DIM
    r_col = lax.broadcasted_iota(jnp.int32, (rows, 1), 0) % Q_BLOCK
    q_pos = i * Q_BLOCK + r_col

    q = q_ref[...] * (HEAD_DIM ** -0.5 * LOG2E)
    parts = []
    for h in range(GROUP * HEAD_DIM // LANES):
        qh = q[:, h * LANES:(h + 1) * LANES]
        parts.append(jnp.where(low, qh, 0.0))
        parts.append(jnp.where(low, pltpu.roll(qh, HEAD_DIM, axis=1), 0.0))
    qs = jnp.concatenate(parts, axis=0).astype(BF16)

    qc = jnp.where(low, qs, cbq_ref[0])
    start = pl.multiple_of(ncp + BAND_OFF - Q_PER_CMP * i, 8)
    kq = (kck_ref[0, 0] + band_ref[pl.ds(start, ncp), :]).astype(BF16)
    s = lax.dot_general(qc, kq, _NT, preferred_element_type=F32)
    cmp_end = lax.broadcasted_iota(jnp.int32, (1, ncp), 1) * CMP_STRIDE + (CMP_BLOCK - 1)
    mask = cmp_end <= q_pos
    m = jnp.max(jnp.where(mask, s, NEG), axis=-1, keepdims=True)
    e = jnp.where(mask, jnp.exp2(s - m), 0.0)
    p_cmp = e * (1.0 / jnp.maximum(jnp.sum(e, axis=-1, keepdims=True), 1e-30))
    o_cmp = jnp.dot(p_cmp.astype(BF16), kvc_ref[0, 0], preferred_element_type=F32)
    imp = p_cmp[0:Q_BLOCK]
    for g in range(1, GROUP):
        imp = imp + p_cmp[g * Q_BLOCK:(g + 1) * Q_BLOCK]
    bst = sum(lax.dot_general(at_ref[...], part, _NT, preferred_element_type=F32)
              for part in _split_bf16(imp, 2))

    jb = lax.broadcasted_iota(jnp.int32, (nbp, Q_BLOCK), 0)
    q_blk = (i * Q_BLOCK + lax.broadcasted_iota(jnp.int32, (nbp, Q_BLOCK), 1)) // SEL_BLOCK
    causal = jb <= q_blk
    forced = (jb == 0) | (jb == q_blk) | (jb == q_blk - 1)
    work = jnp.where(causal, bst + jnp.where(forced, FORCE_BONUS, 0.0), -jnp.inf)
    jbf = jb.astype(F32)
    sel_t = jnp.zeros((nbp, Q_BLOCK), F32)
    for _ in range(n_sel):
        top = jnp.max(work, axis=0, keepdims=True)
        first = jnp.min(jnp.where(work == top, jbf, float(nbp)), axis=0, keepdims=True)
        hit = jbf == first
        sel_t = jnp.where(hit, 1.0, sel_t)
        work = jnp.where(hit, -jnp.inf, work)
    sel = jnp.where(causal, sel_t, 0.0).T.astype(BF16)

    def ones_and_values(kv):
        return jnp.where(low, jnp.ones_like(kv), kv)

    def normalise(acc):
        return jnp.where(low, 0.0, acc / jnp.maximum(pltpu.roll(acc, HEAD_DIM, axis=1), 1e-30))

    n_wt = WINDOW // Q_BLOCK
    kw = kvw_ref[0, 0, pl.ds(pl.multiple_of((i + SUBTILES - n_wt) * Q_BLOCK, Q_BLOCK), WINDOW + Q_BLOCK), :]
    s = lax.dot_general(qs, kw, _NT, preferred_element_type=F32)
    pieces = []
    for u in range(n_wt + 1):
        su = s[:, u * Q_BLOCK:(u + 1) * Q_BLOCK]
        if u == n_wt:
            su = jnp.where(lane <= r_col, su + tb_ref[0, 0], NEG)
        else:
            if u == n_wt - 1:
                su = su + tb_ref[0, 1]
            su = su + jnp.where(i - n_wt + u >= 0, 0.0, NEG)
            if u == 0:
                su = jnp.where(lane > r_col, su, NEG)
        pieces.append(su)
    sm = jnp.concatenate(pieces, axis=1).astype(BF16)
    e = jnp.exp2(sm - jnp.max(sm, axis=-1, keepdims=True))
    o_win = normalise(jnp.dot(e, ones_and_values(kw), preferred_element_type=F32))

    n_tiles = (i + SUBTILES) // SUBTILES

    def keys(u):
        return kvs_ref[0, 0, pl.ds(pl.multiple_of((i + 1 - SUBTILES * u) * Q_BLOCK, Q_BLOCK), KEY_TILE), :]

    def masked_scores(u, diagonal=False):
        t0p = i + 1 - SUBTILES * u
        mk = jnp.concatenate([jnp.dot(sel, e1_ref[t0p + v], preferred_element_type=F32) for v in range(SUBTILES)],
                             axis=1).astype(BF16)
        s3 = lax.dot_general(qs, keys(u), _NT, preferred_element_type=F32).reshape(GROUP, Q_BLOCK, KEY_TILE)
        if diagonal:
            r3 = lax.broadcasted_iota(jnp.int32, (1, Q_BLOCK, LANES), 1)
            c3 = lax.broadcasted_iota(jnp.int32, (1, Q_BLOCK, LANES), 2)
            tb0 = tb_ref[0, 0].reshape(GROUP, Q_BLOCK, LANES)
            tb1 = tb_ref[0, 1].reshape(GROUP, Q_BLOCK, LANES)
            s3 = jnp.concatenate([s3[:, :, :KEY_TILE - 2 * Q_BLOCK],
                                  s3[:, :, KEY_TILE - 2 * Q_BLOCK:KEY_TILE - Q_BLOCK] + tb1,
                                  jnp.where(c3 <= r3, s3[:, :, KEY_TILE - Q_BLOCK:] + tb0, NEG)], axis=2)
        return jnp.where((mk > 0.5)[None], s3.astype(BF16), NEG).reshape(rows, KEY_TILE)

    def tile_step(u, m, acc, sm):
        m_new = jnp.maximum(m, jnp.max(sm, axis=-1, keepdims=True).astype(F32))
        alpha = jnp.exp2(m - m_new)
        e = jnp.exp2(sm - m_new.astype(BF16))
        return m_new, alpha * acc + jnp.dot(e, ones_and_values(keys(u)), preferred_element_type=F32)

    def sel_body(u, carry):
        return tile_step(u, *carry, masked_scores(u))

    init = tile_step(0, jnp.full((rows, 1), NEG, F32), jnp.zeros((rows, LANES), F32),
                     masked_scores(0, diagonal=True))
    m, acc = lax.fori_loop(1, n_tiles, sel_body, init)
    o_sel = normalise(acc)

    sg = jax.nn.sigmoid(gl_ref[...])
    for g in range(GROUP):
        out = jnp.zeros((Q_BLOCK, LANES), F32)
        for j, ob in enumerate((o_cmp, o_sel, o_win)):
            out = out + sg[:, 3 * g + j:3 * g + j + 1] * ob[g * Q_BLOCK:(g + 1) * Q_BLOCK]
        o_ref[0, 0, g] = out.astype(o_ref.dtype)


GATE_COL0 = N_Q_COLS + N_KV_COLS
NP_ATTN = GATE_COL0 + N_KV_HEADS * LANES


def _nsa_prompt(proj, kck, kvc, kvs, kvw, tb, cbq, consts, nb, t):
    ni = t // Q_BLOCK
    ncp, nbp = consts['ncp'], consts['nbp']
    gcol = GATE_COL0 // LANES
    qw = GROUP * HEAD_DIM
    full = lambda a: pl.BlockSpec(a.shape, lambda b, k, i: (0,) * a.ndim)
    per_kvh = lambda a: pl.BlockSpec((1,) + a.shape[1:], lambda b, k, i: (k,) + (0,) * (a.ndim - 1))
    per_bk = lambda a: pl.BlockSpec((1, 1) + a.shape[2:], lambda b, k, i: (b, k, 0, 0))
    c = consts
    n_sel = min(N_SELECT, c['nblk'])
    assert n_sel >= 3
    return pl.pallas_call(
        functools.partial(_nsa_kernel, ncp=ncp, nbp=nbp, n_sel=n_sel),
        grid=(nb, N_KV_HEADS, ni),
        in_specs=[pl.BlockSpec((Q_BLOCK, qw), lambda b, k, i: (b * ni + i, k)),
                  pl.BlockSpec((Q_BLOCK, LANES), lambda b, k, i: (b * ni + i, gcol + k)),
                  per_bk(kck), per_bk(kvc), per_bk(kvs), per_bk(kvw),
                  full(c['at']), full(c['e1']), per_kvh(tb), per_kvh(cbq), full(c['band'])],
        out_specs=pl.BlockSpec((1, 1, GROUP, Q_BLOCK, LANES), lambda b, k, i: (b, k, 0, i, 0)),
        out_shape=jax.ShapeDtypeStruct((nb, N_KV_HEADS, GROUP, t, LANES), BF16),
        compiler_params=_cparams(("arbitrary", "arbitrary", "arbitrary")),
    )(proj, proj, kck, kvc, kvs, kvw, c['at'], c['e1'], tb, cbq, c['band'])


def _heads_out_kernel(a_ref, w_ref, g_ref, gate_ref, x_ref, o_ref):
    y = jnp.zeros(o_ref.shape, F32)
    for k in range(N_KV_HEADS):
        for g in range(0, GROUP, 2):
            a = jnp.concatenate([a_ref[0, k, g], a_ref[0, k, g + 1]], axis=1)
            h = k * GROUP + g
            w = w_ref[h * LANES:(h + 2) * LANES, :]
            y = y + jnp.dot(a, w, preferred_element_type=F32)
    o_ref[...] = x_ref[...] + gate_ref[0] * _rms(y, g_ref[...])


def _heads_out(a, w_pad, g, gate, x, tm):
    nb, _, _, t, _ = a.shape
    d = w_pad.shape[1]
    tpb = t // tm
    return pl.pallas_call(
        _heads_out_kernel,
        grid=(nb * tpb,),
        in_specs=[pl.BlockSpec((1, N_KV_HEADS, GROUP, tm, LANES), lambda i: (i // tpb, 0, 0, i % tpb, 0)),
                  pl.BlockSpec(w_pad.shape, lambda i: (0, 0)),
                  pl.BlockSpec((1, d), lambda i: (0, 0)),
                  pl.BlockSpec((1,) + gate.shape[1:], lambda i: (i // tpb, 0, 0)),
                  pl.BlockSpec((tm, d), lambda i: (i, 0))],
        out_specs=pl.BlockSpec((tm, d), lambda i: (i, 0)),
        out_shape=jax.ShapeDtypeStruct((nb * t, d), F32),
        compiler_params=_cparams(("arbitrary",)),
    )(a, w_pad, g.reshape(1, d), gate, x)


def _row_tile(m, cap):
    return m if m <= cap else cap


def _attn_in_weights(w_in):
    d = w_in.shape[0]
    kv = w_in[:, N_Q_COLS:GATE_COL0].reshape(d, N_KV_SLOTS + N_WIN_SLOTS, N_KV_HEADS, HEAD_DIM)
    kv = jnp.swapaxes(kv, 1, 2).reshape(d, N_KV_COLS)
    gates = w_in[:, GATE_COL0:].reshape(d, N_KV_HEADS, GROUP * 3)
    gates = jnp.pad(gates, ((0, 0), (0, 0), (0, LANES - GROUP * 3))).reshape(d, -1)
    return jnp.concatenate([w_in[:, :N_Q_COLS], kv, gates], axis=1).astype(BF16)


def _attn_out_weights(w_out):
    w = w_out.reshape(N_HEADS, HEAD_DIM, w_out.shape[1])
    return jnp.pad(w, ((0, 0), (LANES - HEAD_DIM, 0), (0, 0))).reshape(N_HEADS * LANES, -1).astype(BF16)


def _attn_in_kernel(x_ref, g_ref, sh_ref, sc_ref, w_ref, zs_ref, zw_ref, o_ref, kvs_ref, kvw_ref, kvt_ref, wint_ref):
    del zs_ref, zw_ref
    h = _rms(x_ref[...], g_ref[...]) * (1.0 + sc_ref[0]) + sh_ref[0]
    r = jnp.dot(h.astype(BF16), w_ref[...], preferred_element_type=F32)
    o_ref[...] = r
    for k in range(N_KV_HEADS):
        c0 = N_Q_COLS + k * KV_HEAD_COLS
        kvs_ref[0, k] = r[:, c0 + LANES:c0 + 2 * LANES].astype(BF16)
        kvw_ref[0, k] = r[:, c0 + 2 * LANES:c0 + 3 * LANES].astype(BF16)
        for p in range(KV_HEAD_COLS // LANES):
            tt = r[:, c0 + p * LANES:c0 + (p + 1) * LANES].T
            for j in range(2):
                slot = 2 * p + j
                dst, s0 = (kvt_ref, slot) if slot < N_KV_SLOTS else (wint_ref, slot - N_KV_SLOTS)
                dst[0, (s0 * N_KV_HEADS + k) * HEAD_DIM:(s0 * N_KV_HEADS + k + 1) * HEAD_DIM, :] = (
                    tt[j * HEAD_DIM:(j + 1) * HEAD_DIM])


def _attn_in_prompt(x, g, shift, scale, w, nb, t):
    m, d = x.shape
    tm = FRONT_PAD
    assert t % tm == 0
    tpb = t // tm
    mod_spec = pl.BlockSpec((1,) + shift.shape[1:], lambda i: (i // tpb, 0, 0))
    pad_shape = (nb, N_KV_HEADS, FRONT_PAD + t, LANES)
    pad_spec = pl.BlockSpec((1, N_KV_HEADS, tm, LANES), lambda i: (i // tpb, 0, 1 + i % tpb, 0))
    zeros = jnp.zeros(pad_shape, BF16)
    return pl.pallas_call(
        _attn_in_kernel,
        grid=(m // tm,),
        in_specs=[pl.BlockSpec((tm, d), lambda i: (i, 0)),
                  pl.BlockSpec((1, d), lambda i: (0, 0)),
                  mod_spec, mod_spec,
                  pl.BlockSpec(w.shape, lambda i: (0, 0)),
                  pl.BlockSpec(memory_space=pl.ANY), pl.BlockSpec(memory_space=pl.ANY)],
        out_specs=[pl.BlockSpec((tm, NP_ATTN), lambda i: (i, 0)), pad_spec, pad_spec,
                   pl.BlockSpec((1, N_KV_SLOTS * KV_ROW, tm), lambda i: (i // tpb, 0, i % tpb)),
                   pl.BlockSpec((1, N_WIN_SLOTS * KV_ROW, tm), lambda i: (i // tpb, 0, i % tpb))],
        out_shape=[jax.ShapeDtypeStruct((m, NP_ATTN), F32),
                   jax.ShapeDtypeStruct(pad_shape, BF16), jax.ShapeDtypeStruct(pad_shape, BF16),
                   jax.ShapeDtypeStruct((nb, N_KV_SLOTS * KV_ROW, t), F32),
                   jax.ShapeDtypeStruct((nb, N_WIN_SLOTS * KV_ROW, t), F32)],
        input_output_aliases={5: 1, 6: 2},
        compiler_params=_cparams(("arbitrary",)),
    )(x, g.reshape(1, d), shift, scale, w, zeros, zeros)


def _unpermute_kv(proj_kv, lead):
    kv = proj_kv.reshape(lead + (N_KV_HEADS, N_KV_SLOTS + N_WIN_SLOTS, HEAD_DIM))
    return jnp.swapaxes(kv, -3, -2)


def _bias_tiles_kernel(t_ref, oh_ref, o_ref):
    oh = oh_ref[...]
    o_ref[...] = sum(jnp.dot(p, oh, preferred_element_type=F32) for p in _split_bf16(t_ref[...], 3))


def _bias_tiles(rel_bias, idxs):
    n = sum(ix.size for ix in idxs)
    d = np.concatenate([ix.reshape(-1) for ix in idxs])
    oh = np.zeros((LANES, n), np.float32)
    near = d < NEAR
    oh[_rel_bucket_np(d[near]), np.nonzero(near)[0]] = 1.0
    table = (rel_bias.astype(F32) - rel_bias[N_BUCKETS - 1].astype(F32)).T * LOG2E
    table = jnp.zeros((N_HEADS, LANES), F32).at[:, :N_BUCKETS].set(table)
    tn = 8192
    assert n % tn == 0
    out = pl.pallas_call(
        _bias_tiles_kernel,
        grid=(n // tn,),
        in_specs=[pl.BlockSpec((N_HEADS, LANES), lambda j: (0, 0)), pl.BlockSpec((LANES, tn), lambda j: (0, j))],
        out_specs=pl.BlockSpec((N_HEADS, tn), lambda j: (0, j)),
        out_shape=jax.ShapeDtypeStruct((N_HEADS, n), F32),
        compiler_params=_cparams(("arbitrary",)),
    )(table, jnp.asarray(oh, BF16))
    tiles, off = [], 0
    for ix in idxs:
        r, c = ix.shape
        tiles.append(out[:, off:off + r * c].reshape(N_KV_HEADS, GROUP * r, c))
        off += r * c
    return tiles


def _nsa_prompt_mixer(xp, g, shift, scale, w_in, cw, rel_bias, nb, t):
    proj, kvs, kvw, kvt, wint = _attn_in_prompt(xp, g, shift, scale, w_in, nb, t)
    as_rows = lambda a: jnp.transpose(a.reshape(nb, -1, N_KV_HEADS, HEAD_DIM, a.shape[-1]), (0, 4, 1, 2, 3))
    kv_new = as_rows(kvt)
    win_new = as_rows(wint[:, :, t - min(WINDOW, t):])
    kck = _compress_prompt(proj, cw, nb, t)
    kvc = kck.astype(BF16)
    kck = jnp.where(jnp.arange(LANES) < HEAD_DIM, kck, 0.0)
    c = _nsa_consts(t)
    tiles = _bias_tiles(rel_bias, [c['idx0'], c['idx1'], c['idxc']])
    tb = jnp.stack(tiles[:2], axis=1)
    hi, lo = _split_bf16(tiles[2], 2)
    cbq = jnp.concatenate([jnp.zeros(hi.shape[:2] + (HEAD_DIM,), BF16), hi[..., :BAND_W], lo[..., :BAND_W]], axis=-1)
    o = _nsa_prompt(proj, kck, kvc, kvs, kvw, tb, cbq, c, nb, t)
    return o, kv_new, win_new


SC_PAGES = 32
CG_PER_PAGE = N_KV_SLOTS * KV_ROW // LANES
ROW_PITCH = CMP_STRIDE + SUB


def _cache_tiles(cache_kv):
    page = cache_kv.shape[2]
    assert page == LANES
    return jnp.transpose(cache_kv, (0, 1, 3, 4, 5, 2)).reshape(-1, page)


def _compress_sample_kernel(pt_ref, cache_ref, pos_ref, w1_ref, wbd_ref, w2bd_ref, o_ref, tbuf_ref, rows0_ref,
                            rows1_ref, sem_ref, p0_ref, p1_ref, *, n_pages, page, row_base):
    row_bufs = (rows0_ref, rows1_ref)
    step = pl.program_id(0) * N_CG + pl.program_id(1)
    nsteps = pl.num_programs(0) * N_CG
    chunk_pages = min(SC_PAGES, n_pages // 2)
    nch = n_pages // chunk_pages
    hpc = chunk_pages * page // CMP_STRIDE
    nh = nch * hpc

    last = nsteps * nch - 1

    def copies(chunk, slot):
        chunk = jnp.minimum(chunk, last)
        step_, ch = chunk // nch, chunk % nch
        b_, c_ = step_ // N_CG, step_ % N_CG
        out = []
        for p in range(chunk_pages):
            pg = pt_ref[b_, ch * chunk_pages + p]
            out.append(pltpu.make_async_copy(
                cache_ref.at[pl.ds(((row_base + pg) * CG_PER_PAGE + c_) * LANES, LANES), :],
                tbuf_ref.at[slot, p], sem_ref.at[slot]))
        return out

    def to_rows(slot):
        for p in range(chunk_pages):
            rows_t = tbuf_ref[slot, p].T
            for n in range(page // CMP_STRIDE):
                r0 = (p * (page // CMP_STRIDE) + n) * ROW_PITCH
                row_bufs[slot][r0:r0 + CMP_STRIDE, :] = rows_t[n * CMP_STRIDE:(n + 1) * CMP_STRIDE]

    @pl.when(step == 0)
    def _():
        for slot in range(2):
            for cp in copies(slot, slot):
                cp.start()
        for cp in copies(0, 0):
            cp.wait()
        to_rows(0)

    for ch in range(nch):
        slot = ch % 2
        chunk = step * nch + ch
        for cp in copies(chunk + 2, slot):
            cp.start()
        for cp in copies(chunk + 1, 1 - slot):
            cp.wait()
        to_rows(1 - slot)
        rows = row_bufs[slot]
        p0, p1 = _compress_parts(lambda s: rows[pl.ds(s, hpc, stride=ROW_PITCH), :], wbd_ref, hpc)
        p0_ref[ch * hpc:(ch + 1) * hpc] = p0
        p1_ref[ch * hpc:(ch + 1) * hpc] = p1

    @pl.when(step == nsteps - 1)
    def _():
        for cp in copies(last, (nch - 1) % 2):
            cp.wait()

    o_ref[0, 0] = _compress_finish(p0_ref[...], p1_ref[...], pos_ref, w1_ref, w2bd_ref, nh)


def _compress_sample(page_table, cache_t, cw, page, row_base):
    nb, n_pages = page_table.shape
    chunk_pages = min(SC_PAGES, n_pages // 2)
    assert n_pages % (2 * chunk_pages) == 0
    nh = n_pages * page // CMP_STRIDE
    grid_spec = pltpu.PrefetchScalarGridSpec(
        num_scalar_prefetch=1,
        grid=(nb, N_CG),
        in_specs=[pl.BlockSpec(memory_space=pl.ANY)] + _compress_specs(cw, lambda b, c, pt: c // CG_PER_SLOT),
        out_specs=pl.BlockSpec((1, 1, nh, LANES), lambda b, c, pt: (b, c, 0, 0)),
        scratch_shapes=[pltpu.VMEM((2, chunk_pages, LANES, page), F32),
                        pltpu.VMEM((chunk_pages * page // CMP_STRIDE * ROW_PITCH, LANES), F32),
                        pltpu.VMEM((chunk_pages * page // CMP_STRIDE * ROW_PITCH, LANES), F32),
                        pltpu.SemaphoreType.DMA((2,)),
                        pltpu.VMEM((nh, CG_HIDDEN), F32),
                        pltpu.VMEM((nh, CG_HIDDEN), F32)])
    return pl.pallas_call(
        functools.partial(_compress_sample_kernel, n_pages=n_pages, page=page, row_base=row_base),
        grid_spec=grid_spec,
        out_shape=jax.ShapeDtypeStruct((nb, N_CG, nh, LANES), F32),
        compiler_params=_cparams(("arbitrary", "arbitrary")),
    )(page_table, cache_t, *cw)


def _softmax_with_new_key(s, s_new):
    m = jnp.maximum(jnp.max(s, axis=-1, keepdims=True), s_new)
    e = jnp.exp(s - m)
    e_new = jnp.exp(s_new - m)
    inv = 1.0 / jnp.maximum(jnp.sum(e, axis=-1, keepdims=True) + e_new, 1e-30)
    return e * inv, e_new * inv


def _nsa_sample_kernel(pt_ref, qs_ref, q_ref, gl_ref, new_ref, ncol_ref, kvc_ref, cache_ref, win_ref, cbs_ref,
                       sbz_ref, wb_ref, b0_ref, as_ref, o_ref, wout_ref, gk_ref, gv_ref, sem_ref, idx_ref,
                       *, past, page, row_base, n_pick):
    b = pl.program_id(0)
    blk_per_page = page // SEL_BLOCK
    nbs = past // SEL_BLOCK
    ncs = kvc_ref.shape[2]
    row = lax.broadcasted_iota(jnp.int32, (SUB, 1), 0)

    cmp_ok = lax.broadcasted_iota(jnp.int32, (SUB, ncs), 1) * CMP_STRIDE + (CMP_BLOCK - 1) <= past
    imp = jnp.zeros((SUB, ncs), F32)
    o_cmp = []
    for k in range(N_KV_HEADS):
        half = k % HEADS_PER_CG
        ck = kvc_ref[0, k // HEADS_PER_CG].astype(BF16)
        cv = kvc_ref[0, CG_PER_SLOT + k // HEADS_PER_CG].astype(BF16)
        s = lax.dot_general(qs_ref[0, k], ck, _NT, preferred_element_type=F32) + cbs_ref[k]
        m = jnp.max(jnp.where(cmp_ok, s, NEG), axis=-1, keepdims=True)
        e = jnp.where(cmp_ok, jnp.exp(s - m), 0.0)
        p = e / jnp.maximum(jnp.sum(e, axis=-1, keepdims=True), 1e-30)
        o_cmp.append(jnp.dot(p.astype(BF16), cv, preferred_element_type=F32)[:, half * HEAD_DIM:(half + 1) * HEAD_DIM])
        imp_k = p[0:1]
        for g in range(1, GROUP):
            imp_k = imp_k + p[g:g + 1]
        imp = imp + jnp.where(row == k, imp_k, 0.0)
    bs = sum(jnp.dot(part, as_ref[...], preferred_element_type=F32) for part in _split_bf16(imp, 3))

    lane_b = lax.broadcasted_iota(jnp.int32, (SUB, nbs), 1)
    lane_f = lane_b.astype(F32)
    score = bs + jnp.where((lane_b == 0) | (lane_b == nbs - 1), FORCE_BONUS, 0.0)

    def gathers(k, r):
        j = idx_ref[k * n_pick + r]
        pg = pt_ref[b, j // blk_per_page]
        tile0 = (row_base + pg) * N_KV_SLOTS
        return [pltpu.make_async_copy(
            cache_ref.at[pl.ds(((tile0 + slot) * N_KV_HEADS + k) * HEAD_DIM, HEAD_DIM), :],
            buf.at[k * n_pick + r], sem_ref.at[0]) for slot, buf in ((2, gk_ref), (3, gv_ref))]

    for r in range(n_pick):
        m = jnp.max(score, axis=-1, keepdims=True)
        pick = jnp.min(jnp.where(score == m, lane_f, 1e9), axis=-1, keepdims=True)
        score = jnp.where(lane_f == pick, -jnp.inf, score)
        for k in range(N_KV_HEADS):
            idx_ref[k * n_pick + r] = jnp.sum(jnp.where(row == k, pick, 0.0)).astype(jnp.int32)
            for cp in gathers(k, r):
                cp.start()

    def new_row(k, j):
        return new_ref[0, k][j:j + 1].astype(BF16).astype(F32)

    tok = lax.broadcasted_iota(jnp.int32, (1, WINDOW), 1)
    o_win = []
    for k in range(N_KV_HEADS):
        for slot in range(N_WIN_SLOTS):
            shifted = pltpu.roll(win_ref[0, slot, k], WINDOW - 1, axis=1)
            wout_ref[0, slot, k] = jnp.where(tok == WINDOW - 1, ncol_ref[0, slot, k], shifted)
        q = q_ref[0, k]
        s = jnp.dot(q, win_ref[0, 0, k].astype(BF16), preferred_element_type=F32) + wb_ref[k]
        s = jnp.where(tok >= 1, s, NEG)
        s_new = jnp.sum(q.astype(F32) * new_row(k, 2), axis=-1, keepdims=True) + b0_ref[k][:, 0:1]
        p, p_new = _softmax_with_new_key(s, s_new)
        o_win.append(lax.dot_general(p.astype(BF16), win_ref[0, 1, k].astype(BF16), _NT,
                                     preferred_element_type=F32) + p_new * new_row(k, 3))

    for k in range(N_KV_HEADS):
        for r in range(n_pick):
            for cp in gathers(k, r):
                cp.wait()

    upper = lax.broadcasted_iota(jnp.int32, (1, LANES), 1) // SEL_BLOCK
    for k in range(N_KV_HEADS):
        q = q_ref[0, k]
        tiles = []
        for r in range(n_pick):
            j = idx_ref[k * n_pick + r]
            near = j - (nbs - NEAR // SEL_BLOCK)
            s = jnp.dot(q, gk_ref[k * n_pick + r].astype(BF16), preferred_element_type=F32)
            s = s + sbz_ref[k, jnp.where(near >= 0, near, NEAR // SEL_BLOCK)]
            tiles.append(jnp.where(upper == j % blk_per_page, s, NEG))
        s_new = jnp.sum(q.astype(F32) * new_row(k, 0), axis=-1, keepdims=True) + b0_ref[k][:, 0:1]
        p, p_new = _softmax_with_new_key(jnp.concatenate(tiles, axis=1), s_new)
        o_sel = p_new * new_row(k, 1)
        for r in range(n_pick):
            o_sel = o_sel + lax.dot_general(p[:, r * LANES:(r + 1) * LANES].astype(BF16),
                                            gv_ref[k * n_pick + r].astype(BF16), _NT, preferred_element_type=F32)
        g_cmp, g_sel, g_win = (jax.nn.sigmoid(gl_ref[0, k, j])[:, :HEAD_DIM] for j in range(3))
        o_ref[0, k] = g_cmp * o_cmp[k] + g_sel * o_sel + g_win * o_win[k]


def _bias_rows(bvz, idx):
    t = jnp.take(bvz, jnp.asarray(idx, jnp.int32), axis=0).reshape(len(idx), N_KV_HEADS, GROUP)
    t = jnp.transpose(t, (1, 2, 0))
    return jnp.pad(t, ((0, 0), (0, SUB - GROUP), (0, 0)))


def _nsa_sample(proj_s, kvc, page_table, cache_t, win_t, bvz, past, page, row_base):
    nb = proj_s.shape[0]
    n_pick = N_SELECT - 1
    nbs = past // SEL_BLOCK
    ncs = past // CMP_STRIDE
    n_near = NEAR // SEL_BLOCK
    blk_per_page = page // SEL_BLOCK
    assert past % page == 0 and page == LANES and nbs > n_pick + n_near and (nbs - n_near) % blk_per_page == 0
    assert win_t.shape[-1] == WINDOW < past
    q = (proj_s[:, :N_Q_COLS] * HEAD_DIM ** -0.5).astype(BF16).reshape(nb, N_KV_HEADS, GROUP, HEAD_DIM)
    q = jnp.pad(q, ((0, 0), (0, 0), (0, SUB - GROUP), (0, 0)))
    qs = jnp.stack([jnp.pad(q[:, k], ((0, 0), (0, 0), ((k % HEADS_PER_CG) * HEAD_DIM,
                                                       LANES - (k % HEADS_PER_CG + 1) * HEAD_DIM)))
                    for k in range(N_KV_HEADS)], axis=1)
    gl = proj_s[:, GATE_COL0:].reshape(nb, N_KV_HEADS, LANES)[:, :, :GROUP * 3].reshape(nb, N_KV_HEADS, GROUP, 3)
    gl = jnp.pad(jnp.transpose(gl, (0, 1, 3, 2)), ((0, 0), (0, 0), (0, 0), (0, SUB - GROUP)))
    gl = jnp.broadcast_to(gl[..., None], gl.shape + (LANES,))
    new = _unpermute_kv(proj_s[:, N_Q_COLS:GATE_COL0], (nb,))[:, 2:]
    new_rows = jnp.transpose(new, (0, 2, 1, 3))
    new_cols = new[:, 2:].reshape(nb, N_WIN_SLOTS, N_KV_HEADS, HEAD_DIM, 1)

    dc = past - (np.arange(ncs) * CMP_STRIDE + CMP_BLOCK - 1)
    cbs = _bias_rows(bvz, np.where((dc >= 0) & (dc < NEAR), dc, NEAR))
    lane = np.arange(LANES)
    sb = []
    for jj in range(n_near + 1):
        d = NEAR - SEL_BLOCK * jj - lane % SEL_BLOCK
        ok = (lane // SEL_BLOCK == jj % blk_per_page) & (jj < n_near) & (d < NEAR)
        sb.append(_bias_rows(bvz, np.where(ok, d, NEAR)))
    sbz = jnp.stack(sb, axis=1)
    dw = WINDOW - np.arange(WINDOW)
    wb = _bias_rows(bvz, np.where(dw < NEAR, dw, NEAR))
    b0 = _bias_rows(bvz, np.zeros(LANES, np.int64))
    a_s = np.zeros((ncs, nbs), np.float32)
    for j in range(nbs):
        for n in range(SEL_RATIO * j - (CMP_RATIO - 1), SEL_RATIO * (j + 1)):
            if 0 <= n < ncs - CMP_RATIO + 1:
                a_s[n, j] = 1.0
    a_s = jnp.asarray(a_s, BF16)

    per_b = lambda a: pl.BlockSpec((1,) + a.shape[1:], lambda b, pt: (b,) + (0,) * (a.ndim - 1))
    full = lambda a: pl.BlockSpec(a.shape, lambda b, pt: (0,) * a.ndim)
    grid_spec = pltpu.PrefetchScalarGridSpec(
        num_scalar_prefetch=1,
        grid=(nb,),
        in_specs=[per_b(qs), per_b(q), per_b(gl), per_b(new_rows), per_b(new_cols), per_b(kvc),
                  pl.BlockSpec(memory_space=pl.ANY), per_b(win_t),
                  full(cbs), full(sbz), full(wb), full(b0), full(a_s)],
        out_specs=[pl.BlockSpec((1, N_KV_HEADS, SUB, HEAD_DIM), lambda b, pt: (b, 0, 0, 0)), per_b(win_t)],
        scratch_shapes=[pltpu.VMEM((N_KV_HEADS * n_pick, HEAD_DIM, page), F32),
                        pltpu.VMEM((N_KV_HEADS * n_pick, HEAD_DIM, page), F32),
                        pltpu.SemaphoreType.DMA((1,)),
                        pltpu.SMEM((N_KV_HEADS * n_pick,), jnp.int32)])
    o, win_out = pl.pallas_call(
        functools.partial(_nsa_sample_kernel, past=past, page=page, row_base=row_base, n_pick=n_pick),
        grid_spec=grid_spec,
        out_shape=[jax.ShapeDtypeStruct((nb, N_KV_HEADS, SUB, HEAD_DIM), F32),
                   jax.ShapeDtypeStruct(win_t.shape, F32)],
        compiler_params=_cparams(("arbitrary",)),
    )(page_table, qs, q, gl, new_rows, new_cols, kvc, cache_t, win_t, cbs, sbz, wb, b0, a_s)
    return o[:, :, :GROUP].reshape(nb, N_Q_COLS), win_out


MXU_WIDTH = 256
SSM_COL_TILE = 5 * MXU_WIDTH
NP_SSM = -(-(SSM_CONV_DIM + SSM_D_INNER + SSM_HEADS) // SSM_COL_TILE) * SSM_COL_TILE
N_BC = SSM_GROUPS * SSM_STATE
HEADS_PER_GROUP = SSM_HEADS // SSM_GROUPS
PAIR = LANES // SSM_HEAD_DIM
N_PAIRS = SSM_HEADS // PAIR
CONV_PAD = 8


def _softplus(x):
    return jnp.maximum(x, 0.0) + jnp.log(1.0 + jnp.exp(-jnp.abs(x)))


def _cumsum_rows(x):
    n = x.shape[0]
    row = lax.broadcasted_iota(jnp.int32, x.shape, 0)
    s = 1
    while s < n:
        x = x + jnp.where(row >= s, pltpu.roll(x, s, axis=0), 0.0)
        s *= 2
    return x


def _ssd_kernel(xbc_ref, z_ref, dt_ref, h0_ref, cinit_ref, cw_ref, cb_ref, dtb_ref, alog_ref, dsk_ref, ng_ref,
                y_ref, hout_ref, xs_ref, act_ref, ybuf_ref, h_ref, *, nc):
    c = pl.program_id(1)
    q = SSM_CHUNK

    @pl.when(c == 0)
    def _():
        xs_ref[...] = cinit_ref[0]
        h_ref[...] = h0_ref[0]

    x = xbc_ref[...]
    prev = xs_ref[...]
    head_row = lax.broadcasted_iota(jnp.int32, (CONV_PAD, 1), 0)
    conv = cb_ref[...] + cw_ref[SSM_CONV - 1:SSM_CONV] * x
    for j in range(1, SSM_CONV):
        xj = pltpu.roll(x, j, axis=0)
        head = jnp.where(head_row < j, pltpu.roll(prev, j, axis=0), xj[:CONV_PAD])
        xj = jnp.concatenate([head, xj[CONV_PAD:]], axis=0)
        conv = conv + cw_ref[SSM_CONV - 1 - j:SSM_CONV - j] * xj
    xs_ref[...] = x[q - CONV_PAD:]
    act_ref[...] = _silu(conv)

    dt = _softplus(dt_ref[...] + dtb_ref[...])
    acum = _cumsum_rows(dt * (-jnp.exp(alog_ref[...])))
    acum_t = acum.T
    dt_t = dt.T
    last = acum[q - 1:q, :]
    ea = jnp.exp(acum)
    te = jnp.exp(last - acum) * dt
    cd = jnp.exp(last)
    ii = lax.broadcasted_iota(jnp.int32, (q, q), 0)
    jj = lax.broadcasted_iota(jnp.int32, (q, q), 1)
    tri = ii >= jj
    low = jj < SSM_HEAD_DIM
    low_rows = ii < SSM_HEAD_DIM

    def col(a, h):
        return a[:, h:h + 1]

    for g in range(SSM_GROUPS):
        bg = act_ref[:, SSM_D_INNER + g * SSM_STATE:SSM_D_INNER + (g + 1) * SSM_STATE].astype(BF16)
        cg = act_ref[:, SSM_D_INNER + N_BC + g * SSM_STATE:SSM_D_INNER + N_BC + (g + 1) * SSM_STATE].astype(BF16)
        cbg = lax.dot_general(cg, bg, _NT, preferred_element_type=F32)
        for pr in range(HEADS_PER_GROUP // PAIR):
            k = g * (HEADS_PER_GROUP // PAIR) + pr
            ha, hb = PAIR * k, PAIR * k + 1
            xp = act_ref[:, k * LANES:(k + 1) * LANES]
            xpb = xp.astype(BF16)
            ys = []
            for h in (ha, hb):
                decay = jnp.exp(jnp.where(tri, col(acum, h) - acum_t[h:h + 1, :], NEG))
                w = cbg * decay * dt_t[h:h + 1, :]
                ys.append(jnp.dot(w.astype(BF16), xpb, preferred_element_type=F32))
            y = jnp.where(low, ys[0], ys[1])
            xs_pair = xp * jnp.where(low, col(te, ha), col(te, hb))
            st = jnp.dot(xs_pair.T.astype(BF16), bg, preferred_element_type=F32)
            hprev = h_ref[k]
            yoff = lax.dot_general(cg, hprev.astype(BF16), _NT, preferred_element_type=F32)
            y = y + yoff * jnp.where(low, col(ea, ha), col(ea, hb)) + dsk_ref[:, k * LANES:(k + 1) * LANES] * xp
            h_ref[k] = hprev * jnp.where(low_rows, cd[:, ha:ha + 1], cd[:, hb:hb + 1]) + st
            ybuf_ref[:, k * LANES:(k + 1) * LANES] = y

    yz = ybuf_ref[...] * _silu(z_ref[...])
    gw = SSM_D_INNER // SSM_GROUPS
    outs = []
    for g in range(SSM_GROUPS):
        seg = yz[:, g * gw:(g + 1) * gw]
        outs.append(seg * lax.rsqrt(jnp.mean(seg * seg, axis=-1, keepdims=True) + EPS))
    y_ref[...] = (jnp.concatenate(outs, axis=1) * ng_ref[...]).astype(y_ref.dtype)

    @pl.when(c == nc - 1)
    def _():
        hout_ref[0] = h_ref[...]


def _ssd(proj, h0, cinit, sw, nb, t):
    q = SSM_CHUNK
    nc = t // q
    vec = lambda a: pl.BlockSpec(a.shape, lambda b, c: (0, 0))
    return pl.pallas_call(
        functools.partial(_ssd_kernel, nc=nc),
        grid=(nb, nc),
        in_specs=[pl.BlockSpec((q, SSM_CONV_DIM), lambda b, c: (b * nc + c, 0)),
                  pl.BlockSpec((q, SSM_D_INNER), lambda b, c: (b * nc + c, SSM_CONV_DIM // SSM_D_INNER)),
                  pl.BlockSpec((q, LANES), lambda b, c: (b * nc + c, (SSM_CONV_DIM + SSM_D_INNER) // LANES)),
                  pl.BlockSpec((1,) + h0.shape[1:], lambda b, c: (b, 0, 0, 0)),
                  pl.BlockSpec((1,) + cinit.shape[1:], lambda b, c: (b, 0, 0)),
                  vec(sw['conv_w']), vec(sw['conv_b']), vec(sw['dt_bias']), vec(sw['a_log']), vec(sw['d_lane']),
                  vec(sw['norm_g'])],
        out_specs=[pl.BlockSpec((q, SSM_D_INNER), lambda b, c: (b * nc + c, 0)),
                   pl.BlockSpec((1,) + h0.shape[1:], lambda b, c: (b, 0, 0, 0))],
        out_shape=[jax.ShapeDtypeStruct((nb * t, SSM_D_INNER), BF16),
                   jax.ShapeDtypeStruct(h0.shape, F32)],
        scratch_shapes=[pltpu.VMEM((CONV_PAD, SSM_CONV_DIM), F32),
                        pltpu.VMEM((q, SSM_CONV_DIM), F32),
                        pltpu.VMEM((q, SSM_D_INNER), F32),
                        pltpu.VMEM(h0.shape[1:], F32)],
        compiler_params=_cparams(("arbitrary", "arbitrary")),
    )(proj, proj, proj, h0, cinit, sw['conv_w'], sw['conv_b'], sw['dt_bias'], sw['a_log'], sw['d_lane'],
      sw['norm_g'])


def _ssm_sample_step(proj_s, state_ssm, state_conv, sw):
    nb = proj_s.shape[0]
    q = SSM_CHUNK
    xbc_new = proj_s[:, :SSM_CONV_DIM]
    n_tail = NP_SSM - SSM_CONV_DIM
    tail = jnp.where(jnp.arange(n_tail) < SSM_D_INNER, 0.0, NEG).astype(F32)
    rows = jnp.concatenate([
        jnp.concatenate([jnp.zeros((nb, q - SSM_CONV, SSM_CONV_DIM), F32), state_conv.astype(F32)], axis=1),
        jnp.broadcast_to(tail, (nb, q - 1, n_tail))], axis=2)
    rows = jnp.concatenate([rows, proj_s[:, None]], axis=1)
    h0 = state_ssm.astype(F32).reshape(nb, N_PAIRS, LANES, SSM_STATE)
    cinit = jnp.zeros((nb, CONV_PAD, SSM_CONV_DIM), F32)
    yn, hfin = _ssd(rows.reshape(nb * q, NP_SSM), h0, cinit, sw, nb, q)
    conv_new = jnp.concatenate([state_conv[:, 1:].astype(F32), xbc_new[:, None]], axis=1)
    return yn.reshape(nb, q, SSM_D_INNER)[:, q - 1], hfin.reshape(state_ssm.shape), conv_new


def _ssm_weights(w_in, conv_w, conv_b, dt_bias, a_log, d_skip, norm_g):
    z_w = w_in[:, :SSM_D_INNER]
    xbc_w = w_in[:, SSM_D_INNER:SSM_D_INNER + SSM_CONV_DIM]
    dt_w = w_in[:, SSM_D_INNER + SSM_CONV_DIM:]
    pad = NP_SSM - w_in.shape[1]
    w = jnp.concatenate([xbc_w, z_w, dt_w, jnp.zeros((w_in.shape[0], pad), w_in.dtype)], axis=1).astype(BF16)
    lane_pad = lambda v: jnp.zeros((1, LANES), F32).at[0, :SSM_HEADS].set(v.astype(F32))
    return dict(w_in=w, conv_w=conv_w.astype(F32), conv_b=conv_b.astype(F32).reshape(1, -1),
                dt_bias=lane_pad(dt_bias), a_log=lane_pad(a_log),
                d_lane=jnp.repeat(d_skip.astype(F32), SSM_HEAD_DIM).reshape(1, -1),
                norm_g=norm_g.astype(F32).reshape(1, -1))


ROW_TILE = 512
MLP_ROW_TILE = 1024
FF_TILE = 1024


def kernel(x_prompt, x_sample, cache_kv, cache_win, state_ssm, state_conv, page_table, c_prompt, c_sample, rel_bias,
           ada_w, ada_b, norm_g, mlp_w1, mlp_w2, attn_w_in, attn_w_out, cmp_pos, cmp_w1, cmp_w2, ssm_w_in,
           ssm_conv_w, ssm_conv_b, ssm_dt_bias, ssm_a_log, ssm_d, ssm_norm_g, ssm_w_out):
    nb, t, d = x_prompt.shape
    db = x_sample.shape[0]
    assert x_sample.shape[1] == 1 and t % SSM_CHUNK == 0 and t % Q_BLOCK == 0
    n_pool, page = cache_kv.shape[1], cache_kv.shape[2]
    past = page_table.shape[1] * page
    depth = ada_w.shape[0]
    tm = _row_tile(t, ROW_TILE)

    xp = x_prompt.reshape(nb * t, d).astype(F32)
    xs = x_sample.reshape(db, d).astype(F32)
    c_all = jnp.concatenate([c_prompt, c_sample], axis=0).astype(F32)
    c_all = jnp.pad(c_all, ((0, (-c_all.shape[0]) % SUB), (0, 0)))
    bvz = _bias_vector(rel_bias)
    cache_t = _cache_tiles(cache_kv)
    win_t = jnp.transpose(cache_win, (0, 1, 3, 4, 5, 2))

    kv_p, win_p, ssm_p, conv_p, kv_s, win_s, ssm_s, conv_s = ([] for _ in range(8))
    for i in range(depth):
        mod = _ada(c_all, ada_w[i], ada_b[i])
        mp = [mod[:nb, j * d:(j + 1) * d].reshape(nb, 1, d) for j in range(N_MOD)]
        ms = [mod[nb:nb + db, j * d:(j + 1) * d].reshape(1, db, d) for j in range(N_MOD)]
        g = norm_g[i].astype(F32)
        if i % 2 == 0:
            a = i // 2
            w_in = _attn_in_weights(attn_w_in[a])
            w_out = attn_w_out[a].astype(BF16)
            cw = _compress_weights(cmp_pos[a], cmp_w1[a], cmp_w2[a], [(0, 1)])
            o_p, kv_new, win_new_p = _nsa_prompt_mixer(xp, g[0], mp[0], mp[1], w_in, cw, rel_bias, nb, t)
            kv_p.append(kv_new)
            win_p.append(win_new_p)
            xp = _heads_out(o_p, _attn_out_weights(attn_w_out[a]), g[1], mp[2], xp, tm)

            proj_s = _nm_matmul(xs, g[0], ms[0], ms[1], w_in, db, NP_ATTN)
            cw_s = _compress_weights(cmp_pos[a], cmp_w1[a], cmp_w2[a], [(0, 0), (1, 1)])
            kvc_s = _compress_sample(page_table, cache_t, cw_s, page, a * n_pool)
            o_s, win_new = _nsa_sample(proj_s, kvc_s, page_table, cache_t, win_t[a], bvz, past, page, a * n_pool)
            kv_s.append(_unpermute_kv(proj_s[:, N_Q_COLS:GATE_COL0], (db, 1))[:, :, :N_KV_SLOTS])
            win_s.append(jnp.transpose(win_new, (0, 4, 1, 2, 3)))
            xs = _mm_norm_res(o_s, w_out, g[1], ms[2], xs, db)
        else:
            m = i // 2
            sw = _ssm_weights(ssm_w_in[m], ssm_conv_w[m], ssm_conv_b[m], ssm_dt_bias[m], ssm_a_log[m], ssm_d[m],
                              ssm_norm_g[m])
            w_out = ssm_w_out[m].astype(BF16)
            proj = _nm_matmul(xp, g[0], mp[0], mp[1], sw['w_in'], tm, SSM_COL_TILE)
            h0 = jnp.zeros((nb, N_PAIRS, LANES, SSM_STATE), F32)
            cinit = jnp.zeros((nb, CONV_PAD, SSM_CONV_DIM), F32)
            yn, hfin = _ssd(proj, h0, cinit, sw, nb, t)
            ssm_p.append(hfin.reshape(nb, SSM_HEADS, SSM_HEAD_DIM, SSM_STATE).astype(state_ssm.dtype))
            conv_p.append(proj.reshape(nb, t, NP_SSM)[:, t - (SSM_CONV - 1):, :SSM_CONV_DIM])
            xp = _mm_norm_res(yn, w_out, g[1], mp[2], xp, tm)

            proj_s = _nm_matmul(xs, g[0], ms[0], ms[1], sw['w_in'], db, SSM_COL_TILE)
            yn_s, h_s, conv_new = _ssm_sample_step(proj_s, state_ssm[m], state_conv[m], sw)
            ssm_s.append(h_s.astype(state_ssm.dtype))
            conv_s.append(conv_new)
            xs = _mm_norm_res(yn_s, w_out, g[1], ms[2], xs, db)
        w1 = mlp_w1[i].astype(BF16)
        w2 = mlp_w2[i].astype(BF16)
        xp = _mlp(xp, g[2], mp[3], mp[4], w1, w2, g[3], mp[5], _row_tile(t, MLP_ROW_TILE), FF_TILE)
        xs = _mlp(xs, g[2], ms[3], ms[4], w1, w2, g[3], ms[5], db, FF_TILE)
    return (xp.reshape(nb, t, d), xs.reshape(db, 1, d), jnp.stack(kv_p), jnp.stack(win_p), jnp.stack(ssm_p),
            jnp.stack(conv_p), jnp.stack(kv_s), jnp.stack(win_s), jnp.stack(ssm_s), jnp.stack(conv_s))
```

```python
import functools
import math

import numpy as np
import jax
import jax.numpy as jnp
from jax import lax
from jax.experimental import pallas as pl
from jax.experimental.pallas import tpu as pltpu

F32 = jnp.float32
BF16 = jnp.bfloat16

D_MODEL = 1024
N_HEADS = 16
HEAD_DIM = 64
N_KV_HEADS = 4
GROUP = N_HEADS // N_KV_HEADS
CMP_BLOCK = 32
CMP_STRIDE = 16
CMP_RATIO = CMP_BLOCK // CMP_STRIDE
CMP_HIDDEN = 2 * HEAD_DIM
SEL_BLOCK = 64
SEL_RATIO = SEL_BLOCK // CMP_STRIDE
N_SELECT = 16
WINDOW = 512
FORCE_BONUS = 1e4
Q_BLOCK = 128
N_KV_SLOTS = 4
N_WIN_SLOTS = 2
N_Q_COLS = N_HEADS * HEAD_DIM
N_KV_COLS = (N_KV_SLOTS + N_WIN_SLOTS) * N_KV_HEADS * HEAD_DIM
N_GATE_COLS = 3 * N_HEADS
KV_ROW = N_KV_HEADS * HEAD_DIM
KV_HEAD_COLS = (N_KV_SLOTS + N_WIN_SLOTS) * HEAD_DIM
N_BUCKETS = 32
MAX_DISTANCE = 128
SSM_D_INNER = 2 * D_MODEL
SSM_HEAD_DIM = 64
SSM_HEADS = SSM_D_INNER // SSM_HEAD_DIM
SSM_GROUPS = 8
SSM_STATE = 128
SSM_CONV = 4
SSM_CONV_DIM = SSM_D_INNER + 2 * SSM_GROUPS * SSM_STATE
SSM_CHUNK = 128
D_FF = 4 * D_MODEL
N_MOD = 6
EPS = 1e-6

LANES = 128
NEG = -1e30
LOG2E = math.log2(math.e)
VMEM_LIMIT = 56 * 1024 * 1024
NEAR = 2 * LANES

_NT = (((1,), (1,)), ((), ()))


def _cparams(sem):
    return pltpu.CompilerParams(dimension_semantics=sem, vmem_limit_bytes=VMEM_LIMIT)


def _rms(x, g):
    return x * lax.rsqrt(jnp.mean(x * x, axis=-1, keepdims=True) + EPS) * g


def _silu(x):
    return x * jax.nn.sigmoid(x)


def _split_bf16(x, n):
    parts = []
    for _ in range(n - 1):
        p = x.astype(BF16)
        parts.append(p)
        x = x - p.astype(F32)
    parts.append(x.astype(BF16))
    return parts


def _ada_kernel(c_ref, w_ref, b_ref, o_ref):
    s = _silu(c_ref[...]).astype(BF16)
    o_ref[...] = jnp.dot(s, w_ref[...].astype(BF16), preferred_element_type=F32) + b_ref[...]


def _ada(c, w, b, tn=1024):
    m, d = c.shape
    n = w.shape[1]
    return pl.pallas_call(
        _ada_kernel,
        grid=(n // tn,),
        in_specs=[pl.BlockSpec((m, d), lambda j: (0, 0)),
                  pl.BlockSpec((d, tn), lambda j: (0, j)),
                  pl.BlockSpec((1, tn), lambda j: (0, j))],
        out_specs=pl.BlockSpec((m, tn), lambda j: (0, j)),
        out_shape=jax.ShapeDtypeStruct((m, n), F32),
        compiler_params=_cparams(("arbitrary",)),
    )(c, w, b.reshape(1, n))


def _nm_matmul_kernel(x_ref, g_ref, sh_ref, sc_ref, w_ref, o_ref, h_ref):
    @pl.when(pl.program_id(1) == 0)
    def _():
        h = _rms(x_ref[...], g_ref[...]) * (1.0 + sc_ref[0]) + sh_ref[0]
        h_ref[...] = h.astype(BF16)

    o_ref[...] = jnp.dot(h_ref[...], w_ref[...], preferred_element_type=F32)


def _nm_matmul(x, g, shift, scale, w, tm, tn):
    m, d = x.shape
    n = w.shape[1]
    nb = shift.shape[0]
    tpb = m // nb // tm
    mod_spec = pl.BlockSpec((1,) + shift.shape[1:], lambda i, j: (i // tpb, 0, 0))
    return pl.pallas_call(
        _nm_matmul_kernel,
        grid=(m // tm, n // tn),
        in_specs=[pl.BlockSpec((tm, d), lambda i, j: (i, 0)),
                  pl.BlockSpec((1, d), lambda i, j: (0, 0)),
                  mod_spec, mod_spec,
                  pl.BlockSpec((d, tn), lambda i, j: (0, j))],
        out_specs=pl.BlockSpec((tm, tn), lambda i, j: (i, j)),
        out_shape=jax.ShapeDtypeStruct((m, n), F32),
        scratch_shapes=[pltpu.VMEM((tm, d), BF16)],
        compiler_params=_cparams(("arbitrary", "arbitrary")),
    )(x, g.reshape(1, d), shift, scale, w)


def _mm_norm_res_kernel(a_ref, w_ref, g_ref, gate_ref, x_ref, o_ref):
    y = jnp.dot(a_ref[...].astype(BF16), w_ref[...], preferred_element_type=F32)
    o_ref[...] = x_ref[...] + gate_ref[0] * _rms(y, g_ref[...])


def _mm_norm_res(a, w, g, gate, x, tm):
    m, k = a.shape
    d = w.shape[1]
    nb = gate.shape[0]
    tpb = m // nb // tm
    return pl.pallas_call(
        _mm_norm_res_kernel,
        grid=(m // tm,),
        in_specs=[pl.BlockSpec((tm, k), lambda i: (i, 0)),
                  pl.BlockSpec((k, d), lambda i: (0, 0)),
                  pl.BlockSpec((1, d), lambda i: (0, 0)),
                  pl.BlockSpec((1,) + gate.shape[1:], lambda i: (i // tpb, 0, 0)),
                  pl.BlockSpec((tm, d), lambda i: (i, 0))],
        out_specs=pl.BlockSpec((tm, d), lambda i: (i, 0)),
        out_shape=jax.ShapeDtypeStruct((m, d), F32),
        compiler_params=_cparams(("arbitrary",)),
    )(a, w, g.reshape(1, d), gate, x)


def _mlp_kernel(x_ref, g1_ref, sh_ref, sc_ref, w1_ref, w2_ref, g2_ref, gate_ref, o_ref, h_ref, acc_ref):
    c = pl.program_id(1)

    @pl.when(c == 0)
    def _():
        h = _rms(x_ref[...], g1_ref[...]) * (1.0 + sc_ref[0]) + sh_ref[0]
        h_ref[...] = h.astype(BF16)
        acc_ref[...] = jnp.zeros_like(acc_ref)

    a = jnp.dot(h_ref[...], w1_ref[...], preferred_element_type=F32)
    a = jnp.square(jnp.maximum(a, 0.0)).astype(BF16)
    acc_ref[...] += jnp.dot(a, w2_ref[...], preferred_element_type=F32)

    @pl.when(c == pl.num_programs(1) - 1)
    def _():
        o_ref[...] = x_ref[...] + gate_ref[0] * _rms(acc_ref[...], g2_ref[...])


def _mlp(x, g1, shift, scale, w1, w2, g2, gate, tm, tf):
    m, d = x.shape
    f = w1.shape[1]
    nb = shift.shape[0]
    tpb = m // nb // tm
    mod_spec = pl.BlockSpec((1,) + shift.shape[1:], lambda i, c: (i // tpb, 0, 0))
    vec_spec = pl.BlockSpec((1, d), lambda i, c: (0, 0))
    return pl.pallas_call(
        _mlp_kernel,
        grid=(m // tm, f // tf),
        in_specs=[pl.BlockSpec((tm, d), lambda i, c: (i, 0)), vec_spec, mod_spec, mod_spec,
                  pl.BlockSpec((d, tf), lambda i, c: (0, c)),
                  pl.BlockSpec((tf, d), lambda i, c: (c, 0)),
                  vec_spec, mod_spec],
        out_specs=pl.BlockSpec((tm, d), lambda i, c: (i, 0)),
        out_shape=jax.ShapeDtypeStruct((m, d), F32),
        scratch_shapes=[pltpu.VMEM((tm, d), BF16), pltpu.VMEM((tm, d), F32)],
        compiler_params=_cparams(("arbitrary", "arbitrary")),
    )(x, g1.reshape(1, d), shift, scale, w1, w2, g2.reshape(1, d), gate)


def _bias_kernel(oh_ref, t_ref, o_ref):
    t = t_ref[...]
    t = t - t[N_BUCKETS - 1:N_BUCKETS, :]
    oh = oh_ref[...]
    o_ref[...] = sum(jnp.dot(oh, p, preferred_element_type=F32) for p in _split_bf16(t, 3))


def _rel_bucket_np(dist):
    n = np.maximum(dist, 0)
    exact = N_BUCKETS // 2
    nf = np.maximum(n, 1).astype(np.float32)
    large = exact + (np.log(nf / exact) / math.log(MAX_DISTANCE / exact) * (N_BUCKETS - exact)).astype(np.int32)
    return np.where(n < exact, n, np.minimum(large, N_BUCKETS - 1))


def _bias_vector(rel_bias):
    assert _rel_bucket_np(np.array([NEAR - 1]))[0] == N_BUCKETS - 1 == _rel_bucket_np(np.array([MAX_DISTANCE]))[0]
    oh = np.zeros((NEAR + 8, LANES), np.float32)
    oh[np.arange(NEAR), _rel_bucket_np(np.arange(NEAR))] = 1.0
    oh[NEAR:, N_BUCKETS - 1] = 1.0
    table = jnp.zeros((LANES, LANES), F32).at[:N_BUCKETS, :N_HEADS].set(rel_bias.astype(F32))
    bv = pl.pallas_call(
        _bias_kernel,
        out_shape=jax.ShapeDtypeStruct((NEAR + 8, LANES), F32),
    )(jnp.asarray(oh, BF16), table)
    return bv[:NEAR + 1, :N_HEADS]


SUB = 8
HEADS_PER_CG = LANES // HEAD_DIM
N_CG = 2 * KV_ROW // LANES
CG_PER_SLOT = KV_ROW // LANES
CG_HIDDEN = HEADS_PER_CG * CMP_HIDDEN


def _compress_parts(rows_s, wbd_ref, nh):
    part0 = jnp.zeros((nh, CG_HIDDEN), F32)
    part1 = jnp.zeros((nh, CG_HIDDEN), F32)
    for s in range(0, CMP_STRIDE, 2):
        xs = jnp.concatenate([rows_s(s), rows_s(s + 1)], axis=1).astype(BF16)
        part0 = part0 + jnp.dot(xs, wbd_ref[0, 0, s // 2], preferred_element_type=F32)
        part1 = part1 + jnp.dot(xs, wbd_ref[0, 1, s // 2], preferred_element_type=F32)
    return part0, part1


def _compress_finish(part0, part1, pos_ref, w1_ref, w2bd_ref, nh):
    pre0 = jnp.dot(pos_ref[0], w1_ref[0], preferred_element_type=F32)[0:1]
    pre = pre0 + part0 + pltpu.roll(part1, nh - 1, axis=0)
    return jnp.dot(_silu(pre).astype(BF16), w2bd_ref[0], preferred_element_type=F32)


def _compress_kernel(x_ref, pos_ref, w1_ref, wbd_ref, w2bd_ref, o_ref, *, nh):
    part0, part1 = _compress_parts(lambda s: x_ref[pl.ds(s, nh, stride=CMP_STRIDE), :], wbd_ref, nh)
    o_ref[0, 0] = _compress_finish(part0, part1, pos_ref, w1_ref, w2bd_ref, nh)


def _block_diag(a, b):
    za = jnp.zeros(a.shape[:-1] + (b.shape[-1],), a.dtype)
    zb = jnp.zeros(b.shape[:-1] + (a.shape[-1],), b.dtype)
    return jnp.concatenate([jnp.concatenate([a, za], axis=-1), jnp.concatenate([zb, b], axis=-1)], axis=-2)


def _compress_weight_set(pos, w1, w2, a, b):
    w1r = w1.reshape(2, CMP_RATIO, CMP_STRIDE, HEAD_DIM, CMP_HIDDEN)
    wbd = _block_diag(w1r[a], w1r[b]).reshape(CMP_RATIO, CMP_STRIDE // 2, 2 * LANES, CG_HIDDEN)
    posx = jnp.zeros((SUB, 2 * CMP_BLOCK * HEAD_DIM), F32).at[0].set(
        jnp.concatenate([pos[a].reshape(-1), pos[b].reshape(-1)]))
    return [x.astype(BF16) for x in (posx, _block_diag(w1[a], w1[b]), wbd, _block_diag(w2[a], w2[b]))]


def _compress_weights(cmp_pos, cmp_w1, cmp_w2, pairs):
    sets = [_compress_weight_set(cmp_pos, cmp_w1, cmp_w2, a, b) for a, b in pairs]
    return [jnp.stack(x) for x in zip(*sets)]


def _compress_specs(cw, imap):
    def spec(a):
        return pl.BlockSpec((1,) + a.shape[1:], lambda *g: (imap(*g),) + (0,) * (a.ndim - 1))
    return [spec(a) for a in cw]


def _compress_prompt(proj, cw, nb, t):
    nh = t // CMP_STRIDE
    return pl.pallas_call(
        functools.partial(_compress_kernel, nh=nh),
        grid=(nb, N_KV_HEADS),
        in_specs=[pl.BlockSpec((t, LANES), lambda b, k: (b, (N_Q_COLS + k * KV_HEAD_COLS) // LANES))]
        + _compress_specs(cw, lambda b, k: 0),
        out_specs=pl.BlockSpec((1, 1, nh, LANES), lambda b, k: (b, k, 0, 0)),
        out_shape=jax.ShapeDtypeStruct((nb, N_KV_HEADS, nh, LANES), F32),
        compiler_params=_cparams(("arbitrary", "arbitrary")),
    )(proj, *cw)


KEY_TILE = 8 * Q_BLOCK
FRONT_PAD = KEY_TILE
SUBTILES = KEY_TILE // Q_BLOCK
BAND_OFF = 2 * Q_BLOCK // CMP_STRIDE
BAND_W = 32
Q_PER_CMP = Q_BLOCK // CMP_STRIDE


def _nsa_consts(t):
    ncp = t // CMP_STRIDE
    nblk = t // SEL_BLOCK
    nbp = -(-nblk // LANES) * LANES
    at = np.zeros((nbp, ncp), np.float32)
    for j in range(nblk):
        for n in range(SEL_RATIO * j - (CMP_RATIO - 1), SEL_RATIO * (j + 1)):
            if 0 <= n < ncp - CMP_RATIO + 1:
                at[j, n] = 1.0
    key_blk = np.arange(t) // SEL_BLOCK
    e = (np.arange(nbp)[:, None] == key_blk[None, :]).astype(np.float32)
    e1 = e.reshape(nbp, t // LANES, LANES).transpose(1, 0, 2)
    e1 = np.concatenate([np.zeros((SUBTILES,) + e1.shape[1:], np.float32), e1], axis=0)
    band = np.zeros((2 * ncp + 2 * Q_PER_CMP, LANES), np.float32)
    u = np.arange(BAND_W)
    band[u + ncp, HEAD_DIM + u] = 1.0
    band[u + ncp, HEAD_DIM + BAND_W + u] = 1.0
    r = np.arange(Q_BLOCK)[:, None]
    c = np.arange(LANES)[None, :]
    d0 = r - c
    idx0 = np.where(d0 >= 0, d0, NEAR)
    idx1 = Q_BLOCK + r - c
    dc = r - CMP_STRIDE * c + (CMP_STRIDE * BAND_OFF - CMP_BLOCK + 1)
    idxc = np.where((dc >= 0) & (dc < NEAR), dc, NEAR)
    assert np.all(idxc[:, BAND_W:] == NEAR)
    bf = lambda a: jnp.asarray(a, BF16)
    return dict(at=bf(at), e1=bf(e1), band=jnp.asarray(band),
                idx0=idx0, idx1=idx1, idxc=idxc, ncp=ncp, nbp=nbp, nblk=nblk)


def _nsa_kernel(q_ref, gl_ref, kck_ref, kvc_ref, kvs_ref, kvw_ref, at_ref, e1_ref, tb_ref, cbq_ref, band_ref,
                o_ref, *, ncp, nbp, n_sel):
    i = pl.program_id(2)
    rows = GROUP * Q_BLOCK
    lane = lax.broadcasted_iota(jnp.int32, (1, LANES), 1)
    low = lane < HEAD_DIM
    r_col = lax.broadcasted_iota(jnp.int32, (rows, 1), 0) % Q_BLOCK
    q_pos = i * Q_BLOCK + r_col

    q = q_ref[...] * (HEAD_DIM ** -0.5 * LOG2E)
    parts = []
    for h in range(GROUP * HEAD_DIM // LANES):
        qh = q[:, h * LANES:(h + 1) * LANES]
        parts.append(jnp.where(low, qh, 0.0))
        parts.append(jnp.where(low, pltpu.roll(qh, HEAD_DIM, axis=1), 0.0))
    qs = jnp.concatenate(parts, axis=0).astype(BF16)

    qc = jnp.where(low, qs, cbq_ref[0])
    start = pl.multiple_of(ncp + BAND_OFF - Q_PER_CMP * i, 8)
    kq = (kck_ref[0, 0] + band_ref[pl.ds(start, ncp), :]).astype(BF16)
    s = lax.dot_general(qc, kq, _NT, preferred_element_type=F32)
    cmp_end = lax.broadcasted_iota(jnp.int32, (1, ncp), 1) * CMP_STRIDE + (CMP_BLOCK - 1)
    mask = cmp_end <= q_pos
    m = jnp.max(jnp.where(mask, s, NEG), axis=-1, keepdims=True)
    e = jnp.where(mask, jnp.exp2(s - m), 0.0)
    p_cmp = e * (1.0 / jnp.maximum(jnp.sum(e, axis=-1, keepdims=True), 1e-30))
    o_cmp = jnp.dot(p_cmp.astype(BF16), kvc_ref[0, 0], preferred_element_type=F32)
    imp = p_cmp[0:Q_BLOCK]
    for g in range(1, GROUP):
        imp = imp + p_cmp[g * Q_BLOCK:(g + 1) * Q_BLOCK]
    bst = sum(lax.dot_general(at_ref[...], part, _NT, preferred_element_type=F32)
              for part in _split_bf16(imp, 2))

    jb = lax.broadcasted_iota(jnp.int32, (nbp, Q_BLOCK), 0)
    q_blk = (i * Q_BLOCK + lax.broadcasted_iota(jnp.int32, (nbp, Q_BLOCK), 1)) // SEL_BLOCK
    causal = jb <= q_blk
    forced = (jb == 0) | (jb == q_blk) | (jb == q_blk - 1)
    work = jnp.where(causal, bst + jnp.where(forced, FORCE_BONUS, 0.0), -jnp.inf)
    jbf = jb.astype(F32)
    sel_t = jnp.zeros((nbp, Q_BLOCK), F32)
    for _ in range(n_sel):
        top = jnp.max(work, axis=0, keepdims=True)
        first = jnp.min(jnp.where(work == top, jbf, float(nbp)), axis=0, keepdims=True)
        hit = jbf == first
        sel_t = jnp.where(hit, 1.0, sel_t)
        work = jnp.where(hit, -jnp.inf, work)
    sel = jnp.where(causal, sel_t, 0.0).T.astype(BF16)

    def ones_and_values(kv):
        return jnp.where(low, jnp.ones_like(kv), kv)

    def normalise(acc):
        return jnp.where(low, 0.0, acc / jnp.maximum(pltpu.roll(acc, HEAD_DIM, axis=1), 1e-30))

    n_wt = WINDOW // Q_BLOCK
    kw = kvw_ref[0, 0, pl.ds(pl.multiple_of((i + SUBTILES - n_wt) * Q_BLOCK, Q_BLOCK), WINDOW + Q_BLOCK), :]
    s = lax.dot_general(qs, kw, _NT, preferred_element_type=F32)
    pieces = []
    for u in range(n_wt + 1):
        su = s[:, u * Q_BLOCK:(u + 1) * Q_BLOCK]
        if u == n_wt:
            su = jnp.where(lane <= r_col, su + tb_ref[0, 0], NEG)
        else:
            if u == n_wt - 1:
                su = su + tb_ref[0, 1]
            su = su + jnp.where(i - n_wt + u >= 0, 0.0, NEG)
            if u == 0:
                su = jnp.where(lane > r_col, su, NEG)
        pieces.append(su)
    sm = jnp.concatenate(pieces, axis=1).astype(BF16)
    e = jnp.exp2(sm - jnp.max(sm, axis=-1, keepdims=True))
    o_win = normalise(jnp.dot(e, ones_and_values(kw), preferred_element_type=F32))

    n_tiles = (i + SUBTILES) // SUBTILES

    def keys(u):
        return kvs_ref[0, 0, pl.ds(pl.multiple_of((i + 1 - SUBTILES * u) * Q_BLOCK, Q_BLOCK), KEY_TILE), :]

    def masked_scores(u, diagonal=False):
        t0p = i + 1 - SUBTILES * u
        mk = jnp.concatenate([jnp.dot(sel, e1_ref[t0p + v], preferred_element_type=F32) for v in range(SUBTILES)],
                             axis=1).astype(BF16)
        s3 = lax.dot_general(qs, keys(u), _NT, preferred_element_type=F32).reshape(GROUP, Q_BLOCK, KEY_TILE)
        if diagonal:
            r3 = lax.broadcasted_iota(jnp.int32, (1, Q_BLOCK, LANES), 1)
            c3 = lax.broadcasted_iota(jnp.int32, (1, Q_BLOCK, LANES), 2)
            tb0 = tb_ref[0, 0].reshape(GROUP, Q_BLOCK, LANES)
            tb1 = tb_ref[0, 1].reshape(GROUP, Q_BLOCK, LANES)
            s3 = jnp.concatenate([s3[:, :, :KEY_TILE - 2 * Q_BLOCK],
                                  s3[:, :, KEY_TILE - 2 * Q_BLOCK:KEY_TILE - Q_BLOCK] + tb1,
                                  jnp.where(c3 <= r3, s3[:, :, KEY_TILE - Q_BLOCK:] + tb0, NEG)], axis=2)
        return jnp.where((mk > 0.5)[None], s3.astype(BF16), NEG).reshape(rows, KEY_TILE)

    def tile_step(u, m, acc, sm):
        m_new = jnp.maximum(m, jnp.max(sm, axis=-1, keepdims=True).astype(F32))
        alpha = jnp.exp2(m - m_new)
        e = jnp.exp2(sm - m_new.astype(BF16))
        return m_new, alpha * acc + jnp.dot(e, ones_and_values(keys(u)), preferred_element_type=F32)

    def sel_body(u, carry):
        return tile_step(u, *carry, masked_scores(u))

    init = tile_step(0, jnp.full((rows, 1), NEG, F32), jnp.zeros((rows, LANES), F32),
                     masked_scores(0, diagonal=True))
    m, acc = lax.fori_loop(1, n_tiles, sel_body, init)
    o_sel = normalise(acc)

    sg = jax.nn.sigmoid(gl_ref[...])
    for g in range(GROUP):
        out = jnp.zeros((Q_BLOCK, LANES), F32)
        for j, ob in enumerate((o_cmp, o_sel, o_win)):
            out = out + sg[:, 3 * g + j:3 * g + j + 1] * ob[g * Q_BLOCK:(g + 1) * Q_BLOCK]
        o_ref[0, 0, g] = out.astype(o_ref.dtype)


GATE_COL0 = N_Q_COLS + N_KV_COLS
NP_ATTN = GATE_COL0 + N_KV_HEADS * LANES


def _nsa_prompt(proj, kck, kvc, kvs, kvw, tb, cbq, consts, nb, t):
    ni = t // Q_BLOCK
    ncp, nbp = consts['ncp'], consts['nbp']
    gcol = GATE_COL0 // LANES
    qw = GROUP * HEAD_DIM
    full = lambda a: pl.BlockSpec(a.shape, lambda b, k, i: (0,) * a.ndim)
    per_kvh = lambda a: pl.BlockSpec((1,) + a.shape[1:], lambda b, k, i: (k,) + (0,) * (a.ndim - 1))
    per_bk = lambda a: pl.BlockSpec((1, 1) + a.shape[2:], lambda b, k, i: (b, k, 0, 0))
    c = consts
    n_sel = min(N_SELECT, c['nblk'])
    assert n_sel >= 3
    return pl.pallas_call(
        functools.partial(_nsa_kernel, ncp=ncp, nbp=nbp, n_sel=n_sel),
        grid=(nb, N_KV_HEADS, ni),
        in_specs=[pl.BlockSpec((Q_BLOCK, qw), lambda b, k, i: (b * ni + i, k)),
                  pl.BlockSpec((Q_BLOCK, LANES), lambda b, k, i: (b * ni + i, gcol + k)),
                  per_bk(kck), per_bk(kvc), per_bk(kvs), per_bk(kvw),
                  full(c['at']), full(c['e1']), per_kvh(tb), per_kvh(cbq), full(c['band'])],
        out_specs=pl.BlockSpec((1, 1, GROUP, Q_BLOCK, LANES), lambda b, k, i: (b, k, 0, i, 0)),
        out_shape=jax.ShapeDtypeStruct((nb, N_KV_HEADS, GROUP, t, LANES), BF16),
        compiler_params=_cparams(("arbitrary", "arbitrary", "arbitrary")),
    )(proj, proj, kck, kvc, kvs, kvw, c['at'], c['e1'], tb, cbq, c['band'])


def _heads_out_kernel(a_ref, w_ref, g_ref, gate_ref, x_ref, o_ref):
    y = jnp.zeros(o_ref.shape, F32)
    for k in range(N_KV_HEADS):
        for g in range(0, GROUP, 2):
            a = jnp.concatenate([a_ref[0, k, g], a_ref[0, k, g + 1]], axis=1)
            h = k * GROUP + g
            w = w_ref[h * LANES:(h + 2) * LANES, :]
            y = y + jnp.dot(a, w, preferred_element_type=F32)
    o_ref[...] = x_ref[...] + gate_ref[0] * _rms(y, g_ref[...])


def _heads_out(a, w_pad, g, gate, x, tm):
    nb, _, _, t, _ = a.shape
    d = w_pad.shape[1]
    tpb = t // tm
    return pl.pallas_call(
        _heads_out_kernel,
        grid=(nb * tpb,),
        in_specs=[pl.BlockSpec((1, N_KV_HEADS, GROUP, tm, LANES), lambda i: (i // tpb, 0, 0, i % tpb, 0)),
                  pl.BlockSpec(w_pad.shape, lambda i: (0, 0)),
                  pl.BlockSpec((1, d), lambda i: (0, 0)),
                  pl.BlockSpec((1,) + gate.shape[1:], lambda i: (i // tpb, 0, 0)),
                  pl.BlockSpec((tm, d), lambda i: (i, 0))],
        out_specs=pl.BlockSpec((tm, d), lambda i: (i, 0)),
        out_shape=jax.ShapeDtypeStruct((nb * t, d), F32),
        compiler_params=_cparams(("arbitrary",)),
    )(a, w_pad, g.reshape(1, d), gate, x)


def _row_tile(m, cap):
    return m if m <= cap else cap


def _attn_in_weights(w_in):
    d = w_in.shape[0]
    kv = w_in[:, N_Q_COLS:GATE_COL0].reshape(d, N_KV_SLOTS + N_WIN_SLOTS, N_KV_HEADS, HEAD_DIM)
    kv = jnp.swapaxes(kv, 1, 2).reshape(d, N_KV_COLS)
    gates = w_in[:, GATE_COL0:].reshape(d, N_KV_HEADS, GROUP * 3)
    gates = jnp.pad(gates, ((0, 0), (0, 0), (0, LANES - GROUP * 3))).reshape(d, -1)
    return jnp.concatenate([w_in[:, :N_Q_COLS], kv, gates], axis=1).astype(BF16)


def _attn_out_weights(w_out):
    w = w_out.reshape(N_HEADS, HEAD_DIM, w_out.shape[1])
    return jnp.pad(w, ((0, 0), (LANES - HEAD_DIM, 0), (0, 0))).reshape(N_HEADS * LANES, -1).astype(BF16)


def _attn_in_kernel(x_ref, g_ref, sh_ref, sc_ref, w_ref, zs_ref, zw_ref, o_ref, kvs_ref, kvw_ref, kvt_ref, wint_ref):
    del zs_ref, zw_ref
    h = _rms(x_ref[...], g_ref[...]) * (1.0 + sc_ref[0]) + sh_ref[0]
    r = jnp.dot(h.astype(BF16), w_ref[...], preferred_element_type=F32)
    o_ref[...] = r
    for k in range(N_KV_HEADS):
        c0 = N_Q_COLS + k * KV_HEAD_COLS
        kvs_ref[0, k] = r[:, c0 + LANES:c0 + 2 * LANES].astype(BF16)
        kvw_ref[0, k] = r[:, c0 + 2 * LANES:c0 + 3 * LANES].astype(BF16)
        for p in range(KV_HEAD_COLS // LANES):
            tt = r[:, c0 + p * LANES:c0 + (p + 1) * LANES].T
            for j in range(2):
                slot = 2 * p + j
                dst, s0 = (kvt_ref, slot) if slot < N_KV_SLOTS else (wint_ref, slot - N_KV_SLOTS)
                dst[0, (s0 * N_KV_HEADS + k) * HEAD_DIM:(s0 * N_KV_HEADS + k + 1) * HEAD_DIM, :] = (
                    tt[j * HEAD_DIM:(j + 1) * HEAD_DIM])


def _attn_in_prompt(x, g, shift, scale, w, nb, t):
    m, d = x.shape
    tm = ROW_TILE
    assert t % tm == 0 and FRONT_PAD % tm == 0
    tpb = t // tm
    mod_spec = pl.BlockSpec((1,) + shift.shape[1:], lambda i: (i // tpb, 0, 0))
    pad_shape = (nb, N_KV_HEADS, FRONT_PAD + t, LANES)
    pad_spec = pl.BlockSpec((1, N_KV_HEADS, tm, LANES), lambda i: (i // tpb, 0, FRONT_PAD // tm + i % tpb, 0))
    zeros = jnp.zeros(pad_shape, BF16)
    return pl.pallas_call(
        _attn_in_kernel,
        grid=(m // tm,),
        in_specs=[pl.BlockSpec((tm, d), lambda i: (i, 0)),
                  pl.BlockSpec((1, d), lambda i: (0, 0)),
                  mod_spec, mod_spec,
                  pl.BlockSpec(w.shape, lambda i: (0, 0)),
                  pl.BlockSpec(memory_space=pl.ANY), pl.BlockSpec(memory_space=pl.ANY)],
        out_specs=[pl.BlockSpec((tm, NP_ATTN), lambda i: (i, 0)), pad_spec, pad_spec,
                   pl.BlockSpec((1, N_KV_SLOTS * KV_ROW, tm), lambda i: (i // tpb, 0, i % tpb)),
                   pl.BlockSpec((1, N_WIN_SLOTS * KV_ROW, tm), lambda i: (i // tpb, 0, i % tpb))],
        out_shape=[jax.ShapeDtypeStruct((m, NP_ATTN), F32),
                   jax.ShapeDtypeStruct(pad_shape, BF16), jax.ShapeDtypeStruct(pad_shape, BF16),
                   jax.ShapeDtypeStruct((nb, N_KV_SLOTS * KV_ROW, t), F32),
                   jax.ShapeDtypeStruct((nb, N_WIN_SLOTS * KV_ROW, t), F32)],
        input_output_aliases={5: 1, 6: 2},
        compiler_params=_cparams(("arbitrary",)),
    )(x, g.reshape(1, d), shift, scale, w, zeros, zeros)


def _unpermute_kv(proj_kv, lead):
    kv = proj_kv.reshape(lead + (N_KV_HEADS, N_KV_SLOTS + N_WIN_SLOTS, HEAD_DIM))
    return jnp.swapaxes(kv, -3, -2)


def _bias_tiles_kernel(t_ref, oh_ref, o_ref):
    oh = oh_ref[...]
    o_ref[...] = sum(jnp.dot(p, oh, preferred_element_type=F32) for p in _split_bf16(t_ref[...], 3))


def _bias_tiles(rel_bias, idxs):
    n = sum(ix.size for ix in idxs)
    d = np.concatenate([ix.reshape(-1) for ix in idxs])
    oh = np.zeros((LANES, n), np.float32)
    near = d < NEAR
    oh[_rel_bucket_np(d[near]), np.nonzero(near)[0]] = 1.0
    table = (rel_bias.astype(F32) - rel_bias[N_BUCKETS - 1].astype(F32)).T * LOG2E
    table = jnp.zeros((N_HEADS, LANES), F32).at[:, :N_BUCKETS].set(table)
    tn = 8192
    assert n % tn == 0
    out = pl.pallas_call(
        _bias_tiles_kernel,
        grid=(n // tn,),
        in_specs=[pl.BlockSpec((N_HEADS, LANES), lambda j: (0, 0)), pl.BlockSpec((LANES, tn), lambda j: (0, j))],
        out_specs=pl.BlockSpec((N_HEADS, tn), lambda j: (0, j)),
        out_shape=jax.ShapeDtypeStruct((N_HEADS, n), F32),
        compiler_params=_cparams(("arbitrary",)),
    )(table, jnp.asarray(oh, BF16))
    tiles, off = [], 0
    for ix in idxs:
        r, c = ix.shape
        tiles.append(out[:, off:off + r * c].reshape(N_KV_HEADS, GROUP * r, c))
        off += r * c
    return tiles


def _nsa_prompt_mixer(xp, g, shift, scale, w_in, cw, rel_bias, nb, t):
    proj, kvs, kvw, kvt, wint = _attn_in_prompt(xp, g, shift, scale, w_in, nb, t)
    as_rows = lambda a: jnp.transpose(a.reshape(nb, -1, N_KV_HEADS, HEAD_DIM, a.shape[-1]), (0, 4, 1, 2, 3))
    kv_new = as_rows(kvt)
    win_new = as_rows(wint[:, :, t - min(WINDOW, t):])
    kck = _compress_prompt(proj, cw, nb, t)
    kvc = kck.astype(BF16)
    kck = jnp.where(jnp.arange(LANES) < HEAD_DIM, kck, 0.0)
    c = _nsa_consts(t)
    tiles = _bias_tiles(rel_bias, [c['idx0'], c['idx1'], c['idxc']])
    tb = jnp.stack(tiles[:2], axis=1)
    hi, lo = _split_bf16(tiles[2], 2)
    cbq = jnp.concatenate([jnp.zeros(hi.shape[:2] + (HEAD_DIM,), BF16), hi[..., :BAND_W], lo[..., :BAND_W]], axis=-1)
    o = _nsa_prompt(proj, kck, kvc, kvs, kvw, tb, cbq, c, nb, t)
    return o, kv_new, win_new


SC_PAGES = 32
CG_PER_PAGE = N_KV_SLOTS * KV_ROW // LANES
ROW_PITCH = CMP_STRIDE + SUB


def _cache_tiles(cache_kv):
    page = cache_kv.shape[2]
    assert page == LANES
    return jnp.transpose(cache_kv, (0, 1, 3, 4, 5, 2)).reshape(-1, page)


def _compress_sample_kernel(pt_ref, cache_ref, pos_ref, w1_ref, wbd_ref, w2bd_ref, o_ref, tbuf_ref, rows0_ref,
                            rows1_ref, sem_ref, p0_ref, p1_ref, *, n_pages, page, row_base):
    row_bufs = (rows0_ref, rows1_ref)
    step = pl.program_id(0) * N_CG + pl.program_id(1)
    nsteps = pl.num_programs(0) * N_CG
    chunk_pages = min(SC_PAGES, n_pages // 2)
    nch = n_pages // chunk_pages
    hpc = chunk_pages * page // CMP_STRIDE
    nh = nch * hpc

    last = nsteps * nch - 1

    def copies(chunk, slot):
        chunk = jnp.minimum(chunk, last)
        step_, ch = chunk // nch, chunk % nch
        b_, c_ = step_ // N_CG, step_ % N_CG
        out = []
        for p in range(chunk_pages):
            pg = pt_ref[b_, ch * chunk_pages + p]
            out.append(pltpu.make_async_copy(
                cache_ref.at[pl.ds(((row_base + pg) * CG_PER_PAGE + c_) * LANES, LANES), :],
                tbuf_ref.at[slot, p], sem_ref.at[slot]))
        return out

    def to_rows(slot):
        for p in range(chunk_pages):
            rows_t = tbuf_ref[slot, p].T
            for n in range(page // CMP_STRIDE):
                r0 = (p * (page // CMP_STRIDE) + n) * ROW_PITCH
                row_bufs[slot][r0:r0 + CMP_STRIDE, :] = rows_t[n * CMP_STRIDE:(n + 1) * CMP_STRIDE]

    @pl.when(step == 0)
    def _():
        for slot in range(2):
            for cp in copies(slot, slot):
                cp.start()
        for cp in copies(0, 0):
            cp.wait()
        to_rows(0)

    for ch in range(nch):
        slot = ch % 2
        chunk = step * nch + ch
        for cp in copies(chunk + 2, slot):
            cp.start()
        for cp in copies(chunk + 1, 1 - slot):
            cp.wait()
        to_rows(1 - slot)
        rows = row_bufs[slot]
        p0, p1 = _compress_parts(lambda s: rows[pl.ds(s, hpc, stride=ROW_PITCH), :], wbd_ref, hpc)
        p0_ref[ch * hpc:(ch + 1) * hpc] = p0
        p1_ref[ch * hpc:(ch + 1) * hpc] = p1

    @pl.when(step == nsteps - 1)
    def _():
        for cp in copies(last, (nch - 1) % 2):
            cp.wait()

    o_ref[0, 0] = _compress_finish(p0_ref[...], p1_ref[...], pos_ref, w1_ref, w2bd_ref, nh)


def _compress_sample(page_table, cache_t, cw, page, row_base):
    nb, n_pages = page_table.shape
    chunk_pages = min(SC_PAGES, n_pages // 2)
    assert n_pages % (2 * chunk_pages) == 0
    nh = n_pages * page // CMP_STRIDE
    grid_spec = pltpu.PrefetchScalarGridSpec(
        num_scalar_prefetch=1,
        grid=(nb, N_CG),
        in_specs=[pl.BlockSpec(memory_space=pl.ANY)] + _compress_specs(cw, lambda b, c, pt: c // CG_PER_SLOT),
        out_specs=pl.BlockSpec((1, 1, nh, LANES), lambda b, c, pt: (b, c, 0, 0)),
        scratch_shapes=[pltpu.VMEM((2, chunk_pages, LANES, page), F32),
                        pltpu.VMEM((chunk_pages * page // CMP_STRIDE * ROW_PITCH, LANES), F32),
                        pltpu.VMEM((chunk_pages * page // CMP_STRIDE * ROW_PITCH, LANES), F32),
                        pltpu.SemaphoreType.DMA((2,)),
                        pltpu.VMEM((nh, CG_HIDDEN), F32),
                        pltpu.VMEM((nh, CG_HIDDEN), F32)])
    return pl.pallas_call(
        functools.partial(_compress_sample_kernel, n_pages=n_pages, page=page, row_base=row_base),
        grid_spec=grid_spec,
        out_shape=jax.ShapeDtypeStruct((nb, N_CG, nh, LANES), F32),
        compiler_params=_cparams(("arbitrary", "arbitrary")),
    )(page_table, cache_t, *cw)


def _softmax_with_new_key(s, s_new):
    m = jnp.maximum(jnp.max(s, axis=-1, keepdims=True), s_new)
    e = jnp.exp(s - m)
    e_new = jnp.exp(s_new - m)
    inv = 1.0 / jnp.maximum(jnp.sum(e, axis=-1, keepdims=True) + e_new, 1e-30)
    return e * inv, e_new * inv


def _nsa_sample_kernel(pt_ref, qs_ref, q_ref, gl_ref, new_ref, ncol_ref, kvc_ref, cache_ref, win_ref, cbs_ref,
                       sbz_ref, wb_ref, b0_ref, as_ref, o_ref, wout_ref, gk_ref, gv_ref, sem_ref, idx_ref,
                       *, past, page, row_base, n_pick):
    b = pl.program_id(0)
    blk_per_page = page // SEL_BLOCK
    nbs = past // SEL_BLOCK
    ncs = kvc_ref.shape[2]
    row = lax.broadcasted_iota(jnp.int32, (SUB, 1), 0)

    cmp_ok = lax.broadcasted_iota(jnp.int32, (SUB, ncs), 1) * CMP_STRIDE + (CMP_BLOCK - 1) <= past
    imp = jnp.zeros((SUB, ncs), F32)
    o_cmp = []
    for k in range(N_KV_HEADS):
        half = k % HEADS_PER_CG
        ck = kvc_ref[0, k // HEADS_PER_CG].astype(BF16)
        cv = kvc_ref[0, CG_PER_SLOT + k // HEADS_PER_CG].astype(BF16)
        s = lax.dot_general(qs_ref[0, k], ck, _NT, preferred_element_type=F32) + cbs_ref[k]
        m = jnp.max(jnp.where(cmp_ok, s, NEG), axis=-1, keepdims=True)
        e = jnp.where(cmp_ok, jnp.exp(s - m), 0.0)
        p = e / jnp.maximum(jnp.sum(e, axis=-1, keepdims=True), 1e-30)
        o_cmp.append(jnp.dot(p.astype(BF16), cv, preferred_element_type=F32)[:, half * HEAD_DIM:(half + 1) * HEAD_DIM])
        imp_k = p[0:1]
        for g in range(1, GROUP):
            imp_k = imp_k + p[g:g + 1]
        imp = imp + jnp.where(row == k, imp_k, 0.0)
    bs = sum(jnp.dot(part, as_ref[...], preferred_element_type=F32) for part in _split_bf16(imp, 3))

    lane_b = lax.broadcasted_iota(jnp.int32, (SUB, nbs), 1)
    lane_f = lane_b.astype(F32)
    score = bs + jnp.where((lane_b == 0) | (lane_b == nbs - 1), FORCE_BONUS, 0.0)

    def gathers(k, r):
        j = idx_ref[k * n_pick + r]
        pg = pt_ref[b, j // blk_per_page]
        tile0 = (row_base + pg) * N_KV_SLOTS
        return [pltpu.make_async_copy(
            cache_ref.at[pl.ds(((tile0 + slot) * N_KV_HEADS + k) * HEAD_DIM, HEAD_DIM), :],
            buf.at[k * n_pick + r], sem_ref.at[0]) for slot, buf in ((2, gk_ref), (3, gv_ref))]

    for r in range(n_pick):
        m = jnp.max(score, axis=-1, keepdims=True)
        pick = jnp.min(jnp.where(score == m, lane_f, 1e9), axis=-1, keepdims=True)
        score = jnp.where(lane_f == pick, -jnp.inf, score)
        for k in range(N_KV_HEADS):
            idx_ref[k * n_pick + r] = jnp.sum(jnp.where(row == k, pick, 0.0)).astype(jnp.int32)
            for cp in gathers(k, r):
                cp.start()

    def new_row(k, j):
        return new_ref[0, k][j:j + 1].astype(BF16).astype(F32)

    tok = lax.broadcasted_iota(jnp.int32, (1, WINDOW), 1)
    o_win = []
    for k in range(N_KV_HEADS):
        for slot in range(N_WIN_SLOTS):
            shifted = pltpu.roll(win_ref[0, slot, k], WINDOW - 1, axis=1)
            wout_ref[0, slot, k] = jnp.where(tok == WINDOW - 1, ncol_ref[0, slot, k], shifted)
        q = q_ref[0, k]
        s = jnp.dot(q, win_ref[0, 0, k].astype(BF16), preferred_element_type=F32) + wb_ref[k]
        s = jnp.where(tok >= 1, s, NEG)
        s_new = jnp.sum(q.astype(F32) * new_row(k, 2), axis=-1, keepdims=True) + b0_ref[k][:, 0:1]
        p, p_new = _softmax_with_new_key(s, s_new)
        o_win.append(lax.dot_general(p.astype(BF16), win_ref[0, 1, k].astype(BF16), _NT,
                                     preferred_element_type=F32) + p_new * new_row(k, 3))

    for k in range(N_KV_HEADS):
        for r in range(n_pick):
            for cp in gathers(k, r):
                cp.wait()

    upper = lax.broadcasted_iota(jnp.int32, (1, LANES), 1) // SEL_BLOCK
    for k in range(N_KV_HEADS):
        q = q_ref[0, k]
        tiles = []
        for r in range(n_pick):
            j = idx_ref[k * n_pick + r]
            near = j - (nbs - NEAR // SEL_BLOCK)
            s = jnp.dot(q, gk_ref[k * n_pick + r].astype(BF16), preferred_element_type=F32)
            s = s + sbz_ref[k, jnp.where(near >= 0, near, NEAR // SEL_BLOCK)]
            tiles.append(jnp.where(upper == j % blk_per_page, s, NEG))
        s_new = jnp.sum(q.astype(F32) * new_row(k, 0), axis=-1, keepdims=True) + b0_ref[k][:, 0:1]
        p, p_new = _softmax_with_new_key(jnp.concatenate(tiles, axis=1), s_new)
        o_sel = p_new * new_row(k, 1)
        for r in range(n_pick):
            o_sel = o_sel + lax.dot_general(p[:, r * LANES:(r + 1) * LANES].astype(BF16),
                                            gv_ref[k * n_pick + r].astype(BF16), _NT, preferred_element_type=F32)
        g_cmp, g_sel, g_win = (jax.nn.sigmoid(gl_ref[0, k, j])[:, :HEAD_DIM] for j in range(3))
        o_ref[0, k] = g_cmp * o_cmp[k] + g_sel * o_sel + g_win * o_win[k]


def _bias_rows(bvz, idx):
    t = jnp.take(bvz, jnp.asarray(idx, jnp.int32), axis=0).reshape(len(idx), N_KV_HEADS, GROUP)
    t = jnp.transpose(t, (1, 2, 0))
    return jnp.pad(t, ((0, 0), (0, SUB - GROUP), (0, 0)))


def _nsa_sample(proj_s, kvc, page_table, cache_t, win_t, bvz, past, page, row_base):
    nb = proj_s.shape[0]
    n_pick = N_SELECT - 1
    nbs = past // SEL_BLOCK
    ncs = past // CMP_STRIDE
    n_near = NEAR // SEL_BLOCK
    blk_per_page = page // SEL_BLOCK
    assert past % page == 0 and page == LANES and nbs > n_pick + n_near and (nbs - n_near) % blk_per_page == 0
    assert win_t.shape[-1] == WINDOW < past
    q = (proj_s[:, :N_Q_COLS] * HEAD_DIM ** -0.5).astype(BF16).reshape(nb, N_KV_HEADS, GROUP, HEAD_DIM)
    q = jnp.pad(q, ((0, 0), (0, 0), (0, SUB - GROUP), (0, 0)))
    qs = jnp.stack([jnp.pad(q[:, k], ((0, 0), (0, 0), ((k % HEADS_PER_CG) * HEAD_DIM,
                                                       LANES - (k % HEADS_PER_CG + 1) * HEAD_DIM)))
                    for k in range(N_KV_HEADS)], axis=1)
    gl = proj_s[:, GATE_COL0:].reshape(nb, N_KV_HEADS, LANES)[:, :, :GROUP * 3].reshape(nb, N_KV_HEADS, GROUP, 3)
    gl = jnp.pad(jnp.transpose(gl, (0, 1, 3, 2)), ((0, 0), (0, 0), (0, 0), (0, SUB - GROUP)))
    gl = jnp.broadcast_to(gl[..., None], gl.shape + (LANES,))
    new = _unpermute_kv(proj_s[:, N_Q_COLS:GATE_COL0], (nb,))[:, 2:]
    new_rows = jnp.transpose(new, (0, 2, 1, 3))
    new_cols = new[:, 2:].reshape(nb, N_WIN_SLOTS, N_KV_HEADS, HEAD_DIM, 1)

    dc = past - (np.arange(ncs) * CMP_STRIDE + CMP_BLOCK - 1)
    cbs = _bias_rows(bvz, np.where((dc >= 0) & (dc < NEAR), dc, NEAR))
    lane = np.arange(LANES)
    sb = []
    for jj in range(n_near + 1):
        d = NEAR - SEL_BLOCK * jj - lane % SEL_BLOCK
        ok = (lane // SEL_BLOCK == jj % blk_per_page) & (jj < n_near) & (d < NEAR)
        sb.append(_bias_rows(bvz, np.where(ok, d, NEAR)))
    sbz = jnp.stack(sb, axis=1)
    dw = WINDOW - np.arange(WINDOW)
    wb = _bias_rows(bvz, np.where(dw < NEAR, dw, NEAR))
    b0 = _bias_rows(bvz, np.zeros(LANES, np.int64))
    a_s = np.zeros((ncs, nbs), np.float32)
    for j in range(nbs):
        for n in range(SEL_RATIO * j - (CMP_RATIO - 1), SEL_RATIO * (j + 1)):
            if 0 <= n < ncs - CMP_RATIO + 1:
                a_s[n, j] = 1.0
    a_s = jnp.asarray(a_s, BF16)

    per_b = lambda a: pl.BlockSpec((1,) + a.shape[1:], lambda b, pt: (b,) + (0,) * (a.ndim - 1))
    full = lambda a: pl.BlockSpec(a.shape, lambda b, pt: (0,) * a.ndim)
    grid_spec = pltpu.PrefetchScalarGridSpec(
        num_scalar_prefetch=1,
        grid=(nb,),
        in_specs=[per_b(qs), per_b(q), per_b(gl), per_b(new_rows), per_b(new_cols), per_b(kvc),
                  pl.BlockSpec(memory_space=pl.ANY), per_b(win_t),
                  full(cbs), full(sbz), full(wb), full(b0), full(a_s)],
        out_specs=[pl.BlockSpec((1, N_KV_HEADS, SUB, HEAD_DIM), lambda b, pt: (b, 0, 0, 0)), per_b(win_t)],
        scratch_shapes=[pltpu.VMEM((N_KV_HEADS * n_pick, HEAD_DIM, page), F32),
                        pltpu.VMEM((N_KV_HEADS * n_pick, HEAD_DIM, page), F32),
                        pltpu.SemaphoreType.DMA((1,)),
                        pltpu.SMEM((N_KV_HEADS * n_pick,), jnp.int32)])
    o, win_out = pl.pallas_call(
        functools.partial(_nsa_sample_kernel, past=past, page=page, row_base=row_base, n_pick=n_pick),
        grid_spec=grid_spec,
        out_shape=[jax.ShapeDtypeStruct((nb, N_KV_HEADS, SUB, HEAD_DIM), F32),
                   jax.ShapeDtypeStruct(win_t.shape, F32)],
        compiler_params=_cparams(("arbitrary",)),
    )(page_table, qs, q, gl, new_rows, new_cols, kvc, cache_t, win_t, cbs, sbz, wb, b0, a_s)
    return o[:, :, :GROUP].reshape(nb, N_Q_COLS), win_out


MXU_WIDTH = 256
SSM_COL_TILE = 5 * MXU_WIDTH
NP_SSM = -(-(SSM_CONV_DIM + SSM_D_INNER + SSM_HEADS) // SSM_COL_TILE) * SSM_COL_TILE
N_BC = SSM_GROUPS * SSM_STATE
HEADS_PER_GROUP = SSM_HEADS // SSM_GROUPS
PAIR = LANES // SSM_HEAD_DIM
N_PAIRS = SSM_HEADS // PAIR
CONV_PAD = 8


def _softplus(x):
    return jnp.maximum(x, 0.0) + jnp.log(1.0 + jnp.exp(-jnp.abs(x)))


def _cumsum_rows(x):
    n = x.shape[0]
    row = lax.broadcasted_iota(jnp.int32, x.shape, 0)
    s = 1
    while s < n:
        x = x + jnp.where(row >= s, pltpu.roll(x, s, axis=0), 0.0)
        s *= 2
    return x


def _ssd_kernel(xbc_ref, z_ref, dt_ref, h0_ref, cinit_ref, cw_ref, cb_ref, dtb_ref, alog_ref, dsk_ref, ng_ref,
                y_ref, hout_ref, xs_ref, act_ref, ybuf_ref, h_ref, *, nc):
    c = pl.program_id(1)
    q = SSM_CHUNK

    @pl.when(c == 0)
    def _():
        xs_ref[...] = cinit_ref[0]
        h_ref[...] = h0_ref[0]

    x = xbc_ref[...]
    prev = xs_ref[...]
    head_row = lax.broadcasted_iota(jnp.int32, (CONV_PAD, 1), 0)
    conv = cb_ref[...] + cw_ref[SSM_CONV - 1:SSM_CONV] * x
    for j in range(1, SSM_CONV):
        xj = pltpu.roll(x, j, axis=0)
        head = jnp.where(head_row < j, pltpu.roll(prev, j, axis=0), xj[:CONV_PAD])
        xj = jnp.concatenate([head, xj[CONV_PAD:]], axis=0)
        conv = conv + cw_ref[SSM_CONV - 1 - j:SSM_CONV - j] * xj
    xs_ref[...] = x[q - CONV_PAD:]
    act_ref[...] = _silu(conv)

    dt = _softplus(dt_ref[...] + dtb_ref[...])
    acum = _cumsum_rows(dt * (-jnp.exp(alog_ref[...])))
    acum_t = acum.T
    dt_t = dt.T
    last = acum[q - 1:q, :]
    ea = jnp.exp(acum)
    te = jnp.exp(last - acum) * dt
    cd = jnp.exp(last)
    ii = lax.broadcasted_iota(jnp.int32, (q, q), 0)
    jj = lax.broadcasted_iota(jnp.int32, (q, q), 1)
    tri = ii >= jj
    low = jj < SSM_HEAD_DIM
    low_rows = ii < SSM_HEAD_DIM

    def col(a, h):
        return a[:, h:h + 1]

    for g in range(SSM_GROUPS):
        bg = act_ref[:, SSM_D_INNER + g * SSM_STATE:SSM_D_INNER + (g + 1) * SSM_STATE].astype(BF16)
        cg = act_ref[:, SSM_D_INNER + N_BC + g * SSM_STATE:SSM_D_INNER + N_BC + (g + 1) * SSM_STATE].astype(BF16)
        cbg = lax.dot_general(cg, bg, _NT, preferred_element_type=F32)
        for pr in range(HEADS_PER_GROUP // PAIR):
            k = g * (HEADS_PER_GROUP // PAIR) + pr
            ha, hb = PAIR * k, PAIR * k + 1
            xp = act_ref[:, k * LANES:(k + 1) * LANES]
            xpb = xp.astype(BF16)
            ys = []
            for h in (ha, hb):
                decay = jnp.exp(jnp.where(tri, col(acum, h) - acum_t[h:h + 1, :], NEG))
                w = cbg * decay * dt_t[h:h + 1, :]
                ys.append(jnp.dot(w.astype(BF16), xpb, preferred_element_type=F32))
            y = jnp.where(low, ys[0], ys[1])
            xs_pair = xp * jnp.where(low, col(te, ha), col(te, hb))
            st = jnp.dot(xs_pair.T.astype(BF16), bg, preferred_element_type=F32)
            hprev = h_ref[k]
            yoff = lax.dot_general(cg, hprev.astype(BF16), _NT, preferred_element_type=F32)
            y = y + yoff * jnp.where(low, col(ea, ha), col(ea, hb)) + dsk_ref[:, k * LANES:(k + 1) * LANES] * xp
            h_ref[k] = hprev * jnp.where(low_rows, cd[:, ha:ha + 1], cd[:, hb:hb + 1]) + st
            ybuf_ref[:, k * LANES:(k + 1) * LANES] = y

    yz = ybuf_ref[...] * _silu(z_ref[...])
    gw = SSM_D_INNER // SSM_GROUPS
    outs = []
    for g in range(SSM_GROUPS):
        seg = yz[:, g * gw:(g + 1) * gw]
        outs.append(seg * lax.rsqrt(jnp.mean(seg * seg, axis=-1, keepdims=True) + EPS))
    y_ref[...] = (jnp.concatenate(outs, axis=1) * ng_ref[...]).astype(y_ref.dtype)

    @pl.when(c == nc - 1)
    def _():
        hout_ref[0] = h_ref[...]


def _ssd(proj, h0, cinit, sw, nb, t):
    q = SSM_CHUNK
    nc = t // q
    vec = lambda a: pl.BlockSpec(a.shape, lambda b, c: (0, 0))
    return pl.pallas_call(
        functools.partial(_ssd_kernel, nc=nc),
        grid=(nb, nc),
        in_specs=[pl.BlockSpec((q, SSM_CONV_DIM), lambda b, c: (b * nc + c, 0)),
                  pl.BlockSpec((q, SSM_D_INNER), lambda b, c: (b * nc + c, SSM_CONV_DIM // SSM_D_INNER)),
                  pl.BlockSpec((q, LANES), lambda b, c: (b * nc + c, (SSM_CONV_DIM + SSM_D_INNER) // LANES)),
                  pl.BlockSpec((1,) + h0.shape[1:], lambda b, c: (b, 0, 0, 0)),
                  pl.BlockSpec((1,) + cinit.shape[1:], lambda b, c: (b, 0, 0)),
                  vec(sw['conv_w']), vec(sw['conv_b']), vec(sw['dt_bias']), vec(sw['a_log']), vec(sw['d_lane']),
                  vec(sw['norm_g'])],
        out_specs=[pl.BlockSpec((q, SSM_D_INNER), lambda b, c: (b * nc + c, 0)),
                   pl.BlockSpec((1,) + h0.shape[1:], lambda b, c: (b, 0, 0, 0))],
        out_shape=[jax.ShapeDtypeStruct((nb * t, SSM_D_INNER), BF16),
                   jax.ShapeDtypeStruct(h0.shape, F32)],
        scratch_shapes=[pltpu.VMEM((CONV_PAD, SSM_CONV_DIM), F32),
                        pltpu.VMEM((q, SSM_CONV_DIM), F32),
                        pltpu.VMEM((q, SSM_D_INNER), F32),
                        pltpu.VMEM(h0.shape[1:], F32)],
        compiler_params=_cparams(("arbitrary", "arbitrary")),
    )(proj, proj, proj, h0, cinit, sw['conv_w'], sw['conv_b'], sw['dt_bias'], sw['a_log'], sw['d_lane'],
      sw['norm_g'])


def _ssm_sample_step(proj_s, state_ssm, state_conv, sw):
    nb = proj_s.shape[0]
    q = SSM_CHUNK
    xbc_new = proj_s[:, :SSM_CONV_DIM]
    n_tail = NP_SSM - SSM_CONV_DIM
    tail = jnp.where(jnp.arange(n_tail) < SSM_D_INNER, 0.0, NEG).astype(F32)
    rows = jnp.concatenate([
        jnp.concatenate([jnp.zeros((nb, q - SSM_CONV, SSM_CONV_DIM), F32), state_conv.astype(F32)], axis=1),
        jnp.broadcast_to(tail, (nb, q - 1, n_tail))], axis=2)
    rows = jnp.concatenate([rows, proj_s[:, None]], axis=1)
    h0 = state_ssm.astype(F32).reshape(nb, N_PAIRS, LANES, SSM_STATE)
    cinit = jnp.zeros((nb, CONV_PAD, SSM_CONV_DIM), F32)
    yn, hfin = _ssd(rows.reshape(nb * q, NP_SSM), h0, cinit, sw, nb, q)
    conv_new = jnp.concatenate([state_conv[:, 1:].astype(F32), xbc_new[:, None]], axis=1)
    return yn.reshape(nb, q, SSM_D_INNER)[:, q - 1], hfin.reshape(state_ssm.shape), conv_new


def _ssm_weights(w_in, conv_w, conv_b, dt_bias, a_log, d_skip, norm_g):
    z_w = w_in[:, :SSM_D_INNER]
    xbc_w = w_in[:, SSM_D_INNER:SSM_D_INNER + SSM_CONV_DIM]
    dt_w = w_in[:, SSM_D_INNER + SSM_CONV_DIM:]
    pad = NP_SSM - w_in.shape[1]
    w = jnp.concatenate([xbc_w, z_w, dt_w, jnp.zeros((w_in.shape[0], pad), w_in.dtype)], axis=1).astype(BF16)
    lane_pad = lambda v: jnp.zeros((1, LANES), F32).at[0, :SSM_HEADS].set(v.astype(F32))
    return dict(w_in=w, conv_w=conv_w.astype(F32), conv_b=conv_b.astype(F32).reshape(1, -1),
                dt_bias=lane_pad(dt_bias), a_log=lane_pad(a_log),
                d_lane=jnp.repeat(d_skip.astype(F32), SSM_HEAD_DIM).reshape(1, -1),
                norm_g=norm_g.astype(F32).reshape(1, -1))


ROW_TILE = 512
MLP_ROW_TILE = 1024
FF_TILE = 1024


def kernel(x_prompt, x_sample, cache_kv, cache_win, state_ssm, state_conv, page_table, c_prompt, c_sample, rel_bias,
           ada_w, ada_b, norm_g, mlp_w1, mlp_w2, attn_w_in, attn_w_out, cmp_pos, cmp_w1, cmp_w2, ssm_w_in,
           ssm_conv_w, ssm_conv_b, ssm_dt_bias, ssm_a_log, ssm_d, ssm_norm_g, ssm_w_out):
    nb, t, d = x_prompt.shape
    db = x_sample.shape[0]
    assert x_sample.shape[1] == 1 and t % SSM_CHUNK == 0 and t % Q_BLOCK == 0
    n_pool, page = cache_kv.shape[1], cache_kv.shape[2]
    past = page_table.shape[1] * page
    depth = ada_w.shape[0]
    tm = _row_tile(t, ROW_TILE)

    xp = x_prompt.reshape(nb * t, d).astype(F32)
    xs = x_sample.reshape(db, d).astype(F32)
    c_all = jnp.concatenate([c_prompt, c_sample], axis=0).astype(F32)
    c_all = jnp.pad(c_all, ((0, (-c_all.shape[0]) % SUB), (0, 0)))
    bvz = _bias_vector(rel_bias)
    cache_t = _cache_tiles(cache_kv)
    win_t = jnp.transpose(cache_win, (0, 1, 3, 4, 5, 2))

    kv_p, win_p, ssm_p, conv_p, kv_s, win_s, ssm_s, conv_s = ([] for _ in range(8))
    for i in range(depth):
        mod = _ada(c_all, ada_w[i], ada_b[i])
        mp = [mod[:nb, j * d:(j + 1) * d].reshape(nb, 1, d) for j in range(N_MOD)]
        ms = [mod[nb:nb + db, j * d:(j + 1) * d].reshape(1, db, d) for j in range(N_MOD)]
        g = norm_g[i].astype(F32)
        if i % 2 == 0:
            a = i // 2
            w_in = _attn_in_weights(attn_w_in[a])
            w_out = attn_w_out[a].astype(BF16)
            cw = _compress_weights(cmp_pos[a], cmp_w1[a], cmp_w2[a], [(0, 1)])
            o_p, kv_new, win_new_p = _nsa_prompt_mixer(xp, g[0], mp[0], mp[1], w_in, cw, rel_bias, nb, t)
            kv_p.append(kv_new)
            win_p.append(win_new_p)
            xp = _heads_out(o_p, _attn_out_weights(attn_w_out[a]), g[1], mp[2], xp, tm)

            proj_s = _nm_matmul(xs, g[0], ms[0], ms[1], w_in, db, NP_ATTN)
            cw_s = _compress_weights(cmp_pos[a], cmp_w1[a], cmp_w2[a], [(0, 0), (1, 1)])
            kvc_s = _compress_sample(page_table, cache_t, cw_s, page, a * n_pool)
            o_s, win_new = _nsa_sample(proj_s, kvc_s, page_table, cache_t, win_t[a], bvz, past, page, a * n_pool)
            kv_s.append(_unpermute_kv(proj_s[:, N_Q_COLS:GATE_COL0], (db, 1))[:, :, :N_KV_SLOTS])
            win_s.append(jnp.transpose(win_new, (0, 4, 1, 2, 3)))
            xs = _mm_norm_res(o_s, w_out, g[1], ms[2], xs, db)
        else:
            m = i // 2
            sw = _ssm_weights(ssm_w_in[m], ssm_conv_w[m], ssm_conv_b[m], ssm_dt_bias[m], ssm_a_log[m], ssm_d[m],
                              ssm_norm_g[m])
            w_out = ssm_w_out[m].astype(BF16)
            proj = _nm_matmul(xp, g[0], mp[0], mp[1], sw['w_in'], tm, SSM_COL_TILE)
            h0 = jnp.zeros((nb, N_PAIRS, LANES, SSM_STATE), F32)
            cinit = jnp.zeros((nb, CONV_PAD, SSM_CONV_DIM), F32)
            yn, hfin = _ssd(proj, h0, cinit, sw, nb, t)
            ssm_p.append(hfin.reshape(nb, SSM_HEADS, SSM_HEAD_DIM, SSM_STATE).astype(state_ssm.dtype))
            conv_p.append(proj.reshape(nb, t, NP_SSM)[:, t - (SSM_CONV - 1):, :SSM_CONV_DIM])
            xp = _mm_norm_res(yn, w_out, g[1], mp[2], xp, tm)

            proj_s = _nm_matmul(xs, g[0], ms[0], ms[1], sw['w_in'], db, SSM_COL_TILE)
            yn_s, h_s, conv_new = _ssm_sample_step(proj_s, state_ssm[m], state_conv[m], sw)
            ssm_s.append(h_s.astype(state_ssm.dtype))
            conv_s.append(conv_new)
            xs = _mm_norm_res(yn_s, w_out, g[1], ms[2], xs, db)
        w1 = mlp_w1[i].astype(BF16)
        w2 = mlp_w2[i].astype(BF16)
        xp = _mlp(xp, g[2], mp[3], mp[4], w1, w2, g[3], mp[5], _row_tile(t, MLP_ROW_TILE), FF_TILE)
        xs = _mlp(xs, g[2], ms[3], ms[4], w1, w2, g[3], ms[5], db, FF_TILE)
    return (xp.reshape(nb, t, d), xs.reshape(db, 1, d), jnp.stack(kv_p), jnp.stack(win_p), jnp.stack(ssm_p),
            jnp.stack(conv_p), jnp.stack(kv_s), jnp.stack(win_s), jnp.stack(ssm_s), jnp.stack(conv_s))
```
